```python
import math
import jax
import jax.numpy as jnp
from jax import lax
import numpy as np

D_MODEL = 1024
BATCH = 4
SEQ = 8192
DEPTH = 1

GRID_W = 64
CTX_LEN = 256
D_MIX = D_MODEL
D_HYENA = D_MIX // 2
HYENA_SHORT_W = 3
FILTER_ORDER = 64
POS_EMB_DIM = 33
DECAY_TARGET = 1e-2
FAST_DECAY_PCT = 0.3
SLOW_DECAY_PCT = 1.5
MIN_DECAY = math.log(DECAY_TARGET) / FAST_DECAY_PCT
MAX_DECAY = math.log(DECAY_TARGET) / SLOW_DECAY_PCT
NOPE_DIM = 128
ROPE_DIM = 64
QK_DIM = NOPE_DIM + ROPE_DIM
V_DIM = 128
MLA_HEADS = (D_MIX - D_HYENA) // V_DIM
Q_RANK = 256
KV_RANK = 128
ROPE_THETA = 10000.0
Q_BLOCK = 128
N_EXPERTS = 32
TOP_K = 4
D_FF = D_MODEL
SWIGLU_ALPHA = 1.702
SWIGLU_LIMIT = 7.0
MOE_BLOCK = 128
NORM_EPS = 1e-6
HY_COLS = 3 * D_HYENA
KV_COL = HY_COLS + Q_RANK
PE_COL = KV_COL + KV_RANK
IN_COLS = PE_COL + ROPE_DIM

kernel_name = "hyena_mla_moe_hybrid_dit_layer"


def rms_norm(x, g):
    xf = x.astype(jnp.float32)
    y = xf * lax.rsqrt(jnp.mean(xf * xf, axis=-1, keepdims=True) + NORM_EPS)
    return (y * g.astype(jnp.float32)).astype(x.dtype)


def modulate(h, shift, scale):
    return h * (1.0 + scale) + shift


def axial_rope_tables(L):
    rows = L // GRID_W
    row = jnp.repeat(jnp.arange(rows, dtype=jnp.int32), GRID_W).astype(jnp.float32)
    col = jnp.tile(jnp.arange(GRID_W, dtype=jnp.int32), rows).astype(jnp.float32)
    n_freq = ROPE_DIM // 4
    inv_freq = jnp.power(ROPE_THETA, -jnp.arange(n_freq, dtype=jnp.float32) / n_freq)
    ang = jnp.stack([row[:, None] * inv_freq, col[:, None] * inv_freq], axis=1)
    return jnp.cos(ang), jnp.sin(ang)


def apply_axial_rope(x, cos, sin):
    shp = x.shape
    xr = x.reshape(*shp[:-1], 2, ROPE_DIM // 2)
    x1, x2 = xr[..., : ROPE_DIM // 4], xr[..., ROPE_DIM // 4:]
    cb = cos[None, :, None].astype(x.dtype)
    sb = sin[None, :, None].astype(x.dtype)
    out = jnp.concatenate([x1 * cb - x2 * sb, x2 * cb + x1 * sb], axis=-1)
    return out.reshape(shp)


def short_conv(u, w, b):
    up = jnp.pad(u, ((0, 0), (1, 1), (0, 0)))
    return up[:, :-2] * w[0] + up[:, 1:-1] * w[1] + up[:, 2:] * w[2] + b


def hyena_filters(L, w1, b1, w2, b2, w3, b3, w4, freq):
    t = jnp.linspace(0.0, 1.0, L, dtype=jnp.float32)[:, None]
    bands = (POS_EMB_DIM - 1) // 2
    w_ang = 2.0 * math.pi * jnp.arange(L, dtype=jnp.float32)[:, None] / L
    f = jnp.linspace(1e-4, bands - 1, bands, dtype=jnp.float32)[None, :]
    z = jnp.concatenate([t, jnp.cos(f * w_ang), -jnp.sin(f * w_ang)], axis=-1)
    fr = freq.astype(jnp.float32)
    h = jnp.sin(fr * (z @ w1.astype(jnp.float32) + b1.astype(jnp.float32)))
    h = jnp.sin(fr * (h @ w2.astype(jnp.float32) + b2.astype(jnp.float32)))
    h = jnp.sin(fr * (h @ w3.astype(jnp.float32) + b3.astype(jnp.float32)))
    h = (h @ w4.astype(jnp.float32)).reshape(L, 2, D_HYENA)
    deltas = jnp.linspace(MIN_DECAY, MAX_DECAY, D_HYENA, dtype=jnp.float32)
    decay = jnp.exp(-t * jnp.abs(deltas))
    h = h * decay[:, None, :]
    h = h / jnp.sum(jnp.abs(h), axis=(0, 1), keepdims=True)
    return h[:, 0], h[:, 1]


def bidir_fft_conv(u, h_fwd, h_bwd):
    B, L, C = u.shape
    kern = jnp.concatenate([h_fwd, jnp.zeros((1, C), h_fwd.dtype), h_bwd[:0:-1]], axis=0)
    k_f = jnp.fft.rfft(kern, n=2 * L, axis=0)
    u_f = jnp.fft.rfft(u.astype(jnp.float32), n=2 * L, axis=1)
    y = jnp.fft.irfft(u_f * k_f[None], n=2 * L, axis=1)[:, :L]
    return y.astype(u.dtype)


def hyena_mixer(u, conv_w, conv_b, w1, b1, w2, b2, w3, b3, w4, freq, bias):
    L = u.shape[1]
    uc = short_conv(u, conv_w, conv_b)
    x0, x1, v = jnp.split(uc, 3, axis=-1)
    h_fwd, h_bwd = hyena_filters(L, w1, b1, w2, b2, w3, b3, w4, freq)
    v = v * x1
    v = bidir_fft_conv(v, h_fwd, h_bwd) + v * bias
    return v * x0


def mla_queries(c_q, q_norm_g, w_uq, qn_g, rope):
    B, L, _ = c_q.shape
    q = (rms_norm(c_q, q_norm_g) @ w_uq).reshape(B, L, MLA_HEADS, QK_DIM)
    q = rms_norm(q, qn_g)
    if rope is not None:
        q = jnp.concatenate([q[..., :NOPE_DIM], apply_axial_rope(q[..., NOPE_DIM:], *rope)], axis=-1)
    return q


def mla_keys_values(c_kv, k_pe, kv_norm_g, w_ukv, kn_g, rope):
    B, L, _ = c_kv.shape
    kv = (rms_norm(c_kv, kv_norm_g) @ w_ukv).reshape(B, L, MLA_HEADS, NOPE_DIM + V_DIM)
    k_nope, v = jnp.split(kv, [NOPE_DIM], axis=-1)
    k_rope = jnp.broadcast_to(k_pe[:, :, None, :], (B, L, MLA_HEADS, ROPE_DIM))
    k = rms_norm(jnp.concatenate([k_nope, k_rope], axis=-1), kn_g)
    if rope is not None:
        k = jnp.concatenate([k[..., :NOPE_DIM], apply_axial_rope(k[..., NOPE_DIM:], *rope)], axis=-1)
    return k, v


def block_attention(q, k, v):
    B, Lq, H, _ = q.shape
    n_blk = Lq // Q_BLOCK
    q_blocks = q.reshape(B, n_blk, Q_BLOCK, H, QK_DIM).transpose(1, 0, 2, 3, 4)
    scale = QK_DIM ** -0.5

    def one_block(qb):
        s = jnp.einsum('bqhd,bkhd->bhqk', qb, k, preferred_element_type=jnp.float32) * scale
        p = jax.nn.softmax(s, axis=-1)
        return jnp.einsum('bhqk,bkhd->bqhd', p.astype(v.dtype), v)

    out = lax.map(one_block, q_blocks)
    return out.transpose(1, 0, 2, 3, 4).reshape(B, Lq, H * V_DIM)


def moe_ffn(h, router_w, router_b, w_gu, b_gu, w_down, b_down):
    T, D = h.shape
    logits = (h @ router_w + router_b).astype(jnp.float32)
    top_logits, top_idx = lax.top_k(logits, TOP_K)
    gates = jax.nn.softmax(top_logits, axis=-1)
    n_pairs = T * TOP_K
    flat_e = top_idx.reshape(-1)
    flat_tok = jnp.repeat(jnp.arange(T, dtype=jnp.int32), TOP_K)
    flat_gate = gates.reshape(-1)
    order = jnp.argsort(flat_e)
    sorted_e = flat_e[order]
    counts = jnp.bincount(flat_e, length=N_EXPERTS)
    starts = jnp.cumsum(counts) - counts
    padded = (counts + MOE_BLOCK - 1) // MOE_BLOCK * MOE_BLOCK
    padded_ends = jnp.cumsum(padded)
    padded_starts = padded_ends - padded
    dest = padded_starts[sorted_e] + jnp.arange(n_pairs, dtype=jnp.int32) - starts[sorted_e]
    n_blocks = -(-n_pairs // MOE_BLOCK) + N_EXPERTS
    n_rows = n_blocks * MOE_BLOCK
    row_tok = jnp.full((n_rows,), T, jnp.int32).at[dest].set(flat_tok[order])
    row_gate = jnp.zeros((n_rows,), jnp.float32).at[dest].set(flat_gate[order])
    block_start = jnp.arange(n_blocks, dtype=jnp.int32) * MOE_BLOCK
    block_e = jnp.minimum(jnp.searchsorted(padded_ends, block_start, side='right'), N_EXPERTS - 1)
    h_pad = jnp.concatenate([h, jnp.zeros((1, D), h.dtype)], axis=0)
    xs = h_pad[row_tok].reshape(n_blocks, MOE_BLOCK, D)

    def expert_block(args):
        xb, e = args
        gu = xb @ w_gu[e] + b_gu[e]
        gate, up = jnp.split(gu, 2, axis=-1)
        gate = jnp.minimum(gate, SWIGLU_LIMIT)
        up = jnp.clip(up, -SWIGLU_LIMIT, SWIGLU_LIMIT)
        glu = gate * jax.nn.sigmoid(SWIGLU_ALPHA * gate)
        return ((up + 1.0) * glu) @ w_down[e] + b_down[e]

    ys = lax.map(expert_block, (xs, block_e)).reshape(n_rows, D)
    out = jax.ops.segment_sum(ys * row_gate[:, None], row_tok, num_segments=T + 1)[:T]
    return out.astype(h.dtype)


def setup_inputs(seed: int = 0) -> dict:
    key = jax.random.key(seed)
    ks = iter(jax.random.split(key, 40))

    def nrm(shape, scale):
        return scale * jax.random.normal(next(ks), shape, jnp.float32)

    def gain(shape):
        return 1.0 + nrm(shape, 0.02)

    L = DEPTH
    return {
        "x": nrm((BATCH, SEQ, D_MODEL), 1.0),
        "c": nrm((BATCH, D_MODEL), 1.0),
        "ctx": nrm((BATCH, CTX_LEN, D_MODEL), 1.0),
        "c_ctx": nrm((D_MODEL,), 1.0),
        "mod_w": nrm((L, D_MODEL, 6 * D_MODEL), D_MODEL ** -0.5),
        "mod_b": nrm((L, 6 * D_MODEL), 0.02),
        "norm1_g": gain((L, D_MODEL)),
        "w_in": nrm((L, D_MODEL, IN_COLS), D_MODEL ** -0.5),
        "hy_conv_w": nrm((L, HYENA_SHORT_W, HY_COLS), HYENA_SHORT_W ** -0.5),
        "hy_conv_b": nrm((L, HY_COLS), 0.02),
        "hy_f_w1": nrm((L, POS_EMB_DIM, FILTER_ORDER), POS_EMB_DIM ** -0.5),
        "hy_f_b1": nrm((L, FILTER_ORDER), 0.02),
        "hy_f_w2": nrm((L, FILTER_ORDER, FILTER_ORDER), FILTER_ORDER ** -0.5),
        "hy_f_b2": nrm((L, FILTER_ORDER), 0.02),
        "hy_f_w3": nrm((L, FILTER_ORDER, FILTER_ORDER), FILTER_ORDER ** -0.5),
        "hy_f_b3": nrm((L, FILTER_ORDER), 0.02),
        "hy_f_w4": nrm((L, FILTER_ORDER, 2 * D_HYENA), FILTER_ORDER ** -0.5),
        "hy_f_freq": gain((L, FILTER_ORDER)),
        "hy_bias": nrm((L, D_HYENA), 1.0),
        "mla_q_norm_g": gain((L, Q_RANK)),
        "mla_w_uq": nrm((L, Q_RANK, MLA_HEADS * QK_DIM), Q_RANK ** -0.5),
        "mla_kv_norm_g": gain((L, KV_RANK)),
        "mla_w_ukv": nrm((L, KV_RANK, MLA_HEADS * (NOPE_DIM + V_DIM)), KV_RANK ** -0.5),
        "qk_norm_q_g": gain((L, QK_DIM)),
        "qk_norm_k_g": gain((L, QK_DIM)),
        "w_out": nrm((L, D_MIX, D_MODEL), D_MIX ** -0.5),
        "norm2_g": gain((L, D_MODEL)),
        "router_w": nrm((L, D_MODEL, N_EXPERTS), D_MODEL ** -0.5),
        "router_b": nrm((L, N_EXPERTS), 0.01),
        "exp_w_gu": nrm((L, N_EXPERTS, D_MODEL, 2 * D_FF), D_MODEL ** -0.5),
        "exp_b_gu": nrm((L, N_EXPERTS, 2 * D_FF), 0.02),
        "exp_w_down": nrm((L, N_EXPERTS, D_FF, D_MODEL), D_FF ** -0.5),
        "exp_b_down": nrm((L, N_EXPERTS, D_MODEL), 0.02),
    }


def reference(x, c, ctx, c_ctx, mod_w, mod_b, norm1_g, w_in, hy_conv_w, hy_conv_b,
              hy_f_w1, hy_f_b1, hy_f_w2, hy_f_b2, hy_f_w3, hy_f_b3, hy_f_w4, hy_f_freq, hy_bias,
              mla_q_norm_g, mla_w_uq, mla_kv_norm_g, mla_w_ukv, qk_norm_q_g, qk_norm_k_g,
              w_out, norm2_g, router_w, router_b, exp_w_gu, exp_b_gu, exp_w_down, exp_b_down):
    rope = axial_rope_tables(x.shape[1])
    xc = ctx
    for layer in range(DEPTH):
        last = layer == DEPTH - 1
        mod = jax.nn.silu(c) @ mod_w[layer] + mod_b[layer]
        sh1, sc1, g1, sh2, sc2, g2 = jnp.split(mod[:, None, :], 6, axis=-1)
        mod_c = jax.nn.silu(c_ctx) @ mod_w[layer] + mod_b[layer]
        csh1, csc1, cg1, csh2, csc2, cg2 = jnp.split(mod_c, 6, axis=-1)
        hy_p = (hy_conv_w[layer], hy_conv_b[layer], hy_f_w1[layer], hy_f_b1[layer],
                hy_f_w2[layer], hy_f_b2[layer], hy_f_w3[layer], hy_f_b3[layer],
                hy_f_w4[layer], hy_f_freq[layer], hy_bias[layer])
        q_p = (mla_q_norm_g[layer], mla_w_uq[layer], qk_norm_q_g[layer])
        kv_p = (mla_kv_norm_g[layer], mla_w_ukv[layer], qk_norm_k_g[layer])
        moe_p = (router_w[layer], router_b[layer], exp_w_gu[layer], exp_b_gu[layer],
                 exp_w_down[layer], exp_b_down[layer])
        w_in_l = w_in[layer]

        hc = modulate(rms_norm(xc, norm1_g[layer]), csh1, csc1)
        if last:
            ckv_c, kpe_c = jnp.split(hc @ w_in_l[:, KV_COL:], [KV_RANK], axis=-1)
        else:
            hy_c, cq_c, ckv_c, kpe_c = jnp.split(hc @ w_in_l, [HY_COLS, KV_COL, PE_COL], axis=-1)
        k_c, v_c = mla_keys_values(ckv_c, kpe_c, *kv_p, rope=None)

        h = modulate(rms_norm(x, norm1_g[layer]), sh1, sc1)
        hy, cq, ckv, kpe = jnp.split(h @ w_in_l, [HY_COLS, KV_COL, PE_COL], axis=-1)
        y_hy = hyena_mixer(hy, *hy_p)
        q = mla_queries(cq, *q_p, rope=rope)
        k, v = mla_keys_values(ckv, kpe, *kv_p, rope=rope)
        y_att = block_attention(q, jnp.concatenate([k, k_c], axis=1),
                                jnp.concatenate([v, v_c], axis=1))
        x_new = x + g1 * (jnp.concatenate([y_hy, y_att], axis=-1) @ w_out[layer])
        h2 = modulate(rms_norm(x_new, norm2_g[layer]), sh2, sc2)
        x_new = x_new + g2 * moe_ffn(h2.reshape(-1, D_MODEL), *moe_p).reshape(x.shape)

        if not last:
            yc_hy = hyena_mixer(hy_c, *hy_p)
            q_c = mla_queries(cq_c, *q_p, rope=None)
            yc_att = block_attention(q_c, k_c, v_c)
            xc = xc + cg1 * (jnp.concatenate([yc_hy, yc_att], axis=-1) @ w_out[layer])
            h2c = modulate(rms_norm(xc, norm2_g[layer]), csh2, csc2)
            xc = xc + cg2 * moe_ffn(h2c.reshape(-1, D_MODEL), *moe_p).reshape(xc.shape)
        x = x_new
    return x
```

```python
import functools
import math

import jax
import jax.numpy as jnp
import numpy as np
from jax import lax
from jax.experimental import pallas as pl
from jax.experimental.pallas import tpu as pltpu

F32 = jnp.float32
BF16 = jnp.bfloat16
HIGHEST = lax.Precision.HIGHEST

GRID_W = 64
D_HYENA = 512
FILTER_ORDER = 64
POS_EMB_DIM = 33
MIN_DECAY = math.log(1e-2) / 0.3
MAX_DECAY = math.log(1e-2) / 1.5
NOPE_DIM = 128
ROPE_DIM = 64
QK_DIM = NOPE_DIM + ROPE_DIM
V_DIM = 128
MLA_HEADS = 4
Q_RANK = 256
KV_RANK = 128
ROPE_THETA = 10000.0
N_EXPERTS = 32
TOP_K = 4
SWIGLU_ALPHA = 1.702
SWIGLU_LIMIT = 7.0
NORM_EPS = 1e-6

LANES = 128
VMEM_LIMIT = 56 * 1024 * 1024

DFT_Q = LANES
MOE_ROWS = 512


def _cparams(sem, vmem=None):
    return pltpu.CompilerParams(dimension_semantics=sem, vmem_limit_bytes=vmem)


def _const_spec(shape):
    nd = len(shape)
    return pl.BlockSpec(shape, lambda *_: (0,) * nd)


def _single_spec(shape, index_map):
    return pl.BlockSpec(shape, index_map, pipeline_mode=pl.Buffered(1))


def _mod_body(c_ref, w_ref, b_ref, o_ref):
    cc = c_ref[...]
    s = cc * jax.nn.sigmoid(cc)
    o_ref[...] = jnp.dot(s, w_ref[...], precision=HIGHEST,
                         preferred_element_type=F32) + b_ref[...]


def _adaln_table(c_rows, mod_w, mod_b):
    rows, d = c_rows.shape
    n = mod_w.shape[1]
    tn = n // 8
    return pl.pallas_call(
        _mod_body,
        grid=(n // tn,),
        in_specs=[_const_spec((rows, d)),
                  pl.BlockSpec((d, tn), lambda j: (0, j)),
                  pl.BlockSpec((1, tn), lambda j: (0, j))],
        out_specs=pl.BlockSpec((rows, tn), lambda j: (0, j)),
        out_shape=jax.ShapeDtypeStruct((rows, n), F32),
        compiler_params=_cparams(("arbitrary",)),
        name="adaln_table",
    )(c_rows, mod_w, mod_b.reshape(1, n))


def _rms(x, eps=NORM_EPS):
    return x * lax.rsqrt(jnp.mean(x * x, axis=-1, keepdims=True) + eps)


def _inproj_body(x_ref, sh_ref, sc_ref, g_ref, why_ref, wmla_ref, qng_ref, wuq_ref,
                 kvng_ref, wukv_ref, ct_ref, st_ref, ctk_ref, gq1_ref, gq2_ref, gkn_ref,
                 gkr_ref, hy_ref, q_ref, k_ref, v_ref, *, q_scale):
    h = _rms(x_ref[...]) * g_ref[...]
    h = h * (1.0 + sc_ref[...]) + sh_ref[...]
    hb = h.astype(BF16)
    hy_ref[...] = jnp.dot(hb, why_ref[...], preferred_element_type=F32).astype(BF16)
    mla = jnp.dot(hb, wmla_ref[...], preferred_element_type=F32)
    cq = mla[:, :Q_RANK]
    ckv = mla[:, Q_RANK:Q_RANK + KV_RANK]
    pe2 = mla[:, Q_RANK + KV_RANK:]
    qf = jnp.dot((_rms(cq) * qng_ref[...]).astype(BF16), wuq_ref[...],
                 preferred_element_type=F32)
    kvf = jnp.dot((_rms(ckv) * kvng_ref[...]).astype(BF16), wukv_ref[...],
                  preferred_element_type=F32)
    ct = ct_ref[...]
    st = st_ref[...]
    lane256 = lax.broadcasted_iota(jnp.int32, (1, 2 * LANES), 1)
    qmask = (lane256 < QK_DIM).astype(F32)
    lane128 = lax.broadcasted_iota(jnp.int32, (1, LANES), 1)
    pemask = (lane128 < ROPE_DIM).astype(F32)
    kr0 = pe2 * gkr_ref[...] * ctk_ref[...]
    krs = kr0 + pltpu.roll(kr0, ROPE_DIM, axis=1)
    pem = pe2 * pemask
    ss_pe = jnp.sum(pem * pem, axis=-1, keepdims=True)
    gq1 = gq1_ref[...]
    gq2 = gq2_ref[...]
    gkn = gkn_ref[...]
    for hd in range(MLA_HEADS):
        slab = qf[:, hd * 2 * LANES:(hd + 1) * 2 * LANES]
        sm = slab * qmask
        rq = lax.rsqrt(jnp.sum(sm * sm, axis=-1, keepdims=True) / QK_DIM + NORM_EPS) * q_scale
        t = slab * gq1 * ct + pltpu.roll(slab * gq2 * st, QK_DIM, axis=1)
        q_ref[hd] = (t * rq)[:, :QK_DIM].astype(BF16)
        kn = kvf[:, hd * 2 * LANES:hd * 2 * LANES + NOPE_DIM]
        rk = lax.rsqrt((jnp.sum(kn * kn, axis=-1, keepdims=True) + ss_pe) / QK_DIM + NORM_EPS)
        kslab = jnp.concatenate([kn * gkn, krs], axis=-1) * rk
        k_ref[hd] = kslab[:, :QK_DIM].astype(BF16)
        v_ref[hd] = kvf[:, hd * 2 * LANES + NOPE_DIM:(hd + 1) * 2 * LANES].astype(BF16)


def _rope_lane_tables(length, use_rope):
    ones = jnp.ones((length, LANES), F32)
    if use_rope:
        n_freq = ROPE_DIM // 4
        t = jnp.arange(length, dtype=jnp.int32)
        row = (t // GRID_W).astype(F32)
        col = (t % GRID_W).astype(F32)
        inv_freq = jnp.power(ROPE_THETA, -jnp.arange(n_freq, dtype=F32) / n_freq)
        ar = row[:, None] * inv_freq
        ac = col[:, None] * inv_freq
        c64 = jnp.concatenate([jnp.cos(ar), jnp.cos(ar), jnp.cos(ac), jnp.cos(ac)], axis=-1)
        s64 = jnp.concatenate([-jnp.sin(ar), jnp.sin(ar), -jnp.sin(ac), jnp.sin(ac)], axis=-1)
    else:
        c64 = jnp.ones((length, ROPE_DIM), F32)
        s64 = jnp.zeros((length, ROPE_DIM), F32)
    z64 = jnp.zeros((length, ROPE_DIM), F32)
    ct = jnp.concatenate([ones, c64, z64], axis=-1)
    st = jnp.concatenate([jnp.zeros((length, LANES), F32), z64, s64], axis=-1)
    ctk = jnp.concatenate([c64, s64], axis=-1)
    return ct, st, ctk


_SWAP16 = np.concatenate([np.arange(16, 32), np.arange(0, 16), np.arange(48, 64), np.arange(32, 48)])


def _mla_weights(w_in, mla_w_uq, mla_w_ukv, qk_norm_q_g, qk_norm_k_g):
    hy_cols = 3 * D_HYENA
    w_hy = w_in[:, :hy_cols].astype(BF16)
    w_pe = w_in[:, hy_cols + Q_RANK + KV_RANK:]
    w_mla = jnp.concatenate([w_in[:, hy_cols:hy_cols + Q_RANK + KV_RANK], w_pe, w_pe[:, _SWAP16]],
                            axis=-1).astype(BF16)
    wq = mla_w_uq.reshape(Q_RANK, MLA_HEADS, QK_DIM)
    wq_rope = wq[:, :, NOPE_DIM:]
    w_uq2 = jnp.concatenate([wq[:, :, :NOPE_DIM], wq_rope, wq_rope[:, :, _SWAP16]], axis=-1)
    w_uq2 = w_uq2.reshape(Q_RANK, MLA_HEADS * 2 * LANES).astype(BF16)
    w_ukv2 = mla_w_ukv.astype(BF16)
    gq_r = qk_norm_q_g[NOPE_DIM:]
    z64 = jnp.zeros((ROPE_DIM,), F32)
    gq1 = jnp.concatenate([qk_norm_q_g[:NOPE_DIM], gq_r, z64]).reshape(1, -1)
    gq2 = jnp.concatenate([jnp.zeros((NOPE_DIM,), F32), z64, gq_r[_SWAP16]]).reshape(1, -1)
    gkn = qk_norm_k_g[:NOPE_DIM].reshape(1, -1)
    gk_r = qk_norm_k_g[NOPE_DIM:]
    gkr = jnp.concatenate([gk_r, gk_r[_SWAP16]]).reshape(1, -1)
    return w_hy, w_mla, w_uq2, w_ukv2, gq1, gq2, gkn, gkr


def _inproj(x, shift, scale, norm_g, weights, q_norm_g, kv_norm_g, use_rope, tl):
    b, length, d = x.shape
    w_hy, w_mla, w_uq2, w_ukv2, gq1, gq2, gkn, gkr = weights
    ct, st, ctk = _rope_lane_tables(length, use_rope)
    q_scale = QK_DIM ** -0.5 * math.log2(math.e)
    tok = lambda bi, i: (bi, i, 0)
    per_b = lambda bi, i: (bi, 0, 0)
    pos = lambda bi, i: (i, 0)
    hyc = w_hy.shape[1]
    return pl.pallas_call(
        functools.partial(_inproj_body, q_scale=q_scale),
        grid=(b, length // tl),
        in_specs=[pl.BlockSpec((None, tl, d), tok),
                  pl.BlockSpec((None, 1, d), per_b),
                  pl.BlockSpec((None, 1, d), per_b),
                  _const_spec((1, d)),
                  _const_spec(w_hy.shape), _const_spec(w_mla.shape),
                  _const_spec((1, Q_RANK)), _const_spec(w_uq2.shape),
                  _const_spec((1, KV_RANK)), _const_spec(w_ukv2.shape),
                  pl.BlockSpec((tl, 2 * LANES), pos), pl.BlockSpec((tl, 2 * LANES), pos),
                  pl.BlockSpec((tl, LANES), pos),
                  _const_spec((1, 2 * LANES)), _const_spec((1, 2 * LANES)),
                  _const_spec((1, LANES)), _const_spec((1, LANES))],
        out_specs=[pl.BlockSpec((None, tl, hyc), tok),
                   pl.BlockSpec((None, MLA_HEADS, tl, QK_DIM), lambda bi, i: (bi, 0, i, 0)),
                   pl.BlockSpec((None, MLA_HEADS, tl, QK_DIM), lambda bi, i: (bi, 0, i, 0)),
                   pl.BlockSpec((None, MLA_HEADS, tl, V_DIM), lambda bi, i: (bi, 0, i, 0))],
        out_shape=[jax.ShapeDtypeStruct((b, length, hyc), BF16),
                   jax.ShapeDtypeStruct((b, MLA_HEADS, length, QK_DIM), BF16),
                   jax.ShapeDtypeStruct((b, MLA_HEADS, length, QK_DIM), BF16),
                   jax.ShapeDtypeStruct((b, MLA_HEADS, length, V_DIM), BF16)],
        compiler_params=_cparams(("parallel", "parallel"), VMEM_LIMIT),
        name="inproj_mla",
    )(x, shift, scale, norm_g.reshape(1, d), w_hy, w_mla, q_norm_g.reshape(1, -1), w_uq2,
      kv_norm_g.reshape(1, -1), w_ukv2, ct, st, ctk, gq1, gq2, gkn, gkr)


def _filter_body(z_ref, w1_ref, b1_ref, w2_ref, b2_ref, w3_ref, b3_ref, w4_ref, fr_ref, dl_ref,
                 kern_ref, asum_ref, *, zero_row, tr):
    i = pl.program_id(0)
    fr = fr_ref[...]
    z = z_ref[...]
    dot = functools.partial(jnp.dot, precision=HIGHEST, preferred_element_type=F32)
    h = jnp.sin(fr * (dot(z, w1_ref[...]) + b1_ref[...]))
    h = jnp.sin(fr * (dot(h, w2_ref[...]) + b2_ref[...]))
    h = jnp.sin(fr * (dot(h, w3_ref[...]) + b3_ref[...]))
    o = dot(h, w4_ref[...]) * jnp.exp(-z[:, 0:1] * dl_ref[...])

    @pl.when(i == 0)
    def _():
        asum_ref[...] = jnp.zeros_like(asum_ref)

    asum_ref[...] += jnp.sum(jnp.abs(o), axis=0, keepdims=True)
    row = i * tr + lax.broadcasted_iota(jnp.int32, (tr, 1), 0)
    kern_ref[...] = jnp.where(row == zero_row, 0.0, o)


def _hyena_kernel_taps(length, w1, b1, w2, b2, w3, b3, w4, freq):
    n = 2 * length
    bands = (POS_EMB_DIM - 1) // 2
    pos = np.concatenate([np.arange(length), (n - np.arange(length, n)) % length])
    pos = jnp.asarray(pos, jnp.int32)
    t_tab = jnp.linspace(0.0, 1.0, length, dtype=F32)[:, None]
    w_ang = 2.0 * math.pi * jnp.arange(length, dtype=F32)[:, None] / length
    f = jnp.linspace(1e-4, bands - 1, bands, dtype=F32)[None, :]
    z_tab = jnp.concatenate([t_tab, jnp.cos(f * w_ang), -jnp.sin(f * w_ang)], axis=-1)
    z = jnp.pad(z_tab[pos], ((0, 0), (0, LANES - POS_EMB_DIM)))
    w1p = jnp.pad(w1, ((0, LANES - POS_EMB_DIM), (0, 0)))
    deltas = jnp.abs(jnp.linspace(MIN_DECAY, MAX_DECAY, D_HYENA, dtype=F32)).reshape(1, -1)
    tr = min(1024, length)
    half_steps = length // tr
    fo = FILTER_ORDER
    kern, asum = pl.pallas_call(
        functools.partial(_filter_body, zero_row=length, tr=tr),
        grid=(n // tr,),
        in_specs=[pl.BlockSpec((tr, LANES), lambda i: (i, 0)),
                  _const_spec((LANES, fo)), _const_spec((1, fo)),
                  _const_spec((fo, fo)), _const_spec((1, fo)),
                  _const_spec((fo, fo)), _const_spec((1, fo)),
                  pl.BlockSpec((fo, D_HYENA), lambda i: (0, i // half_steps)),
                  _const_spec((1, fo)), _const_spec((1, D_HYENA))],
        out_specs=[pl.BlockSpec((tr, D_HYENA), lambda i: (i, 0)),
                   _const_spec((1, D_HYENA))],
        out_shape=[jax.ShapeDtypeStruct((n, D_HYENA), F32),
                   jax.ShapeDtypeStruct((1, D_HYENA), F32)],
        compiler_params=_cparams(("arbitrary",), VMEM_LIMIT),
        name="hyena_filter",
    )(z, w1p, b1.reshape(1, fo), w2, b2.reshape(1, fo), w3, b3.reshape(1, fo), w4,
      freq.reshape(1, fo), deltas)
    return kern, asum


@functools.lru_cache(maxsize=None)
def _dft_tables(n, p_in):
    q_sz = DFT_Q
    p_sz = n // q_sz
    r = np.arange(p_sz, dtype=np.float64)
    qq = np.arange(q_sz, dtype=np.float64)
    pp = np.arange(p_in, dtype=np.float64)
    tt = q_sz * pp[None, None, :] + qq[:, None, None]
    ang = -2.0 * np.pi * r[None, :, None] * tt / n
    t1 = np.concatenate([np.cos(ang), np.sin(ang)], axis=1)
    t4 = np.transpose(t1, (0, 2, 1)) / n
    a2 = -2.0 * np.pi * np.outer(qq, qq) / q_sz
    fre, fim = np.cos(a2), np.sin(a2)
    m2 = np.block([[fre, -fim], [fim, fre]])
    m2c = np.block([[fre, fim], [-fim, fre]])
    return (t1.astype(np.float32), t4.astype(np.float32), m2.astype(np.float32),
            m2c.astype(np.float32))


def _dft_stage1(x_ref, t1_ref, spec_ref, p_sz, p_in):
    q_sz = DFT_Q

    def body(q, carry):
        xq = x_ref[pl.ds(q, p_in, stride=q_sz), :].astype(BF16)
        a = jnp.dot(t1_ref[q], xq, preferred_element_type=F32)
        spec_ref[pl.ds(q, p_sz, stride=2 * q_sz), :] = a[:p_sz]
        spec_ref[pl.ds(q_sz + q, p_sz, stride=2 * q_sz), :] = a[p_sz:]
        return carry

    lax.fori_loop(0, q_sz, body, 0)


def _spectrum_body(kern_ref, asum_ref, t1_ref, m2_ref, kf_ref, spec_ref, *, p_sz):
    q_sz = DFT_Q
    _dft_stage1(kern_ref, t1_ref, spec_ref, p_sz, p_sz)
    inv = 1.0 / asum_ref[...]
    m2 = m2_ref[...]

    def body(r, carry):
        blk = spec_ref[pl.ds(pl.multiple_of(r * 2 * q_sz, 2 * q_sz), 2 * q_sz), :]
        xf = jnp.dot(m2, blk.astype(BF16), preferred_element_type=F32)
        kf_ref[r] = (xf * inv).astype(BF16)
        return carry

    lax.fori_loop(0, p_sz, body, 0)


def _hyena_filter_spectrum(kern, asum):
    n, c = kern.shape
    p_sz = n // DFT_Q
    t1, _, m2, _ = _dft_tables(n, p_sz)
    t1 = jnp.asarray(t1, BF16)
    m2 = jnp.asarray(m2, BF16)
    ct = LANES
    return pl.pallas_call(
        functools.partial(_spectrum_body, p_sz=p_sz),
        grid=(c // ct,),
        in_specs=[_single_spec((n, ct), lambda j: (0, j)),
                  pl.BlockSpec((1, ct), lambda j: (0, j)),
                  _single_spec(t1.shape, lambda j: (0, 0, 0)),
                  _const_spec(m2.shape)],
        out_specs=pl.BlockSpec((p_sz, 2 * DFT_Q, ct), lambda j: (0, 0, j)),
        out_shape=jax.ShapeDtypeStruct((p_sz, 2 * DFT_Q, c), BF16),
        scratch_shapes=[pltpu.VMEM((p_sz * 2 * DFT_Q, ct), F32)],
        compiler_params=_cparams(("parallel",), VMEM_LIMIT),
        name="hyena_filter_spectrum",
    )(kern, asum, t1, m2)


def _short_conv_chunk(u_ref, w_ref, b_ref, i, rows, length):
    pack = 16
    base = pl.multiple_of(i * rows, rows)
    u = u_ref[pl.ds(base, rows), :].astype(F32)
    lo = pl.multiple_of(jnp.maximum(base - pack, 0), pack)
    hi = pl.multiple_of(jnp.minimum(base + rows, length - pack), pack)
    prev = u_ref[pl.ds(lo, pack), :].astype(F32)[pack - 1:pack]
    nxt = u_ref[pl.ds(hi, pack), :].astype(F32)[0:1]
    prev = jnp.where(base == 0, 0.0, prev)
    nxt = jnp.where(base + rows == length, 0.0, nxt)
    ridx = lax.broadcasted_iota(jnp.int32, (rows, 1), 0)
    up = jnp.where(ridx == 0, prev, pltpu.roll(u, 1, axis=0))
    dn = jnp.where(ridx == rows - 1, nxt, pltpu.roll(u, rows - 1, axis=0))
    w = w_ref[...]
    return up * w[0:1] + u * w[1:2] + dn * w[2:3] + b_ref[...]


def _hyena_conv_body(x0_ref, x1_ref, v_ref, w0_ref, w1_ref, wv_ref, b0_ref, b1_ref, bv_ref,
                     bias_ref, kf_ref, t1_ref, t4_ref, m2_ref, m2c_ref, o_ref, vx_ref, spec_ref,
                     *, length, rows):
    q_sz = DFT_Q
    p_sz = 2 * length // q_sz
    p_in = length // q_sz
    n_chunks = length // rows

    def gate_in(i, carry):
        x1c = _short_conv_chunk(x1_ref, w1_ref, b1_ref, i, rows, length)
        vc = _short_conv_chunk(v_ref, wv_ref, bv_ref, i, rows, length)
        vx_ref[pl.ds(pl.multiple_of(i * rows, rows), rows), :] = vc * x1c
        return carry

    lax.fori_loop(0, n_chunks, gate_in, 0)
    _dft_stage1(vx_ref, t1_ref, spec_ref, p_sz, p_in)
    m2 = m2_ref[...]
    m2c = m2c_ref[...]

    def mid(r, carry):
        sl = pl.ds(pl.multiple_of(r * 2 * q_sz, 2 * q_sz), 2 * q_sz)
        xf = jnp.dot(m2, spec_ref[sl, :].astype(BF16), preferred_element_type=F32)
        kf = kf_ref[r].astype(F32)
        xre, xim = xf[:q_sz], xf[q_sz:]
        kre, kim = kf[:q_sz], kf[q_sz:]
        z = jnp.concatenate([xre * kre - xim * kim, xre * kim + xim * kre], axis=0)
        spec_ref[sl, :] = jnp.dot(m2c, z.astype(BF16), preferred_element_type=F32)
        return carry

    lax.fori_loop(0, p_sz, mid, 0)
    bias = bias_ref[...]

    def last(q, carry):
        bre = spec_ref[pl.ds(q, p_sz, stride=2 * q_sz), :]
        bim = spec_ref[pl.ds(q_sz + q, p_sz, stride=2 * q_sz), :]
        bq = jnp.concatenate([bre, bim], axis=0).astype(BF16)
        y = jnp.dot(t4_ref[q], bq, preferred_element_type=F32)
        sl = pl.ds(q, p_in, stride=q_sz)
        vx_ref[sl, :] = y + bias * vx_ref[sl, :]
        return carry

    lax.fori_loop(0, q_sz, last, 0)

    def gate_out(i, carry):
        x0c = _short_conv_chunk(x0_ref, w0_ref, b0_ref, i, rows, length)
        sl = pl.ds(pl.multiple_of(i * rows, rows), rows)
        o_ref[sl, :] = (vx_ref[sl, :] * x0c).astype(BF16)
        return carry

    lax.fori_loop(0, n_chunks, gate_out, 0)


def _hyena_conv(hy, conv_w, conv_b, bias, kf):
    b, length, _ = hy.shape
    ch = D_HYENA
    ct = LANES
    nct = ch // ct
    n = 2 * length
    p_sz = n // DFT_Q
    p_in = length // DFT_Q
    t1, t4, m2, m2c = _dft_tables(n, p_in)
    t1, t4, m2, m2c = (jnp.asarray(a, BF16) for a in (t1, t4, m2, m2c))
    rows = min(512, length)
    col = lambda off: (lambda j, bi: (bi, 0, off * nct + j))
    wcol = lambda off: (lambda j, bi: (0, off * nct + j))
    conv_b2 = conv_b.reshape(1, -1)
    return pl.pallas_call(
        functools.partial(_hyena_conv_body, length=length, rows=rows),
        grid=(nct, b),
        in_specs=[_single_spec((None, length, ct), col(0)),
                  _single_spec((None, length, ct), col(1)),
                  _single_spec((None, length, ct), col(2)),
                  pl.BlockSpec((3, ct), wcol(0)), pl.BlockSpec((3, ct), wcol(1)),
                  pl.BlockSpec((3, ct), wcol(2)),
                  pl.BlockSpec((1, ct), wcol(0)), pl.BlockSpec((1, ct), wcol(1)),
                  pl.BlockSpec((1, ct), wcol(2)),
                  pl.BlockSpec((1, ct), lambda j, bi: (0, j)),
                  _single_spec((p_sz, 2 * DFT_Q, ct), lambda j, bi: (0, 0, j)),
                  _single_spec(t1.shape, lambda j, bi: (0, 0, 0)),
                  _single_spec(t4.shape, lambda j, bi: (0, 0, 0)),
                  _const_spec(m2.shape), _const_spec(m2c.shape)],
        out_specs=pl.BlockSpec((None, length, ct), lambda j, bi: (bi, 0, j)),
        out_shape=jax.ShapeDtypeStruct((b, length, ch), BF16),
        scratch_shapes=[pltpu.VMEM((length, ct), F32),
                        pltpu.VMEM((p_sz * 2 * DFT_Q, ct), F32)],
        compiler_params=_cparams(("parallel", "parallel"), VMEM_LIMIT),
        name="hyena_conv",
    )(hy, hy, hy, conv_w, conv_w, conv_w, conv_b2, conv_b2, conv_b2, bias.reshape(1, ch), kf,
      t1, t4, m2, m2c)


def _flash_body(q_ref, k_ref, v_ref, kc_ref, vc_ref, o_ref, m_sc, l_sc, acc_sc):
    ki = pl.program_id(3)
    nk = pl.num_programs(3)

    @pl.when(ki == 0)
    def _():
        m_sc[...] = jnp.full_like(m_sc, -jnp.inf)
        l_sc[...] = jnp.zeros_like(l_sc)
        acc_sc[...] = jnp.zeros_like(acc_sc)

    def update(k, v):
        s = lax.dot_general(q_ref[...], k, (((1,), (1,)), ((), ())),
                            preferred_element_type=F32)
        m_prev = m_sc[...]
        m_new = jnp.maximum(m_prev, jnp.max(s, axis=-1, keepdims=True))
        alpha = jnp.exp2(m_prev - m_new)
        p = jnp.exp2(s - m_new)
        l_sc[...] = alpha * l_sc[...] + jnp.sum(p, axis=-1, keepdims=True)
        acc_sc[...] = alpha * acc_sc[...] + jnp.dot(p.astype(BF16), v,
                                                    preferred_element_type=F32)
        m_sc[...] = m_new

    update(k_ref[...], v_ref[...])

    @pl.when(ki == nk - 1)
    def _():
        update(kc_ref[...], vc_ref[...])
        o_ref[...] = (acc_sc[...] / l_sc[...]).astype(o_ref.dtype)


def _flash_attention(q, k, v, kc, vc, tq, tk):
    b, hds, length, _ = q.shape
    nc = kc.shape[2]
    return pl.pallas_call(
        _flash_body,
        grid=(b, hds, length // tq, length // tk),
        in_specs=[pl.BlockSpec((None, None, tq, QK_DIM), lambda bi, h, qi, ki: (bi, h, qi, 0)),
                  pl.BlockSpec((None, None, tk, QK_DIM), lambda bi, h, qi, ki: (bi, h, ki, 0)),
                  pl.BlockSpec((None, None, tk, V_DIM), lambda bi, h, qi, ki: (bi, h, ki, 0)),
                  pl.BlockSpec((None, None, nc, QK_DIM), lambda bi, h, qi, ki: (bi, h, 0, 0)),
                  pl.BlockSpec((None, None, nc, V_DIM), lambda bi, h, qi, ki: (bi, h, 0, 0))],
        out_specs=pl.BlockSpec((None, tq, V_DIM), lambda bi, h, qi, ki: (bi, qi, h)),
        out_shape=jax.ShapeDtypeStruct((b, length, hds * V_DIM), BF16),
        scratch_shapes=[pltpu.VMEM((tq, 1), F32), pltpu.VMEM((tq, 1), F32),
                        pltpu.VMEM((tq, V_DIM), F32)],
        compiler_params=_cparams(("parallel", "parallel", "parallel", "arbitrary"), VMEM_LIMIT),
        name="mla_flash_attention",
    )(q, k, v, kc, vc)


def _outproj_body(yh_ref, ya_ref, x_ref, g1_ref, sh2_ref, sc2_ref, n2g_ref, wo1_ref, wo2_ref,
                  rwt_ref, rb_ref, tri_ref, xn_ref, h2_ref, idx_ref, gate_ref, rank_ref, cnt_ref,
                  carry_sc):
    i = pl.program_id(0)

    @pl.when(i == 0)
    def _():
        carry_sc[...] = jnp.zeros_like(carry_sc)

    mix = (jnp.dot(yh_ref[...], wo1_ref[...], preferred_element_type=F32)
           + jnp.dot(ya_ref[...], wo2_ref[...], preferred_element_type=F32))
    xn = x_ref[...] + g1_ref[...] * mix
    xn_ref[...] = xn
    h2 = _rms(xn) * n2g_ref[...]
    h2 = h2 * (1.0 + sc2_ref[...]) + sh2_ref[...]
    h2_ref[...] = h2.astype(BF16)
    logits = lax.dot_general(rwt_ref[...], h2, (((1,), (1,)), ((), ())), precision=HIGHEST,
                             preferred_element_type=F32) + rb_ref[...]
    n_e, tt = logits.shape
    eidx = lax.broadcasted_iota(jnp.int32, (n_e, tt), 0).astype(F32)
    work = logits
    vals, sels, idxs = [], [], []
    for _ in range(TOP_K):
        m = jnp.max(work, axis=0, keepdims=True)
        ix = jnp.min(jnp.where(work == m, eidx, float(n_e)), axis=0, keepdims=True)
        sel = eidx == ix
        work = jnp.where(sel, -jnp.inf, work)
        vals.append(m)
        idxs.append(ix)
        sels.append(sel)
    es = [jnp.exp(vk - vals[0]) for vk in vals]
    den = es[0] + es[1] + es[2] + es[3]
    gate_ref[...] = jnp.concatenate(es, axis=0) / den
    idx_ref[...] = jnp.concatenate(idxs, axis=0).astype(jnp.int32)
    onehot = jnp.zeros((n_e, tt), F32)
    for sel in sels:
        onehot = onehot + sel.astype(F32)
    prefix = jnp.dot(onehot.astype(BF16), tri_ref[...], preferred_element_type=F32) + carry_sc[...]
    ranks = [jnp.sum(jnp.where(sel, prefix, 0.0), axis=0, keepdims=True) for sel in sels]
    rank_ref[...] = jnp.concatenate(ranks, axis=0).astype(jnp.int32)
    carry_sc[...] += jnp.sum(onehot, axis=1, keepdims=True)
    cnt_ref[...] = jnp.broadcast_to(carry_sc[...], cnt_ref.shape)


def _outproj_router(y_hy, y_att, x2, g1, sh2, sc2, norm2_g, w_out, router_w, router_b, tokens_per_batch,
                    tt):
    t, d = x2.shape
    ch = y_hy.shape[1]
    n_e = router_w.shape[1]
    wo1 = w_out[:ch].astype(BF16)
    wo2 = w_out[ch:].astype(BF16)
    tri = jnp.asarray(np.triu(np.ones((tt, tt), np.float32), k=1), BF16)
    steps_per_batch = tokens_per_batch // tt
    tok = lambda i: (i, 0)
    per_b = lambda i: (i // steps_per_batch, 0, 0)
    lanes_tok = lambda i: (0, i)
    return pl.pallas_call(
        _outproj_body,
        grid=(t // tt,),
        in_specs=[pl.BlockSpec((tt, ch), tok), pl.BlockSpec((tt, ch), tok),
                  pl.BlockSpec((tt, d), tok),
                  pl.BlockSpec((None, 1, d), per_b), pl.BlockSpec((None, 1, d), per_b),
                  pl.BlockSpec((None, 1, d), per_b),
                  _const_spec((1, d)), _const_spec(wo1.shape), _const_spec(wo2.shape),
                  _const_spec((n_e, d)), _const_spec((n_e, 1)), _const_spec((tt, tt))],
        out_specs=[pl.BlockSpec((tt, d), tok), pl.BlockSpec((tt, d), tok),
                   pl.BlockSpec((TOP_K, tt), lanes_tok), pl.BlockSpec((TOP_K, tt), lanes_tok),
                   pl.BlockSpec((TOP_K, tt), lanes_tok), _const_spec((n_e, LANES))],
        out_shape=[jax.ShapeDtypeStruct((t, d), F32), jax.ShapeDtypeStruct((t, d), BF16),
                   jax.ShapeDtypeStruct((TOP_K, t), jnp.int32),
                   jax.ShapeDtypeStruct((TOP_K, t), F32),
                   jax.ShapeDtypeStruct((TOP_K, t), jnp.int32),
                   jax.ShapeDtypeStruct((n_e, LANES), F32)],
        scratch_shapes=[pltpu.VMEM((n_e, 1), F32)],
        compiler_params=_cparams(("arbitrary",), VMEM_LIMIT),
        name="outproj_router",
    )(y_hy, y_att, x2, g1, sh2, sc2, norm2_g.reshape(1, d), wo1, wo2, router_w.T,
      router_b.reshape(n_e, 1), tri)


def _expert_body(be_ref, nused_ref, xs_ref, wgu_ref, bgu_ref, wd_ref, bd_ref, ys_ref):
    i = pl.program_id(0)

    @pl.when(i < nused_ref[0])
    def _():
        gu = jnp.dot(xs_ref[...], wgu_ref[...], preferred_element_type=F32) + bgu_ref[...]
        dff = gu.shape[1] // 2
        gate = jnp.minimum(gu[:, :dff], SWIGLU_LIMIT)
        up = jnp.clip(gu[:, dff:], -SWIGLU_LIMIT, SWIGLU_LIMIT)
        act = (up + 1.0) * (gate * jax.nn.sigmoid(SWIGLU_ALPHA * gate))
        ys = jnp.dot(act.astype(BF16), wd_ref[...], preferred_element_type=F32) + bd_ref[...]
        ys_ref[...] = ys.astype(ys_ref.dtype)

    @pl.when(i >= nused_ref[0])
    def _():
        ys_ref[...] = jnp.zeros_like(ys_ref)


def _expert_blocks(xs, block_e, n_used, w_gu, b_gu, w_down, b_down):
    n_rows, d = xs.shape
    n_e, _, dff2 = w_gu.shape
    bm = MOE_ROWS
    grid_spec = pltpu.PrefetchScalarGridSpec(
        num_scalar_prefetch=2,
        grid=(n_rows // bm,),
        in_specs=[pl.BlockSpec((bm, d), lambda i, be, nu: (i, 0)),
                  pl.BlockSpec((None, d, dff2), lambda i, be, nu: (be[i], 0, 0)),
                  pl.BlockSpec((None, 1, dff2), lambda i, be, nu: (be[i], 0, 0)),
                  pl.BlockSpec((None, dff2 // 2, d), lambda i, be, nu: (be[i], 0, 0)),
                  pl.BlockSpec((None, 1, d), lambda i, be, nu: (be[i], 0, 0))],
        out_specs=pl.BlockSpec((bm, d), lambda i, be, nu: (i, 0)),
    )
    return pl.pallas_call(
        _expert_body,
        grid_spec=grid_spec,
        out_shape=jax.ShapeDtypeStruct((n_rows, d), BF16),
        compiler_params=_cparams(("arbitrary",), VMEM_LIMIT),
        name="moe_experts",
    )(block_e, n_used, xs, w_gu.astype(BF16), b_gu.reshape(n_e, 1, dff2), w_down.astype(BF16),
      b_down.reshape(n_e, 1, d))


def _moe(h2, xn, g2_rows, idx, gates, ranks, counts, w_gu, b_gu, w_down, b_down):
    t, d = h2.shape
    bm = MOE_ROWS
    n_e = w_gu.shape[0]
    cnt = counts[:, 0].astype(jnp.int32)
    padded = (cnt + bm - 1) // bm * bm
    padded_ends = jnp.cumsum(padded)
    padded_starts = padded_ends - padded
    dest = padded_starts[idx] + ranks
    n_blocks = t * TOP_K // bm + n_e
    n_rows = n_blocks * bm
    block_start = jnp.arange(n_blocks, dtype=jnp.int32) * bm
    block_e = jnp.minimum(jnp.searchsorted(padded_ends, block_start, side='right'),
                          n_e - 1).astype(jnp.int32)
    n_used = (padded_ends[-1] // bm).astype(jnp.int32).reshape(1)
    tok = jnp.broadcast_to(jnp.arange(t, dtype=jnp.int32)[None, :], (TOP_K, t))
    row_tok = jnp.full((n_rows,), t, jnp.int32).at[dest.reshape(-1)].set(tok.reshape(-1))
    h_pad = jnp.concatenate([h2, jnp.zeros((1, d), h2.dtype)], axis=0)
    xs = h_pad[row_tok]
    ys = _expert_blocks(xs, block_e, n_used, w_gu, b_gu, w_down, b_down)
    picked = ys[dest].astype(F32)
    moe = jnp.sum(picked * gates[:, :, None], axis=0)
    return xn + g2_rows * moe


def kernel(x, c, ctx, c_ctx, mod_w, mod_b, norm1_g, w_in, hy_conv_w, hy_conv_b, hy_f_w1, hy_f_b1,
           hy_f_w2, hy_f_b2, hy_f_w3, hy_f_b3, hy_f_w4, hy_f_freq, hy_bias, mla_q_norm_g, mla_w_uq,
           mla_kv_norm_g, mla_w_ukv, qk_norm_q_g, qk_norm_k_g, w_out, norm2_g, router_w, router_b,
           exp_w_gu, exp_b_gu, exp_w_down, exp_b_down):
    b, length, d = x.shape
    depth = mod_w.shape[0]
    assert depth == 1, "single-layer kernel"
    ly = 0
    c_rows = jnp.concatenate([c, c_ctx[None, :], jnp.zeros((8 - b - 1, d), F32)], axis=0)
    mod = _adaln_table(c_rows, mod_w[ly], mod_b[ly])
    mod6 = mod.reshape(8, 6, d)
    sh1, sc1, g1, sh2, sc2, g2 = (mod6[:b, j][:, None, :] for j in range(6))
    csh1 = mod6[b:b + 1, 0][:, None, :]
    csc1 = mod6[b:b + 1, 1][:, None, :]

    weights = _mla_weights(w_in[ly], mla_w_uq[ly], mla_w_ukv[ly], qk_norm_q_g[ly], qk_norm_k_g[ly])
    n_ctx = ctx.shape[1]
    _, _, k_c, v_c = _inproj(ctx, jnp.broadcast_to(csh1, (b, 1, d)), jnp.broadcast_to(csc1, (b, 1, d)),
                             norm1_g[ly], weights, mla_q_norm_g[ly], mla_kv_norm_g[ly], False, n_ctx)
    hy, q, k, v = _inproj(x, sh1, sc1, norm1_g[ly], weights, mla_q_norm_g[ly], mla_kv_norm_g[ly],
                          True, min(512, length))

    kern, asum = _hyena_kernel_taps(length, hy_f_w1[ly], hy_f_b1[ly], hy_f_w2[ly], hy_f_b2[ly],
                                    hy_f_w3[ly], hy_f_b3[ly], hy_f_w4[ly], hy_f_freq[ly])
    kf = _hyena_filter_spectrum(kern, asum)
    y_hy = _hyena_conv(hy, hy_conv_w[ly], hy_conv_b[ly], hy_bias[ly], kf)

    y_att = _flash_attention(q, k, v, k_c, v_c, min(512, length), min(512, length))

    t = b * length
    xn, h2, idx, gates, ranks, counts = _outproj_router(
        y_hy.reshape(t, -1), y_att.reshape(t, -1), x.reshape(t, d), g1, sh2, sc2, norm2_g[ly],
        w_out[ly], router_w[ly], router_b[ly], length, min(512, length))
    g2_rows = jnp.broadcast_to(g2, (b, length, d)).reshape(t, d)
    out = _moe(h2, xn, g2_rows, idx, gates, ranks, counts, exp_w_gu[ly], exp_b_gu[ly],
               exp_w_down[ly], exp_b_down[ly])
    return out.reshape(b, length, d)
```

```python
import functools
import math

import jax
import jax.numpy as jnp
import numpy as np
from jax import lax
from jax.experimental import pallas as pl
from jax.experimental.pallas import tpu as pltpu

F32 = jnp.float32
BF16 = jnp.bfloat16
HIGHEST = lax.Precision.HIGHEST

GRID_W = 64
D_HYENA = 512
FILTER_ORDER = 64
POS_EMB_DIM = 33
MIN_DECAY = math.log(1e-2) / 0.3
MAX_DECAY = math.log(1e-2) / 1.5
NOPE_DIM = 128
ROPE_DIM = 64
QK_DIM = NOPE_DIM + ROPE_DIM
V_DIM = 128
MLA_HEADS = 4
Q_RANK = 256
KV_RANK = 128
ROPE_THETA = 10000.0
N_EXPERTS = 32
TOP_K = 4
SWIGLU_ALPHA = 1.702
SWIGLU_LIMIT = 7.0
NORM_EPS = 1e-6

LANES = 128
VMEM_LIMIT = 56 * 1024 * 1024

DFT_Q = LANES
MOE_ROWS = 512


def _cparams(sem, vmem=None):
    return pltpu.CompilerParams(dimension_semantics=sem, vmem_limit_bytes=vmem)


def _const_spec(shape):
    nd = len(shape)
    return pl.BlockSpec(shape, lambda *_: (0,) * nd)


def _single_spec(shape, index_map):
    return pl.BlockSpec(shape, index_map, pipeline_mode=pl.Buffered(1))


def _mod_body(c_ref, w_ref, b_ref, o_ref):
    cc = c_ref[...]
    s = cc * jax.nn.sigmoid(cc)
    o_ref[...] = jnp.dot(s, w_ref[...], precision=HIGHEST,
                         preferred_element_type=F32) + b_ref[...]


def _adaln_table(c_rows, mod_w, mod_b):
    rows, d = c_rows.shape
    n = mod_w.shape[1]
    tn = n // 8
    return pl.pallas_call(
        _mod_body,
        grid=(n // tn,),
        in_specs=[_const_spec((rows, d)),
                  pl.BlockSpec((d, tn), lambda j: (0, j)),
                  pl.BlockSpec((1, tn), lambda j: (0, j))],
        out_specs=pl.BlockSpec((rows, tn), lambda j: (0, j)),
        out_shape=jax.ShapeDtypeStruct((rows, n), F32),
        compiler_params=_cparams(("arbitrary",)),
        name="adaln_table",
    )(c_rows, mod_w, mod_b.reshape(1, n))


def _rms(x, eps=NORM_EPS):
    return x * lax.rsqrt(jnp.mean(x * x, axis=-1, keepdims=True) + eps)


def _inproj_body(x_ref, sh_ref, sc_ref, g_ref, why_ref, wmla_ref, qng_ref, wuq_ref,
                 kvng_ref, wukv_ref, ct_ref, st_ref, ctk_ref, gq1_ref, gq2_ref, gkn_ref,
                 gkr_ref, hy_ref, q_ref, k_ref, v_ref, *, q_scale):
    h = _rms(x_ref[...]) * g_ref[...]
    h = h * (1.0 + sc_ref[...]) + sh_ref[...]
    hb = h.astype(BF16)
    hy_ref[...] = jnp.dot(hb, why_ref[...], preferred_element_type=F32).astype(BF16)
    mla = jnp.dot(hb, wmla_ref[...], preferred_element_type=F32)
    cq = mla[:, :Q_RANK]
    ckv = mla[:, Q_RANK:Q_RANK + KV_RANK]
    pe2 = mla[:, Q_RANK + KV_RANK:]
    qf = jnp.dot((_rms(cq) * qng_ref[...]).astype(BF16), wuq_ref[...],
                 preferred_element_type=F32)
    kvf = jnp.dot((_rms(ckv) * kvng_ref[...]).astype(BF16), wukv_ref[...],
                  preferred_element_type=F32)
    ct = ct_ref[...]
    st = st_ref[...]
    lane256 = lax.broadcasted_iota(jnp.int32, (1, 2 * LANES), 1)
    qmask = (lane256 < QK_DIM).astype(F32)
    lane128 = lax.broadcasted_iota(jnp.int32, (1, LANES), 1)
    pemask = (lane128 < ROPE_DIM).astype(F32)
    kr0 = pe2 * gkr_ref[...] * ctk_ref[...]
    krs = kr0 + pltpu.roll(kr0, ROPE_DIM, axis=1)
    pem = pe2 * pemask
    ss_pe = jnp.sum(pem * pem, axis=-1, keepdims=True)
    gq1 = gq1_ref[...]
    gq2 = gq2_ref[...]
    gkn = gkn_ref[...]
    for hd in range(MLA_HEADS):
        slab = qf[:, hd * 2 * LANES:(hd + 1) * 2 * LANES]
        sm = slab * qmask
        rq = lax.rsqrt(jnp.sum(sm * sm, axis=-1, keepdims=True) / QK_DIM + NORM_EPS) * q_scale
        t = slab * gq1 * ct + pltpu.roll(slab * gq2 * st, QK_DIM, axis=1)
        q_ref[hd] = (t * rq)[:, :QK_DIM].astype(BF16)
        kn = kvf[:, hd * 2 * LANES:hd * 2 * LANES + NOPE_DIM]
        rk = lax.rsqrt((jnp.sum(kn * kn, axis=-1, keepdims=True) + ss_pe) / QK_DIM + NORM_EPS)
        kslab = jnp.concatenate([kn * gkn, krs], axis=-1) * rk
        k_ref[hd] = kslab[:, :QK_DIM].astype(BF16)
        v_ref[hd] = kvf[:, hd * 2 * LANES + NOPE_DIM:(hd + 1) * 2 * LANES].astype(BF16)


def _rope_lane_tables(length, use_rope):
    ones = jnp.ones((length, LANES), F32)
    if use_rope:
        n_freq = ROPE_DIM // 4
        t = jnp.arange(length, dtype=jnp.int32)
        row = (t // GRID_W).astype(F32)
        col = (t % GRID_W).astype(F32)
        inv_freq = jnp.power(ROPE_THETA, -jnp.arange(n_freq, dtype=F32) / n_freq)
        ar = row[:, None] * inv_freq
        ac = col[:, None] * inv_freq
        c64 = jnp.concatenate([jnp.cos(ar), jnp.cos(ar), jnp.cos(ac), jnp.cos(ac)], axis=-1)
        s64 = jnp.concatenate([-jnp.sin(ar), jnp.sin(ar), -jnp.sin(ac), jnp.sin(ac)], axis=-1)
    else:
        c64 = jnp.ones((length, ROPE_DIM), F32)
        s64 = jnp.zeros((length, ROPE_DIM), F32)
    z64 = jnp.zeros((length, ROPE_DIM), F32)
    ct = jnp.concatenate([ones, c64, z64], axis=-1)
    st = jnp.concatenate([jnp.zeros((length, LANES), F32), z64, s64], axis=-1)
    ctk = jnp.concatenate([c64, s64], axis=-1)
    return ct, st, ctk


_SWAP16 = np.concatenate([np.arange(16, 32), np.arange(0, 16), np.arange(48, 64), np.arange(32, 48)])


def _mla_weights(w_in, mla_w_uq, mla_w_ukv, qk_norm_q_g, qk_norm_k_g):
    hy_cols = 3 * D_HYENA
    w_hy = w_in[:, :hy_cols].astype(BF16)
    w_pe = w_in[:, hy_cols + Q_RANK + KV_RANK:]
    w_mla = jnp.concatenate([w_in[:, hy_cols:hy_cols + Q_RANK + KV_RANK], w_pe, w_pe[:, _SWAP16]],
                            axis=-1).astype(BF16)
    wq = mla_w_uq.reshape(Q_RANK, MLA_HEADS, QK_DIM)
    wq_rope = wq[:, :, NOPE_DIM:]
    w_uq2 = jnp.concatenate([wq[:, :, :NOPE_DIM], wq_rope, wq_rope[:, :, _SWAP16]], axis=-1)
    w_uq2 = w_uq2.reshape(Q_RANK, MLA_HEADS * 2 * LANES).astype(BF16)
    w_ukv2 = mla_w_ukv.astype(BF16)
    gq_r = qk_norm_q_g[NOPE_DIM:]
    z64 = jnp.zeros((ROPE_DIM,), F32)
    gq1 = jnp.concatenate([qk_norm_q_g[:NOPE_DIM], gq_r, z64]).reshape(1, -1)
    gq2 = jnp.concatenate([jnp.zeros((NOPE_DIM,), F32), z64, gq_r[_SWAP16]]).reshape(1, -1)
    gkn = qk_norm_k_g[:NOPE_DIM].reshape(1, -1)
    gk_r = qk_norm_k_g[NOPE_DIM:]
    gkr = jnp.concatenate([gk_r, gk_r[_SWAP16]]).reshape(1, -1)
    return w_hy, w_mla, w_uq2, w_ukv2, gq1, gq2, gkn, gkr


def _inproj(x, shift, scale, norm_g, weights, q_norm_g, kv_norm_g, use_rope, tl):
    b, length, d = x.shape
    w_hy, w_mla, w_uq2, w_ukv2, gq1, gq2, gkn, gkr = weights
    ct, st, ctk = _rope_lane_tables(length, use_rope)
    q_scale = QK_DIM ** -0.5 * math.log2(math.e)
    tok = lambda bi, i: (bi, i, 0)
    per_b = lambda bi, i: (bi, 0, 0)
    pos = lambda bi, i: (i, 0)
    hyc = w_hy.shape[1]
    return pl.pallas_call(
        functools.partial(_inproj_body, q_scale=q_scale),
        grid=(b, length // tl),
        in_specs=[pl.BlockSpec((None, tl, d), tok),
                  pl.BlockSpec((None, 1, d), per_b),
                  pl.BlockSpec((None, 1, d), per_b),
                  _const_spec((1, d)),
                  _const_spec(w_hy.shape), _const_spec(w_mla.shape),
                  _const_spec((1, Q_RANK)), _const_spec(w_uq2.shape),
                  _const_spec((1, KV_RANK)), _const_spec(w_ukv2.shape),
                  pl.BlockSpec((tl, 2 * LANES), pos), pl.BlockSpec((tl, 2 * LANES), pos),
                  pl.BlockSpec((tl, LANES), pos),
                  _const_spec((1, 2 * LANES)), _const_spec((1, 2 * LANES)),
                  _const_spec((1, LANES)), _const_spec((1, LANES))],
        out_specs=[pl.BlockSpec((None, tl, hyc), tok),
                   pl.BlockSpec((None, MLA_HEADS, tl, QK_DIM), lambda bi, i: (bi, 0, i, 0)),
                   pl.BlockSpec((None, MLA_HEADS, tl, QK_DIM), lambda bi, i: (bi, 0, i, 0)),
                   pl.BlockSpec((None, MLA_HEADS, tl, V_DIM), lambda bi, i: (bi, 0, i, 0))],
        out_shape=[jax.ShapeDtypeStruct((b, length, hyc), BF16),
                   jax.ShapeDtypeStruct((b, MLA_HEADS, length, QK_DIM), BF16),
                   jax.ShapeDtypeStruct((b, MLA_HEADS, length, QK_DIM), BF16),
                   jax.ShapeDtypeStruct((b, MLA_HEADS, length, V_DIM), BF16)],
        compiler_params=_cparams(("parallel", "parallel"), VMEM_LIMIT),
        name="inproj_mla",
    )(x, shift, scale, norm_g.reshape(1, d), w_hy, w_mla, q_norm_g.reshape(1, -1), w_uq2,
      kv_norm_g.reshape(1, -1), w_ukv2, ct, st, ctk, gq1, gq2, gkn, gkr)


def _filter_body(z_ref, w1_ref, b1_ref, w2_ref, b2_ref, w3_ref, b3_ref, w4_ref, fr_ref, dl_ref,
                 kern_ref, asum_ref, *, zero_row, tr):
    i = pl.program_id(0)
    fr = fr_ref[...]
    z = z_ref[...]
    dot = functools.partial(jnp.dot, precision=HIGHEST, preferred_element_type=F32)
    h = jnp.sin(fr * (dot(z, w1_ref[...]) + b1_ref[...]))
    h = jnp.sin(fr * (dot(h, w2_ref[...]) + b2_ref[...]))
    h = jnp.sin(fr * (dot(h, w3_ref[...]) + b3_ref[...]))
    o = dot(h, w4_ref[...]) * jnp.exp(-z[:, 0:1] * dl_ref[...])

    @pl.when(i == 0)
    def _():
        asum_ref[...] = jnp.zeros_like(asum_ref)

    asum_ref[...] += jnp.sum(jnp.abs(o), axis=0, keepdims=True)
    row = i * tr + lax.broadcasted_iota(jnp.int32, (tr, 1), 0)
    kern_ref[...] = jnp.where(row == zero_row, 0.0, o)


def _hyena_kernel_taps(length, w1, b1, w2, b2, w3, b3, w4, freq):
    n = 2 * length
    bands = (POS_EMB_DIM - 1) // 2
    pos = np.concatenate([np.arange(length), (n - np.arange(length, n)) % length])
    pos = jnp.asarray(pos, jnp.int32)
    t_tab = jnp.linspace(0.0, 1.0, length, dtype=F32)[:, None]
    w_ang = 2.0 * math.pi * jnp.arange(length, dtype=F32)[:, None] / length
    f = jnp.linspace(1e-4, bands - 1, bands, dtype=F32)[None, :]
    z_tab = jnp.concatenate([t_tab, jnp.cos(f * w_ang), -jnp.sin(f * w_ang)], axis=-1)
    z = jnp.pad(z_tab[pos], ((0, 0), (0, LANES - POS_EMB_DIM)))
    w1p = jnp.pad(w1, ((0, LANES - POS_EMB_DIM), (0, 0)))
    deltas = jnp.abs(jnp.linspace(MIN_DECAY, MAX_DECAY, D_HYENA, dtype=F32)).reshape(1, -1)
    tr = min(1024, length)
    half_steps = length // tr
    fo = FILTER_ORDER
    kern, asum = pl.pallas_call(
        functools.partial(_filter_body, zero_row=length, tr=tr),
        grid=(n // tr,),
        in_specs=[pl.BlockSpec((tr, LANES), lambda i: (i, 0)),
                  _const_spec((LANES, fo)), _const_spec((1, fo)),
                  _const_spec((fo, fo)), _const_spec((1, fo)),
                  _const_spec((fo, fo)), _const_spec((1, fo)),
                  pl.BlockSpec((fo, D_HYENA), lambda i: (0, i // half_steps)),
                  _const_spec((1, fo)), _const_spec((1, D_HYENA))],
        out_specs=[pl.BlockSpec((tr, D_HYENA), lambda i: (i, 0)),
                   _const_spec((1, D_HYENA))],
        out_shape=[jax.ShapeDtypeStruct((n, D_HYENA), F32),
                   jax.ShapeDtypeStruct((1, D_HYENA), F32)],
        compiler_params=_cparams(("arbitrary",), VMEM_LIMIT),
        name="hyena_filter",
    )(z, w1p, b1.reshape(1, fo), w2, b2.reshape(1, fo), w3, b3.reshape(1, fo), w4,
      freq.reshape(1, fo), deltas)
    return kern, asum


def _half_rows(n):
    n_half = n // DFT_Q // 2 + 1
    return n_half, -(-n_half // 8) * 8


@functools.lru_cache(maxsize=None)
def _dft_tables(n, p_in):
    q_sz = DFT_Q
    p_sz = n // q_sz
    n_half, n_r = _half_rows(n)
    r = np.arange(n_r, dtype=np.float64)
    keep = (r < n_half).astype(np.float64)
    qq = np.arange(q_sz, dtype=np.float64)
    pp = np.arange(p_in, dtype=np.float64)
    tt = q_sz * pp[None, None, :] + qq[:, None, None]
    ang = -2.0 * np.pi * r[None, :, None] * tt / n
    t1 = np.concatenate([np.cos(ang), np.sin(ang)], axis=1) * np.tile(keep, 2)[None, :, None]
    mirror = np.where((r == 0) | (r == p_sz // 2), 1.0, 2.0)
    t4 = np.transpose(t1 * np.tile(mirror, 2)[None, :, None], (0, 2, 1)) / n
    a2 = -2.0 * np.pi * np.outer(qq, qq) / q_sz
    fre, fim = np.cos(a2), np.sin(a2)
    m2 = np.block([[fre, -fim], [fim, fre]])
    m2c = np.block([[fre, fim], [-fim, fre]])
    return (t1.astype(np.float32), t4.astype(np.float32), m2.astype(np.float32),
            m2c.astype(np.float32))


def _dft_stage1(x_ref, t1_ref, spec_ref, n_r, p_in):
    q_sz = DFT_Q

    def body(q, carry):
        xq = x_ref[pl.ds(q, p_in, stride=q_sz), :].astype(BF16)
        a = jnp.dot(t1_ref[q], xq, preferred_element_type=F32)
        spec_ref[pl.ds(q, n_r, stride=2 * q_sz), :] = a[:n_r]
        spec_ref[pl.ds(q_sz + q, n_r, stride=2 * q_sz), :] = a[n_r:]
        return carry

    lax.fori_loop(0, q_sz, body, 0, unroll=4)


def _spectrum_body(kern_ref, asum_ref, t1_ref, m2_ref, kf_ref, spec_ref, *, n_r, p_in):
    q_sz = DFT_Q
    _dft_stage1(kern_ref, t1_ref, spec_ref, n_r, p_in)
    inv = 1.0 / asum_ref[...]
    m2 = m2_ref[...]

    def body(r, carry):
        blk = spec_ref[pl.ds(pl.multiple_of(r * 2 * q_sz, 2 * q_sz), 2 * q_sz), :]
        xf = jnp.dot(m2, blk.astype(BF16), preferred_element_type=F32)
        kf_ref[r] = (xf * inv).astype(BF16)
        return carry

    lax.fori_loop(0, n_r, body, 0, unroll=2)


def _hyena_filter_spectrum(kern, asum):
    n, c = kern.shape
    p_sz = n // DFT_Q
    _, n_r = _half_rows(n)
    t1, _, m2, _ = _dft_tables(n, p_sz)
    t1 = jnp.asarray(t1).astype(BF16)
    m2 = jnp.asarray(m2).astype(BF16)
    ct = LANES
    return pl.pallas_call(
        functools.partial(_spectrum_body, n_r=n_r, p_in=p_sz),
        grid=(c // ct,),
        in_specs=[_single_spec((n, ct), lambda j: (0, j)),
                  pl.BlockSpec((1, ct), lambda j: (0, j)),
                  _single_spec(t1.shape, lambda j: (0, 0, 0)),
                  _const_spec(m2.shape)],
        out_specs=pl.BlockSpec((n_r, 2 * DFT_Q, ct), lambda j: (0, 0, j)),
        out_shape=jax.ShapeDtypeStruct((n_r, 2 * DFT_Q, c), BF16),
        scratch_shapes=[pltpu.VMEM((n_r * 2 * DFT_Q, ct), F32)],
        compiler_params=_cparams(("parallel",), VMEM_LIMIT),
        name="hyena_filter_spectrum",
    )(kern, asum, t1, m2)


def _short_conv_chunk(u_ref, w_ref, b_ref, i, rows, length):
    pack = 16
    base = pl.multiple_of(i * rows, rows)
    u = u_ref[pl.ds(base, rows), :].astype(F32)
    lo = pl.multiple_of(jnp.maximum(base - pack, 0), pack)
    hi = pl.multiple_of(jnp.minimum(base + rows, length - pack), pack)
    prev = u_ref[pl.ds(lo, pack), :].astype(F32)[pack - 1:pack]
    nxt = u_ref[pl.ds(hi, pack), :].astype(F32)[0:1]
    prev = jnp.where(base == 0, 0.0, prev)
    nxt = jnp.where(base + rows == length, 0.0, nxt)
    ridx = lax.broadcasted_iota(jnp.int32, (rows, 1), 0)
    up = jnp.where(ridx == 0, prev, pltpu.roll(u, 1, axis=0))
    dn = jnp.where(ridx == rows - 1, nxt, pltpu.roll(u, rows - 1, axis=0))
    w = w_ref[...]
    return up * w[0:1] + u * w[1:2] + dn * w[2:3] + b_ref[...]


def _hyena_conv_body(x0_ref, x1_ref, v_ref, w0_ref, w1_ref, wv_ref, b0_ref, b1_ref, bv_ref,
                     bias_ref, kf_ref, t1_ref, t4_ref, m2_ref, m2c_ref, o_ref, vx_ref, spec_ref,
                     *, length, rows):
    q_sz = DFT_Q
    n_half, n_r = _half_rows(2 * length)
    p_in = length // q_sz
    n_chunks = length // rows

    def gate_in(i, carry):
        x1c = _short_conv_chunk(x1_ref, w1_ref, b1_ref, i, rows, length)
        vc = _short_conv_chunk(v_ref, wv_ref, bv_ref, i, rows, length)
        vx_ref[pl.ds(pl.multiple_of(i * rows, rows), rows), :] = vc * x1c
        return carry

    lax.fori_loop(0, n_chunks, gate_in, 0)
    _dft_stage1(vx_ref, t1_ref, spec_ref, n_r, p_in)
    m2 = m2_ref[...]
    m2c = m2c_ref[...]

    def mid(r, carry):
        sl = pl.ds(pl.multiple_of(r * 2 * q_sz, 2 * q_sz), 2 * q_sz)
        xf = jnp.dot(m2, spec_ref[sl, :].astype(BF16), preferred_element_type=F32)
        kf = kf_ref[r].astype(F32)
        xre, xim = xf[:q_sz], xf[q_sz:]
        kre, kim = kf[:q_sz], kf[q_sz:]
        z = jnp.concatenate([xre * kre - xim * kim, xre * kim + xim * kre], axis=0)
        spec_ref[sl, :] = jnp.dot(m2c, z.astype(BF16), preferred_element_type=F32)
        return carry

    lax.fori_loop(0, n_half, mid, 0, unroll=2)
    bias = bias_ref[...]

    def last(q, carry):
        bre = spec_ref[pl.ds(q, n_r, stride=2 * q_sz), :]
        bim = spec_ref[pl.ds(q_sz + q, n_r, stride=2 * q_sz), :]
        bq = jnp.concatenate([bre, bim], axis=0).astype(BF16)
        y = jnp.dot(t4_ref[q], bq, preferred_element_type=F32)
        sl = pl.ds(q, p_in, stride=q_sz)
        vx_ref[sl, :] = y + bias * vx_ref[sl, :]
        return carry

    lax.fori_loop(0, q_sz, last, 0, unroll=4)

    def gate_out(i, carry):
        x0c = _short_conv_chunk(x0_ref, w0_ref, b0_ref, i, rows, length)
        sl = pl.ds(pl.multiple_of(i * rows, rows), rows)
        o_ref[sl, :] = (vx_ref[sl, :] * x0c).astype(BF16)
        return carry

    lax.fori_loop(0, n_chunks, gate_out, 0)


def _hyena_conv(hy, conv_w, conv_b, bias, kf):
    b, length, _ = hy.shape
    ch = D_HYENA
    ct = LANES
    nct = ch // ct
    n = 2 * length
    _, n_r = _half_rows(n)
    p_in = length // DFT_Q
    t1, t4, m2, m2c = _dft_tables(n, p_in)
    t1, t4, m2, m2c = (jnp.asarray(a).astype(BF16) for a in (t1, t4, m2, m2c))
    rows = min(512, length)
    col = lambda off: (lambda j, bi: (bi, 0, off * nct + j))
    wcol = lambda off: (lambda j, bi: (0, off * nct + j))
    conv_b2 = conv_b.reshape(1, -1)
    return pl.pallas_call(
        functools.partial(_hyena_conv_body, length=length, rows=rows),
        grid=(nct, b),
        in_specs=[_single_spec((None, length, ct), col(0)),
                  _single_spec((None, length, ct), col(1)),
                  _single_spec((None, length, ct), col(2)),
                  pl.BlockSpec((3, ct), wcol(0)), pl.BlockSpec((3, ct), wcol(1)),
                  pl.BlockSpec((3, ct), wcol(2)),
                  pl.BlockSpec((1, ct), wcol(0)), pl.BlockSpec((1, ct), wcol(1)),
                  pl.BlockSpec((1, ct), wcol(2)),
                  pl.BlockSpec((1, ct), lambda j, bi: (0, j)),
                  _single_spec((n_r, 2 * DFT_Q, ct), lambda j, bi: (0, 0, j)),
                  _single_spec(t1.shape, lambda j, bi: (0, 0, 0)),
                  _single_spec(t4.shape, lambda j, bi: (0, 0, 0)),
                  _const_spec(m2.shape), _const_spec(m2c.shape)],
        out_specs=pl.BlockSpec((None, length, ct), lambda j, bi: (bi, 0, j)),
        out_shape=jax.ShapeDtypeStruct((b, length, ch), BF16),
        scratch_shapes=[pltpu.VMEM((length, ct), F32),
                        pltpu.VMEM((n_r * 2 * DFT_Q, ct), F32)],
        compiler_params=_cparams(("parallel", "parallel"), VMEM_LIMIT),
        name="hyena_conv",
    )(hy, hy, hy, conv_w, conv_w, conv_w, conv_b2, conv_b2, conv_b2, bias.reshape(1, ch), kf,
      t1, t4, m2, m2c)


def _flash_body(q_ref, k_ref, v_ref, kc_ref, vc_ref, o_ref, sa_ref, sb_ref, p_ref, m_ref, l_ref,
                acc_ref, *, tk, n_sub, rg):
    tq = q_ref.shape[0]
    sub = tq // n_sub
    nk = k_ref.shape[0] // tk
    nc = kc_ref.shape[0]
    nt = (((1,), (1,)), ((), ()))

    def scores(j, dst_ref):
        off = pl.multiple_of(jnp.minimum(j, nk - 1) * tk, tk)
        dst_ref[...] = lax.dot_general(q_ref[...], k_ref[pl.ds(off, tk), :], nt,
                                       preferred_element_type=F32)

    def softmax_pv(src_ref, v, width):
        for i in range(n_sub):
            for g in range(sub // rg):
                rows = slice(i * sub + g * rg, i * sub + (g + 1) * rg)
                s = src_ref[rows, :width]
                m_prev = m_ref[rows, :]
                m_new = jnp.maximum(m_prev, jnp.max(s, axis=-1, keepdims=True))
                alpha = jnp.exp2(m_prev - m_new)
                p = jnp.exp2(s - pltpu.repeat(m_new, width // LANES, axis=1))
                l_ref[rows, :] = alpha * l_ref[rows, :] + jnp.sum(p, axis=-1, keepdims=True)
                m_ref[rows, :] = m_new
                acc_ref[rows, :] = alpha * acc_ref[rows, :]
                p_ref[rows, :width] = p.astype(BF16)
            srows = slice(i * sub, (i + 1) * sub)
            acc_ref[srows, :] += jnp.dot(p_ref[srows, :width], v, preferred_element_type=F32)

    def body(jj, carry):
        j = 2 * jj
        scores(j + 1, sb_ref)
        softmax_pv(sa_ref, v_ref[pl.ds(pl.multiple_of(j * tk, tk), tk), :], tk)
        scores(j + 2, sa_ref)
        softmax_pv(sb_ref, v_ref[pl.ds(pl.multiple_of((j + 1) * tk, tk), tk), :], tk)
        return carry

    m_ref[...] = jnp.full_like(m_ref, -jnp.inf)
    l_ref[...] = jnp.zeros_like(l_ref)
    acc_ref[...] = jnp.zeros_like(acc_ref)
    scores(0, sa_ref)
    lax.fori_loop(0, nk // 2, body, 0)
    sa_ref[:, :nc] = lax.dot_general(q_ref[...], kc_ref[...], nt, preferred_element_type=F32)
    softmax_pv(sa_ref, vc_ref[...], nc)
    o_ref[...] = (acc_ref[...] / l_ref[...]).astype(o_ref.dtype)


def _flash_attention(q, k, v, kc, vc, tq, tk, n_sub, rg):
    b, hds, length, _ = q.shape
    nc = kc.shape[2]
    assert (length // tk) % 2 == 0 and nc <= tk and nc % LANES == 0 and V_DIM == LANES
    return pl.pallas_call(
        functools.partial(_flash_body, tk=tk, n_sub=n_sub, rg=rg),
        grid=(b, hds, length // tq),
        in_specs=[pl.BlockSpec((None, None, tq, QK_DIM), lambda bi, h, qi: (bi, h, qi, 0)),
                  pl.BlockSpec((None, None, length, QK_DIM), lambda bi, h, qi: (bi, h, 0, 0)),
                  pl.BlockSpec((None, None, length, V_DIM), lambda bi, h, qi: (bi, h, 0, 0)),
                  pl.BlockSpec((None, None, nc, QK_DIM), lambda bi, h, qi: (bi, h, 0, 0)),
                  pl.BlockSpec((None, None, nc, V_DIM), lambda bi, h, qi: (bi, h, 0, 0))],
        out_specs=pl.BlockSpec((None, tq, V_DIM), lambda bi, h, qi: (bi, qi, h)),
        out_shape=jax.ShapeDtypeStruct((b, length, hds * V_DIM), BF16),
        scratch_shapes=[pltpu.VMEM((tq, tk), F32), pltpu.VMEM((tq, tk), F32),
                        pltpu.VMEM((tq, tk), BF16), pltpu.VMEM((tq, LANES), F32),
                        pltpu.VMEM((tq, LANES), F32), pltpu.VMEM((tq, V_DIM), F32)],
        compiler_params=_cparams(("parallel", "parallel", "parallel"), VMEM_LIMIT),
        name="mla_flash_attention",
    )(q, k, v, kc, vc)


def _outproj_body(yh_ref, ya_ref, x_ref, g1_ref, sh2_ref, sc2_ref, n2g_ref, wo1_ref, wo2_ref,
                  rwt_ref, rb_ref, tri_ref, xn_ref, h2_ref, idx_ref, gate_ref, rank_ref, cnt_ref,
                  carry_sc):
    i = pl.program_id(0)

    @pl.when(i == 0)
    def _():
        carry_sc[...] = jnp.zeros_like(carry_sc)

    mix = (jnp.dot(yh_ref[...], wo1_ref[...], preferred_element_type=F32)
           + jnp.dot(ya_ref[...], wo2_ref[...], preferred_element_type=F32))
    xn = x_ref[...] + g1_ref[...] * mix
    xn_ref[...] = xn
    h2 = _rms(xn) * n2g_ref[...]
    h2 = h2 * (1.0 + sc2_ref[...]) + sh2_ref[...]
    h2_ref[...] = h2.astype(BF16)
    logits = lax.dot_general(rwt_ref[...], h2, (((1,), (1,)), ((), ())), precision=HIGHEST,
                             preferred_element_type=F32) + rb_ref[...]
    n_e, tt = logits.shape
    eidx = lax.broadcasted_iota(jnp.int32, (n_e, tt), 0).astype(F32)
    work = logits
    vals, sels, idxs = [], [], []
    for _ in range(TOP_K):
        m = jnp.max(work, axis=0, keepdims=True)
        ix = jnp.min(jnp.where(work == m, eidx, float(n_e)), axis=0, keepdims=True)
        sel = eidx == ix
        work = jnp.where(sel, -jnp.inf, work)
        vals.append(m)
        idxs.append(ix)
        sels.append(sel)
    es = [jnp.exp(vk - vals[0]) for vk in vals]
    den = es[0] + es[1] + es[2] + es[3]
    gate_ref[...] = jnp.concatenate(es, axis=0) / den
    idx_ref[...] = jnp.concatenate(idxs, axis=0).astype(jnp.int32)
    onehot = jnp.zeros((n_e, tt), F32)
    for sel in sels:
        onehot = onehot + sel.astype(F32)
    prefix = jnp.dot(onehot.astype(BF16), tri_ref[...], preferred_element_type=F32) + carry_sc[...]
    ranks = [jnp.sum(jnp.where(sel, prefix, 0.0), axis=0, keepdims=True) for sel in sels]
    rank_ref[...] = jnp.concatenate(ranks, axis=0).astype(jnp.int32)
    carry_sc[...] += jnp.sum(onehot, axis=1, keepdims=True)
    cnt_ref[...] = jnp.broadcast_to(carry_sc[...], cnt_ref.shape)


def _outproj_router(y_hy, y_att, x2, g1, sh2, sc2, norm2_g, w_out, router_w, router_b, tokens_per_batch,
                    tt):
    t, d = x2.shape
    ch = y_hy.shape[1]
    n_e = router_w.shape[1]
    wo1 = w_out[:ch].astype(BF16)
    wo2 = w_out[ch:].astype(BF16)
    tri = jnp.asarray(np.triu(np.ones((tt, tt), np.float32), k=1), BF16)
    steps_per_batch = tokens_per_batch // tt
    tok = lambda i: (i, 0)
    per_b = lambda i: (i // steps_per_batch, 0, 0)
    lanes_tok = lambda i: (0, i)
    return pl.pallas_call(
        _outproj_body,
        grid=(t // tt,),
        in_specs=[pl.BlockSpec((tt, ch), tok), pl.BlockSpec((tt, ch), tok),
                  pl.BlockSpec((tt, d), tok),
                  pl.BlockSpec((None, 1, d), per_b), pl.BlockSpec((None, 1, d), per_b),
                  pl.BlockSpec((None, 1, d), per_b),
                  _const_spec((1, d)), _const_spec(wo1.shape), _const_spec(wo2.shape),
                  _const_spec((n_e, d)), _const_spec((n_e, 1)), _const_spec((tt, tt))],
        out_specs=[pl.BlockSpec((tt, d), tok), pl.BlockSpec((tt, d), tok),
                   pl.BlockSpec((TOP_K, tt), lanes_tok), pl.BlockSpec((TOP_K, tt), lanes_tok),
                   pl.BlockSpec((TOP_K, tt), lanes_tok), _const_spec((n_e, LANES))],
        out_shape=[jax.ShapeDtypeStruct((t, d), F32), jax.ShapeDtypeStruct((t, d), BF16),
                   jax.ShapeDtypeStruct((TOP_K, t), jnp.int32),
                   jax.ShapeDtypeStruct((TOP_K, t), F32),
                   jax.ShapeDtypeStruct((TOP_K, t), jnp.int32),
                   jax.ShapeDtypeStruct((n_e, LANES), F32)],
        scratch_shapes=[pltpu.VMEM((n_e, 1), F32)],
        compiler_params=_cparams(("arbitrary",), VMEM_LIMIT),
        name="outproj_router",
    )(y_hy, y_att, x2, g1, sh2, sc2, norm2_g.reshape(1, d), wo1, wo2, router_w.T,
      router_b.reshape(n_e, 1), tri)


def _cast_rows(src_ref, dst_ref, chunk):
    def body(c, carry):
        sl = pl.ds(pl.multiple_of(c * chunk, chunk), chunk)
        dst_ref[sl, :] = src_ref[sl, :].astype(dst_ref.dtype)
        return carry

    lax.fori_loop(0, src_ref.shape[0] // chunk, body, 0)


def _expert_body(be_ref, nused_ref, xs_ref, wgu_ref, bgu_ref, wd_ref, bd_ref, ys_ref, wgu_bf, wd_bf):
    i = pl.program_id(0)
    active = i < nused_ref[0]
    new_expert = jnp.logical_or(i == 0, be_ref[i] != be_ref[jnp.maximum(i - 1, 0)])

    @pl.when(jnp.logical_and(active, new_expert))
    def _():
        _cast_rows(wgu_ref, wgu_bf, 128)
        _cast_rows(wd_ref, wd_bf, 128)

    @pl.when(active)
    def _():
        gu = jnp.dot(xs_ref[...], wgu_bf[...], preferred_element_type=F32) + bgu_ref[...]
        dff = gu.shape[1] // 2
        gate = jnp.minimum(gu[:, :dff], SWIGLU_LIMIT)
        up = jnp.clip(gu[:, dff:], -SWIGLU_LIMIT, SWIGLU_LIMIT)
        act = (up + 1.0) * (gate * jax.nn.sigmoid(SWIGLU_ALPHA * gate))
        ys = jnp.dot(act.astype(BF16), wd_bf[...], preferred_element_type=F32) + bd_ref[...]
        ys_ref[...] = ys.astype(ys_ref.dtype)

    @pl.when(i >= nused_ref[0])
    def _():
        ys_ref[...] = jnp.zeros_like(ys_ref)


def _expert_blocks(xs, block_e, n_used, w_gu, b_gu, w_down, b_down):
    n_rows, d = xs.shape
    n_e, _, dff2 = w_gu.shape
    bm = MOE_ROWS
    grid_spec = pltpu.PrefetchScalarGridSpec(
        num_scalar_prefetch=2,
        grid=(n_rows // bm,),
        in_specs=[pl.BlockSpec((bm, d), lambda i, be, nu: (i, 0)),
                  pl.BlockSpec((None, d, dff2), lambda i, be, nu: (be[i], 0, 0)),
                  pl.BlockSpec((None, 1, dff2), lambda i, be, nu: (be[i], 0, 0)),
                  pl.BlockSpec((None, dff2 // 2, d), lambda i, be, nu: (be[i], 0, 0)),
                  pl.BlockSpec((None, 1, d), lambda i, be, nu: (be[i], 0, 0))],
        out_specs=pl.BlockSpec((bm, d), lambda i, be, nu: (i, 0)),
        scratch_shapes=[pltpu.VMEM((d, dff2), BF16), pltpu.VMEM((dff2 // 2, d), BF16)],
    )
    return pl.pallas_call(
        _expert_body,
        grid_spec=grid_spec,
        out_shape=jax.ShapeDtypeStruct((n_rows, d), BF16),
        compiler_params=_cparams(("arbitrary",), VMEM_LIMIT),
        name="moe_experts",
    )(block_e, n_used, xs, w_gu, b_gu.reshape(n_e, 1, dff2), w_down, b_down.reshape(n_e, 1, d))


def _moe(h2, xn, g2_rows, idx, gates, ranks, counts, w_gu, b_gu, w_down, b_down):
    t, d = h2.shape
    bm = MOE_ROWS
    n_e = w_gu.shape[0]
    cnt = counts[:, 0].astype(jnp.int32)
    padded = (cnt + bm - 1) // bm * bm
    padded_ends = jnp.cumsum(padded)
    padded_starts = padded_ends - padded
    dest = padded_starts[idx] + ranks
    n_blocks = t * TOP_K // bm + n_e
    n_rows = n_blocks * bm
    block_start = jnp.arange(n_blocks, dtype=jnp.int32) * bm
    block_e = jnp.minimum(jnp.sum(padded_ends[None, :] <= block_start[:, None], axis=1),
                          n_e - 1).astype(jnp.int32)
    n_used = (padded_ends[-1] // bm).astype(jnp.int32).reshape(1)
    tok = jnp.broadcast_to(jnp.arange(t, dtype=jnp.int32)[None, :], (TOP_K, t))
    row_tok = jnp.zeros((n_rows,), jnp.int32).at[dest.reshape(-1)].set(tok.reshape(-1))
    xs = h2[row_tok]
    ys = _expert_blocks(xs, block_e, n_used, w_gu, b_gu, w_down, b_down)
    picked = ys[dest].astype(F32)
    moe = jnp.sum(picked * gates[:, :, None], axis=0)
    return xn + g2_rows * moe


def kernel(x, c, ctx, c_ctx, mod_w, mod_b, norm1_g, w_in, hy_conv_w, hy_conv_b, hy_f_w1, hy_f_b1,
           hy_f_w2, hy_f_b2, hy_f_w3, hy_f_b3, hy_f_w4, hy_f_freq, hy_bias, mla_q_norm_g, mla_w_uq,
           mla_kv_norm_g, mla_w_ukv, qk_norm_q_g, qk_norm_k_g, w_out, norm2_g, router_w, router_b,
           exp_w_gu, exp_b_gu, exp_w_down, exp_b_down):
    b, length, d = x.shape
    depth = mod_w.shape[0]
    assert depth == 1, "single-layer kernel"
    ly = 0
    c_rows = jnp.concatenate([c, c_ctx[None, :], jnp.zeros((8 - b - 1, d), F32)], axis=0)
    mod = _adaln_table(c_rows, mod_w[ly], mod_b[ly])
    mod6 = mod.reshape(8, 6, d)
    sh1, sc1, g1, sh2, sc2, g2 = (mod6[:b, j][:, None, :] for j in range(6))
    csh1 = mod6[b:b + 1, 0][:, None, :]
    csc1 = mod6[b:b + 1, 1][:, None, :]

    weights = _mla_weights(w_in[ly], mla_w_uq[ly], mla_w_ukv[ly], qk_norm_q_g[ly], qk_norm_k_g[ly])
    n_ctx = ctx.shape[1]
    _, _, k_c, v_c = _inproj(ctx, jnp.broadcast_to(csh1, (b, 1, d)), jnp.broadcast_to(csc1, (b, 1, d)),
                             norm1_g[ly], weights, mla_q_norm_g[ly], mla_kv_norm_g[ly], False, n_ctx)
    hy, q, k, v = _inproj(x, sh1, sc1, norm1_g[ly], weights, mla_q_norm_g[ly], mla_kv_norm_g[ly],
                          True, min(512, length))

    kern, asum = _hyena_kernel_taps(length, hy_f_w1[ly], hy_f_b1[ly], hy_f_w2[ly], hy_f_b2[ly],
                                    hy_f_w3[ly], hy_f_b3[ly], hy_f_w4[ly], hy_f_freq[ly])
    kf = _hyena_filter_spectrum(kern, asum)
    y_hy = _hyena_conv(hy, hy_conv_w[ly], hy_conv_b[ly], hy_bias[ly], kf)

    y_att = _flash_attention(q, k, v, k_c, v_c, min(512, length), min(512, length), 2, 32)

    t = b * length
    xn, h2, idx, gates, ranks, counts = _outproj_router(
        y_hy.reshape(t, -1), y_att.reshape(t, -1), x.reshape(t, d), g1, sh2, sc2, norm2_g[ly],
        w_out[ly], router_w[ly], router_b[ly], length, min(512, length))
    g2_rows = jnp.broadcast_to(g2, (b, length, d)).reshape(t, d)
    out = _moe(h2, xn, g2_rows, idx, gates, ranks, counts, exp_w_gu[ly], exp_b_gu[ly],
               exp_w_down[ly], exp_b_down[ly])
    return out.reshape(b, length, d)
```

```python
import functools
import math

import jax
import jax.numpy as jnp
import numpy as np
from jax import lax
from jax.experimental import pallas as pl
from jax.experimental.pallas import tpu as pltpu
from jax.experimental.pallas import tpu_sc as plsc

F32 = jnp.float32
BF16 = jnp.bfloat16
HIGHEST = lax.Precision.HIGHEST

GRID_W = 64
D_HYENA = 512
FILTER_ORDER = 64
POS_EMB_DIM = 33
MIN_DECAY = math.log(1e-2) / 0.3
MAX_DECAY = math.log(1e-2) / 1.5
NOPE_DIM = 128
ROPE_DIM = 64
QK_DIM = NOPE_DIM + ROPE_DIM
V_DIM = 128
MLA_HEADS = 4
Q_RANK = 256
KV_RANK = 128
ROPE_THETA = 10000.0
N_EXPERTS = 32
TOP_K = 4
SWIGLU_ALPHA = 1.702
SWIGLU_LIMIT = 7.0
NORM_EPS = 1e-6

LANES = 128
VMEM_LIMIT = 56 * 1024 * 1024

DFT_Q = LANES
MOE_ROWS = 512
SC_WINDOW = 128


def _cparams(sem, vmem=None):
    return pltpu.CompilerParams(dimension_semantics=sem, vmem_limit_bytes=vmem)


def _const_spec(shape):
    nd = len(shape)
    return pl.BlockSpec(shape, lambda *_: (0,) * nd)


def _single_spec(shape, index_map):
    return pl.BlockSpec(shape, index_map, pipeline_mode=pl.Buffered(1))


def _mod_body(c_ref, w_ref, b_ref, o_ref):
    cc = c_ref[...]
    s = cc * jax.nn.sigmoid(cc)
    o_ref[...] = jnp.dot(s, w_ref[...], precision=HIGHEST,
                         preferred_element_type=F32) + b_ref[...]


def _adaln_table(c_rows, mod_w, mod_b):
    rows, d = c_rows.shape
    n = mod_w.shape[1]
    tn = n // 8
    return pl.pallas_call(
        _mod_body,
        grid=(n // tn,),
        in_specs=[_const_spec((rows, d)),
                  pl.BlockSpec((d, tn), lambda j: (0, j)),
                  pl.BlockSpec((1, tn), lambda j: (0, j))],
        out_specs=pl.BlockSpec((rows, tn), lambda j: (0, j)),
        out_shape=jax.ShapeDtypeStruct((rows, n), F32),
        compiler_params=_cparams(("arbitrary",)),
        name="adaln_table",
    )(c_rows, mod_w, mod_b.reshape(1, n))


def _rms(x, eps=NORM_EPS):
    return x * lax.rsqrt(jnp.mean(x * x, axis=-1, keepdims=True) + eps)


def _inproj_body(x_ref, sh_ref, sc_ref, g_ref, why_ref, wmla_ref, qng_ref, wuq_ref,
                 kvng_ref, wukv_ref, ct_ref, st_ref, ctk_ref, gq1_ref, gq2_ref, gkn_ref,
                 gkr_ref, hy_ref, q_ref, k_ref, v_ref, *, q_scale):
    h = _rms(x_ref[...]) * g_ref[...]
    h = h * (1.0 + sc_ref[...]) + sh_ref[...]
    hb = h.astype(BF16)
    hy_ref[...] = jnp.dot(hb, why_ref[...], preferred_element_type=F32).astype(BF16)
    mla = jnp.dot(hb, wmla_ref[...], preferred_element_type=F32)
    cq = mla[:, :Q_RANK]
    ckv = mla[:, Q_RANK:Q_RANK + KV_RANK]
    pe2 = mla[:, Q_RANK + KV_RANK:]
    qf = jnp.dot((_rms(cq) * qng_ref[...]).astype(BF16), wuq_ref[...],
                 preferred_element_type=F32)
    kvf = jnp.dot((_rms(ckv) * kvng_ref[...]).astype(BF16), wukv_ref[...],
                  preferred_element_type=F32)
    ct = ct_ref[...]
    st = st_ref[...]
    lane256 = lax.broadcasted_iota(jnp.int32, (1, 2 * LANES), 1)
    qmask = (lane256 < QK_DIM).astype(F32)
    lane128 = lax.broadcasted_iota(jnp.int32, (1, LANES), 1)
    pemask = (lane128 < ROPE_DIM).astype(F32)
    kr0 = pe2 * gkr_ref[...] * ctk_ref[...]
    krs = kr0 + pltpu.roll(kr0, ROPE_DIM, axis=1)
    pem = pe2 * pemask
    ss_pe = jnp.sum(pem * pem, axis=-1, keepdims=True)
    gq1 = gq1_ref[...]
    gq2 = gq2_ref[...]
    gkn = gkn_ref[...]
    for hd in range(MLA_HEADS):
        slab = qf[:, hd * 2 * LANES:(hd + 1) * 2 * LANES]
        sm = slab * qmask
        rq = lax.rsqrt(jnp.sum(sm * sm, axis=-1, keepdims=True) / QK_DIM + NORM_EPS) * q_scale
        t = slab * gq1 * ct + pltpu.roll(slab * gq2 * st, QK_DIM, axis=1)
        q_ref[hd] = (t * rq)[:, :QK_DIM].astype(BF16)
        kn = kvf[:, hd * 2 * LANES:hd * 2 * LANES + NOPE_DIM]
        rk = lax.rsqrt((jnp.sum(kn * kn, axis=-1, keepdims=True) + ss_pe) / QK_DIM + NORM_EPS)
        kslab = jnp.concatenate([kn * gkn, krs], axis=-1) * rk
        k_ref[hd] = kslab[:, :QK_DIM].astype(BF16)
        v_ref[hd] = kvf[:, hd * 2 * LANES + NOPE_DIM:(hd + 1) * 2 * LANES].astype(BF16)


def _rope_lane_tables(length, use_rope):
    ones = jnp.ones((length, LANES), F32)
    if use_rope:
        n_freq = ROPE_DIM // 4
        t = jnp.arange(length, dtype=jnp.int32)
        row = (t // GRID_W).astype(F32)
        col = (t % GRID_W).astype(F32)
        inv_freq = jnp.power(ROPE_THETA, -jnp.arange(n_freq, dtype=F32) / n_freq)
        ar = row[:, None] * inv_freq
        ac = col[:, None] * inv_freq
        c64 = jnp.concatenate([jnp.cos(ar), jnp.cos(ar), jnp.cos(ac), jnp.cos(ac)], axis=-1)
        s64 = jnp.concatenate([-jnp.sin(ar), jnp.sin(ar), -jnp.sin(ac), jnp.sin(ac)], axis=-1)
    else:
        c64 = jnp.ones((length, ROPE_DIM), F32)
        s64 = jnp.zeros((length, ROPE_DIM), F32)
    z64 = jnp.zeros((length, ROPE_DIM), F32)
    ct = jnp.concatenate([ones, c64, z64], axis=-1)
    st = jnp.concatenate([jnp.zeros((length, LANES), F32), z64, s64], axis=-1)
    ctk = jnp.concatenate([c64, s64], axis=-1)
    return ct, st, ctk


_SWAP16 = np.concatenate([np.arange(16, 32), np.arange(0, 16), np.arange(48, 64), np.arange(32, 48)])


def _mla_weights(w_in, mla_w_uq, mla_w_ukv, qk_norm_q_g, qk_norm_k_g):
    hy_cols = 3 * D_HYENA
    w_hy = w_in[:, :hy_cols].astype(BF16)
    w_pe = w_in[:, hy_cols + Q_RANK + KV_RANK:]
    w_mla = jnp.concatenate([w_in[:, hy_cols:hy_cols + Q_RANK + KV_RANK], w_pe, w_pe[:, _SWAP16]],
                            axis=-1).astype(BF16)
    wq = mla_w_uq.reshape(Q_RANK, MLA_HEADS, QK_DIM)
    wq_rope = wq[:, :, NOPE_DIM:]
    w_uq2 = jnp.concatenate([wq[:, :, :NOPE_DIM], wq_rope, wq_rope[:, :, _SWAP16]], axis=-1)
    w_uq2 = w_uq2.reshape(Q_RANK, MLA_HEADS * 2 * LANES).astype(BF16)
    w_ukv2 = mla_w_ukv.astype(BF16)
    gq_r = qk_norm_q_g[NOPE_DIM:]
    z64 = jnp.zeros((ROPE_DIM,), F32)
    gq1 = jnp.concatenate([qk_norm_q_g[:NOPE_DIM], gq_r, z64]).reshape(1, -1)
    gq2 = jnp.concatenate([jnp.zeros((NOPE_DIM,), F32), z64, gq_r[_SWAP16]]).reshape(1, -1)
    gkn = qk_norm_k_g[:NOPE_DIM].reshape(1, -1)
    gk_r = qk_norm_k_g[NOPE_DIM:]
    gkr = jnp.concatenate([gk_r, gk_r[_SWAP16]]).reshape(1, -1)
    return w_hy, w_mla, w_uq2, w_ukv2, gq1, gq2, gkn, gkr


def _inproj(x, shift, scale, norm_g, weights, q_norm_g, kv_norm_g, use_rope, tl):
    b, length, d = x.shape
    w_hy, w_mla, w_uq2, w_ukv2, gq1, gq2, gkn, gkr = weights
    ct, st, ctk = _rope_lane_tables(length, use_rope)
    q_scale = QK_DIM ** -0.5 * math.log2(math.e)
    tok = lambda bi, i: (bi, i, 0)
    per_b = lambda bi, i: (bi, 0, 0)
    pos = lambda bi, i: (i, 0)
    hyc = w_hy.shape[1]
    return pl.pallas_call(
        functools.partial(_inproj_body, q_scale=q_scale),
        grid=(b, length // tl),
        in_specs=[pl.BlockSpec((None, tl, d), tok),
                  pl.BlockSpec((None, 1, d), per_b),
                  pl.BlockSpec((None, 1, d), per_b),
                  _const_spec((1, d)),
                  _const_spec(w_hy.shape), _const_spec(w_mla.shape),
                  _const_spec((1, Q_RANK)), _const_spec(w_uq2.shape),
                  _const_spec((1, KV_RANK)), _const_spec(w_ukv2.shape),
                  pl.BlockSpec((tl, 2 * LANES), pos), pl.BlockSpec((tl, 2 * LANES), pos),
                  pl.BlockSpec((tl, LANES), pos),
                  _const_spec((1, 2 * LANES)), _const_spec((1, 2 * LANES)),
                  _const_spec((1, LANES)), _const_spec((1, LANES))],
        out_specs=[pl.BlockSpec((None, tl, hyc), tok),
                   pl.BlockSpec((None, MLA_HEADS, tl, QK_DIM), lambda bi, i: (bi, 0, i, 0)),
                   pl.BlockSpec((None, MLA_HEADS, tl, QK_DIM), lambda bi, i: (bi, 0, i, 0)),
                   pl.BlockSpec((None, MLA_HEADS, tl, V_DIM), lambda bi, i: (bi, 0, i, 0))],
        out_shape=[jax.ShapeDtypeStruct((b, length, hyc), BF16),
                   jax.ShapeDtypeStruct((b, MLA_HEADS, length, QK_DIM), BF16),
                   jax.ShapeDtypeStruct((b, MLA_HEADS, length, QK_DIM), BF16),
                   jax.ShapeDtypeStruct((b, MLA_HEADS, length, V_DIM), BF16)],
        compiler_params=_cparams(("parallel", "parallel"), VMEM_LIMIT),
        name="inproj_mla",
    )(x, shift, scale, norm_g.reshape(1, d), w_hy, w_mla, q_norm_g.reshape(1, -1), w_uq2,
      kv_norm_g.reshape(1, -1), w_ukv2, ct, st, ctk, gq1, gq2, gkn, gkr)


def _filter_body(z_ref, w1_ref, b1_ref, w2_ref, b2_ref, w3_ref, b3_ref, w4_ref, fr_ref, dl_ref,
                 kern_ref, asum_ref, *, zero_row, tr):
    i = pl.program_id(0)
    fr = fr_ref[...]
    z = z_ref[...]
    dot = functools.partial(jnp.dot, precision=HIGHEST, preferred_element_type=F32)
    h = jnp.sin(fr * (dot(z, w1_ref[...]) + b1_ref[...]))
    h = jnp.sin(fr * (dot(h, w2_ref[...]) + b2_ref[...]))
    h = jnp.sin(fr * (dot(h, w3_ref[...]) + b3_ref[...]))
    o = dot(h, w4_ref[...]) * jnp.exp(-z[:, 0:1] * dl_ref[...])

    @pl.when(i == 0)
    def _():
        asum_ref[...] = jnp.zeros_like(asum_ref)

    asum_ref[...] += jnp.sum(jnp.abs(o), axis=0, keepdims=True)
    row = i * tr + lax.broadcasted_iota(jnp.int32, (tr, 1), 0)
    kern_ref[...] = jnp.where(row == zero_row, 0.0, o)


def _hyena_kernel_taps(length, w1, b1, w2, b2, w3, b3, w4, freq):
    n = 2 * length
    bands = (POS_EMB_DIM - 1) // 2
    pos = np.concatenate([np.arange(length), (n - np.arange(length, n)) % length])
    pos = jnp.asarray(pos, jnp.int32)
    t_tab = jnp.linspace(0.0, 1.0, length, dtype=F32)[:, None]
    w_ang = 2.0 * math.pi * jnp.arange(length, dtype=F32)[:, None] / length
    f = jnp.linspace(1e-4, bands - 1, bands, dtype=F32)[None, :]
    z_tab = jnp.concatenate([t_tab, jnp.cos(f * w_ang), -jnp.sin(f * w_ang)], axis=-1)
    z = jnp.pad(z_tab[pos], ((0, 0), (0, LANES - POS_EMB_DIM)))
    w1p = jnp.pad(w1, ((0, LANES - POS_EMB_DIM), (0, 0)))
    deltas = jnp.abs(jnp.linspace(MIN_DECAY, MAX_DECAY, D_HYENA, dtype=F32)).reshape(1, -1)
    tr = min(1024, length)
    half_steps = length // tr
    fo = FILTER_ORDER
    kern, asum = pl.pallas_call(
        functools.partial(_filter_body, zero_row=length, tr=tr),
        grid=(n // tr,),
        in_specs=[pl.BlockSpec((tr, LANES), lambda i: (i, 0)),
                  _const_spec((LANES, fo)), _const_spec((1, fo)),
                  _const_spec((fo, fo)), _const_spec((1, fo)),
                  _const_spec((fo, fo)), _const_spec((1, fo)),
                  pl.BlockSpec((fo, D_HYENA), lambda i: (0, i // half_steps)),
                  _const_spec((1, fo)), _const_spec((1, D_HYENA))],
        out_specs=[pl.BlockSpec((tr, D_HYENA), lambda i: (i, 0)),
                   _const_spec((1, D_HYENA))],
        out_shape=[jax.ShapeDtypeStruct((n, D_HYENA), F32),
                   jax.ShapeDtypeStruct((1, D_HYENA), F32)],
        compiler_params=_cparams(("arbitrary",), VMEM_LIMIT),
        name="hyena_filter",
    )(z, w1p, b1.reshape(1, fo), w2, b2.reshape(1, fo), w3, b3.reshape(1, fo), w4,
      freq.reshape(1, fo), deltas)
    return kern, asum


def _half_rows(n):
    n_half = n // DFT_Q // 2 + 1
    return n_half, -(-n_half // 8) * 8


@functools.lru_cache(maxsize=None)
def _dft_tables(n, p_in):
    q_sz = DFT_Q
    p_sz = n // q_sz
    n_half, n_r = _half_rows(n)
    r = np.arange(n_r, dtype=np.float64)
    keep = (r < n_half).astype(np.float64)
    qq = np.arange(q_sz, dtype=np.float64)
    pp = np.arange(p_in, dtype=np.float64)
    tt = q_sz * pp[None, None, :] + qq[:, None, None]
    ang = -2.0 * np.pi * r[None, :, None] * tt / n
    t1 = np.concatenate([np.cos(ang), np.sin(ang)], axis=1) * np.tile(keep, 2)[None, :, None]
    mirror = np.where((r == 0) | (r == p_sz // 2), 1.0, 2.0)
    t4 = np.transpose(t1 * np.tile(mirror, 2)[None, :, None], (0, 2, 1)) / n
    a2 = -2.0 * np.pi * np.outer(qq, qq) / q_sz
    fre, fim = np.cos(a2), np.sin(a2)
    m2 = np.block([[fre, -fim], [fim, fre]])
    m2c = np.block([[fre, fim], [-fim, fre]])
    return (t1.astype(np.float32), t4.astype(np.float32), m2.astype(np.float32),
            m2c.astype(np.float32))


def _dft_stage1(x_ref, t1_ref, spec_ref, n_r, p_in):
    q_sz = DFT_Q

    def body(q, carry):
        xq = x_ref[pl.ds(q, p_in, stride=q_sz), :].astype(BF16)
        a = jnp.dot(t1_ref[q], xq, preferred_element_type=F32)
        spec_ref[pl.ds(q, n_r, stride=2 * q_sz), :] = a[:n_r]
        spec_ref[pl.ds(q_sz + q, n_r, stride=2 * q_sz), :] = a[n_r:]
        return carry

    lax.fori_loop(0, q_sz, body, 0, unroll=4)


def _spectrum_body(kern_ref, asum_ref, t1_ref, m2_ref, kf_ref, spec_ref, *, n_r, p_in):
    q_sz = DFT_Q
    _dft_stage1(kern_ref, t1_ref, spec_ref, n_r, p_in)
    inv = 1.0 / asum_ref[...]
    m2 = m2_ref[...]

    def body(r, carry):
        blk = spec_ref[pl.ds(pl.multiple_of(r * 2 * q_sz, 2 * q_sz), 2 * q_sz), :]
        xf = jnp.dot(m2, blk.astype(BF16), preferred_element_type=F32)
        kf_ref[r] = (xf * inv).astype(BF16)
        return carry

    lax.fori_loop(0, n_r, body, 0, unroll=2)


def _hyena_filter_spectrum(kern, asum):
    n, c = kern.shape
    p_sz = n // DFT_Q
    _, n_r = _half_rows(n)
    t1, _, m2, _ = _dft_tables(n, p_sz)
    t1 = jnp.asarray(t1).astype(BF16)
    m2 = jnp.asarray(m2).astype(BF16)
    ct = LANES
    return pl.pallas_call(
        functools.partial(_spectrum_body, n_r=n_r, p_in=p_sz),
        grid=(c // ct,),
        in_specs=[_single_spec((n, ct), lambda j: (0, j)),
                  pl.BlockSpec((1, ct), lambda j: (0, j)),
                  _single_spec(t1.shape, lambda j: (0, 0, 0)),
                  _const_spec(m2.shape)],
        out_specs=pl.BlockSpec((n_r, 2 * DFT_Q, ct), lambda j: (0, 0, j)),
        out_shape=jax.ShapeDtypeStruct((n_r, 2 * DFT_Q, c), BF16),
        scratch_shapes=[pltpu.VMEM((n_r * 2 * DFT_Q, ct), F32)],
        compiler_params=_cparams(("parallel",), VMEM_LIMIT),
        name="hyena_filter_spectrum",
    )(kern, asum, t1, m2)


def _short_conv_chunk(u_ref, w_ref, b_ref, i, rows, length):
    pack = 16
    base = pl.multiple_of(i * rows, rows)
    u = u_ref[pl.ds(base, rows), :].astype(F32)
    lo = pl.multiple_of(jnp.maximum(base - pack, 0), pack)
    hi = pl.multiple_of(jnp.minimum(base + rows, length - pack), pack)
    prev = u_ref[pl.ds(lo, pack), :].astype(F32)[pack - 1:pack]
    nxt = u_ref[pl.ds(hi, pack), :].astype(F32)[0:1]
    prev = jnp.where(base == 0, 0.0, prev)
    nxt = jnp.where(base + rows == length, 0.0, nxt)
    ridx = lax.broadcasted_iota(jnp.int32, (rows, 1), 0)
    up = jnp.where(ridx == 0, prev, pltpu.roll(u, 1, axis=0))
    dn = jnp.where(ridx == rows - 1, nxt, pltpu.roll(u, rows - 1, axis=0))
    w = w_ref[...]
    return up * w[0:1] + u * w[1:2] + dn * w[2:3] + b_ref[...]


def _hyena_conv_body(x0_ref, x1_ref, v_ref, w0_ref, w1_ref, wv_ref, b0_ref, b1_ref, bv_ref,
                     bias_ref, kf_ref, t1_ref, t4_ref, m2_ref, m2c_ref, o_ref, vx_ref, spec_ref,
                     *, length, rows):
    q_sz = DFT_Q
    n_half, n_r = _half_rows(2 * length)
    p_in = length // q_sz
    n_chunks = length // rows

    def gate_in(i, carry):
        x1c = _short_conv_chunk(x1_ref, w1_ref, b1_ref, i, rows, length)
        vc = _short_conv_chunk(v_ref, wv_ref, bv_ref, i, rows, length)
        vx_ref[pl.ds(pl.multiple_of(i * rows, rows), rows), :] = vc * x1c
        return carry

    lax.fori_loop(0, n_chunks, gate_in, 0)
    _dft_stage1(vx_ref, t1_ref, spec_ref, n_r, p_in)
    m2 = m2_ref[...]
    m2c = m2c_ref[...]

    def mid(r, carry):
        sl = pl.ds(pl.multiple_of(r * 2 * q_sz, 2 * q_sz), 2 * q_sz)
        xf = jnp.dot(m2, spec_ref[sl, :].astype(BF16), preferred_element_type=F32)
        kf = kf_ref[r].astype(F32)
        xre, xim = xf[:q_sz], xf[q_sz:]
        kre, kim = kf[:q_sz], kf[q_sz:]
        z = jnp.concatenate([xre * kre - xim * kim, xre * kim + xim * kre], axis=0)
        spec_ref[sl, :] = jnp.dot(m2c, z.astype(BF16), preferred_element_type=F32)
        return carry

    lax.fori_loop(0, n_half, mid, 0, unroll=2)
    bias = bias_ref[...]

    def last(q, carry):
        bre = spec_ref[pl.ds(q, n_r, stride=2 * q_sz), :]
        bim = spec_ref[pl.ds(q_sz + q, n_r, stride=2 * q_sz), :]
        bq = jnp.concatenate([bre, bim], axis=0).astype(BF16)
        y = jnp.dot(t4_ref[q], bq, preferred_element_type=F32)
        sl = pl.ds(q, p_in, stride=q_sz)
        vx_ref[sl, :] = y + bias * vx_ref[sl, :]
        return carry

    lax.fori_loop(0, q_sz, last, 0, unroll=4)

    def gate_out(i, carry):
        x0c = _short_conv_chunk(x0_ref, w0_ref, b0_ref, i, rows, length)
        sl = pl.ds(pl.multiple_of(i * rows, rows), rows)
        o_ref[sl, :] = (vx_ref[sl, :] * x0c).astype(BF16)
        return carry

    lax.fori_loop(0, n_chunks, gate_out, 0)


def _hyena_conv(hy, conv_w, conv_b, bias, kf):
    b, length, _ = hy.shape
    ch = D_HYENA
    ct = LANES
    nct = ch // ct
    n = 2 * length
    _, n_r = _half_rows(n)
    p_in = length // DFT_Q
    t1, t4, m2, m2c = _dft_tables(n, p_in)
    t1, t4, m2, m2c = (jnp.asarray(a).astype(BF16) for a in (t1, t4, m2, m2c))
    rows = min(512, length)
    col = lambda off: (lambda j, bi: (bi, 0, off * nct + j))
    wcol = lambda off: (lambda j, bi: (0, off * nct + j))
    conv_b2 = conv_b.reshape(1, -1)
    return pl.pallas_call(
        functools.partial(_hyena_conv_body, length=length, rows=rows),
        grid=(nct, b),
        in_specs=[_single_spec((None, length, ct), col(0)),
                  _single_spec((None, length, ct), col(1)),
                  _single_spec((None, length, ct), col(2)),
                  pl.BlockSpec((3, ct), wcol(0)), pl.BlockSpec((3, ct), wcol(1)),
                  pl.BlockSpec((3, ct), wcol(2)),
                  pl.BlockSpec((1, ct), wcol(0)), pl.BlockSpec((1, ct), wcol(1)),
                  pl.BlockSpec((1, ct), wcol(2)),
                  pl.BlockSpec((1, ct), lambda j, bi: (0, j)),
                  _single_spec((n_r, 2 * DFT_Q, ct), lambda j, bi: (0, 0, j)),
                  _single_spec(t1.shape, lambda j, bi: (0, 0, 0)),
                  _single_spec(t4.shape, lambda j, bi: (0, 0, 0)),
                  _const_spec(m2.shape), _const_spec(m2c.shape)],
        out_specs=pl.BlockSpec((None, length, ct), lambda j, bi: (bi, 0, j)),
        out_shape=jax.ShapeDtypeStruct((b, length, ch), BF16),
        scratch_shapes=[pltpu.VMEM((length, ct), F32),
                        pltpu.VMEM((n_r * 2 * DFT_Q, ct), F32)],
        compiler_params=_cparams(("parallel", "parallel"), VMEM_LIMIT),
        name="hyena_conv",
    )(hy, hy, hy, conv_w, conv_w, conv_w, conv_b2, conv_b2, conv_b2, bias.reshape(1, ch), kf,
      t1, t4, m2, m2c)


def _flash_body(q_ref, k_ref, v_ref, kc_ref, vc_ref, o_ref, sa_ref, sb_ref, p_ref, m_ref, l_ref,
                acc_ref, *, tk, n_sub, rg):
    tq = q_ref.shape[0]
    sub = tq // n_sub
    nk = k_ref.shape[0] // tk
    nc = kc_ref.shape[0]
    nt = (((1,), (1,)), ((), ()))

    def scores(j, dst_ref):
        off = pl.multiple_of(jnp.minimum(j, nk - 1) * tk, tk)
        dst_ref[...] = lax.dot_general(q_ref[...], k_ref[pl.ds(off, tk), :], nt,
                                       preferred_element_type=F32)

    def softmax_pv(src_ref, v, width):
        for i in range(n_sub):
            for g in range(sub // rg):
                rows = slice(i * sub + g * rg, i * sub + (g + 1) * rg)
                s = src_ref[rows, :width]
                m_prev = m_ref[rows, :]
                m_new = jnp.maximum(m_prev, jnp.max(s, axis=-1, keepdims=True))
                alpha = jnp.exp2(m_prev - m_new)
                p = jnp.exp2(s - jnp.concatenate([m_new] * (width // LANES), axis=1))
                l_ref[rows, :] = alpha * l_ref[rows, :] + jnp.sum(p, axis=-1, keepdims=True)
                m_ref[rows, :] = m_new
                acc_ref[rows, :] = alpha * acc_ref[rows, :]
                p_ref[rows, :width] = p.astype(BF16)
            srows = slice(i * sub, (i + 1) * sub)
            acc_ref[srows, :] += jnp.dot(p_ref[srows, :width], v, preferred_element_type=F32)

    def body(jj, carry):
        j = 2 * jj
        scores(j + 1, sb_ref)
        softmax_pv(sa_ref, v_ref[pl.ds(pl.multiple_of(j * tk, tk), tk), :], tk)
        scores(j + 2, sa_ref)
        softmax_pv(sb_ref, v_ref[pl.ds(pl.multiple_of((j + 1) * tk, tk), tk), :], tk)
        return carry

    m_ref[...] = jnp.full_like(m_ref, -jnp.inf)
    l_ref[...] = jnp.zeros_like(l_ref)
    acc_ref[...] = jnp.zeros_like(acc_ref)
    scores(0, sa_ref)
    lax.fori_loop(0, nk // 2, body, 0)
    sa_ref[:, :nc] = lax.dot_general(q_ref[...], kc_ref[...], nt, preferred_element_type=F32)
    softmax_pv(sa_ref, vc_ref[...], nc)
    o_ref[...] = (acc_ref[...] / l_ref[...]).astype(o_ref.dtype)


def _flash_attention(q, k, v, kc, vc, tq, tk, n_sub, rg):
    b, hds, length, _ = q.shape
    nc = kc.shape[2]
    assert (length // tk) % 2 == 0 and nc <= tk and nc % LANES == 0 and V_DIM == LANES
    return pl.pallas_call(
        functools.partial(_flash_body, tk=tk, n_sub=n_sub, rg=rg),
        grid=(b, hds, length // tq),
        in_specs=[pl.BlockSpec((None, None, tq, QK_DIM), lambda bi, h, qi: (bi, h, qi, 0)),
                  pl.BlockSpec((None, None, length, QK_DIM), lambda bi, h, qi: (bi, h, 0, 0)),
                  pl.BlockSpec((None, None, length, V_DIM), lambda bi, h, qi: (bi, h, 0, 0)),
                  pl.BlockSpec((None, None, nc, QK_DIM), lambda bi, h, qi: (bi, h, 0, 0)),
                  pl.BlockSpec((None, None, nc, V_DIM), lambda bi, h, qi: (bi, h, 0, 0))],
        out_specs=pl.BlockSpec((None, tq, V_DIM), lambda bi, h, qi: (bi, qi, h)),
        out_shape=jax.ShapeDtypeStruct((b, length, hds * V_DIM), BF16),
        scratch_shapes=[pltpu.VMEM((tq, tk), F32), pltpu.VMEM((tq, tk), F32),
                        pltpu.VMEM((tq, tk), BF16), pltpu.VMEM((tq, LANES), F32),
                        pltpu.VMEM((tq, LANES), F32), pltpu.VMEM((tq, V_DIM), F32)],
        compiler_params=_cparams(("parallel", "parallel", "parallel"), VMEM_LIMIT),
        name="mla_flash_attention",
    )(q, k, v, kc, vc)


def _store_packed_rows(dst_ref, x):
    half = x.shape[1] // 2
    for j in range(half // LANES):
        hi = x[:, j * LANES:(j + 1) * LANES].astype(BF16).astype(F32)
        lo = x[:, half + j * LANES:half + (j + 1) * LANES].astype(BF16).astype(F32)
        dst_ref[j] = (lax.bitcast_convert_type(hi, jnp.uint32)
                      | (lax.bitcast_convert_type(lo, jnp.uint32) >> 16))


def _unpack_words(w):
    hi = lax.bitcast_convert_type(w & jnp.uint32(0xFFFF0000), F32)
    lo = lax.bitcast_convert_type(w << 16, F32)
    return hi, lo


def _load_packed_rows(src_ref):
    parts = [_unpack_words(src_ref[j]) for j in range(src_ref.shape[0])]
    return jnp.concatenate([p[0] for p in parts] + [p[1] for p in parts], axis=-1)


def _outproj_body(yh_ref, ya_ref, x_ref, g1_ref, sh2_ref, sc2_ref, n2g_ref, wo1_ref, wo2_ref,
                  rwt_ref, rb_ref, tri_ref, xn_ref, h2_ref, idx_ref, gate_ref, rank_ref, cnt_ref,
                  carry_sc):
    i = pl.program_id(0)

    @pl.when(i == 0)
    def _():
        carry_sc[...] = jnp.zeros_like(carry_sc)

    mix = (jnp.dot(yh_ref[...], wo1_ref[...], preferred_element_type=F32)
           + jnp.dot(ya_ref[...], wo2_ref[...], preferred_element_type=F32))
    xn = x_ref[...] + g1_ref[...] * mix
    xn_ref[...] = xn
    h2 = _rms(xn) * n2g_ref[...]
    h2 = h2 * (1.0 + sc2_ref[...]) + sh2_ref[...]
    _store_packed_rows(h2_ref, h2)
    logits = lax.dot_general(rwt_ref[...], h2, (((1,), (1,)), ((), ())), precision=HIGHEST,
                             preferred_element_type=F32) + rb_ref[...]
    n_e, tt = logits.shape
    eidx = lax.broadcasted_iota(jnp.int32, (n_e, tt), 0).astype(F32)
    work = logits
    vals, sels, idxs = [], [], []
    for _ in range(TOP_K):
        m = jnp.max(work, axis=0, keepdims=True)
        ix = jnp.min(jnp.where(work == m, eidx, float(n_e)), axis=0, keepdims=True)
        sel = eidx == ix
        work = jnp.where(sel, -jnp.inf, work)
        vals.append(m)
        idxs.append(ix)
        sels.append(sel)
    es = [jnp.exp(vk - vals[0]) for vk in vals]
    den = es[0] + es[1] + es[2] + es[3]
    gate_ref[...] = jnp.concatenate(es, axis=0) / den
    idx_ref[...] = jnp.concatenate(idxs, axis=0).astype(jnp.int32)
    onehot = jnp.zeros((n_e, tt), F32)
    for sel in sels:
        onehot = onehot + sel.astype(F32)
    prefix = jnp.dot(onehot.astype(BF16), tri_ref[...], preferred_element_type=F32) + carry_sc[...]
    ranks = [jnp.sum(jnp.where(sel, prefix, 0.0), axis=0, keepdims=True) for sel in sels]
    rank_ref[...] = jnp.concatenate(ranks, axis=0).astype(jnp.int32)
    carry_sc[...] += jnp.sum(onehot, axis=1, keepdims=True)
    cnt_ref[...] = jnp.broadcast_to(carry_sc[...], cnt_ref.shape)


def _outproj_router(y_hy, y_att, x2, g1, sh2, sc2, norm2_g, w_out, router_w, router_b, tokens_per_batch,
                    tt):
    t, d = x2.shape
    ch = y_hy.shape[1]
    n_e = router_w.shape[1]
    wo1 = w_out[:ch].astype(BF16)
    wo2 = w_out[ch:].astype(BF16)
    tri = jnp.asarray(np.triu(np.ones((tt, tt), np.float32), k=1), BF16)
    steps_per_batch = tokens_per_batch // tt
    tok = lambda i: (i, 0)
    per_b = lambda i: (i // steps_per_batch, 0, 0)
    lanes_tok = lambda i: (0, i)
    return pl.pallas_call(
        _outproj_body,
        grid=(t // tt,),
        in_specs=[pl.BlockSpec((tt, ch), tok), pl.BlockSpec((tt, ch), tok),
                  pl.BlockSpec((tt, d), tok),
                  pl.BlockSpec((None, 1, d), per_b), pl.BlockSpec((None, 1, d), per_b),
                  pl.BlockSpec((None, 1, d), per_b),
                  _const_spec((1, d)), _const_spec(wo1.shape), _const_spec(wo2.shape),
                  _const_spec((n_e, d)), _const_spec((n_e, 1)), _const_spec((tt, tt))],
        out_specs=[pl.BlockSpec((tt, d), tok),
                   pl.BlockSpec((d // (2 * LANES), tt, LANES), lambda i: (0, i, 0)),
                   pl.BlockSpec((TOP_K, tt), lanes_tok), pl.BlockSpec((TOP_K, tt), lanes_tok),
                   pl.BlockSpec((TOP_K, tt), lanes_tok), _const_spec((n_e, LANES))],
        out_shape=[jax.ShapeDtypeStruct((t, d), F32),
                   jax.ShapeDtypeStruct((d // (2 * LANES), t, LANES), jnp.uint32),
                   jax.ShapeDtypeStruct((TOP_K, t), jnp.int32),
                   jax.ShapeDtypeStruct((TOP_K, t), F32),
                   jax.ShapeDtypeStruct((TOP_K, t), jnp.int32),
                   jax.ShapeDtypeStruct((n_e, LANES), F32)],
        scratch_shapes=[pltpu.VMEM((n_e, 1), F32)],
        compiler_params=_cparams(("arbitrary",), VMEM_LIMIT),
        name="outproj_router",
    )(y_hy, y_att, x2, g1, sh2, sc2, norm2_g.reshape(1, d), wo1, wo2, router_w.T,
      router_b.reshape(n_e, 1), tri)


def _cast_rows(src_ref, dst_ref, chunk):
    def body(c, carry):
        sl = pl.ds(pl.multiple_of(c * chunk, chunk), chunk)
        dst_ref[sl, :] = src_ref[sl, :].astype(dst_ref.dtype)
        return carry

    lax.fori_loop(0, src_ref.shape[0] // chunk, body, 0)


def _expert_body(be_ref, nused_ref, xs_ref, wgu_ref, bgu_ref, wd_ref, bd_ref, ys_ref, wgu_bf, wd_bf):
    i = pl.program_id(0)
    active = i < nused_ref[0]
    new_expert = jnp.logical_or(i == 0, be_ref[i] != be_ref[jnp.maximum(i - 1, 0)])

    @pl.when(jnp.logical_and(active, new_expert))
    def _():
        _cast_rows(wgu_ref, wgu_bf, 128)
        _cast_rows(wd_ref, wd_bf, 128)

    @pl.when(active)
    def _():
        xs = _load_packed_rows(xs_ref).astype(BF16)
        gu = jnp.dot(xs, wgu_bf[...], preferred_element_type=F32) + bgu_ref[...]
        dff = gu.shape[1] // 2
        gate = jnp.minimum(gu[:, :dff], SWIGLU_LIMIT)
        up = jnp.clip(gu[:, dff:], -SWIGLU_LIMIT, SWIGLU_LIMIT)
        act = (up + 1.0) * (gate * jax.nn.sigmoid(SWIGLU_ALPHA * gate))
        ys = jnp.dot(act.astype(BF16), wd_bf[...], preferred_element_type=F32) + bd_ref[...]
        _store_packed_rows(ys_ref, ys)

    @pl.when(i >= nused_ref[0])
    def _():
        ys_ref[...] = jnp.zeros_like(ys_ref)


def _expert_blocks(xs, block_e, n_used, w_gu, b_gu, w_down, b_down):
    n_seg, n_rows, _ = xs.shape
    n_e, d, dff2 = w_gu.shape
    bm = MOE_ROWS
    seg_block = pl.BlockSpec((n_seg, bm, LANES), lambda i, be, nu: (0, i, 0))
    grid_spec = pltpu.PrefetchScalarGridSpec(
        num_scalar_prefetch=2,
        grid=(n_rows // bm,),
        in_specs=[seg_block,
                  pl.BlockSpec((None, d, dff2), lambda i, be, nu: (be[i], 0, 0)),
                  pl.BlockSpec((None, 1, dff2), lambda i, be, nu: (be[i], 0, 0)),
                  pl.BlockSpec((None, dff2 // 2, d), lambda i, be, nu: (be[i], 0, 0)),
                  pl.BlockSpec((None, 1, d), lambda i, be, nu: (be[i], 0, 0))],
        out_specs=seg_block,
        scratch_shapes=[pltpu.VMEM((d, dff2), BF16), pltpu.VMEM((dff2 // 2, d), BF16)],
    )
    return pl.pallas_call(
        _expert_body,
        grid_spec=grid_spec,
        out_shape=jax.ShapeDtypeStruct(xs.shape, jnp.uint32),
        compiler_params=_cparams(("arbitrary",), VMEM_LIMIT),
        name="moe_experts",
    )(block_e, n_used, xs, w_gu, b_gu.reshape(n_e, 1, dff2), w_down, b_down.reshape(n_e, 1, d))


def _sc_gather(table, idx):
    n = idx.shape[0]
    width = table.shape[1]
    mesh = plsc.VectorSubcoreMesh(core_axis_name="core", subcore_axis_name="subcore")
    n_workers = mesh.num_cores * mesh.num_subcores
    assert width == LANES and n % (SC_WINDOW * n_workers) == 0

    @functools.partial(pl.kernel, out_type=jax.ShapeDtypeStruct((n, width), table.dtype), mesh=mesh)
    def gather_kernel(table_hbm, idx_hbm, out_hbm):
        def body(idx_vmem, out_vmem):
            pltpu.sync_copy(table_hbm.at[idx_vmem.at[0]], out_vmem)

        pltpu.emit_pipeline(
            body,
            grid=(n // SC_WINDOW,),
            in_specs=[pl.BlockSpec((1, SC_WINDOW), lambda i: (0, i))],
            out_specs=[pl.BlockSpec((SC_WINDOW, width), lambda i: (i, 0))],
            core_axis_name=("core", "subcore"),
            dimension_semantics=(pltpu.PARALLEL,),
        )(idx_hbm, out_hbm)

    return gather_kernel(table, idx.reshape(1, n))


def _combine_body(pk_ref, gt_ref, xn_ref, g2_ref, o_ref):
    n_seg = pk_ref.shape[0]
    half = o_ref.shape[1] // 2
    gt = gt_ref[...]
    g2 = g2_ref[...]
    for j in range(n_seg):
        acc_hi = None
        acc_lo = None
        for kk in range(TOP_K):
            hi, lo = _unpack_words(pk_ref[j, kk])
            g = gt[:, kk:kk + 1]
            acc_hi = g * hi if acc_hi is None else acc_hi + g * hi
            acc_lo = g * lo if acc_lo is None else acc_lo + g * lo
        c_hi = slice(j * LANES, (j + 1) * LANES)
        c_lo = slice(half + j * LANES, half + (j + 1) * LANES)
        o_ref[:, c_hi] = xn_ref[:, c_hi] + g2[:, c_hi] * acc_hi
        o_ref[:, c_lo] = xn_ref[:, c_lo] + g2[:, c_lo] * acc_lo


def _combine(picked, gates_t, xn, g2, tokens_per_batch, tt):
    n_seg, _, t, _ = picked.shape
    d = xn.shape[1]
    steps_per_batch = tokens_per_batch // tt
    return pl.pallas_call(
        _combine_body,
        grid=(t // tt,),
        in_specs=[pl.BlockSpec((n_seg, TOP_K, tt, LANES), lambda i: (0, 0, i, 0)),
                  pl.BlockSpec((tt, TOP_K), lambda i: (i, 0)),
                  pl.BlockSpec((tt, d), lambda i: (i, 0)),
                  pl.BlockSpec((None, 1, d), lambda i: (i // steps_per_batch, 0, 0))],
        out_specs=pl.BlockSpec((tt, d), lambda i: (i, 0)),
        out_shape=jax.ShapeDtypeStruct((t, d), F32),
        compiler_params=_cparams(("parallel",), VMEM_LIMIT),
        name="moe_combine",
    )(picked, gates_t, xn, g2)


def _moe(h2p, xn, g2, idx, gates, ranks, counts, w_gu, b_gu, w_down, b_down, tokens_per_batch):
    n_seg, t, _ = h2p.shape
    bm = MOE_ROWS
    n_e = w_gu.shape[0]
    cnt = counts[:, 0].astype(jnp.int32)
    padded = (cnt + bm - 1) // bm * bm
    padded_ends = jnp.cumsum(padded)
    padded_starts = padded_ends - padded
    dest = padded_starts[idx] + ranks
    n_blocks = t * TOP_K // bm + n_e
    n_rows = n_blocks * bm
    block_start = jnp.arange(n_blocks, dtype=jnp.int32) * bm
    block_e = jnp.minimum(jnp.sum(padded_ends[None, :] <= block_start[:, None], axis=1),
                          n_e - 1).astype(jnp.int32)
    n_used = (padded_ends[-1] // bm).astype(jnp.int32).reshape(1)
    tok = jnp.broadcast_to(jnp.arange(t, dtype=jnp.int32)[None, :], (TOP_K, t))
    spread = jnp.arange(n_rows, dtype=jnp.int32) % t
    row_tok = spread.at[dest.reshape(-1)].set(tok.reshape(-1))
    seg = jnp.arange(n_seg, dtype=jnp.int32)
    xs = _sc_gather(h2p.reshape(n_seg * t, LANES),
                    (seg[:, None] * t + row_tok[None, :]).reshape(-1))
    ys = _expert_blocks(xs.reshape(n_seg, n_rows, LANES), block_e, n_used, w_gu, b_gu, w_down, b_down)
    picked = _sc_gather(ys.reshape(n_seg * n_rows, LANES),
                        (seg[:, None, None] * n_rows + dest[None]).reshape(-1))
    return _combine(picked.reshape(n_seg, TOP_K, t, LANES), gates.T, xn, g2, tokens_per_batch,
                    min(512, tokens_per_batch))


def kernel(x, c, ctx, c_ctx, mod_w, mod_b, norm1_g, w_in, hy_conv_w, hy_conv_b, hy_f_w1, hy_f_b1,
           hy_f_w2, hy_f_b2, hy_f_w3, hy_f_b3, hy_f_w4, hy_f_freq, hy_bias, mla_q_norm_g, mla_w_uq,
           mla_kv_norm_g, mla_w_ukv, qk_norm_q_g, qk_norm_k_g, w_out, norm2_g, router_w, router_b,
           exp_w_gu, exp_b_gu, exp_w_down, exp_b_down):
    b, length, d = x.shape
    depth = mod_w.shape[0]
    assert depth == 1, "single-layer kernel"
    ly = 0
    c_rows = jnp.concatenate([c, c_ctx[None, :], jnp.zeros((8 - b - 1, d), F32)], axis=0)
    mod = _adaln_table(c_rows, mod_w[ly], mod_b[ly])
    mod6 = mod.reshape(8, 6, d)
    sh1, sc1, g1, sh2, sc2, g2 = (mod6[:b, j][:, None, :] for j in range(6))
    csh1 = mod6[b:b + 1, 0][:, None, :]
    csc1 = mod6[b:b + 1, 1][:, None, :]

    weights = _mla_weights(w_in[ly], mla_w_uq[ly], mla_w_ukv[ly], qk_norm_q_g[ly], qk_norm_k_g[ly])
    n_ctx = ctx.shape[1]
    _, _, k_c, v_c = _inproj(ctx, jnp.broadcast_to(csh1, (b, 1, d)), jnp.broadcast_to(csc1, (b, 1, d)),
                             norm1_g[ly], weights, mla_q_norm_g[ly], mla_kv_norm_g[ly], False, n_ctx)
    hy, q, k, v = _inproj(x, sh1, sc1, norm1_g[ly], weights, mla_q_norm_g[ly], mla_kv_norm_g[ly],
                          True, min(512, length))

    kern, asum = _hyena_kernel_taps(length, hy_f_w1[ly], hy_f_b1[ly], hy_f_w2[ly], hy_f_b2[ly],
                                    hy_f_w3[ly], hy_f_b3[ly], hy_f_w4[ly], hy_f_freq[ly])
    kf = _hyena_filter_spectrum(kern, asum)
    y_hy = _hyena_conv(hy, hy_conv_w[ly], hy_conv_b[ly], hy_bias[ly], kf)

    y_att = _flash_attention(q, k, v, k_c, v_c, min(512, length), min(512, length), 2, 32)

    t = b * length
    xn, h2, idx, gates, ranks, counts = _outproj_router(
        y_hy.reshape(t, -1), y_att.reshape(t, -1), x.reshape(t, d), g1, sh2, sc2, norm2_g[ly],
        w_out[ly], router_w[ly], router_b[ly], length, min(512, length))
    out = _moe(h2, xn, g2, idx, gates, ranks, counts, exp_w_gu[ly], exp_b_gu[ly],
               exp_w_down[ly], exp_b_down[ly], length)
    return out.reshape(b, length, d)
```

```python
import functools
import math

import jax
import jax.numpy as jnp
import numpy as np
from jax import lax
from jax.experimental import pallas as pl
from jax.experimental.pallas import tpu as pltpu
from jax.experimental.pallas import tpu_sc as plsc

F32 = jnp.float32
BF16 = jnp.bfloat16
HIGHEST = lax.Precision.HIGHEST

GRID_W = 64
D_HYENA = 512
FILTER_ORDER = 64
POS_EMB_DIM = 33
MIN_DECAY = math.log(1e-2) / 0.3
MAX_DECAY = math.log(1e-2) / 1.5
NOPE_DIM = 128
ROPE_DIM = 64
QK_DIM = NOPE_DIM + ROPE_DIM
V_DIM = 128
MLA_HEADS = 4
Q_RANK = 256
KV_RANK = 128
ROPE_THETA = 10000.0
N_EXPERTS = 32
TOP_K = 4
SWIGLU_ALPHA = 1.702
SWIGLU_LIMIT = 7.0
NORM_EPS = 1e-6

LANES = 128
VMEM_LIMIT = 56 * 1024 * 1024

DFT_Q = LANES
MOE_ROWS = 512
SC_WINDOW = 128


def _cparams(sem, vmem=None):
    return pltpu.CompilerParams(dimension_semantics=sem, vmem_limit_bytes=vmem)


def _const_spec(shape):
    nd = len(shape)
    return pl.BlockSpec(shape, lambda *_: (0,) * nd)


def _single_spec(shape, index_map):
    return pl.BlockSpec(shape, index_map, pipeline_mode=pl.Buffered(1))


def _mod_body(c_ref, w_ref, b_ref, o_ref):
    cc = c_ref[...]
    s = cc * jax.nn.sigmoid(cc)
    o_ref[...] = jnp.dot(s, w_ref[...], precision=HIGHEST,
                         preferred_element_type=F32) + b_ref[...]


def _adaln_table(c_rows, mod_w, mod_b):
    rows, d = c_rows.shape
    n = mod_w.shape[1]
    tn = n // 8
    return pl.pallas_call(
        _mod_body,
        grid=(n // tn,),
        in_specs=[_const_spec((rows, d)),
                  pl.BlockSpec((d, tn), lambda j: (0, j)),
                  pl.BlockSpec((1, tn), lambda j: (0, j))],
        out_specs=pl.BlockSpec((rows, tn), lambda j: (0, j)),
        out_shape=jax.ShapeDtypeStruct((rows, n), F32),
        compiler_params=_cparams(("arbitrary",)),
        name="adaln_table",
    )(c_rows, mod_w, mod_b.reshape(1, n))


def _rms(x, eps=NORM_EPS):
    return x * lax.rsqrt(jnp.mean(x * x, axis=-1, keepdims=True) + eps)


def _inproj_body(x_ref, sh_ref, sc_ref, g_ref, why_ref, wmla_ref, qng_ref, wuq_ref,
                 kvng_ref, wukv_ref, ct_ref, st_ref, ctk_ref, gq1_ref, gq2_ref, gkn_ref,
                 gkr_ref, hy_ref, q_ref, k_ref, v_ref, *, q_scale):
    h = _rms(x_ref[...]) * g_ref[...]
    h = h * (1.0 + sc_ref[...]) + sh_ref[...]
    hb = h.astype(BF16)
    hy_ref[...] = jnp.dot(hb, why_ref[...], preferred_element_type=F32).astype(BF16)
    mla = jnp.dot(hb, wmla_ref[...], preferred_element_type=F32)
    cq = mla[:, :Q_RANK]
    ckv = mla[:, Q_RANK:Q_RANK + KV_RANK]
    pe2 = mla[:, Q_RANK + KV_RANK:]
    qf = jnp.dot((_rms(cq) * qng_ref[...]).astype(BF16), wuq_ref[...],
                 preferred_element_type=F32)
    kvf = jnp.dot((_rms(ckv) * kvng_ref[...]).astype(BF16), wukv_ref[...],
                  preferred_element_type=F32)
    ct = ct_ref[...]
    st = st_ref[...]
    lane256 = lax.broadcasted_iota(jnp.int32, (1, 2 * LANES), 1)
    qmask = (lane256 < QK_DIM).astype(F32)
    lane128 = lax.broadcasted_iota(jnp.int32, (1, LANES), 1)
    pemask = (lane128 < ROPE_DIM).astype(F32)
    kr0 = pe2 * gkr_ref[...] * ctk_ref[...]
    krs = kr0 + pltpu.roll(kr0, ROPE_DIM, axis=1)
    pem = pe2 * pemask
    ss_pe = jnp.sum(pem * pem, axis=-1, keepdims=True)
    gq1 = gq1_ref[...]
    gq2 = gq2_ref[...]
    gkn = gkn_ref[...]
    for hd in range(MLA_HEADS):
        slab = qf[:, hd * 2 * LANES:(hd + 1) * 2 * LANES]
        sm = slab * qmask
        rq = lax.rsqrt(jnp.sum(sm * sm, axis=-1, keepdims=True) / QK_DIM + NORM_EPS) * q_scale
        t = slab * gq1 * ct + pltpu.roll(slab * gq2 * st, QK_DIM, axis=1)
        q_ref[hd] = (t * rq)[:, :QK_DIM].astype(BF16)
        kn = kvf[:, hd * 2 * LANES:hd * 2 * LANES + NOPE_DIM]
        rk = lax.rsqrt((jnp.sum(kn * kn, axis=-1, keepdims=True) + ss_pe) / QK_DIM + NORM_EPS)
        kslab = jnp.concatenate([kn * gkn, krs], axis=-1) * rk
        k_ref[hd] = kslab[:, :QK_DIM].astype(BF16)
        v_ref[hd] = kvf[:, hd * 2 * LANES + NOPE_DIM:(hd + 1) * 2 * LANES].astype(BF16)


def _rope_lane_tables(length, use_rope):
    ones = jnp.ones((length, LANES), F32)
    if use_rope:
        n_freq = ROPE_DIM // 4
        t = jnp.arange(length, dtype=jnp.int32)
        row = (t // GRID_W).astype(F32)
        col = (t % GRID_W).astype(F32)
        inv_freq = jnp.power(ROPE_THETA, -jnp.arange(n_freq, dtype=F32) / n_freq)
        ar = row[:, None] * inv_freq
        ac = col[:, None] * inv_freq
        c64 = jnp.concatenate([jnp.cos(ar), jnp.cos(ar), jnp.cos(ac), jnp.cos(ac)], axis=-1)
        s64 = jnp.concatenate([-jnp.sin(ar), jnp.sin(ar), -jnp.sin(ac), jnp.sin(ac)], axis=-1)
    else:
        c64 = jnp.ones((length, ROPE_DIM), F32)
        s64 = jnp.zeros((length, ROPE_DIM), F32)
    z64 = jnp.zeros((length, ROPE_DIM), F32)
    ct = jnp.concatenate([ones, c64, z64], axis=-1)
    st = jnp.concatenate([jnp.zeros((length, LANES), F32), z64, s64], axis=-1)
    ctk = jnp.concatenate([c64, s64], axis=-1)
    return ct, st, ctk


_SWAP16 = np.concatenate([np.arange(16, 32), np.arange(0, 16), np.arange(48, 64), np.arange(32, 48)])


def _mla_weights(w_in, mla_w_uq, mla_w_ukv, qk_norm_q_g, qk_norm_k_g):
    hy_cols = 3 * D_HYENA
    w_hy = w_in[:, :hy_cols].astype(BF16)
    w_pe = w_in[:, hy_cols + Q_RANK + KV_RANK:]
    w_mla = jnp.concatenate([w_in[:, hy_cols:hy_cols + Q_RANK + KV_RANK], w_pe, w_pe[:, _SWAP16]],
                            axis=-1).astype(BF16)
    wq = mla_w_uq.reshape(Q_RANK, MLA_HEADS, QK_DIM)
    wq_rope = wq[:, :, NOPE_DIM:]
    w_uq2 = jnp.concatenate([wq[:, :, :NOPE_DIM], wq_rope, wq_rope[:, :, _SWAP16]], axis=-1)
    w_uq2 = w_uq2.reshape(Q_RANK, MLA_HEADS * 2 * LANES).astype(BF16)
    w_ukv2 = mla_w_ukv.astype(BF16)
    gq_r = qk_norm_q_g[NOPE_DIM:]
    z64 = jnp.zeros((ROPE_DIM,), F32)
    gq1 = jnp.concatenate([qk_norm_q_g[:NOPE_DIM], gq_r, z64]).reshape(1, -1)
    gq2 = jnp.concatenate([jnp.zeros((NOPE_DIM,), F32), z64, gq_r[_SWAP16]]).reshape(1, -1)
    gkn = qk_norm_k_g[:NOPE_DIM].reshape(1, -1)
    gk_r = qk_norm_k_g[NOPE_DIM:]
    gkr = jnp.concatenate([gk_r, gk_r[_SWAP16]]).reshape(1, -1)
    return w_hy, w_mla, w_uq2, w_ukv2, gq1, gq2, gkn, gkr


def _inproj(x, shift, scale, norm_g, weights, q_norm_g, kv_norm_g, use_rope, tl):
    b, length, d = x.shape
    w_hy, w_mla, w_uq2, w_ukv2, gq1, gq2, gkn, gkr = weights
    ct, st, ctk = _rope_lane_tables(length, use_rope)
    q_scale = QK_DIM ** -0.5 * math.log2(math.e)
    tok = lambda bi, i: (bi, i, 0)
    per_b = lambda bi, i: (bi, 0, 0)
    pos = lambda bi, i: (i, 0)
    hyc = w_hy.shape[1]
    return pl.pallas_call(
        functools.partial(_inproj_body, q_scale=q_scale),
        grid=(b, length // tl),
        in_specs=[pl.BlockSpec((None, tl, d), tok),
                  pl.BlockSpec((None, 1, d), per_b),
                  pl.BlockSpec((None, 1, d), per_b),
                  _const_spec((1, d)),
                  _const_spec(w_hy.shape), _const_spec(w_mla.shape),
                  _const_spec((1, Q_RANK)), _const_spec(w_uq2.shape),
                  _const_spec((1, KV_RANK)), _const_spec(w_ukv2.shape),
                  pl.BlockSpec((tl, 2 * LANES), pos), pl.BlockSpec((tl, 2 * LANES), pos),
                  pl.BlockSpec((tl, LANES), pos),
                  _const_spec((1, 2 * LANES)), _const_spec((1, 2 * LANES)),
                  _const_spec((1, LANES)), _const_spec((1, LANES))],
        out_specs=[pl.BlockSpec((None, tl, hyc), tok),
                   pl.BlockSpec((None, MLA_HEADS, tl, QK_DIM), lambda bi, i: (bi, 0, i, 0)),
                   pl.BlockSpec((None, MLA_HEADS, tl, QK_DIM), lambda bi, i: (bi, 0, i, 0)),
                   pl.BlockSpec((None, MLA_HEADS, tl, V_DIM), lambda bi, i: (bi, 0, i, 0))],
        out_shape=[jax.ShapeDtypeStruct((b, length, hyc), BF16),
                   jax.ShapeDtypeStruct((b, MLA_HEADS, length, QK_DIM), BF16),
                   jax.ShapeDtypeStruct((b, MLA_HEADS, length, QK_DIM), BF16),
                   jax.ShapeDtypeStruct((b, MLA_HEADS, length, V_DIM), BF16)],
        compiler_params=_cparams(("parallel", "parallel"), VMEM_LIMIT),
        name="inproj_mla",
    )(x, shift, scale, norm_g.reshape(1, d), w_hy, w_mla, q_norm_g.reshape(1, -1), w_uq2,
      kv_norm_g.reshape(1, -1), w_ukv2, ct, st, ctk, gq1, gq2, gkn, gkr)


def _filter_body(z_ref, w1_ref, b1_ref, w2_ref, b2_ref, w3_ref, b3_ref, w4_ref, fr_ref, dl_ref,
                 kern_ref, asum_ref, *, zero_row, tr):
    i = pl.program_id(0)
    fr = fr_ref[...]
    z = z_ref[...]
    dot = functools.partial(jnp.dot, precision=HIGHEST, preferred_element_type=F32)
    h = jnp.sin(fr * (dot(z, w1_ref[...]) + b1_ref[...]))
    h = jnp.sin(fr * (dot(h, w2_ref[...]) + b2_ref[...]))
    h = jnp.sin(fr * (dot(h, w3_ref[...]) + b3_ref[...]))
    o = dot(h, w4_ref[...]) * jnp.exp(-z[:, 0:1] * dl_ref[...])

    @pl.when(i == 0)
    def _():
        asum_ref[...] = jnp.zeros_like(asum_ref)

    asum_ref[...] += jnp.sum(jnp.abs(o), axis=0, keepdims=True)
    row = i * tr + lax.broadcasted_iota(jnp.int32, (tr, 1), 0)
    kern_ref[...] = jnp.where(row == zero_row, 0.0, o)


def _hyena_kernel_taps(length, w1, b1, w2, b2, w3, b3, w4, freq):
    n = 2 * length
    bands = (POS_EMB_DIM - 1) // 2
    pos = np.concatenate([np.arange(length), (n - np.arange(length, n)) % length])
    pos = jnp.asarray(pos, jnp.int32)
    t_tab = jnp.linspace(0.0, 1.0, length, dtype=F32)[:, None]
    w_ang = 2.0 * math.pi * jnp.arange(length, dtype=F32)[:, None] / length
    f = jnp.linspace(1e-4, bands - 1, bands, dtype=F32)[None, :]
    z_tab = jnp.concatenate([t_tab, jnp.cos(f * w_ang), -jnp.sin(f * w_ang)], axis=-1)
    z = jnp.pad(z_tab[pos], ((0, 0), (0, LANES - POS_EMB_DIM)))
    w1p = jnp.pad(w1, ((0, LANES - POS_EMB_DIM), (0, 0)))
    deltas = jnp.abs(jnp.linspace(MIN_DECAY, MAX_DECAY, D_HYENA, dtype=F32)).reshape(1, -1)
    tr = min(1024, length)
    half_steps = length // tr
    fo = FILTER_ORDER
    kern, asum = pl.pallas_call(
        functools.partial(_filter_body, zero_row=length, tr=tr),
        grid=(n // tr,),
        in_specs=[pl.BlockSpec((tr, LANES), lambda i: (i, 0)),
                  _const_spec((LANES, fo)), _const_spec((1, fo)),
                  _const_spec((fo, fo)), _const_spec((1, fo)),
                  _const_spec((fo, fo)), _const_spec((1, fo)),
                  pl.BlockSpec((fo, D_HYENA), lambda i: (0, i // half_steps)),
                  _const_spec((1, fo)), _const_spec((1, D_HYENA))],
        out_specs=[pl.BlockSpec((tr, D_HYENA), lambda i: (i, 0)),
                   _const_spec((1, D_HYENA))],
        out_shape=[jax.ShapeDtypeStruct((n, D_HYENA), F32),
                   jax.ShapeDtypeStruct((1, D_HYENA), F32)],
        compiler_params=_cparams(("arbitrary",), VMEM_LIMIT),
        name="hyena_filter",
    )(z, w1p, b1.reshape(1, fo), w2, b2.reshape(1, fo), w3, b3.reshape(1, fo), w4,
      freq.reshape(1, fo), deltas)
    return kern, asum


def _half_rows(n):
    n_half = n // DFT_Q // 2 + 1
    return n_half, -(-n_half // 8) * 8


@functools.lru_cache(maxsize=None)
def _dft_tables(n, p_in):
    q_sz = DFT_Q
    p_sz = n // q_sz
    n_half, n_r = _half_rows(n)
    r = np.arange(n_r, dtype=np.float64)
    keep = (r < n_half).astype(np.float64)
    qq = np.arange(q_sz, dtype=np.float64)
    pp = np.arange(p_in, dtype=np.float64)
    tt = q_sz * pp[None, None, :] + qq[:, None, None]
    ang = -2.0 * np.pi * r[None, :, None] * tt / n
    t1 = np.concatenate([np.cos(ang), np.sin(ang)], axis=1) * np.tile(keep, 2)[None, :, None]
    mirror = np.where((r == 0) | (r == p_sz // 2), 1.0, 2.0)
    t4 = np.transpose(t1 * np.tile(mirror, 2)[None, :, None], (0, 2, 1)) / n
    a2 = -2.0 * np.pi * np.outer(qq, qq) / q_sz
    fre, fim = np.cos(a2), np.sin(a2)
    m2 = np.block([[fre, -fim], [fim, fre]])
    m2c = np.block([[fre, fim], [-fim, fre]])
    return (t1.astype(np.float32), t4.astype(np.float32), m2.astype(np.float32),
            m2c.astype(np.float32))


def _dft_stage1(x_ref, t1_ref, spec_ref, n_r, p_in):
    q_sz = DFT_Q

    def body(q, carry):
        xq = x_ref[pl.ds(q, p_in, stride=q_sz), :].astype(BF16)
        a = jnp.dot(t1_ref[q], xq, preferred_element_type=F32)
        spec_ref[pl.ds(q, n_r, stride=2 * q_sz), :] = a[:n_r]
        spec_ref[pl.ds(q_sz + q, n_r, stride=2 * q_sz), :] = a[n_r:]
        return carry

    lax.fori_loop(0, q_sz, body, 0, unroll=4)


def _spectrum_body(kern_ref, asum_ref, t1_ref, m2_ref, kf_ref, spec_ref, *, n_r, p_in):
    q_sz = DFT_Q
    _dft_stage1(kern_ref, t1_ref, spec_ref, n_r, p_in)
    inv = 1.0 / asum_ref[...]
    m2 = m2_ref[...]

    def body(r, carry):
        blk = spec_ref[pl.ds(pl.multiple_of(r * 2 * q_sz, 2 * q_sz), 2 * q_sz), :]
        xf = jnp.dot(m2, blk.astype(BF16), preferred_element_type=F32)
        kf_ref[r] = (xf * inv).astype(BF16)
        return carry

    lax.fori_loop(0, n_r, body, 0, unroll=2)


def _hyena_filter_spectrum(kern, asum):
    n, c = kern.shape
    p_sz = n // DFT_Q
    _, n_r = _half_rows(n)
    t1, _, m2, _ = _dft_tables(n, p_sz)
    t1 = jnp.asarray(t1).astype(BF16)
    m2 = jnp.asarray(m2).astype(BF16)
    ct = LANES
    return pl.pallas_call(
        functools.partial(_spectrum_body, n_r=n_r, p_in=p_sz),
        grid=(c // ct,),
        in_specs=[_single_spec((n, ct), lambda j: (0, j)),
                  pl.BlockSpec((1, ct), lambda j: (0, j)),
                  _single_spec(t1.shape, lambda j: (0, 0, 0)),
                  _const_spec(m2.shape)],
        out_specs=pl.BlockSpec((n_r, 2 * DFT_Q, ct), lambda j: (0, 0, j)),
        out_shape=jax.ShapeDtypeStruct((n_r, 2 * DFT_Q, c), BF16),
        scratch_shapes=[pltpu.VMEM((n_r * 2 * DFT_Q, ct), F32)],
        compiler_params=_cparams(("parallel",), VMEM_LIMIT),
        name="hyena_filter_spectrum",
    )(kern, asum, t1, m2)


def _short_conv_chunk(u_ref, w_ref, b_ref, i, rows, length):
    pack = 16
    base = pl.multiple_of(i * rows, rows)
    u = u_ref[pl.ds(base, rows), :].astype(F32)
    lo = pl.multiple_of(jnp.maximum(base - pack, 0), pack)
    hi = pl.multiple_of(jnp.minimum(base + rows, length - pack), pack)
    prev = u_ref[pl.ds(lo, pack), :].astype(F32)[pack - 1:pack]
    nxt = u_ref[pl.ds(hi, pack), :].astype(F32)[0:1]
    prev = jnp.where(base == 0, 0.0, prev)
    nxt = jnp.where(base + rows == length, 0.0, nxt)
    ridx = lax.broadcasted_iota(jnp.int32, (rows, 1), 0)
    up = jnp.where(ridx == 0, prev, pltpu.roll(u, 1, axis=0))
    dn = jnp.where(ridx == rows - 1, nxt, pltpu.roll(u, rows - 1, axis=0))
    w = w_ref[...]
    return up * w[0:1] + u * w[1:2] + dn * w[2:3] + b_ref[...]


def _hyena_conv_body(x0_ref, x1_ref, v_ref, w0_ref, w1_ref, wv_ref, b0_ref, b1_ref, bv_ref,
                     bias_ref, kf_ref, t1_ref, t4_ref, m2_ref, m2c_ref, o_ref, vx_ref, spec_ref,
                     *, length, rows):
    q_sz = DFT_Q
    n_half, n_r = _half_rows(2 * length)
    p_in = length // q_sz
    n_chunks = length // rows

    def gate_in(i, carry):
        x1c = _short_conv_chunk(x1_ref, w1_ref, b1_ref, i, rows, length)
        vc = _short_conv_chunk(v_ref, wv_ref, bv_ref, i, rows, length)
        vx_ref[pl.ds(pl.multiple_of(i * rows, rows), rows), :] = vc * x1c
        return carry

    lax.fori_loop(0, n_chunks, gate_in, 0)
    _dft_stage1(vx_ref, t1_ref, spec_ref, n_r, p_in)
    m2 = m2_ref[...]
    m2c = m2c_ref[...]

    def mid(r, carry):
        sl = pl.ds(pl.multiple_of(r * 2 * q_sz, 2 * q_sz), 2 * q_sz)
        xf = jnp.dot(m2, spec_ref[sl, :].astype(BF16), preferred_element_type=F32)
        kf = kf_ref[r].astype(F32)
        xre, xim = xf[:q_sz], xf[q_sz:]
        kre, kim = kf[:q_sz], kf[q_sz:]
        z = jnp.concatenate([xre * kre - xim * kim, xre * kim + xim * kre], axis=0)
        spec_ref[sl, :] = jnp.dot(m2c, z.astype(BF16), preferred_element_type=F32)
        return carry

    lax.fori_loop(0, n_half, mid, 0, unroll=2)
    bias = bias_ref[...]

    def last(q, carry):
        bre = spec_ref[pl.ds(q, n_r, stride=2 * q_sz), :]
        bim = spec_ref[pl.ds(q_sz + q, n_r, stride=2 * q_sz), :]
        bq = jnp.concatenate([bre, bim], axis=0).astype(BF16)
        y = jnp.dot(t4_ref[q], bq, preferred_element_type=F32)
        sl = pl.ds(q, p_in, stride=q_sz)
        vx_ref[sl, :] = y + bias * vx_ref[sl, :]
        return carry

    lax.fori_loop(0, q_sz, last, 0, unroll=4)

    def gate_out(i, carry):
        x0c = _short_conv_chunk(x0_ref, w0_ref, b0_ref, i, rows, length)
        sl = pl.ds(pl.multiple_of(i * rows, rows), rows)
        o_ref[sl, :] = (vx_ref[sl, :] * x0c).astype(BF16)
        return carry

    lax.fori_loop(0, n_chunks, gate_out, 0)


def _hyena_conv(hy, conv_w, conv_b, bias, kf):
    b, length, _ = hy.shape
    ch = D_HYENA
    ct = LANES
    nct = ch // ct
    n = 2 * length
    _, n_r = _half_rows(n)
    p_in = length // DFT_Q
    t1, t4, m2, m2c = _dft_tables(n, p_in)
    t1, t4, m2, m2c = (jnp.asarray(a).astype(BF16) for a in (t1, t4, m2, m2c))
    rows = min(512, length)
    col = lambda off: (lambda j, bi: (bi, 0, off * nct + j))
    wcol = lambda off: (lambda j, bi: (0, off * nct + j))
    conv_b2 = conv_b.reshape(1, -1)
    return pl.pallas_call(
        functools.partial(_hyena_conv_body, length=length, rows=rows),
        grid=(nct, b),
        in_specs=[_single_spec((None, length, ct), col(0)),
                  _single_spec((None, length, ct), col(1)),
                  _single_spec((None, length, ct), col(2)),
                  pl.BlockSpec((3, ct), wcol(0)), pl.BlockSpec((3, ct), wcol(1)),
                  pl.BlockSpec((3, ct), wcol(2)),
                  pl.BlockSpec((1, ct), wcol(0)), pl.BlockSpec((1, ct), wcol(1)),
                  pl.BlockSpec((1, ct), wcol(2)),
                  pl.BlockSpec((1, ct), lambda j, bi: (0, j)),
                  _single_spec((n_r, 2 * DFT_Q, ct), lambda j, bi: (0, 0, j)),
                  _single_spec(t1.shape, lambda j, bi: (0, 0, 0)),
                  _single_spec(t4.shape, lambda j, bi: (0, 0, 0)),
                  _const_spec(m2.shape), _const_spec(m2c.shape)],
        out_specs=pl.BlockSpec((None, length, ct), lambda j, bi: (bi, 0, j)),
        out_shape=jax.ShapeDtypeStruct((b, length, ch), BF16),
        scratch_shapes=[pltpu.VMEM((length, ct), F32),
                        pltpu.VMEM((n_r * 2 * DFT_Q, ct), F32)],
        compiler_params=_cparams(("parallel", "parallel"), VMEM_LIMIT),
        name="hyena_conv",
    )(hy, hy, hy, conv_w, conv_w, conv_w, conv_b2, conv_b2, conv_b2, bias.reshape(1, ch), kf,
      t1, t4, m2, m2c)


def _flash_body(q_ref, k_ref, v_ref, kc_ref, vc_ref, o_ref, sa_ref, sb_ref, p_ref, m_ref, l_ref,
                acc_ref, *, tk, n_sub, rg):
    tq = q_ref.shape[0]
    sub = tq // n_sub
    nk = k_ref.shape[0] // tk
    nc = kc_ref.shape[0]
    nt = (((1,), (1,)), ((), ()))

    def scores(j, dst_ref):
        off = pl.multiple_of(jnp.minimum(j, nk - 1) * tk, tk)
        dst_ref[...] = lax.dot_general(q_ref[...], k_ref[pl.ds(off, tk), :], nt,
                                       preferred_element_type=F32)

    def softmax_pv(src_ref, v, width):
        for i in range(n_sub):
            for g in range(sub // rg):
                rows = slice(i * sub + g * rg, i * sub + (g + 1) * rg)
                s = src_ref[rows, :width]
                m_prev = m_ref[rows, :]
                m_new = jnp.maximum(m_prev, jnp.max(s, axis=-1, keepdims=True))
                alpha = jnp.exp2(m_prev - m_new)
                p = jnp.exp2(s - jnp.concatenate([m_new] * (width // LANES), axis=1))
                l_ref[rows, :] = alpha * l_ref[rows, :] + jnp.sum(p, axis=-1, keepdims=True)
                m_ref[rows, :] = m_new
                acc_ref[rows, :] = alpha * acc_ref[rows, :]
                p_ref[rows, :width] = p.astype(BF16)
            srows = slice(i * sub, (i + 1) * sub)
            acc_ref[srows, :] += jnp.dot(p_ref[srows, :width], v, preferred_element_type=F32)

    def body(jj, carry):
        j = 2 * jj
        scores(j + 1, sb_ref)
        softmax_pv(sa_ref, v_ref[pl.ds(pl.multiple_of(j * tk, tk), tk), :], tk)
        scores(j + 2, sa_ref)
        softmax_pv(sb_ref, v_ref[pl.ds(pl.multiple_of((j + 1) * tk, tk), tk), :], tk)
        return carry

    m_ref[...] = jnp.full_like(m_ref, -jnp.inf)
    l_ref[...] = jnp.zeros_like(l_ref)
    acc_ref[...] = jnp.zeros_like(acc_ref)
    scores(0, sa_ref)
    lax.fori_loop(0, nk // 2, body, 0)
    sa_ref[:, :nc] = lax.dot_general(q_ref[...], kc_ref[...], nt, preferred_element_type=F32)
    softmax_pv(sa_ref, vc_ref[...], nc)
    o_ref[...] = (acc_ref[...] / l_ref[...]).astype(o_ref.dtype)


def _flash_attention(q, k, v, kc, vc, tq, tk, n_sub, rg):
    b, hds, length, _ = q.shape
    nc = kc.shape[2]
    assert (length // tk) % 2 == 0 and nc <= tk and nc % LANES == 0 and V_DIM == LANES
    return pl.pallas_call(
        functools.partial(_flash_body, tk=tk, n_sub=n_sub, rg=rg),
        grid=(b, hds, length // tq),
        in_specs=[pl.BlockSpec((None, None, tq, QK_DIM), lambda bi, h, qi: (bi, h, qi, 0)),
                  pl.BlockSpec((None, None, length, QK_DIM), lambda bi, h, qi: (bi, h, 0, 0)),
                  pl.BlockSpec((None, None, length, V_DIM), lambda bi, h, qi: (bi, h, 0, 0)),
                  pl.BlockSpec((None, None, nc, QK_DIM), lambda bi, h, qi: (bi, h, 0, 0)),
                  pl.BlockSpec((None, None, nc, V_DIM), lambda bi, h, qi: (bi, h, 0, 0))],
        out_specs=pl.BlockSpec((None, tq, V_DIM), lambda bi, h, qi: (bi, qi, h)),
        out_shape=jax.ShapeDtypeStruct((b, length, hds * V_DIM), BF16),
        scratch_shapes=[pltpu.VMEM((tq, tk), F32), pltpu.VMEM((tq, tk), F32),
                        pltpu.VMEM((tq, tk), BF16), pltpu.VMEM((tq, LANES), F32),
                        pltpu.VMEM((tq, LANES), F32), pltpu.VMEM((tq, V_DIM), F32)],
        compiler_params=_cparams(("parallel", "parallel", "parallel"), VMEM_LIMIT),
        name="mla_flash_attention",
    )(q, k, v, kc, vc)


def _store_packed_rows(dst_ref, x):
    half = x.shape[1] // 2
    for j in range(half // LANES):
        hi = x[:, j * LANES:(j + 1) * LANES].astype(BF16).astype(F32)
        lo = x[:, half + j * LANES:half + (j + 1) * LANES].astype(BF16).astype(F32)
        dst_ref[j] = (lax.bitcast_convert_type(hi, jnp.uint32)
                      | (lax.bitcast_convert_type(lo, jnp.uint32) >> 16))


def _unpack_words(w):
    hi = lax.bitcast_convert_type(w & jnp.uint32(0xFFFF0000), F32)
    lo = lax.bitcast_convert_type(w << 16, F32)
    return hi, lo


def _load_packed_rows(src_ref):
    parts = [_unpack_words(src_ref[j]) for j in range(src_ref.shape[0])]
    return jnp.concatenate([p[0] for p in parts] + [p[1] for p in parts], axis=-1)


def _outproj_body(yh_ref, ya_ref, x_ref, g1_ref, sh2_ref, sc2_ref, n2g_ref, wo1_ref, wo2_ref,
                  rwt_ref, rb_ref, tri_ref, xn_ref, h2_ref, idx_ref, gate_ref, rank_ref, cnt_ref,
                  carry_sc):
    i = pl.program_id(0)

    @pl.when(i == 0)
    def _():
        carry_sc[...] = jnp.zeros_like(carry_sc)

    mix = (jnp.dot(yh_ref[...], wo1_ref[...], preferred_element_type=F32)
           + jnp.dot(ya_ref[...], wo2_ref[...], preferred_element_type=F32))
    xn = x_ref[...] + g1_ref[...] * mix
    xn_ref[...] = xn
    h2 = _rms(xn) * n2g_ref[...]
    h2 = h2 * (1.0 + sc2_ref[...]) + sh2_ref[...]
    _store_packed_rows(h2_ref, h2)
    logits = lax.dot_general(rwt_ref[...], h2, (((1,), (1,)), ((), ())), precision=HIGHEST,
                             preferred_element_type=F32) + rb_ref[...]
    n_e, tt = logits.shape
    eidx = lax.broadcasted_iota(jnp.int32, (n_e, tt), 0).astype(F32)
    work = logits
    vals, sels, idxs = [], [], []
    for _ in range(TOP_K):
        m = jnp.max(work, axis=0, keepdims=True)
        ix = jnp.min(jnp.where(work == m, eidx, float(n_e)), axis=0, keepdims=True)
        sel = eidx == ix
        work = jnp.where(sel, -jnp.inf, work)
        vals.append(m)
        idxs.append(ix)
        sels.append(sel)
    es = [jnp.exp(vk - vals[0]) for vk in vals]
    den = es[0] + es[1] + es[2] + es[3]
    gate_ref[...] = jnp.concatenate(es, axis=0) / den
    idx_ref[...] = jnp.concatenate(idxs, axis=0).astype(jnp.int32)
    onehot = jnp.zeros((n_e, tt), F32)
    for sel in sels:
        onehot = onehot + sel.astype(F32)
    prefix = jnp.dot(onehot.astype(BF16), tri_ref[...], preferred_element_type=F32) + carry_sc[...]
    ranks = [jnp.sum(jnp.where(sel, prefix, 0.0), axis=0, keepdims=True) for sel in sels]
    rank_ref[...] = jnp.concatenate(ranks, axis=0).astype(jnp.int32)
    carry_sc[...] += jnp.sum(onehot, axis=1, keepdims=True)
    cnt_ref[...] = jnp.broadcast_to(carry_sc[...], cnt_ref.shape)


def _outproj_router(y_hy, y_att, x2, g1, sh2, sc2, norm2_g, w_out, router_w, router_b, tokens_per_batch,
                    tt):
    t, d = x2.shape
    ch = y_hy.shape[1]
    n_e = router_w.shape[1]
    wo1 = w_out[:ch].astype(BF16)
    wo2 = w_out[ch:].astype(BF16)
    tri = jnp.asarray(np.triu(np.ones((tt, tt), np.float32), k=1), BF16)
    steps_per_batch = tokens_per_batch // tt
    tok = lambda i: (i, 0)
    per_b = lambda i: (i // steps_per_batch, 0, 0)
    lanes_tok = lambda i: (0, i)
    return pl.pallas_call(
        _outproj_body,
        grid=(t // tt,),
        in_specs=[pl.BlockSpec((tt, ch), tok), pl.BlockSpec((tt, ch), tok),
                  pl.BlockSpec((tt, d), tok),
                  pl.BlockSpec((None, 1, d), per_b), pl.BlockSpec((None, 1, d), per_b),
                  pl.BlockSpec((None, 1, d), per_b),
                  _const_spec((1, d)), _const_spec(wo1.shape), _const_spec(wo2.shape),
                  _const_spec((n_e, d)), _const_spec((n_e, 1)), _const_spec((tt, tt))],
        out_specs=[pl.BlockSpec((tt, d), tok),
                   pl.BlockSpec((d // (2 * LANES), tt, LANES), lambda i: (0, i, 0)),
                   pl.BlockSpec((TOP_K, tt), lanes_tok), pl.BlockSpec((TOP_K, tt), lanes_tok),
                   pl.BlockSpec((TOP_K, tt), lanes_tok), _const_spec((n_e, LANES))],
        out_shape=[jax.ShapeDtypeStruct((t, d), F32),
                   jax.ShapeDtypeStruct((d // (2 * LANES), t, LANES), jnp.uint32),
                   jax.ShapeDtypeStruct((TOP_K, t), jnp.int32),
                   jax.ShapeDtypeStruct((TOP_K, t), F32),
                   jax.ShapeDtypeStruct((TOP_K, t), jnp.int32),
                   jax.ShapeDtypeStruct((n_e, LANES), F32)],
        scratch_shapes=[pltpu.VMEM((n_e, 1), F32)],
        compiler_params=_cparams(("arbitrary",), VMEM_LIMIT),
        name="outproj_router",
    )(y_hy, y_att, x2, g1, sh2, sc2, norm2_g.reshape(1, d), wo1, wo2, router_w.T,
      router_b.reshape(n_e, 1), tri)


def _cast_rows(src_ref, dst_ref, chunk):
    def body(c, carry):
        sl = pl.ds(pl.multiple_of(c * chunk, chunk), chunk)
        dst_ref[sl, :] = src_ref[sl, :].astype(dst_ref.dtype)
        return carry

    lax.fori_loop(0, src_ref.shape[0] // chunk, body, 0)


def _expert_body(be_ref, nvalid_ref, xs_ref, wgu_ref, bgu_ref, wd_ref, bd_ref, ys_ref, wgu_bf, wd_bf):
    i = pl.program_id(0)
    n_valid = nvalid_ref[i]
    active = n_valid > 0
    new_expert = jnp.logical_or(i == 0, be_ref[i] != be_ref[jnp.maximum(i - 1, 0)])

    @pl.when(jnp.logical_and(active, new_expert))
    def _():
        _cast_rows(wgu_ref, wgu_bf, 128)
        _cast_rows(wd_ref, wd_bf, 128)

    @pl.when(active)
    def _():
        row = lax.broadcasted_iota(jnp.int32, (xs_ref.shape[1], 1), 0)
        xs = jnp.where(row < n_valid, _load_packed_rows(xs_ref), 0.0).astype(BF16)
        gu = jnp.dot(xs, wgu_bf[...], preferred_element_type=F32) + bgu_ref[...]
        dff = gu.shape[1] // 2
        gate = jnp.minimum(gu[:, :dff], SWIGLU_LIMIT)
        up = jnp.clip(gu[:, dff:], -SWIGLU_LIMIT, SWIGLU_LIMIT)
        act = (up + 1.0) * (gate * jax.nn.sigmoid(SWIGLU_ALPHA * gate))
        ys = jnp.dot(act.astype(BF16), wd_bf[...], preferred_element_type=F32) + bd_ref[...]
        _store_packed_rows(ys_ref, ys)

    @pl.when(jnp.logical_not(active))
    def _():
        ys_ref[...] = jnp.zeros_like(ys_ref)


def _expert_blocks(xs, block_e, n_valid, w_gu, b_gu, w_down, b_down):
    n_seg, n_rows, _ = xs.shape
    n_e, d, dff2 = w_gu.shape
    bm = MOE_ROWS
    seg_block = pl.BlockSpec((n_seg, bm, LANES), lambda i, be, nu: (0, i, 0))
    grid_spec = pltpu.PrefetchScalarGridSpec(
        num_scalar_prefetch=2,
        grid=(n_rows // bm,),
        in_specs=[seg_block,
                  pl.BlockSpec((None, d, dff2), lambda i, be, nu: (be[i], 0, 0)),
                  pl.BlockSpec((None, 1, dff2), lambda i, be, nu: (be[i], 0, 0)),
                  pl.BlockSpec((None, dff2 // 2, d), lambda i, be, nu: (be[i], 0, 0)),
                  pl.BlockSpec((None, 1, d), lambda i, be, nu: (be[i], 0, 0))],
        out_specs=seg_block,
        scratch_shapes=[pltpu.VMEM((d, dff2), BF16), pltpu.VMEM((dff2 // 2, d), BF16)],
    )
    return pl.pallas_call(
        _expert_body,
        grid_spec=grid_spec,
        out_shape=jax.ShapeDtypeStruct(xs.shape, jnp.uint32),
        compiler_params=_cparams(("arbitrary",), VMEM_LIMIT),
        name="moe_experts",
    )(block_e, n_valid, xs, w_gu, b_gu.reshape(n_e, 1, dff2), w_down, b_down.reshape(n_e, 1, d))


def _sc_gather(table, idx):
    n = idx.shape[0]
    width = table.shape[1]
    mesh = plsc.VectorSubcoreMesh(core_axis_name="core", subcore_axis_name="subcore")
    n_workers = mesh.num_cores * mesh.num_subcores
    assert width == LANES and n % (SC_WINDOW * n_workers) == 0

    @functools.partial(pl.kernel, out_type=jax.ShapeDtypeStruct((n, width), table.dtype), mesh=mesh)
    def gather_kernel(table_hbm, idx_hbm, out_hbm):
        def body(idx_vmem, out_vmem):
            pltpu.sync_copy(table_hbm.at[idx_vmem.at[0]], out_vmem)

        pltpu.emit_pipeline(
            body,
            grid=(n // SC_WINDOW,),
            in_specs=[pl.BlockSpec((1, SC_WINDOW), lambda i: (0, i))],
            out_specs=[pl.BlockSpec((SC_WINDOW, width), lambda i: (i, 0))],
            core_axis_name=("core", "subcore"),
            dimension_semantics=(pltpu.PARALLEL,),
        )(idx_hbm, out_hbm)

    return gather_kernel(table, idx.reshape(1, n))


def _sc_scatter(rows, idx, n_copies, n_out):
    n, width = rows.shape
    mesh = plsc.VectorSubcoreMesh(core_axis_name="core", subcore_axis_name="subcore")
    n_workers = mesh.num_cores * mesh.num_subcores
    assert width == LANES and n % (SC_WINDOW * n_workers) == 0 and idx.shape == (8, n)

    @functools.partial(pl.kernel, out_type=jax.ShapeDtypeStruct((n_out, width), rows.dtype), mesh=mesh)
    def scatter_kernel(rows_hbm, idx_hbm, out_hbm):
        def body(rows_vmem, idx_vmem):
            for k in range(n_copies):
                pltpu.sync_copy(rows_vmem, out_hbm.at[idx_vmem.at[k]])

        pltpu.emit_pipeline(
            body,
            grid=(n // SC_WINDOW,),
            in_specs=[pl.BlockSpec((SC_WINDOW, width), lambda i: (i, 0)),
                      pl.BlockSpec((8, SC_WINDOW), lambda i: (0, i))],
            out_specs=[],
            core_axis_name=("core", "subcore"),
            dimension_semantics=(pltpu.PARALLEL,),
        )(rows_hbm, idx_hbm)

    return scatter_kernel(rows, idx)


def _combine_body(pk_ref, gt_ref, xn_ref, g2_ref, o_ref):
    n_seg = pk_ref.shape[0]
    half = o_ref.shape[1] // 2
    gt = gt_ref[...]
    g2 = g2_ref[...]
    for j in range(n_seg):
        acc_hi = None
        acc_lo = None
        for kk in range(TOP_K):
            hi, lo = _unpack_words(pk_ref[j, kk])
            g = gt[:, kk:kk + 1]
            acc_hi = g * hi if acc_hi is None else acc_hi + g * hi
            acc_lo = g * lo if acc_lo is None else acc_lo + g * lo
        c_hi = slice(j * LANES, (j + 1) * LANES)
        c_lo = slice(half + j * LANES, half + (j + 1) * LANES)
        o_ref[:, c_hi] = xn_ref[:, c_hi] + g2[:, c_hi] * acc_hi
        o_ref[:, c_lo] = xn_ref[:, c_lo] + g2[:, c_lo] * acc_lo


def _combine(picked, gates_t, xn, g2, tokens_per_batch, tt):
    n_seg, _, t, _ = picked.shape
    d = xn.shape[1]
    steps_per_batch = tokens_per_batch // tt
    return pl.pallas_call(
        _combine_body,
        grid=(t // tt,),
        in_specs=[pl.BlockSpec((n_seg, TOP_K, tt, LANES), lambda i: (0, 0, i, 0)),
                  pl.BlockSpec((tt, TOP_K), lambda i: (i, 0)),
                  pl.BlockSpec((tt, d), lambda i: (i, 0)),
                  pl.BlockSpec((None, 1, d), lambda i: (i // steps_per_batch, 0, 0))],
        out_specs=pl.BlockSpec((tt, d), lambda i: (i, 0)),
        out_shape=jax.ShapeDtypeStruct((t, d), F32),
        compiler_params=_cparams(("parallel",), VMEM_LIMIT),
        name="moe_combine",
    )(picked, gates_t, xn, g2)


def _moe(h2p, xn, g2, idx, gates, ranks, counts, w_gu, b_gu, w_down, b_down, tokens_per_batch):
    n_seg, t, _ = h2p.shape
    bm = MOE_ROWS
    n_e = w_gu.shape[0]
    cnt = counts[:, 0].astype(jnp.int32)
    padded = (cnt + bm - 1) // bm * bm
    padded_ends = jnp.cumsum(padded)
    padded_starts = padded_ends - padded
    experts = jnp.arange(n_e, dtype=jnp.int32)[:, None, None]
    dest = ranks + jnp.sum(jnp.where(idx[None] == experts, padded_starts[:, None, None], 0), axis=0)
    n_blocks = t * TOP_K // bm + n_e
    n_rows = n_blocks * bm
    block_start = jnp.arange(n_blocks, dtype=jnp.int32) * bm
    block_e = jnp.minimum(jnp.sum(padded_ends[None, :] <= block_start[:, None], axis=1),
                          n_e - 1).astype(jnp.int32)
    n_valid = jnp.clip(cnt[block_e] - (block_start - padded_starts[block_e]), 0, bm).astype(jnp.int32)
    seg = jnp.arange(n_seg, dtype=jnp.int32)
    scatter_idx = (seg[None, :, None] * n_rows + dest[:, None, :]).reshape(TOP_K, n_seg * t)
    scatter_idx = jnp.concatenate([scatter_idx, scatter_idx], axis=0)
    xs = _sc_scatter(h2p.reshape(n_seg * t, LANES), scatter_idx, TOP_K, n_seg * n_rows)
    ys = _expert_blocks(xs.reshape(n_seg, n_rows, LANES), block_e, n_valid, w_gu, b_gu, w_down, b_down)
    picked = _sc_gather(ys.reshape(n_seg * n_rows, LANES),
                        (seg[:, None, None] * n_rows + dest[None]).reshape(-1))
    return _combine(picked.reshape(n_seg, TOP_K, t, LANES), gates.T, xn, g2, tokens_per_batch,
                    min(512, tokens_per_batch))


def kernel(x, c, ctx, c_ctx, mod_w, mod_b, norm1_g, w_in, hy_conv_w, hy_conv_b, hy_f_w1, hy_f_b1,
           hy_f_w2, hy_f_b2, hy_f_w3, hy_f_b3, hy_f_w4, hy_f_freq, hy_bias, mla_q_norm_g, mla_w_uq,
           mla_kv_norm_g, mla_w_ukv, qk_norm_q_g, qk_norm_k_g, w_out, norm2_g, router_w, router_b,
           exp_w_gu, exp_b_gu, exp_w_down, exp_b_down):
    b, length, d = x.shape
    depth = mod_w.shape[0]
    assert depth == 1, "single-layer kernel"
    ly = 0
    c_rows = jnp.concatenate([c, c_ctx[None, :], jnp.zeros((8 - b - 1, d), F32)], axis=0)
    mod = _adaln_table(c_rows, mod_w[ly], mod_b[ly])
    mod6 = mod.reshape(8, 6, d)
    sh1, sc1, g1, sh2, sc2, g2 = (mod6[:b, j][:, None, :] for j in range(6))
    csh1 = mod6[b:b + 1, 0][:, None, :]
    csc1 = mod6[b:b + 1, 1][:, None, :]

    weights = _mla_weights(w_in[ly], mla_w_uq[ly], mla_w_ukv[ly], qk_norm_q_g[ly], qk_norm_k_g[ly])
    n_ctx = ctx.shape[1]
    _, _, k_c, v_c = _inproj(ctx, jnp.broadcast_to(csh1, (b, 1, d)), jnp.broadcast_to(csc1, (b, 1, d)),
                             norm1_g[ly], weights, mla_q_norm_g[ly], mla_kv_norm_g[ly], False, n_ctx)
    hy, q, k, v = _inproj(x, sh1, sc1, norm1_g[ly], weights, mla_q_norm_g[ly], mla_kv_norm_g[ly],
                          True, min(512, length))

    kern, asum = _hyena_kernel_taps(length, hy_f_w1[ly], hy_f_b1[ly], hy_f_w2[ly], hy_f_b2[ly],
                                    hy_f_w3[ly], hy_f_b3[ly], hy_f_w4[ly], hy_f_freq[ly])
    kf = _hyena_filter_spectrum(kern, asum)
    y_hy = _hyena_conv(hy, hy_conv_w[ly], hy_conv_b[ly], hy_bias[ly], kf)

    y_att = _flash_attention(q, k, v, k_c, v_c, min(512, length), min(512, length), 2, 32)

    t = b * length
    xn, h2, idx, gates, ranks, counts = _outproj_router(
        y_hy.reshape(t, -1), y_att.reshape(t, -1), x.reshape(t, d), g1, sh2, sc2, norm2_g[ly],
        w_out[ly], router_w[ly], router_b[ly], length, min(512, length))
    out = _moe(h2, xn, g2, idx, gates, ranks, counts, exp_w_gu[ly], exp_b_gu[ly],
               exp_w_down[ly], exp_b_down[ly], length)
    return out.reshape(b, length, d)
```

```python
import functools
import math

import jax
import jax.numpy as jnp
import numpy as np
from jax import lax
from jax.experimental import pallas as pl
from jax.experimental.pallas import tpu as pltpu
from jax.experimental.pallas import tpu_sc as plsc

F32 = jnp.float32
BF16 = jnp.bfloat16
HIGHEST = lax.Precision.HIGHEST

GRID_W = 64
D_HYENA = 512
FILTER_ORDER = 64
POS_EMB_DIM = 33
MIN_DECAY = math.log(1e-2) / 0.3
MAX_DECAY = math.log(1e-2) / 1.5
NOPE_DIM = 128
ROPE_DIM = 64
QK_DIM = NOPE_DIM + ROPE_DIM
V_DIM = 128
MLA_HEADS = 4
Q_RANK = 256
KV_RANK = 128
ROPE_THETA = 10000.0
N_EXPERTS = 32
TOP_K = 4
SWIGLU_ALPHA = 1.702
SWIGLU_LIMIT = 7.0
NORM_EPS = 1e-6

LANES = 128
VMEM_LIMIT = 56 * 1024 * 1024

DFT_Q = LANES
MOE_ROWS = 512
SC_WINDOW = 128


def _cparams(sem, vmem=None):
    return pltpu.CompilerParams(dimension_semantics=sem, vmem_limit_bytes=vmem)


def _const_spec(shape):
    nd = len(shape)
    return pl.BlockSpec(shape, lambda *_: (0,) * nd)


def _single_spec(shape, index_map):
    return pl.BlockSpec(shape, index_map, pipeline_mode=pl.Buffered(1))


def _mod_body(c_ref, w_ref, b_ref, o_ref):
    cc = c_ref[...]
    s = cc * jax.nn.sigmoid(cc)
    o_ref[...] = jnp.dot(s, w_ref[...], precision=HIGHEST,
                         preferred_element_type=F32) + b_ref[...]


def _adaln_table(c_rows, mod_w, mod_b):
    rows, d = c_rows.shape
    n = mod_w.shape[1]
    tn = n // 8
    return pl.pallas_call(
        _mod_body,
        grid=(n // tn,),
        in_specs=[_const_spec((rows, d)),
                  pl.BlockSpec((d, tn), lambda j: (0, j)),
                  pl.BlockSpec((1, tn), lambda j: (0, j))],
        out_specs=pl.BlockSpec((rows, tn), lambda j: (0, j)),
        out_shape=jax.ShapeDtypeStruct((rows, n), F32),
        compiler_params=_cparams(("arbitrary",)),
        name="adaln_table",
    )(c_rows, mod_w, mod_b.reshape(1, n))


def _rms(x, eps=NORM_EPS):
    return x * lax.rsqrt(jnp.mean(x * x, axis=-1, keepdims=True) + eps)


def _inproj_body(x_ref, sh_ref, sc_ref, g_ref, why_ref, wmla_ref, qng_ref, wuq_ref,
                 kvng_ref, wukv_ref, ct_ref, st_ref, ctk_ref, gq1_ref, gq2_ref, gkn_ref,
                 gkr_ref, hy_ref, q_ref, k_ref, v_ref, *, q_scale):
    tl = x_ref.shape[0]
    n_parts = 2 if tl % 32 == 0 else 1
    lane256 = lax.broadcasted_iota(jnp.int32, (1, 2 * LANES), 1)
    qmask = (lane256 < QK_DIM).astype(F32)
    lane128 = lax.broadcasted_iota(jnp.int32, (1, LANES), 1)
    pemask = (lane128 < ROPE_DIM).astype(F32)
    gq1 = gq1_ref[...]
    gq2 = gq2_ref[...]
    gkn = gkn_ref[...]

    for part in range(n_parts):
        rows = slice(part * (tl // n_parts), (part + 1) * (tl // n_parts))
        h = _rms(x_ref[rows, :]) * g_ref[...]
        h = h * (1.0 + sc_ref[...]) + sh_ref[...]
        hb = h.astype(BF16)
        hy_ref[rows, :] = jnp.dot(hb, why_ref[...], preferred_element_type=F32).astype(BF16)
        mla = jnp.dot(hb, wmla_ref[...], preferred_element_type=F32)
        cq = mla[:, :Q_RANK]
        ckv = mla[:, Q_RANK:Q_RANK + KV_RANK]
        pe2 = mla[:, Q_RANK + KV_RANK:]
        qf = jnp.dot((_rms(cq) * qng_ref[...]).astype(BF16), wuq_ref[...],
                     preferred_element_type=F32)
        kvf = jnp.dot((_rms(ckv) * kvng_ref[...]).astype(BF16), wukv_ref[...],
                      preferred_element_type=F32)
        ct = ct_ref[rows, :]
        st = st_ref[rows, :]
        kr0 = pe2 * gkr_ref[...] * ctk_ref[rows, :]
        krs = kr0 + pltpu.roll(kr0, ROPE_DIM, axis=1)
        pem = pe2 * pemask
        ss_pe = jnp.sum(pem * pem, axis=-1, keepdims=True)
        for hd in range(MLA_HEADS):
            slab = qf[:, hd * 2 * LANES:(hd + 1) * 2 * LANES]
            sm = slab * qmask
            rq = lax.rsqrt(jnp.sum(sm * sm, axis=-1, keepdims=True) / QK_DIM + NORM_EPS) * q_scale
            t = slab * gq1 * ct + pltpu.roll(slab * gq2 * st, QK_DIM, axis=1)
            q_ref[hd, rows, :] = (t * rq)[:, :QK_DIM].astype(BF16)
            kn = kvf[:, hd * 2 * LANES:hd * 2 * LANES + NOPE_DIM]
            rk = lax.rsqrt((jnp.sum(kn * kn, axis=-1, keepdims=True) + ss_pe) / QK_DIM + NORM_EPS)
            kslab = jnp.concatenate([kn * gkn, krs], axis=-1) * rk
            k_ref[hd, rows, :] = kslab[:, :QK_DIM].astype(BF16)
            v_ref[hd, rows, :] = kvf[:, hd * 2 * LANES + NOPE_DIM:(hd + 1) * 2 * LANES].astype(BF16)


@functools.lru_cache(maxsize=None)
def _rope_lane_tables(length, use_rope):
    if use_rope:
        n_freq = ROPE_DIM // 4
        t = np.arange(length)
        inv_freq = np.power(ROPE_THETA, -np.arange(n_freq, dtype=np.float64) / n_freq)
        ar = (t // GRID_W).astype(np.float64)[:, None] * inv_freq
        ac = (t % GRID_W).astype(np.float64)[:, None] * inv_freq
        c64 = np.concatenate([np.cos(ar), np.cos(ar), np.cos(ac), np.cos(ac)], axis=-1)
        s64 = np.concatenate([-np.sin(ar), np.sin(ar), -np.sin(ac), np.sin(ac)], axis=-1)
    else:
        c64 = np.ones((length, ROPE_DIM))
        s64 = np.zeros((length, ROPE_DIM))
    z64 = np.zeros((length, ROPE_DIM))
    ct = np.concatenate([np.ones((length, LANES)), c64, z64], axis=-1)
    st = np.concatenate([np.zeros((length, LANES)), z64, s64], axis=-1)
    ctk = np.concatenate([c64, s64], axis=-1)
    return ct.astype(np.float32), st.astype(np.float32), ctk.astype(np.float32)


_SWAP16 = np.concatenate([np.arange(16, 32), np.arange(0, 16), np.arange(48, 64), np.arange(32, 48)])


def _mla_weights(w_in, mla_w_uq, mla_w_ukv, qk_norm_q_g, qk_norm_k_g):
    hy_cols = 3 * D_HYENA
    w_hy = w_in[:, :hy_cols].astype(BF16)
    w_pe = w_in[:, hy_cols + Q_RANK + KV_RANK:]
    w_mla = jnp.concatenate([w_in[:, hy_cols:hy_cols + Q_RANK + KV_RANK], w_pe, w_pe[:, _SWAP16]],
                            axis=-1).astype(BF16)
    wq = mla_w_uq.reshape(Q_RANK, MLA_HEADS, QK_DIM)
    wq_rope = wq[:, :, NOPE_DIM:]
    w_uq2 = jnp.concatenate([wq[:, :, :NOPE_DIM], wq_rope, wq_rope[:, :, _SWAP16]], axis=-1)
    w_uq2 = w_uq2.reshape(Q_RANK, MLA_HEADS * 2 * LANES).astype(BF16)
    w_ukv2 = mla_w_ukv.astype(BF16)
    gq_r = qk_norm_q_g[NOPE_DIM:]
    z64 = jnp.zeros((ROPE_DIM,), F32)
    gq1 = jnp.concatenate([qk_norm_q_g[:NOPE_DIM], gq_r, z64]).reshape(1, -1)
    gq2 = jnp.concatenate([jnp.zeros((NOPE_DIM,), F32), z64, gq_r[_SWAP16]]).reshape(1, -1)
    gkn = qk_norm_k_g[:NOPE_DIM].reshape(1, -1)
    gk_r = qk_norm_k_g[NOPE_DIM:]
    gkr = jnp.concatenate([gk_r, gk_r[_SWAP16]]).reshape(1, -1)
    return w_hy, w_mla, w_uq2, w_ukv2, gq1, gq2, gkn, gkr


def _inproj(x, shift, scale, norm_g, weights, q_norm_g, kv_norm_g, use_rope, tl):
    b, length, d = x.shape
    w_hy, w_mla, w_uq2, w_ukv2, gq1, gq2, gkn, gkr = weights
    ct, st, ctk = _rope_lane_tables(length, use_rope)
    q_scale = QK_DIM ** -0.5 * math.log2(math.e)
    tok = lambda bi, i: (bi, i, 0)
    per_b = lambda bi, i: (bi, 0, 0)
    pos = lambda bi, i: (i, 0)
    hyc = w_hy.shape[1]
    return pl.pallas_call(
        functools.partial(_inproj_body, q_scale=q_scale),
        grid=(b, length // tl),
        in_specs=[pl.BlockSpec((None, tl, d), tok),
                  pl.BlockSpec((None, 1, d), per_b),
                  pl.BlockSpec((None, 1, d), per_b),
                  _const_spec((1, d)),
                  _const_spec(w_hy.shape), _const_spec(w_mla.shape),
                  _const_spec((1, Q_RANK)), _const_spec(w_uq2.shape),
                  _const_spec((1, KV_RANK)), _const_spec(w_ukv2.shape),
                  pl.BlockSpec((tl, 2 * LANES), pos), pl.BlockSpec((tl, 2 * LANES), pos),
                  pl.BlockSpec((tl, LANES), pos),
                  _const_spec((1, 2 * LANES)), _const_spec((1, 2 * LANES)),
                  _const_spec((1, LANES)), _const_spec((1, LANES))],
        out_specs=[pl.BlockSpec((None, tl, hyc), tok),
                   pl.BlockSpec((None, MLA_HEADS, tl, QK_DIM), lambda bi, i: (bi, 0, i, 0)),
                   pl.BlockSpec((None, MLA_HEADS, tl, QK_DIM), lambda bi, i: (bi, 0, i, 0)),
                   pl.BlockSpec((None, MLA_HEADS, tl, V_DIM), lambda bi, i: (bi, 0, i, 0))],
        out_shape=[jax.ShapeDtypeStruct((b, length, hyc), BF16),
                   jax.ShapeDtypeStruct((b, MLA_HEADS, length, QK_DIM), BF16),
                   jax.ShapeDtypeStruct((b, MLA_HEADS, length, QK_DIM), BF16),
                   jax.ShapeDtypeStruct((b, MLA_HEADS, length, V_DIM), BF16)],
        compiler_params=_cparams(("parallel", "parallel"), VMEM_LIMIT),
        name="inproj_mla",
    )(x, shift, scale, norm_g.reshape(1, d), w_hy, w_mla, q_norm_g.reshape(1, -1), w_uq2,
      kv_norm_g.reshape(1, -1), w_ukv2, ct, st, ctk, gq1, gq2, gkn, gkr)


def _filter_body(z_ref, w1_ref, b1_ref, w2_ref, b2_ref, w3_ref, b3_ref, w4_ref, fr_ref, dl_ref,
                 kern_ref, asum_ref, *, zero_row, tr):
    i = pl.program_id(0)
    fr = fr_ref[...]
    z = z_ref[...]
    dot = functools.partial(jnp.dot, precision=HIGHEST, preferred_element_type=F32)
    h = jnp.sin(fr * (dot(z, w1_ref[...]) + b1_ref[...]))
    h = jnp.sin(fr * (dot(h, w2_ref[...]) + b2_ref[...]))
    h = jnp.sin(fr * (dot(h, w3_ref[...]) + b3_ref[...]))
    o = dot(h, w4_ref[...]) * jnp.exp(-z[:, 0:1] * dl_ref[...])

    @pl.when(i == 0)
    def _():
        asum_ref[...] = jnp.zeros_like(asum_ref)

    asum_ref[...] += jnp.sum(jnp.abs(o), axis=0, keepdims=True)
    row = i * tr + lax.broadcasted_iota(jnp.int32, (tr, 1), 0)
    kern_ref[...] = jnp.where(row == zero_row, 0.0, o)


@functools.lru_cache(maxsize=None)
def _filter_features(length):
    n = 2 * length
    bands = (POS_EMB_DIM - 1) // 2
    pos = np.concatenate([np.arange(length), (n - np.arange(length, n)) % length])
    t_tab = np.linspace(0.0, 1.0, length)[:, None]
    w_ang = 2.0 * np.pi * np.arange(length, dtype=np.float64)[:, None] / length
    f = np.linspace(1e-4, bands - 1, bands)[None, :]
    z_tab = np.concatenate([t_tab, np.cos(f * w_ang), -np.sin(f * w_ang)], axis=-1)
    z = np.pad(z_tab[pos], ((0, 0), (0, LANES - POS_EMB_DIM)))
    return z.astype(np.float32)


def _hyena_kernel_taps(length, w1, b1, w2, b2, w3, b3, w4, freq):
    n = 2 * length
    z = jnp.asarray(_filter_features(length))
    w1p = jnp.pad(w1, ((0, LANES - POS_EMB_DIM), (0, 0)))
    deltas = jnp.abs(jnp.linspace(MIN_DECAY, MAX_DECAY, D_HYENA, dtype=F32)).reshape(1, -1)
    tr = min(1024, length)
    half_steps = length // tr
    fo = FILTER_ORDER
    kern, asum = pl.pallas_call(
        functools.partial(_filter_body, zero_row=length, tr=tr),
        grid=(n // tr,),
        in_specs=[pl.BlockSpec((tr, LANES), lambda i: (i, 0)),
                  _const_spec((LANES, fo)), _const_spec((1, fo)),
                  _const_spec((fo, fo)), _const_spec((1, fo)),
                  _const_spec((fo, fo)), _const_spec((1, fo)),
                  pl.BlockSpec((fo, D_HYENA), lambda i: (0, i // half_steps)),
                  _const_spec((1, fo)), _const_spec((1, D_HYENA))],
        out_specs=[pl.BlockSpec((tr, D_HYENA), lambda i: (i, 0)),
                   _const_spec((1, D_HYENA))],
        out_shape=[jax.ShapeDtypeStruct((n, D_HYENA), F32),
                   jax.ShapeDtypeStruct((1, D_HYENA), F32)],
        compiler_params=_cparams(("arbitrary",), VMEM_LIMIT),
        name="hyena_filter",
    )(z, w1p, b1.reshape(1, fo), w2, b2.reshape(1, fo), w3, b3.reshape(1, fo), w4,
      freq.reshape(1, fo), deltas)
    return kern, asum


def _half_rows(n):
    n_half = n // DFT_Q // 2 + 1
    return n_half, -(-n_half // 8) * 8


@functools.lru_cache(maxsize=None)
def _dft_tables(n, p_in):
    q_sz = DFT_Q
    p_sz = n // q_sz
    n_half, n_r = _half_rows(n)
    r = np.arange(n_r, dtype=np.float64)
    keep = (r < n_half).astype(np.float64)
    qq = np.arange(q_sz, dtype=np.float64)
    pp = np.arange(p_in, dtype=np.float64)
    tt = q_sz * pp[None, None, :] + qq[:, None, None]
    ang = -2.0 * np.pi * r[None, :, None] * tt / n
    t1 = np.concatenate([np.cos(ang), np.sin(ang)], axis=1) * np.tile(keep, 2)[None, :, None]
    mirror = np.where((r == 0) | (r == p_sz // 2), 1.0, 2.0)
    t4 = np.transpose(t1 * np.tile(mirror, 2)[None, :, None], (0, 2, 1)) / n
    a2 = -2.0 * np.pi * np.outer(qq, qq) / q_sz
    fre, fim = np.cos(a2), np.sin(a2)
    m2 = np.block([[fre, -fim], [fim, fre]])
    m2c = np.block([[fre, fim], [-fim, fre]])
    return (t1.astype(np.float32), t4.astype(np.float32), m2.astype(np.float32),
            m2c.astype(np.float32))


def _dft_stage1(x_ref, t1_ref, spec_ref, n_r, p_in):
    q_sz = DFT_Q

    def body(q, carry):
        xq = x_ref[pl.ds(q, p_in, stride=q_sz), :].astype(BF16)
        a = jnp.dot(t1_ref[q], xq, preferred_element_type=F32)
        spec_ref[pl.ds(q, n_r, stride=2 * q_sz), :] = a[:n_r]
        spec_ref[pl.ds(q_sz + q, n_r, stride=2 * q_sz), :] = a[n_r:]
        return carry

    lax.fori_loop(0, q_sz, body, 0, unroll=8)


def _spectrum_body(kern_ref, asum_ref, t1_ref, m2_ref, kf_ref, spec_ref, *, n_r, p_in):
    q_sz = DFT_Q
    _dft_stage1(kern_ref, t1_ref, spec_ref, n_r, p_in)
    inv = 1.0 / asum_ref[...]
    m2 = m2_ref[...]

    def body(r, carry):
        blk = spec_ref[pl.ds(pl.multiple_of(r * 2 * q_sz, 2 * q_sz), 2 * q_sz), :]
        xf = jnp.dot(m2, blk.astype(BF16), preferred_element_type=F32)
        kf_ref[r] = (xf * inv).astype(BF16)
        return carry

    lax.fori_loop(0, n_r, body, 0, unroll=2)


def _hyena_filter_spectrum(kern, asum):
    n, c = kern.shape
    p_sz = n // DFT_Q
    _, n_r = _half_rows(n)
    t1, _, m2, _ = _dft_tables(n, p_sz)
    t1 = jnp.asarray(t1).astype(BF16)
    m2 = jnp.asarray(m2).astype(BF16)
    ct = LANES
    return pl.pallas_call(
        functools.partial(_spectrum_body, n_r=n_r, p_in=p_sz),
        grid=(c // ct,),
        in_specs=[_single_spec((n, ct), lambda j: (0, j)),
                  pl.BlockSpec((1, ct), lambda j: (0, j)),
                  _single_spec(t1.shape, lambda j: (0, 0, 0)),
                  _const_spec(m2.shape)],
        out_specs=pl.BlockSpec((n_r, 2 * DFT_Q, ct), lambda j: (0, 0, j)),
        out_shape=jax.ShapeDtypeStruct((n_r, 2 * DFT_Q, c), BF16),
        scratch_shapes=[pltpu.VMEM((n_r * 2 * DFT_Q, ct), F32)],
        compiler_params=_cparams(("parallel",), VMEM_LIMIT),
        name="hyena_filter_spectrum",
    )(kern, asum, t1, m2)


def _short_conv_chunk(u_ref, w_ref, b_ref, i, rows, length):
    pack = 16
    base = pl.multiple_of(i * rows, rows)
    u = u_ref[pl.ds(base, rows), :].astype(F32)
    lo = pl.multiple_of(jnp.maximum(base - pack, 0), pack)
    hi = pl.multiple_of(jnp.minimum(base + rows, length - pack), pack)
    prev = u_ref[pl.ds(lo, pack), :].astype(F32)[pack - 1:pack]
    nxt = u_ref[pl.ds(hi, pack), :].astype(F32)[0:1]
    prev = jnp.where(base == 0, 0.0, prev)
    nxt = jnp.where(base + rows == length, 0.0, nxt)
    ridx = lax.broadcasted_iota(jnp.int32, (rows, 1), 0)
    up = jnp.where(ridx == 0, prev, pltpu.roll(u, 1, axis=0))
    dn = jnp.where(ridx == rows - 1, nxt, pltpu.roll(u, rows - 1, axis=0))
    w = w_ref[...]
    return up * w[0:1] + u * w[1:2] + dn * w[2:3] + b_ref[...]


def _hyena_conv_body(x0_ref, x1_ref, v_ref, w0_ref, w1_ref, wv_ref, b0_ref, b1_ref, bv_ref,
                     bias_ref, kf_ref, t1_ref, t4_ref, m2_ref, m2c_ref, o_ref, vx_ref, spec_ref,
                     *, length, rows):
    q_sz = DFT_Q
    n_half, n_r = _half_rows(2 * length)
    p_in = length // q_sz
    n_chunks = length // rows

    def gate_in(i, carry):
        x1c = _short_conv_chunk(x1_ref, w1_ref, b1_ref, i, rows, length)
        vc = _short_conv_chunk(v_ref, wv_ref, bv_ref, i, rows, length)
        vx_ref[pl.ds(pl.multiple_of(i * rows, rows), rows), :] = vc * x1c
        return carry

    lax.fori_loop(0, n_chunks, gate_in, 0)
    _dft_stage1(vx_ref, t1_ref, spec_ref, n_r, p_in)
    m2 = m2_ref[...]
    m2c = m2c_ref[...]

    def mid(r, carry):
        sl = pl.ds(pl.multiple_of(r * 2 * q_sz, 2 * q_sz), 2 * q_sz)
        xf = jnp.dot(m2, spec_ref[sl, :].astype(BF16), preferred_element_type=F32)
        kf = kf_ref[r].astype(F32)
        xre, xim = xf[:q_sz], xf[q_sz:]
        kre, kim = kf[:q_sz], kf[q_sz:]
        z = jnp.concatenate([xre * kre - xim * kim, xre * kim + xim * kre], axis=0)
        spec_ref[sl, :] = jnp.dot(m2c, z.astype(BF16), preferred_element_type=F32)
        return carry

    lax.fori_loop(0, n_half, mid, 0, unroll=8)
    bias = bias_ref[...]

    def last(q, carry):
        bre = spec_ref[pl.ds(q, n_r, stride=2 * q_sz), :]
        bim = spec_ref[pl.ds(q_sz + q, n_r, stride=2 * q_sz), :]
        bq = jnp.concatenate([bre, bim], axis=0).astype(BF16)
        y = jnp.dot(t4_ref[q], bq, preferred_element_type=F32)
        sl = pl.ds(q, p_in, stride=q_sz)
        vx_ref[sl, :] = y + bias * vx_ref[sl, :]
        return carry

    lax.fori_loop(0, q_sz, last, 0, unroll=8)

    def gate_out(i, carry):
        x0c = _short_conv_chunk(x0_ref, w0_ref, b0_ref, i, rows, length)
        sl = pl.ds(pl.multiple_of(i * rows, rows), rows)
        o_ref[sl, :] = (vx_ref[sl, :] * x0c).astype(BF16)
        return carry

    lax.fori_loop(0, n_chunks, gate_out, 0)


def _hyena_conv(hy, conv_w, conv_b, bias, kf):
    b, length, _ = hy.shape
    ch = D_HYENA
    ct = LANES
    nct = ch // ct
    n = 2 * length
    _, n_r = _half_rows(n)
    p_in = length // DFT_Q
    t1, t4, m2, m2c = _dft_tables(n, p_in)
    t1, t4, m2, m2c = (jnp.asarray(a).astype(BF16) for a in (t1, t4, m2, m2c))
    rows = min(512, length)
    col = lambda off: (lambda j, bi: (bi, 0, off * nct + j))
    wcol = lambda off: (lambda j, bi: (0, off * nct + j))
    conv_b2 = conv_b.reshape(1, -1)
    return pl.pallas_call(
        functools.partial(_hyena_conv_body, length=length, rows=rows),
        grid=(nct, b),
        in_specs=[_single_spec((None, length, ct), col(0)),
                  _single_spec((None, length, ct), col(1)),
                  _single_spec((None, length, ct), col(2)),
                  pl.BlockSpec((3, ct), wcol(0)), pl.BlockSpec((3, ct), wcol(1)),
                  pl.BlockSpec((3, ct), wcol(2)),
                  pl.BlockSpec((1, ct), wcol(0)), pl.BlockSpec((1, ct), wcol(1)),
                  pl.BlockSpec((1, ct), wcol(2)),
                  pl.BlockSpec((1, ct), lambda j, bi: (0, j)),
                  _single_spec((n_r, 2 * DFT_Q, ct), lambda j, bi: (0, 0, j)),
                  _single_spec(t1.shape, lambda j, bi: (0, 0, 0)),
                  _single_spec(t4.shape, lambda j, bi: (0, 0, 0)),
                  _const_spec(m2.shape), _const_spec(m2c.shape)],
        out_specs=pl.BlockSpec((None, length, ct), lambda j, bi: (bi, 0, j)),
        out_shape=jax.ShapeDtypeStruct((b, length, ch), BF16),
        scratch_shapes=[pltpu.VMEM((length, ct), F32),
                        pltpu.VMEM((n_r * 2 * DFT_Q, ct), F32)],
        compiler_params=_cparams(("parallel", "parallel"), VMEM_LIMIT),
        name="hyena_conv",
    )(hy, hy, hy, conv_w, conv_w, conv_w, conv_b2, conv_b2, conv_b2, bias.reshape(1, ch), kf,
      t1, t4, m2, m2c)


def _flash_body(q_ref, k_ref, v_ref, kc_ref, vc_ref, o_ref, sa_ref, sb_ref, p_ref, m_ref, l_ref,
                acc_ref, *, tk, n_sub, rg):
    tq = q_ref.shape[0]
    sub = tq // n_sub
    nk = k_ref.shape[0] // tk
    nc = kc_ref.shape[0]
    nt = (((1,), (1,)), ((), ()))

    def scores(j, dst_ref):
        off = pl.multiple_of(jnp.minimum(j, nk - 1) * tk, tk)
        dst_ref[...] = lax.dot_general(q_ref[...], k_ref[pl.ds(off, tk), :], nt,
                                       preferred_element_type=F32)

    def softmax_pv(src_ref, v, width):
        for i in range(n_sub):
            for g in range(sub // rg):
                rows = slice(i * sub + g * rg, i * sub + (g + 1) * rg)
                s = src_ref[rows, :width]
                m_prev = m_ref[rows, :]
                m_new = jnp.maximum(m_prev, jnp.max(s, axis=-1, keepdims=True))
                alpha = jnp.exp2(m_prev - m_new)
                p = jnp.exp2(s - jnp.concatenate([m_new] * (width // LANES), axis=1))
                l_ref[rows, :] = alpha * l_ref[rows, :] + jnp.sum(p, axis=-1, keepdims=True)
                m_ref[rows, :] = m_new
                acc_ref[rows, :] = alpha * acc_ref[rows, :]
                p_ref[rows, :width] = p.astype(BF16)
            srows = slice(i * sub, (i + 1) * sub)
            acc_ref[srows, :] += jnp.dot(p_ref[srows, :width], v, preferred_element_type=F32)

    def body(jj, carry):
        j = 2 * jj
        scores(j + 1, sb_ref)
        softmax_pv(sa_ref, v_ref[pl.ds(pl.multiple_of(j * tk, tk), tk), :], tk)
        scores(j + 2, sa_ref)
        softmax_pv(sb_ref, v_ref[pl.ds(pl.multiple_of((j + 1) * tk, tk), tk), :], tk)
        return carry

    m_ref[...] = jnp.full_like(m_ref, -jnp.inf)
    l_ref[...] = jnp.zeros_like(l_ref)
    acc_ref[...] = jnp.zeros_like(acc_ref)
    scores(0, sa_ref)
    lax.fori_loop(0, nk // 2, body, 0, unroll=2 if (nk // 2) % 2 == 0 else 1)
    sa_ref[:, :nc] = lax.dot_general(q_ref[...], kc_ref[...], nt, preferred_element_type=F32)
    softmax_pv(sa_ref, vc_ref[...], nc)
    o_ref[...] = (acc_ref[...] / l_ref[...]).astype(o_ref.dtype)


def _flash_attention(q, k, v, kc, vc, tq, tk, n_sub, rg):
    b, hds, length, _ = q.shape
    nc = kc.shape[2]
    assert (length // tk) % 2 == 0 and nc <= tk and nc % LANES == 0 and V_DIM == LANES
    return pl.pallas_call(
        functools.partial(_flash_body, tk=tk, n_sub=n_sub, rg=rg),
        grid=(b, hds, length // tq),
        in_specs=[pl.BlockSpec((None, None, tq, QK_DIM), lambda bi, h, qi: (bi, h, qi, 0)),
                  pl.BlockSpec((None, None, length, QK_DIM), lambda bi, h, qi: (bi, h, 0, 0)),
                  pl.BlockSpec((None, None, length, V_DIM), lambda bi, h, qi: (bi, h, 0, 0)),
                  pl.BlockSpec((None, None, nc, QK_DIM), lambda bi, h, qi: (bi, h, 0, 0)),
                  pl.BlockSpec((None, None, nc, V_DIM), lambda bi, h, qi: (bi, h, 0, 0))],
        out_specs=pl.BlockSpec((None, tq, V_DIM), lambda bi, h, qi: (bi, qi, h)),
        out_shape=jax.ShapeDtypeStruct((b, length, hds * V_DIM), BF16),
        scratch_shapes=[pltpu.VMEM((tq, tk), F32), pltpu.VMEM((tq, tk), F32),
                        pltpu.VMEM((tq, tk), BF16), pltpu.VMEM((tq, LANES), F32),
                        pltpu.VMEM((tq, LANES), F32), pltpu.VMEM((tq, V_DIM), F32)],
        compiler_params=_cparams(("parallel", "parallel", "parallel"), VMEM_LIMIT),
        name="mla_flash_attention",
    )(q, k, v, kc, vc)


def _store_packed_rows(dst_ref, x):
    half = x.shape[1] // 2
    for j in range(half // LANES):
        hi = x[:, j * LANES:(j + 1) * LANES].astype(BF16).astype(F32)
        lo = x[:, half + j * LANES:half + (j + 1) * LANES].astype(BF16).astype(F32)
        dst_ref[j] = (lax.bitcast_convert_type(hi, jnp.uint32)
                      | (lax.bitcast_convert_type(lo, jnp.uint32) >> 16))


def _unpack_words(w):
    hi = lax.bitcast_convert_type(w & jnp.uint32(0xFFFF0000), F32)
    lo = lax.bitcast_convert_type(w << 16, F32)
    return hi, lo


def _load_packed_rows(src_ref):
    parts = [_unpack_words(src_ref[j]) for j in range(src_ref.shape[0])]
    return jnp.concatenate([p[0] for p in parts] + [p[1] for p in parts], axis=-1)


def _outproj_body(yh_ref, ya_ref, x_ref, g1_ref, sh2_ref, sc2_ref, n2g_ref, wo1_ref, wo2_ref,
                  rwt_ref, rb_ref, tri_ref, xn_ref, h2_ref, idx_ref, gate_ref, rank_ref, cnt_ref,
                  carry_sc):
    i = pl.program_id(0)

    @pl.when(i == 0)
    def _():
        carry_sc[...] = jnp.zeros_like(carry_sc)

    mix = (jnp.dot(yh_ref[...], wo1_ref[...], preferred_element_type=F32)
           + jnp.dot(ya_ref[...], wo2_ref[...], preferred_element_type=F32))
    xn = x_ref[...] + g1_ref[...] * mix
    xn_ref[...] = xn
    h2 = _rms(xn) * n2g_ref[...]
    h2 = h2 * (1.0 + sc2_ref[...]) + sh2_ref[...]
    _store_packed_rows(h2_ref, h2)
    logits = lax.dot_general(rwt_ref[...], h2, (((1,), (1,)), ((), ())), precision=HIGHEST,
                             preferred_element_type=F32) + rb_ref[...]
    n_e, tt = logits.shape
    eidx = lax.broadcasted_iota(jnp.int32, (n_e, tt), 0).astype(F32)
    work = logits
    vals, sels, idxs = [], [], []
    for _ in range(TOP_K):
        m = jnp.max(work, axis=0, keepdims=True)
        ix = jnp.min(jnp.where(work == m, eidx, float(n_e)), axis=0, keepdims=True)
        sel = eidx == ix
        work = jnp.where(sel, -jnp.inf, work)
        vals.append(m)
        idxs.append(ix)
        sels.append(sel)
    es = [jnp.exp(vk - vals[0]) for vk in vals]
    den = es[0] + es[1] + es[2] + es[3]
    gate_ref[...] = jnp.concatenate(es, axis=0) / den
    idx_ref[...] = jnp.concatenate(idxs, axis=0).astype(jnp.int32)
    onehot = jnp.zeros((n_e, tt), F32)
    for sel in sels:
        onehot = onehot + sel.astype(F32)
    prefix = jnp.dot(onehot.astype(BF16), tri_ref[...], preferred_element_type=F32) + carry_sc[...]
    ranks = [jnp.sum(jnp.where(sel, prefix, 0.0), axis=0, keepdims=True) for sel in sels]
    rank_ref[...] = jnp.concatenate(ranks, axis=0).astype(jnp.int32)
    carry_sc[...] += jnp.sum(onehot, axis=1, keepdims=True)
    cnt_ref[...] = jnp.broadcast_to(carry_sc[...], cnt_ref.shape)


def _outproj_router(y_hy, y_att, x2, g1, sh2, sc2, norm2_g, w_out, router_w, router_b, tokens_per_batch,
                    tt):
    t, d = x2.shape
    ch = y_hy.shape[1]
    n_e = router_w.shape[1]
    wo1 = w_out[:ch].astype(BF16)
    wo2 = w_out[ch:].astype(BF16)
    tri = jnp.asarray(np.triu(np.ones((tt, tt), np.float32), k=1), BF16)
    steps_per_batch = tokens_per_batch // tt
    tok = lambda i: (i, 0)
    per_b = lambda i: (i // steps_per_batch, 0, 0)
    lanes_tok = lambda i: (0, i)
    return pl.pallas_call(
        _outproj_body,
        grid=(t // tt,),
        in_specs=[pl.BlockSpec((tt, ch), tok), pl.BlockSpec((tt, ch), tok),
                  pl.BlockSpec((tt, d), tok),
                  pl.BlockSpec((None, 1, d), per_b), pl.BlockSpec((None, 1, d), per_b),
                  pl.BlockSpec((None, 1, d), per_b),
                  _const_spec((1, d)), _const_spec(wo1.shape), _const_spec(wo2.shape),
                  _const_spec((n_e, d)), _const_spec((n_e, 1)), _const_spec((tt, tt))],
        out_specs=[pl.BlockSpec((tt, d), tok),
                   pl.BlockSpec((d // (2 * LANES), tt, LANES), lambda i: (0, i, 0)),
                   pl.BlockSpec((TOP_K, tt), lanes_tok), pl.BlockSpec((TOP_K, tt), lanes_tok),
                   pl.BlockSpec((TOP_K, tt), lanes_tok), _const_spec((n_e, LANES))],
        out_shape=[jax.ShapeDtypeStruct((t, d), F32),
                   jax.ShapeDtypeStruct((d // (2 * LANES), t, LANES), jnp.uint32),
                   jax.ShapeDtypeStruct((TOP_K, t), jnp.int32),
                   jax.ShapeDtypeStruct((TOP_K, t), F32),
                   jax.ShapeDtypeStruct((TOP_K, t), jnp.int32),
                   jax.ShapeDtypeStruct((n_e, LANES), F32)],
        scratch_shapes=[pltpu.VMEM((n_e, 1), F32)],
        compiler_params=_cparams(("arbitrary",), VMEM_LIMIT),
        name="outproj_router",
    )(y_hy, y_att, x2, g1, sh2, sc2, norm2_g.reshape(1, d), wo1, wo2, router_w.T,
      router_b.reshape(n_e, 1), tri)


def _cast_rows(src_ref, dst_ref, chunk):
    def body(c, carry):
        sl = pl.ds(pl.multiple_of(c * chunk, chunk), chunk)
        dst_ref[sl, :] = src_ref[sl, :].astype(dst_ref.dtype)
        return carry

    lax.fori_loop(0, src_ref.shape[0] // chunk, body, 0)


def _expert_body(be_ref, nvalid_ref, xs_ref, wgu_ref, bgu_ref, wd_ref, bd_ref, ys_ref, wgu_bf, wd_bf):
    i = pl.program_id(0)
    n_valid = nvalid_ref[i]
    active = n_valid > 0
    new_expert = jnp.logical_or(i == 0, be_ref[i] != be_ref[jnp.maximum(i - 1, 0)])

    @pl.when(jnp.logical_and(active, new_expert))
    def _():
        _cast_rows(wgu_ref, wgu_bf, 128)
        _cast_rows(wd_ref, wd_bf, 128)

    @pl.when(active)
    def _():
        row = lax.broadcasted_iota(jnp.int32, (xs_ref.shape[1], 1), 0)
        xs = jnp.where(row < n_valid, _load_packed_rows(xs_ref), 0.0).astype(BF16)
        gu = jnp.dot(xs, wgu_bf[...], preferred_element_type=F32) + bgu_ref[...]
        dff = gu.shape[1] // 2
        gate = jnp.minimum(gu[:, :dff], SWIGLU_LIMIT)
        up = jnp.clip(gu[:, dff:], -SWIGLU_LIMIT, SWIGLU_LIMIT)
        act = (up + 1.0) * (gate * jax.nn.sigmoid(SWIGLU_ALPHA * gate))
        ys = jnp.dot(act.astype(BF16), wd_bf[...], preferred_element_type=F32) + bd_ref[...]
        _store_packed_rows(ys_ref, ys)

    @pl.when(jnp.logical_not(active))
    def _():
        ys_ref[...] = jnp.zeros_like(ys_ref)


def _expert_blocks(xs, block_e, n_valid, w_gu, b_gu, w_down, b_down):
    n_seg, n_rows, _ = xs.shape
    n_e, d, dff2 = w_gu.shape
    bm = MOE_ROWS
    seg_block = pl.BlockSpec((n_seg, bm, LANES), lambda i, be, nu: (0, i, 0))
    grid_spec = pltpu.PrefetchScalarGridSpec(
        num_scalar_prefetch=2,
        grid=(n_rows // bm,),
        in_specs=[seg_block,
                  pl.BlockSpec((None, d, dff2), lambda i, be, nu: (be[i], 0, 0)),
                  pl.BlockSpec((None, 1, dff2), lambda i, be, nu: (be[i], 0, 0)),
                  pl.BlockSpec((None, dff2 // 2, d), lambda i, be, nu: (be[i], 0, 0)),
                  pl.BlockSpec((None, 1, d), lambda i, be, nu: (be[i], 0, 0))],
        out_specs=seg_block,
        scratch_shapes=[pltpu.VMEM((d, dff2), BF16), pltpu.VMEM((dff2 // 2, d), BF16)],
    )
    return pl.pallas_call(
        _expert_body,
        grid_spec=grid_spec,
        out_shape=jax.ShapeDtypeStruct(xs.shape, jnp.uint32),
        compiler_params=_cparams(("arbitrary",), VMEM_LIMIT),
        name="moe_experts",
    )(block_e, n_valid, xs, w_gu, b_gu.reshape(n_e, 1, dff2), w_down, b_down.reshape(n_e, 1, d))


def _sc_gather(table, idx):
    n = idx.shape[0]
    width = table.shape[1]
    mesh = plsc.VectorSubcoreMesh(core_axis_name="core", subcore_axis_name="subcore")
    n_workers = mesh.num_cores * mesh.num_subcores
    assert width == LANES and n % (SC_WINDOW * n_workers) == 0

    @functools.partial(pl.kernel, out_type=jax.ShapeDtypeStruct((n, width), table.dtype), mesh=mesh)
    def gather_kernel(table_hbm, idx_hbm, out_hbm):
        def body(idx_vmem, out_vmem):
            pltpu.sync_copy(table_hbm.at[idx_vmem.at[0]], out_vmem)

        pltpu.emit_pipeline(
            body,
            grid=(n // SC_WINDOW,),
            in_specs=[pl.BlockSpec((1, SC_WINDOW), lambda i: (0, i))],
            out_specs=[pl.BlockSpec((SC_WINDOW, width), lambda i: (i, 0))],
            core_axis_name=("core", "subcore"),
            dimension_semantics=(pltpu.PARALLEL,),
        )(idx_hbm, out_hbm)

    return gather_kernel(table, idx.reshape(1, n))


def _sc_scatter(rows, idx, n_copies, n_out):
    n, width = rows.shape
    mesh = plsc.VectorSubcoreMesh(core_axis_name="core", subcore_axis_name="subcore")
    n_workers = mesh.num_cores * mesh.num_subcores
    assert width == LANES and n % (SC_WINDOW * n_workers) == 0 and idx.shape == (8, n)

    @functools.partial(pl.kernel, out_type=jax.ShapeDtypeStruct((n_out, width), rows.dtype), mesh=mesh)
    def scatter_kernel(rows_hbm, idx_hbm, out_hbm):
        def body(rows_vmem, idx_vmem):
            for k in range(n_copies):
                pltpu.sync_copy(rows_vmem, out_hbm.at[idx_vmem.at[k]])

        pltpu.emit_pipeline(
            body,
            grid=(n // SC_WINDOW,),
            in_specs=[pl.BlockSpec((SC_WINDOW, width), lambda i: (i, 0)),
                      pl.BlockSpec((8, SC_WINDOW), lambda i: (0, i))],
            out_specs=[],
            core_axis_name=("core", "subcore"),
            dimension_semantics=(pltpu.PARALLEL,),
        )(rows_hbm, idx_hbm)

    return scatter_kernel(rows, idx)


def _combine_body(pk_ref, gt_ref, xn_ref, g2_ref, o_ref):
    n_seg = pk_ref.shape[0]
    half = o_ref.shape[1] // 2
    gt = gt_ref[...]
    g2 = g2_ref[...]
    for j in range(n_seg):
        acc_hi = None
        acc_lo = None
        for kk in range(TOP_K):
            hi, lo = _unpack_words(pk_ref[j, kk])
            g = gt[:, kk:kk + 1]
            acc_hi = g * hi if acc_hi is None else acc_hi + g * hi
            acc_lo = g * lo if acc_lo is None else acc_lo + g * lo
        c_hi = slice(j * LANES, (j + 1) * LANES)
        c_lo = slice(half + j * LANES, half + (j + 1) * LANES)
        o_ref[:, c_hi] = xn_ref[:, c_hi] + g2[:, c_hi] * acc_hi
        o_ref[:, c_lo] = xn_ref[:, c_lo] + g2[:, c_lo] * acc_lo


def _combine(picked, gates_t, xn, g2, tokens_per_batch, tt):
    n_seg, _, t, _ = picked.shape
    d = xn.shape[1]
    steps_per_batch = tokens_per_batch // tt
    return pl.pallas_call(
        _combine_body,
        grid=(t // tt,),
        in_specs=[pl.BlockSpec((n_seg, TOP_K, tt, LANES), lambda i: (0, 0, i, 0)),
                  pl.BlockSpec((tt, TOP_K), lambda i: (i, 0)),
                  pl.BlockSpec((tt, d), lambda i: (i, 0)),
                  pl.BlockSpec((None, 1, d), lambda i: (i // steps_per_batch, 0, 0))],
        out_specs=pl.BlockSpec((tt, d), lambda i: (i, 0)),
        out_shape=jax.ShapeDtypeStruct((t, d), F32),
        compiler_params=_cparams(("parallel",), VMEM_LIMIT),
        name="moe_combine",
    )(picked, gates_t, xn, g2)


def _moe(h2p, xn, g2, idx, gates, ranks, counts, w_gu, b_gu, w_down, b_down, tokens_per_batch):
    n_seg, t, _ = h2p.shape
    bm = MOE_ROWS
    n_e = w_gu.shape[0]
    cnt = counts[:, 0].astype(jnp.int32)
    padded = (cnt + bm - 1) // bm * bm
    padded_ends = jnp.cumsum(padded)
    padded_starts = padded_ends - padded
    experts = jnp.arange(n_e, dtype=jnp.int32)[:, None, None]
    dest = ranks + jnp.sum(jnp.where(idx[None] == experts, padded_starts[:, None, None], 0), axis=0)
    n_blocks = t * TOP_K // bm + n_e
    n_rows = n_blocks * bm
    block_start = jnp.arange(n_blocks, dtype=jnp.int32) * bm
    block_e = jnp.minimum(jnp.sum(padded_ends[None, :] <= block_start[:, None], axis=1),
                          n_e - 1).astype(jnp.int32)
    n_valid = jnp.clip(cnt[block_e] - (block_start - padded_starts[block_e]), 0, bm).astype(jnp.int32)
    seg = jnp.arange(n_seg, dtype=jnp.int32)
    scatter_idx = (seg[None, :, None] * n_rows + dest[:, None, :]).reshape(TOP_K, n_seg * t)
    scatter_idx = jnp.concatenate([scatter_idx, scatter_idx], axis=0)
    xs = _sc_scatter(h2p.reshape(n_seg * t, LANES), scatter_idx, TOP_K, n_seg * n_rows)
    ys = _expert_blocks(xs.reshape(n_seg, n_rows, LANES), block_e, n_valid, w_gu, b_gu, w_down, b_down)
    picked = _sc_gather(ys.reshape(n_seg * n_rows, LANES),
                        (seg[:, None, None] * n_rows + dest[None]).reshape(-1))
    return _combine(picked.reshape(n_seg, TOP_K, t, LANES), gates.T, xn, g2, tokens_per_batch,
                    min(512, tokens_per_batch))


def kernel(x, c, ctx, c_ctx, mod_w, mod_b, norm1_g, w_in, hy_conv_w, hy_conv_b, hy_f_w1, hy_f_b1,
           hy_f_w2, hy_f_b2, hy_f_w3, hy_f_b3, hy_f_w4, hy_f_freq, hy_bias, mla_q_norm_g, mla_w_uq,
           mla_kv_norm_g, mla_w_ukv, qk_norm_q_g, qk_norm_k_g, w_out, norm2_g, router_w, router_b,
           exp_w_gu, exp_b_gu, exp_w_down, exp_b_down):
    b, length, d = x.shape
    depth = mod_w.shape[0]
    assert depth == 1, "single-layer kernel"
    ly = 0
    c_rows = jnp.concatenate([c, c_ctx[None, :], jnp.zeros((8 - b - 1, d), F32)], axis=0)
    mod = _adaln_table(c_rows, mod_w[ly], mod_b[ly])
    mod6 = mod.reshape(8, 6, d)
    sh1, sc1, g1, sh2, sc2, g2 = (mod6[:b, j][:, None, :] for j in range(6))
    csh1 = mod6[b:b + 1, 0][:, None, :]
    csc1 = mod6[b:b + 1, 1][:, None, :]

    weights = _mla_weights(w_in[ly], mla_w_uq[ly], mla_w_ukv[ly], qk_norm_q_g[ly], qk_norm_k_g[ly])
    n_ctx = ctx.shape[1]
    _, _, k_c, v_c = _inproj(ctx, jnp.broadcast_to(csh1, (b, 1, d)), jnp.broadcast_to(csc1, (b, 1, d)),
                             norm1_g[ly], weights, mla_q_norm_g[ly], mla_kv_norm_g[ly], False, n_ctx)
    hy, q, k, v = _inproj(x, sh1, sc1, norm1_g[ly], weights, mla_q_norm_g[ly], mla_kv_norm_g[ly],
                          True, min(512, length))

    kern, asum = _hyena_kernel_taps(length, hy_f_w1[ly], hy_f_b1[ly], hy_f_w2[ly], hy_f_b2[ly],
                                    hy_f_w3[ly], hy_f_b3[ly], hy_f_w4[ly], hy_f_freq[ly])
    kf = _hyena_filter_spectrum(kern, asum)
    y_hy = _hyena_conv(hy, hy_conv_w[ly], hy_conv_b[ly], hy_bias[ly], kf)

    y_att = _flash_attention(q, k, v, k_c, v_c, min(512, length), min(512, length), 2, 32)

    t = b * length
    xn, h2, idx, gates, ranks, counts = _outproj_router(
        y_hy.reshape(t, -1), y_att.reshape(t, -1), x.reshape(t, d), g1, sh2, sc2, norm2_g[ly],
        w_out[ly], router_w[ly], router_b[ly], length, min(512, length))
    out = _moe(h2, xn, g2, idx, gates, ranks, counts, exp_w_gu[ly], exp_b_gu[ly],
               exp_w_down[ly], exp_b_down[ly], length)
    return out.reshape(b, length, d)
```

```python
import functools
import math

import jax
import jax.numpy as jnp
import numpy as np
from jax import lax
from jax.experimental import pallas as pl
from jax.experimental.pallas import tpu as pltpu
from jax.experimental.pallas import tpu_sc as plsc

F32 = jnp.float32
BF16 = jnp.bfloat16
HIGHEST = lax.Precision.HIGHEST

GRID_W = 64
D_HYENA = 512
FILTER_ORDER = 64
POS_EMB_DIM = 33
MIN_DECAY = math.log(1e-2) / 0.3
MAX_DECAY = math.log(1e-2) / 1.5
NOPE_DIM = 128
ROPE_DIM = 64
QK_DIM = NOPE_DIM + ROPE_DIM
V_DIM = 128
MLA_HEADS = 4
Q_RANK = 256
KV_RANK = 128
ROPE_THETA = 10000.0
N_EXPERTS = 32
TOP_K = 4
SWIGLU_ALPHA = 1.702
SWIGLU_LIMIT = 7.0
NORM_EPS = 1e-6

LANES = 128
VMEM_LIMIT = 56 * 1024 * 1024

DFT_Q = LANES
MOE_ROWS = 512
SC_WINDOW = 128


def _cparams(sem, vmem=None):
    return pltpu.CompilerParams(dimension_semantics=sem, vmem_limit_bytes=vmem)


def _const_spec(shape):
    nd = len(shape)
    return pl.BlockSpec(shape, lambda *_: (0,) * nd)


def _single_spec(shape, index_map):
    return pl.BlockSpec(shape, index_map, pipeline_mode=pl.Buffered(1))


def _mod_body(c_ref, w_ref, b_ref, o_ref):
    cc = c_ref[...]
    s = cc * jax.nn.sigmoid(cc)
    o_ref[...] = jnp.dot(s, w_ref[...], precision=HIGHEST,
                         preferred_element_type=F32) + b_ref[...]


def _adaln_table(c_rows, mod_w, mod_b):
    rows, d = c_rows.shape
    n = mod_w.shape[1]
    tn = n // 8
    return pl.pallas_call(
        _mod_body,
        grid=(n // tn,),
        in_specs=[_const_spec((rows, d)),
                  pl.BlockSpec((d, tn), lambda j: (0, j)),
                  pl.BlockSpec((1, tn), lambda j: (0, j))],
        out_specs=pl.BlockSpec((rows, tn), lambda j: (0, j)),
        out_shape=jax.ShapeDtypeStruct((rows, n), F32),
        compiler_params=_cparams(("arbitrary",)),
        name="adaln_table",
    )(c_rows, mod_w, mod_b.reshape(1, n))


def _rms(x, eps=NORM_EPS):
    return x * lax.rsqrt(jnp.mean(x * x, axis=-1, keepdims=True) + eps)


def _inproj_body(x_ref, sh_ref, sc_ref, g_ref, why_ref, wmla_ref, qng_ref, wuq_ref,
                 kvng_ref, wukv_ref, ct_ref, st_ref, ctk_ref, gq1_ref, gq2_ref, gkn_ref,
                 gkr_ref, hy_ref, q_ref, k_ref, v_ref, mla_sc, *, q_scale):
    i = pl.program_id(0)

    @pl.when(i == 0)
    def _():
        mla_sc[...] = jnp.zeros_like(mla_sc)

    mla = mla_sc[...]
    cq = mla[:, :Q_RANK]
    ckv = mla[:, Q_RANK:Q_RANK + KV_RANK]
    pe2 = mla[:, Q_RANK + KV_RANK:]
    qf = jnp.dot((_rms(cq) * qng_ref[...]).astype(BF16), wuq_ref[...],
                 preferred_element_type=F32)
    kvf = jnp.dot((_rms(ckv) * kvng_ref[...]).astype(BF16), wukv_ref[...],
                  preferred_element_type=F32)
    ct = ct_ref[...]
    st = st_ref[...]
    lane256 = lax.broadcasted_iota(jnp.int32, (1, 2 * LANES), 1)
    qmask = (lane256 < QK_DIM).astype(F32)
    lane128 = lax.broadcasted_iota(jnp.int32, (1, LANES), 1)
    pemask = (lane128 < ROPE_DIM).astype(F32)
    kr0 = pe2 * gkr_ref[...] * ctk_ref[...]
    krs = kr0 + pltpu.roll(kr0, ROPE_DIM, axis=1)
    pem = pe2 * pemask
    ss_pe = jnp.sum(pem * pem, axis=-1, keepdims=True)
    gq1 = gq1_ref[...]
    gq2 = gq2_ref[...]
    gkn = gkn_ref[...]
    for hd in range(MLA_HEADS):
        slab = qf[:, hd * 2 * LANES:(hd + 1) * 2 * LANES]
        sm = slab * qmask
        rq = lax.rsqrt(jnp.sum(sm * sm, axis=-1, keepdims=True) / QK_DIM + NORM_EPS) * q_scale
        t = slab * gq1 * ct + pltpu.roll(slab * gq2 * st, QK_DIM, axis=1)
        q_ref[hd] = (t * rq)[:, :QK_DIM].astype(BF16)
        kn = kvf[:, hd * 2 * LANES:hd * 2 * LANES + NOPE_DIM]
        rk = lax.rsqrt((jnp.sum(kn * kn, axis=-1, keepdims=True) + ss_pe) / QK_DIM + NORM_EPS)
        kslab = jnp.concatenate([kn * gkn, krs], axis=-1) * rk
        k_ref[hd] = kslab[:, :QK_DIM].astype(BF16)
        v_ref[hd] = kvf[:, hd * 2 * LANES + NOPE_DIM:(hd + 1) * 2 * LANES].astype(BF16)

    h = _rms(x_ref[...]) * g_ref[...]
    h = h * (1.0 + sc_ref[...]) + sh_ref[...]
    hb = h.astype(BF16)
    hy_ref[...] = jnp.dot(hb, why_ref[...], preferred_element_type=F32).astype(BF16)
    mla_sc[...] = jnp.dot(hb, wmla_ref[...], preferred_element_type=F32)


@functools.lru_cache(maxsize=None)
def _rope_lane_tables(length, use_rope):
    if use_rope:
        n_freq = ROPE_DIM // 4
        t = np.arange(length)
        inv_freq = np.power(ROPE_THETA, -np.arange(n_freq, dtype=np.float64) / n_freq)
        ar = (t // GRID_W).astype(np.float64)[:, None] * inv_freq
        ac = (t % GRID_W).astype(np.float64)[:, None] * inv_freq
        c64 = np.concatenate([np.cos(ar), np.cos(ar), np.cos(ac), np.cos(ac)], axis=-1)
        s64 = np.concatenate([-np.sin(ar), np.sin(ar), -np.sin(ac), np.sin(ac)], axis=-1)
    else:
        c64 = np.ones((length, ROPE_DIM))
        s64 = np.zeros((length, ROPE_DIM))
    z64 = np.zeros((length, ROPE_DIM))
    ct = np.concatenate([np.ones((length, LANES)), c64, z64], axis=-1)
    st = np.concatenate([np.zeros((length, LANES)), z64, s64], axis=-1)
    ctk = np.concatenate([c64, s64], axis=-1)
    return ct.astype(np.float32), st.astype(np.float32), ctk.astype(np.float32)


_SWAP16 = np.concatenate([np.arange(16, 32), np.arange(0, 16), np.arange(48, 64), np.arange(32, 48)])


def _mla_weights(w_in, mla_w_uq, mla_w_ukv, qk_norm_q_g, qk_norm_k_g):
    hy_cols = 3 * D_HYENA
    w_hy = w_in[:, :hy_cols].astype(BF16)
    w_pe = w_in[:, hy_cols + Q_RANK + KV_RANK:]
    w_mla = jnp.concatenate([w_in[:, hy_cols:hy_cols + Q_RANK + KV_RANK], w_pe, w_pe[:, _SWAP16]],
                            axis=-1).astype(BF16)
    wq = mla_w_uq.reshape(Q_RANK, MLA_HEADS, QK_DIM)
    wq_rope = wq[:, :, NOPE_DIM:]
    w_uq2 = jnp.concatenate([wq[:, :, :NOPE_DIM], wq_rope, wq_rope[:, :, _SWAP16]], axis=-1)
    w_uq2 = w_uq2.reshape(Q_RANK, MLA_HEADS * 2 * LANES).astype(BF16)
    w_ukv2 = mla_w_ukv.astype(BF16)
    gq_r = qk_norm_q_g[NOPE_DIM:]
    z64 = jnp.zeros((ROPE_DIM,), F32)
    gq1 = jnp.concatenate([qk_norm_q_g[:NOPE_DIM], gq_r, z64]).reshape(1, -1)
    gq2 = jnp.concatenate([jnp.zeros((NOPE_DIM,), F32), z64, gq_r[_SWAP16]]).reshape(1, -1)
    gkn = qk_norm_k_g[:NOPE_DIM].reshape(1, -1)
    gk_r = qk_norm_k_g[NOPE_DIM:]
    gkr = jnp.concatenate([gk_r, gk_r[_SWAP16]]).reshape(1, -1)
    return w_hy, w_mla, w_uq2, w_ukv2, gq1, gq2, gkn, gkr


def _inproj(x, shift, scale, norm_g, weights, q_norm_g, kv_norm_g, use_rope, tl):
    b, length, d = x.shape
    w_hy, w_mla, w_uq2, w_ukv2, gq1, gq2, gkn, gkr = weights
    ct, st, ctk = _rope_lane_tables(length, use_rope)
    q_scale = QK_DIM ** -0.5 * math.log2(math.e)
    nt = length // tl
    n_tiles = b * nt
    hyc = w_hy.shape[1]
    cur = lambda i: jnp.minimum(i, n_tiles - 1)
    prev = lambda i: jnp.maximum(i - 1, 0)
    tok = lambda i: (cur(i), 0)
    per_b = lambda i: (cur(i) // nt, 0, 0)
    pos = lambda i: (prev(i) % nt, 0)
    head_blk = lambda i: (prev(i) // nt, 0, prev(i) % nt, 0)
    hy, q, k, v = pl.pallas_call(
        functools.partial(_inproj_body, q_scale=q_scale),
        grid=(n_tiles + 1,),
        in_specs=[pl.BlockSpec((tl, d), tok),
                  pl.BlockSpec((None, 1, d), per_b),
                  pl.BlockSpec((None, 1, d), per_b),
                  _const_spec((1, d)),
                  _const_spec(w_hy.shape), _const_spec(w_mla.shape),
                  _const_spec((1, Q_RANK)), _const_spec(w_uq2.shape),
                  _const_spec((1, KV_RANK)), _const_spec(w_ukv2.shape),
                  pl.BlockSpec((tl, 2 * LANES), pos), pl.BlockSpec((tl, 2 * LANES), pos),
                  pl.BlockSpec((tl, LANES), pos),
                  _const_spec((1, 2 * LANES)), _const_spec((1, 2 * LANES)),
                  _const_spec((1, LANES)), _const_spec((1, LANES))],
        out_specs=[pl.BlockSpec((tl, hyc), tok),
                   pl.BlockSpec((None, MLA_HEADS, tl, QK_DIM), head_blk),
                   pl.BlockSpec((None, MLA_HEADS, tl, QK_DIM), head_blk),
                   pl.BlockSpec((None, MLA_HEADS, tl, V_DIM), head_blk)],
        out_shape=[jax.ShapeDtypeStruct((b * length, hyc), BF16),
                   jax.ShapeDtypeStruct((b, MLA_HEADS, length, QK_DIM), BF16),
                   jax.ShapeDtypeStruct((b, MLA_HEADS, length, QK_DIM), BF16),
                   jax.ShapeDtypeStruct((b, MLA_HEADS, length, V_DIM), BF16)],
        scratch_shapes=[pltpu.VMEM((tl, w_mla.shape[1]), F32)],
        compiler_params=_cparams(("arbitrary",), VMEM_LIMIT),
        name="inproj_mla",
    )(x.reshape(b * length, d), shift, scale, norm_g.reshape(1, d), w_hy, w_mla,
      q_norm_g.reshape(1, -1), w_uq2, kv_norm_g.reshape(1, -1), w_ukv2, ct, st, ctk, gq1, gq2, gkn, gkr)
    return hy.reshape(b, length, hyc), q, k, v


def _filter_body(z_ref, w1_ref, b1_ref, w2_ref, b2_ref, w3_ref, b3_ref, w4_ref, fr_ref, dl_ref,
                 kern_ref, asum_ref, *, zero_row, tr):
    i = pl.program_id(0)
    fr = fr_ref[...]
    z = z_ref[...]
    dot = functools.partial(jnp.dot, precision=HIGHEST, preferred_element_type=F32)
    h = jnp.sin(fr * (dot(z, w1_ref[...]) + b1_ref[...]))
    h = jnp.sin(fr * (dot(h, w2_ref[...]) + b2_ref[...]))
    h = jnp.sin(fr * (dot(h, w3_ref[...]) + b3_ref[...]))
    o = dot(h, w4_ref[...]) * jnp.exp(-z[:, 0:1] * dl_ref[...])

    @pl.when(i == 0)
    def _():
        asum_ref[...] = jnp.zeros_like(asum_ref)

    asum_ref[...] += jnp.sum(jnp.abs(o), axis=0, keepdims=True)
    row = i * tr + lax.broadcasted_iota(jnp.int32, (tr, 1), 0)
    kern_ref[...] = jnp.where(row == zero_row, 0.0, o)


@functools.lru_cache(maxsize=None)
def _filter_features(length):
    n = 2 * length
    bands = (POS_EMB_DIM - 1) // 2
    pos = np.concatenate([np.arange(length), (n - np.arange(length, n)) % length])
    t_tab = np.linspace(0.0, 1.0, length)[:, None]
    w_ang = 2.0 * np.pi * np.arange(length, dtype=np.float64)[:, None] / length
    f = np.linspace(1e-4, bands - 1, bands)[None, :]
    z_tab = np.concatenate([t_tab, np.cos(f * w_ang), -np.sin(f * w_ang)], axis=-1)
    z = np.pad(z_tab[pos], ((0, 0), (0, LANES - POS_EMB_DIM)))
    return z.astype(np.float32)


def _hyena_kernel_taps(length, w1, b1, w2, b2, w3, b3, w4, freq):
    n = 2 * length
    z = jnp.asarray(_filter_features(length))
    w1p = jnp.pad(w1, ((0, LANES - POS_EMB_DIM), (0, 0)))
    deltas = jnp.abs(jnp.linspace(MIN_DECAY, MAX_DECAY, D_HYENA, dtype=F32)).reshape(1, -1)
    tr = min(1024, length)
    half_steps = length // tr
    fo = FILTER_ORDER
    kern, asum = pl.pallas_call(
        functools.partial(_filter_body, zero_row=length, tr=tr),
        grid=(n // tr,),
        in_specs=[pl.BlockSpec((tr, LANES), lambda i: (i, 0)),
                  _const_spec((LANES, fo)), _const_spec((1, fo)),
                  _const_spec((fo, fo)), _const_spec((1, fo)),
                  _const_spec((fo, fo)), _const_spec((1, fo)),
                  pl.BlockSpec((fo, D_HYENA), lambda i: (0, i // half_steps)),
                  _const_spec((1, fo)), _const_spec((1, D_HYENA))],
        out_specs=[pl.BlockSpec((tr, D_HYENA), lambda i: (i, 0)),
                   _const_spec((1, D_HYENA))],
        out_shape=[jax.ShapeDtypeStruct((n, D_HYENA), F32),
                   jax.ShapeDtypeStruct((1, D_HYENA), F32)],
        compiler_params=_cparams(("arbitrary",), VMEM_LIMIT),
        name="hyena_filter",
    )(z, w1p, b1.reshape(1, fo), w2, b2.reshape(1, fo), w3, b3.reshape(1, fo), w4,
      freq.reshape(1, fo), deltas)
    return kern, asum


def _half_rows(n):
    n_half = n // DFT_Q // 2 + 1
    return n_half, -(-n_half // 8) * 8


@functools.lru_cache(maxsize=None)
def _dft_tables(n, p_in):
    q_sz = DFT_Q
    p_sz = n // q_sz
    n_half, n_r = _half_rows(n)
    r = np.arange(n_r, dtype=np.float64)
    keep = (r < n_half).astype(np.float64)
    qq = np.arange(q_sz, dtype=np.float64)
    pp = np.arange(p_in, dtype=np.float64)
    tt = q_sz * pp[None, None, :] + qq[:, None, None]
    ang = -2.0 * np.pi * r[None, :, None] * tt / n
    t1 = np.concatenate([np.cos(ang), np.sin(ang)], axis=1) * np.tile(keep, 2)[None, :, None]
    mirror = np.where((r == 0) | (r == p_sz // 2), 1.0, 2.0)
    t4 = np.transpose(t1 * np.tile(mirror, 2)[None, :, None], (0, 2, 1)) / n
    a2 = -2.0 * np.pi * np.outer(qq, qq) / q_sz
    fre, fim = np.cos(a2), np.sin(a2)
    m2 = np.block([[fre, -fim], [fim, fre]])
    m2c = np.block([[fre, fim], [-fim, fre]])
    return (t1.astype(np.float32), t4.astype(np.float32), m2.astype(np.float32),
            m2c.astype(np.float32))


SUBLANES = 8


def _spec_block(rg, h):
    return (rg * 2 + h) * SUBLANES * DFT_Q


def _dft_stage1(x_ref, t1_ref, spec_ref, n_r, p_in):
    q_sz = DFT_Q

    def body(q, carry):
        xq = x_ref[pl.ds(q, p_in, stride=q_sz), :].astype(BF16)
        a = jnp.dot(t1_ref[q], xq, preferred_element_type=F32)
        row = pl.multiple_of(q * SUBLANES, SUBLANES)
        for h in range(2):
            for rg in range(n_r // SUBLANES):
                src = h * n_r + rg * SUBLANES
                spec_ref[pl.ds(_spec_block(rg, h) + row, SUBLANES), :] = a[src:src + SUBLANES]
        return carry

    lax.fori_loop(0, q_sz, body, 0, unroll=8)


def _spec_rows(rg, h, r8):
    return pl.ds(_spec_block(rg, h) + r8, DFT_Q, stride=SUBLANES)


def _spectrum_body(kern_ref, asum_ref, t1_ref, m2_ref, kf_ref, spec_ref, *, n_r, p_in):
    _dft_stage1(kern_ref, t1_ref, spec_ref, n_r, p_in)
    inv = 1.0 / asum_ref[...]
    m2 = m2_ref[...]

    def body(rg, carry):
        for r8 in range(SUBLANES):
            blk = jnp.concatenate([spec_ref[_spec_rows(rg, 0, r8), :],
                                   spec_ref[_spec_rows(rg, 1, r8), :]], axis=0)
            xf = jnp.dot(m2, blk.astype(BF16), preferred_element_type=F32)
            kf_ref[rg * SUBLANES + r8] = (xf * inv).astype(BF16)
        return carry

    lax.fori_loop(0, n_r // SUBLANES, body, 0)


def _hyena_filter_spectrum(kern, asum):
    n, c = kern.shape
    p_sz = n // DFT_Q
    _, n_r = _half_rows(n)
    t1, _, m2, _ = _dft_tables(n, p_sz)
    t1 = jnp.asarray(t1).astype(BF16)
    m2 = jnp.asarray(m2).astype(BF16)
    ct = LANES
    return pl.pallas_call(
        functools.partial(_spectrum_body, n_r=n_r, p_in=p_sz),
        grid=(c // ct,),
        in_specs=[_single_spec((n, ct), lambda j: (0, j)),
                  pl.BlockSpec((1, ct), lambda j: (0, j)),
                  _single_spec(t1.shape, lambda j: (0, 0, 0)),
                  _const_spec(m2.shape)],
        out_specs=pl.BlockSpec((n_r, 2 * DFT_Q, ct), lambda j: (0, 0, j)),
        out_shape=jax.ShapeDtypeStruct((n_r, 2 * DFT_Q, c), BF16),
        scratch_shapes=[pltpu.VMEM((n_r * 2 * DFT_Q, ct), F32)],
        compiler_params=_cparams(("parallel",), VMEM_LIMIT),
        name="hyena_filter_spectrum",
    )(kern, asum, t1, m2)


def _short_conv_chunk(u_ref, w_ref, b_ref, i, rows, length):
    pack = 16
    base = pl.multiple_of(i * rows, rows)
    u = u_ref[pl.ds(base, rows), :].astype(F32)
    lo = pl.multiple_of(jnp.maximum(base - pack, 0), pack)
    hi = pl.multiple_of(jnp.minimum(base + rows, length - pack), pack)
    prev = u_ref[pl.ds(lo, pack), :].astype(F32)[pack - 1:pack]
    nxt = u_ref[pl.ds(hi, pack), :].astype(F32)[0:1]
    prev = jnp.where(base == 0, 0.0, prev)
    nxt = jnp.where(base + rows == length, 0.0, nxt)
    ridx = lax.broadcasted_iota(jnp.int32, (rows, 1), 0)
    up = jnp.where(ridx == 0, prev, pltpu.roll(u, 1, axis=0))
    dn = jnp.where(ridx == rows - 1, nxt, pltpu.roll(u, rows - 1, axis=0))
    w = w_ref[...]
    return up * w[0:1] + u * w[1:2] + dn * w[2:3] + b_ref[...]


def _hyena_conv_body(x0_ref, x1_ref, v_ref, w0_ref, w1_ref, wv_ref, b0_ref, b1_ref, bv_ref,
                     bias_ref, kf_ref, t1_ref, t4_ref, m2_ref, m2c_ref, o_ref, vx_ref, spec_ref,
                     *, length, rows):
    q_sz = DFT_Q
    n_half, n_r = _half_rows(2 * length)
    p_in = length // q_sz
    n_chunks = length // rows

    def gate_in(i, carry):
        x1c = _short_conv_chunk(x1_ref, w1_ref, b1_ref, i, rows, length)
        vc = _short_conv_chunk(v_ref, wv_ref, bv_ref, i, rows, length)
        vx_ref[pl.ds(pl.multiple_of(i * rows, rows), rows), :] = vc * x1c
        return carry

    lax.fori_loop(0, n_chunks, gate_in, 0)
    _dft_stage1(vx_ref, t1_ref, spec_ref, n_r, p_in)
    m2 = m2_ref[...]
    m2c = m2c_ref[...]

    def mid_row(rg, r8):
        blk = jnp.concatenate([spec_ref[_spec_rows(rg, 0, r8), :],
                               spec_ref[_spec_rows(rg, 1, r8), :]], axis=0)
        xf = jnp.dot(m2, blk.astype(BF16), preferred_element_type=F32)
        kf = kf_ref[rg * SUBLANES + r8].astype(F32)
        xre, xim = xf[:q_sz], xf[q_sz:]
        kre, kim = kf[:q_sz], kf[q_sz:]
        z = jnp.concatenate([xre * kre - xim * kim, xre * kim + xim * kre], axis=0)
        bf = jnp.dot(m2c, z.astype(BF16), preferred_element_type=F32)
        spec_ref[_spec_rows(rg, 0, r8), :] = bf[:q_sz]
        spec_ref[_spec_rows(rg, 1, r8), :] = bf[q_sz:]

    def mid(rg, carry):
        for r8 in range(SUBLANES):
            mid_row(rg, r8)
        return carry

    lax.fori_loop(0, n_half // SUBLANES, mid, 0)
    for r in range(n_half // SUBLANES * SUBLANES, n_half):
        mid_row(r // SUBLANES, r % SUBLANES)
    bias = bias_ref[...]

    def last(q, carry):
        row = pl.multiple_of(q * SUBLANES, SUBLANES)
        tiles = [spec_ref[pl.ds(_spec_block(rg, h) + row, SUBLANES), :]
                 for h in range(2) for rg in range(n_r // SUBLANES)]
        bq = jnp.concatenate(tiles, axis=0).astype(BF16)
        y = jnp.dot(t4_ref[q], bq, preferred_element_type=F32)
        sl = pl.ds(q, p_in, stride=q_sz)
        vx_ref[sl, :] = y + bias * vx_ref[sl, :]
        return carry

    lax.fori_loop(0, q_sz, last, 0, unroll=8)

    def gate_out(i, carry):
        x0c = _short_conv_chunk(x0_ref, w0_ref, b0_ref, i, rows, length)
        sl = pl.ds(pl.multiple_of(i * rows, rows), rows)
        o_ref[sl, :] = (vx_ref[sl, :] * x0c).astype(BF16)
        return carry

    lax.fori_loop(0, n_chunks, gate_out, 0)


def _hyena_conv(hy, conv_w, conv_b, bias, kf):
    b, length, _ = hy.shape
    ch = D_HYENA
    ct = LANES
    nct = ch // ct
    n = 2 * length
    _, n_r = _half_rows(n)
    p_in = length // DFT_Q
    t1, t4, m2, m2c = _dft_tables(n, p_in)
    t1, t4, m2, m2c = (jnp.asarray(a).astype(BF16) for a in (t1, t4, m2, m2c))
    rows = min(512, length)
    col = lambda off: (lambda j, bi: (bi, 0, off * nct + j))
    wcol = lambda off: (lambda j, bi: (0, off * nct + j))
    conv_b2 = conv_b.reshape(1, -1)
    return pl.pallas_call(
        functools.partial(_hyena_conv_body, length=length, rows=rows),
        grid=(nct, b),
        in_specs=[_single_spec((None, length, ct), col(0)),
                  _single_spec((None, length, ct), col(1)),
                  _single_spec((None, length, ct), col(2)),
                  pl.BlockSpec((3, ct), wcol(0)), pl.BlockSpec((3, ct), wcol(1)),
                  pl.BlockSpec((3, ct), wcol(2)),
                  pl.BlockSpec((1, ct), wcol(0)), pl.BlockSpec((1, ct), wcol(1)),
                  pl.BlockSpec((1, ct), wcol(2)),
                  pl.BlockSpec((1, ct), lambda j, bi: (0, j)),
                  _single_spec((n_r, 2 * DFT_Q, ct), lambda j, bi: (0, 0, j)),
                  _single_spec(t1.shape, lambda j, bi: (0, 0, 0)),
                  _single_spec(t4.shape, lambda j, bi: (0, 0, 0)),
                  _const_spec(m2.shape), _const_spec(m2c.shape)],
        out_specs=pl.BlockSpec((None, length, ct), lambda j, bi: (bi, 0, j)),
        out_shape=jax.ShapeDtypeStruct((b, length, ch), BF16),
        scratch_shapes=[pltpu.VMEM((length, ct), F32),
                        pltpu.VMEM((n_r * 2 * DFT_Q, ct), F32)],
        compiler_params=_cparams(("parallel", "parallel"), VMEM_LIMIT),
        name="hyena_conv",
    )(hy, hy, hy, conv_w, conv_w, conv_w, conv_b2, conv_b2, conv_b2, bias.reshape(1, ch), kf,
      t1, t4, m2, m2c)


def _flash_body(q_ref, k_ref, v_ref, kc_ref, vc_ref, o_ref, s_ref, p_ref, al_ref, m_ref, l_ref,
                acc_ref, *, tk, n_sub, rg):
    tq = q_ref.shape[0]
    sub = tq // n_sub
    nk = k_ref.shape[0] // tk
    nc = kc_ref.shape[0]
    n_chunks = nk + 1
    nt = (((1,), (1,)), ((), ()))

    def width(c):
        return nc if c == nk else tk

    def scores(c):
        keys = kc_ref[...] if c == nk else k_ref[c * tk:(c + 1) * tk, :]
        s_ref[c % 3, :, :width(c)] = lax.dot_general(q_ref[...], keys, nt, preferred_element_type=F32)

    def weighted_values(c):
        vals = vc_ref[...] if c == nk else v_ref[c * tk:(c + 1) * tk, :]
        return [jnp.dot(p_ref[c % 2, i * sub:(i + 1) * sub, :width(c)], vals,
                        preferred_element_type=F32) for i in range(n_sub)]

    def softmax(c):
        w = width(c)
        for g in range(tq // rg):
            rows = slice(g * rg, (g + 1) * rg)
            s = s_ref[c % 3, rows, :w]
            m_prev = m_ref[rows, :]
            m_new = jnp.maximum(m_prev, jnp.max(s, axis=-1, keepdims=True))
            alpha = jnp.exp2(m_prev - m_new)
            p = jnp.exp2(s - jnp.concatenate([m_new] * (w // LANES), axis=1))
            part = p[:, :LANES]
            for j in range(1, w // LANES):
                part = part + p[:, j * LANES:(j + 1) * LANES]
            l_ref[rows, :] = alpha * l_ref[rows, :] + part
            m_ref[rows, :] = m_new
            al_ref[rows, :] = alpha
            p_ref[c % 2, rows, :w] = p.astype(BF16)

    m_ref[...] = jnp.full_like(m_ref, -jnp.inf)
    l_ref[...] = jnp.zeros_like(l_ref)
    acc_ref[...] = jnp.zeros_like(acc_ref)
    scores(0)
    for c in range(n_chunks):
        prod = weighted_values(c - 1) if c >= 1 else None
        if c + 1 < n_chunks:
            scores(c + 1)
        softmax(c)
        if prod is not None:
            for i in range(n_sub):
                srows = slice(i * sub, (i + 1) * sub)
                acc_ref[srows, :] = al_ref[srows, :] * (acc_ref[srows, :] + prod[i])
    prod = weighted_values(n_chunks - 1)
    l_fin = jnp.sum(l_ref[...], axis=-1, keepdims=True)
    for i in range(n_sub):
        srows = slice(i * sub, (i + 1) * sub)
        o_ref[srows, :] = ((acc_ref[srows, :] + prod[i]) / l_fin[srows]).astype(o_ref.dtype)


def _flash_attention(q, k, v, kc, vc, tq, tk, n_sub, rg):
    b, hds, length, _ = q.shape
    nc = kc.shape[2]
    assert length % tk == 0 and nc <= tk and nc % LANES == 0 and V_DIM == LANES
    return pl.pallas_call(
        functools.partial(_flash_body, tk=tk, n_sub=n_sub, rg=rg),
        grid=(b, hds, length // tq),
        in_specs=[pl.BlockSpec((None, None, tq, QK_DIM), lambda bi, h, qi: (bi, h, qi, 0)),
                  pl.BlockSpec((None, None, length, QK_DIM), lambda bi, h, qi: (bi, h, 0, 0)),
                  pl.BlockSpec((None, None, length, V_DIM), lambda bi, h, qi: (bi, h, 0, 0)),
                  pl.BlockSpec((None, None, nc, QK_DIM), lambda bi, h, qi: (bi, h, 0, 0)),
                  pl.BlockSpec((None, None, nc, V_DIM), lambda bi, h, qi: (bi, h, 0, 0))],
        out_specs=pl.BlockSpec((None, tq, V_DIM), lambda bi, h, qi: (bi, qi, h)),
        out_shape=jax.ShapeDtypeStruct((b, length, hds * V_DIM), BF16),
        scratch_shapes=[pltpu.VMEM((3, tq, tk), F32), pltpu.VMEM((2, tq, tk), BF16),
                        pltpu.VMEM((tq, LANES), F32), pltpu.VMEM((tq, LANES), F32),
                        pltpu.VMEM((tq, LANES), F32), pltpu.VMEM((tq, V_DIM), F32)],
        compiler_params=_cparams(("parallel", "parallel", "parallel"), VMEM_LIMIT),
        name="mla_flash_attention",
    )(q, k, v, kc, vc)


def _store_packed_rows(dst_ref, x):
    half = x.shape[1] // 2
    for j in range(half // LANES):
        hi = x[:, j * LANES:(j + 1) * LANES].astype(BF16).astype(F32)
        lo = x[:, half + j * LANES:half + (j + 1) * LANES].astype(BF16).astype(F32)
        dst_ref[j] = (lax.bitcast_convert_type(hi, jnp.uint32)
                      | (lax.bitcast_convert_type(lo, jnp.uint32) >> 16))


def _unpack_words(w):
    hi = lax.bitcast_convert_type(w & jnp.uint32(0xFFFF0000), F32)
    lo = lax.bitcast_convert_type(w << 16, F32)
    return hi, lo


def _load_packed_rows(src_ref):
    parts = [_unpack_words(src_ref[j]) for j in range(src_ref.shape[0])]
    return jnp.concatenate([p[0] for p in parts] + [p[1] for p in parts], axis=-1)


def _outproj_body(yh_ref, ya_ref, x_ref, g1_ref, sh2_ref, sc2_ref, n2g_ref, wo1_ref, wo2_ref,
                  rwt_ref, rb_ref, tri_ref, xn_ref, h2_ref, idx_ref, gate_ref, rank_ref, cnt_ref,
                  carry_sc):
    i = pl.program_id(0)

    @pl.when(i == 0)
    def _():
        carry_sc[...] = jnp.zeros_like(carry_sc)

    mix = (jnp.dot(yh_ref[...], wo1_ref[...], preferred_element_type=F32)
           + jnp.dot(ya_ref[...], wo2_ref[...], preferred_element_type=F32))
    xn = x_ref[...] + g1_ref[...] * mix
    xn_ref[...] = xn
    h2 = _rms(xn) * n2g_ref[...]
    h2 = h2 * (1.0 + sc2_ref[...]) + sh2_ref[...]
    _store_packed_rows(h2_ref, h2)
    logits = lax.dot_general(rwt_ref[...], h2, (((1,), (1,)), ((), ())), precision=HIGHEST,
                             preferred_element_type=F32) + rb_ref[...]
    n_e, tt = logits.shape
    eidx = lax.broadcasted_iota(jnp.int32, (n_e, tt), 0).astype(F32)
    work = logits
    vals, sels, idxs = [], [], []
    for _ in range(TOP_K):
        m = jnp.max(work, axis=0, keepdims=True)
        ix = jnp.min(jnp.where(work == m, eidx, float(n_e)), axis=0, keepdims=True)
        sel = eidx == ix
        work = jnp.where(sel, -jnp.inf, work)
        vals.append(m)
        idxs.append(ix)
        sels.append(sel)
    es = [jnp.exp(vk - vals[0]) for vk in vals]
    den = es[0] + es[1] + es[2] + es[3]
    gate_ref[...] = jnp.concatenate(es, axis=0) / den
    idx_ref[...] = jnp.concatenate(idxs, axis=0).astype(jnp.int32)
    onehot = jnp.zeros((n_e, tt), F32)
    for sel in sels:
        onehot = onehot + sel.astype(F32)
    prefix = jnp.dot(onehot.astype(BF16), tri_ref[...], preferred_element_type=F32) + carry_sc[...]
    ranks = [jnp.sum(jnp.where(sel, prefix, 0.0), axis=0, keepdims=True) for sel in sels]
    rank_ref[...] = jnp.concatenate(ranks, axis=0).astype(jnp.int32)
    carry_sc[...] += jnp.sum(onehot, axis=1, keepdims=True)
    cnt_ref[...] = jnp.broadcast_to(carry_sc[...], cnt_ref.shape)


def _outproj_router(y_hy, y_att, x2, g1, sh2, sc2, norm2_g, w_out, router_w, router_b, tokens_per_batch,
                    tt):
    t, d = x2.shape
    ch = y_hy.shape[1]
    n_e = router_w.shape[1]
    wo1 = w_out[:ch].astype(BF16)
    wo2 = w_out[ch:].astype(BF16)
    tri = jnp.asarray(np.triu(np.ones((tt, tt), np.float32), k=1), BF16)
    steps_per_batch = tokens_per_batch // tt
    tok = lambda i: (i, 0)
    per_b = lambda i: (i // steps_per_batch, 0, 0)
    lanes_tok = lambda i: (0, i)
    return pl.pallas_call(
        _outproj_body,
        grid=(t // tt,),
        in_specs=[pl.BlockSpec((tt, ch), tok), pl.BlockSpec((tt, ch), tok),
                  pl.BlockSpec((tt, d), tok),
                  pl.BlockSpec((None, 1, d), per_b), pl.BlockSpec((None, 1, d), per_b),
                  pl.BlockSpec((None, 1, d), per_b),
                  _const_spec((1, d)), _const_spec(wo1.shape), _const_spec(wo2.shape),
                  _const_spec((n_e, d)), _const_spec((n_e, 1)), _const_spec((tt, tt))],
        out_specs=[pl.BlockSpec((tt, d), tok),
                   pl.BlockSpec((d // (2 * LANES), tt, LANES), lambda i: (0, i, 0)),
                   pl.BlockSpec((TOP_K, tt), lanes_tok), pl.BlockSpec((TOP_K, tt), lanes_tok),
                   pl.BlockSpec((TOP_K, tt), lanes_tok), _const_spec((n_e, LANES))],
        out_shape=[jax.ShapeDtypeStruct((t, d), F32),
                   jax.ShapeDtypeStruct((d // (2 * LANES), t, LANES), jnp.uint32),
                   jax.ShapeDtypeStruct((TOP_K, t), jnp.int32),
                   jax.ShapeDtypeStruct((TOP_K, t), F32),
                   jax.ShapeDtypeStruct((TOP_K, t), jnp.int32),
                   jax.ShapeDtypeStruct((n_e, LANES), F32)],
        scratch_shapes=[pltpu.VMEM((n_e, 1), F32)],
        compiler_params=_cparams(("arbitrary",), VMEM_LIMIT),
        name="outproj_router",
    )(y_hy, y_att, x2, g1, sh2, sc2, norm2_g.reshape(1, d), wo1, wo2, router_w.T,
      router_b.reshape(n_e, 1), tri)


def _cast_rows(src_ref, dst_ref, chunk):
    def body(c, carry):
        sl = pl.ds(pl.multiple_of(c * chunk, chunk), chunk)
        dst_ref[sl, :] = src_ref[sl, :].astype(dst_ref.dtype)
        return carry

    lax.fori_loop(0, src_ref.shape[0] // chunk, body, 0)


def _expert_body(be_ref, nvalid_ref, xs_ref, wgu_ref, bgu_ref, wd_ref, bd_ref, ys_ref, wgu_bf, wd_bf):
    i = pl.program_id(0)
    n_valid = nvalid_ref[i]
    active = n_valid > 0
    new_expert = jnp.logical_or(i == 0, be_ref[i] != be_ref[jnp.maximum(i - 1, 0)])

    @pl.when(jnp.logical_and(active, new_expert))
    def _():
        _cast_rows(wgu_ref, wgu_bf, 128)
        _cast_rows(wd_ref, wd_bf, 128)

    @pl.when(active)
    def _():
        row = lax.broadcasted_iota(jnp.int32, (xs_ref.shape[1], 1), 0)
        xs = jnp.where(row < n_valid, _load_packed_rows(xs_ref), 0.0).astype(BF16)
        gu = jnp.dot(xs, wgu_bf[...], preferred_element_type=F32) + bgu_ref[...]
        dff = gu.shape[1] // 2
        gate = jnp.minimum(gu[:, :dff], SWIGLU_LIMIT)
        up = jnp.clip(gu[:, dff:], -SWIGLU_LIMIT, SWIGLU_LIMIT)
        act = (up + 1.0) * (gate * jax.nn.sigmoid(SWIGLU_ALPHA * gate))
        ys = jnp.dot(act.astype(BF16), wd_bf[...], preferred_element_type=F32) + bd_ref[...]
        _store_packed_rows(ys_ref, ys)

    @pl.when(jnp.logical_not(active))
    def _():
        ys_ref[...] = jnp.zeros_like(ys_ref)


def _expert_blocks(xs, block_e, n_valid, w_gu, b_gu, w_down, b_down):
    n_seg, n_rows, _ = xs.shape
    n_e, d, dff2 = w_gu.shape
    bm = MOE_ROWS
    seg_block = pl.BlockSpec((n_seg, bm, LANES), lambda i, be, nu: (0, i, 0))
    grid_spec = pltpu.PrefetchScalarGridSpec(
        num_scalar_prefetch=2,
        grid=(n_rows // bm,),
        in_specs=[seg_block,
                  pl.BlockSpec((None, d, dff2), lambda i, be, nu: (be[i], 0, 0)),
                  pl.BlockSpec((None, 1, dff2), lambda i, be, nu: (be[i], 0, 0)),
                  pl.BlockSpec((None, dff2 // 2, d), lambda i, be, nu: (be[i], 0, 0)),
                  pl.BlockSpec((None, 1, d), lambda i, be, nu: (be[i], 0, 0))],
        out_specs=seg_block,
        scratch_shapes=[pltpu.VMEM((d, dff2), BF16), pltpu.VMEM((dff2 // 2, d), BF16)],
    )
    return pl.pallas_call(
        _expert_body,
        grid_spec=grid_spec,
        out_shape=jax.ShapeDtypeStruct(xs.shape, jnp.uint32),
        compiler_params=_cparams(("arbitrary",), VMEM_LIMIT),
        name="moe_experts",
    )(block_e, n_valid, xs, w_gu, b_gu.reshape(n_e, 1, dff2), w_down, b_down.reshape(n_e, 1, d))


def _sc_gather(table, idx):
    n = idx.shape[0]
    width = table.shape[1]
    mesh = plsc.VectorSubcoreMesh(core_axis_name="core", subcore_axis_name="subcore")
    n_workers = mesh.num_cores * mesh.num_subcores
    assert width == LANES and n % (SC_WINDOW * n_workers) == 0

    @functools.partial(pl.kernel, out_type=jax.ShapeDtypeStruct((n, width), table.dtype), mesh=mesh)
    def gather_kernel(table_hbm, idx_hbm, out_hbm):
        def body(idx_vmem, out_vmem):
            pltpu.sync_copy(table_hbm.at[idx_vmem.at[0]], out_vmem)

        pltpu.emit_pipeline(
            body,
            grid=(n // SC_WINDOW,),
            in_specs=[pl.BlockSpec((1, SC_WINDOW), lambda i: (0, i))],
            out_specs=[pl.BlockSpec((SC_WINDOW, width), lambda i: (i, 0))],
            core_axis_name=("core", "subcore"),
            dimension_semantics=(pltpu.PARALLEL,),
        )(idx_hbm, out_hbm)

    return gather_kernel(table, idx.reshape(1, n))


def _sc_scatter(rows, idx, n_copies, n_out):
    n, width = rows.shape
    mesh = plsc.VectorSubcoreMesh(core_axis_name="core", subcore_axis_name="subcore")
    n_workers = mesh.num_cores * mesh.num_subcores
    assert width == LANES and n % (SC_WINDOW * n_workers) == 0 and idx.shape == (8, n)

    @functools.partial(pl.kernel, out_type=jax.ShapeDtypeStruct((n_out, width), rows.dtype), mesh=mesh)
    def scatter_kernel(rows_hbm, idx_hbm, out_hbm):
        def body(rows_vmem, idx_vmem):
            for k in range(n_copies):
                pltpu.sync_copy(rows_vmem, out_hbm.at[idx_vmem.at[k]])

        pltpu.emit_pipeline(
            body,
            grid=(n // SC_WINDOW,),
            in_specs=[pl.BlockSpec((SC_WINDOW, width), lambda i: (i, 0)),
                      pl.BlockSpec((8, SC_WINDOW), lambda i: (0, i))],
            out_specs=[],
            core_axis_name=("core", "subcore"),
            dimension_semantics=(pltpu.PARALLEL,),
        )(rows_hbm, idx_hbm)

    return scatter_kernel(rows, idx)


def _combine_body(pk_ref, gt_ref, xn_ref, g2_ref, o_ref):
    n_seg = pk_ref.shape[0]
    half = o_ref.shape[1] // 2
    gt = gt_ref[...]
    g2 = g2_ref[...]
    for j in range(n_seg):
        acc_hi = None
        acc_lo = None
        for kk in range(TOP_K):
            hi, lo = _unpack_words(pk_ref[j, kk])
            g = gt[:, kk:kk + 1]
            acc_hi = g * hi if acc_hi is None else acc_hi + g * hi
            acc_lo = g * lo if acc_lo is None else acc_lo + g * lo
        c_hi = slice(j * LANES, (j + 1) * LANES)
        c_lo = slice(half + j * LANES, half + (j + 1) * LANES)
        o_ref[:, c_hi] = xn_ref[:, c_hi] + g2[:, c_hi] * acc_hi
        o_ref[:, c_lo] = xn_ref[:, c_lo] + g2[:, c_lo] * acc_lo


def _combine(picked, gates_t, xn, g2, tokens_per_batch, tt):
    n_seg, _, t, _ = picked.shape
    d = xn.shape[1]
    steps_per_batch = tokens_per_batch // tt
    return pl.pallas_call(
        _combine_body,
        grid=(t // tt,),
        in_specs=[pl.BlockSpec((n_seg, TOP_K, tt, LANES), lambda i: (0, 0, i, 0)),
                  pl.BlockSpec((tt, TOP_K), lambda i: (i, 0)),
                  pl.BlockSpec((tt, d), lambda i: (i, 0)),
                  pl.BlockSpec((None, 1, d), lambda i: (i // steps_per_batch, 0, 0))],
        out_specs=pl.BlockSpec((tt, d), lambda i: (i, 0)),
        out_shape=jax.ShapeDtypeStruct((t, d), F32),
        compiler_params=_cparams(("parallel",), VMEM_LIMIT),
        name="moe_combine",
    )(picked, gates_t, xn, g2)


def _moe(h2p, xn, g2, idx, gates, ranks, counts, w_gu, b_gu, w_down, b_down, tokens_per_batch):
    n_seg, t, _ = h2p.shape
    bm = MOE_ROWS
    n_e = w_gu.shape[0]
    cnt = counts[:, 0].astype(jnp.int32)
    padded = (cnt + bm - 1) // bm * bm
    padded_ends = jnp.cumsum(padded)
    padded_starts = padded_ends - padded
    experts = jnp.arange(n_e, dtype=jnp.int32)[:, None, None]
    dest = ranks + jnp.sum(jnp.where(idx[None] == experts, padded_starts[:, None, None], 0), axis=0)
    n_blocks = t * TOP_K // bm + n_e
    n_rows = n_blocks * bm
    block_start = jnp.arange(n_blocks, dtype=jnp.int32) * bm
    block_e = jnp.minimum(jnp.sum(padded_ends[None, :] <= block_start[:, None], axis=1),
                          n_e - 1).astype(jnp.int32)
    n_valid = jnp.clip(cnt[block_e] - (block_start - padded_starts[block_e]), 0, bm).astype(jnp.int32)
    seg = jnp.arange(n_seg, dtype=jnp.int32)
    scatter_idx = (seg[None, :, None] * n_rows + dest[:, None, :]).reshape(TOP_K, n_seg * t)
    scatter_idx = jnp.concatenate([scatter_idx, scatter_idx], axis=0)
    xs = _sc_scatter(h2p.reshape(n_seg * t, LANES), scatter_idx, TOP_K, n_seg * n_rows)
    ys = _expert_blocks(xs.reshape(n_seg, n_rows, LANES), block_e, n_valid, w_gu, b_gu, w_down, b_down)
    picked = _sc_gather(ys.reshape(n_seg * n_rows, LANES),
                        (seg[:, None, None] * n_rows + dest[None]).reshape(-1))
    return _combine(picked.reshape(n_seg, TOP_K, t, LANES), gates.T, xn, g2, tokens_per_batch,
                    min(512, tokens_per_batch))


def kernel(x, c, ctx, c_ctx, mod_w, mod_b, norm1_g, w_in, hy_conv_w, hy_conv_b, hy_f_w1, hy_f_b1,
           hy_f_w2, hy_f_b2, hy_f_w3, hy_f_b3, hy_f_w4, hy_f_freq, hy_bias, mla_q_norm_g, mla_w_uq,
           mla_kv_norm_g, mla_w_ukv, qk_norm_q_g, qk_norm_k_g, w_out, norm2_g, router_w, router_b,
           exp_w_gu, exp_b_gu, exp_w_down, exp_b_down):
    b, length, d = x.shape
    depth = mod_w.shape[0]
    assert depth == 1, "single-layer kernel"
    ly = 0
    c_rows = jnp.concatenate([c, c_ctx[None, :], jnp.zeros((8 - b - 1, d), F32)], axis=0)
    mod = _adaln_table(c_rows, mod_w[ly], mod_b[ly])
    mod6 = mod.reshape(8, 6, d)
    sh1, sc1, g1, sh2, sc2, g2 = (mod6[:b, j][:, None, :] for j in range(6))
    csh1 = mod6[b:b + 1, 0][:, None, :]
    csc1 = mod6[b:b + 1, 1][:, None, :]

    weights = _mla_weights(w_in[ly], mla_w_uq[ly], mla_w_ukv[ly], qk_norm_q_g[ly], qk_norm_k_g[ly])
    n_ctx = ctx.shape[1]
    _, _, k_c, v_c = _inproj(ctx, jnp.broadcast_to(csh1, (b, 1, d)), jnp.broadcast_to(csc1, (b, 1, d)),
                             norm1_g[ly], weights, mla_q_norm_g[ly], mla_kv_norm_g[ly], False, n_ctx)
    hy, q, k, v = _inproj(x, sh1, sc1, norm1_g[ly], weights, mla_q_norm_g[ly], mla_kv_norm_g[ly],
                          True, min(512, length))

    kern, asum = _hyena_kernel_taps(length, hy_f_w1[ly], hy_f_b1[ly], hy_f_w2[ly], hy_f_b2[ly],
                                    hy_f_w3[ly], hy_f_b3[ly], hy_f_w4[ly], hy_f_freq[ly])
    kf = _hyena_filter_spectrum(kern, asum)
    y_hy = _hyena_conv(hy, hy_conv_w[ly], hy_conv_b[ly], hy_bias[ly], kf)

    y_att = _flash_attention(q, k, v, k_c, v_c, min(512, length), min(512, length), 2, 32)

    t = b * length
    xn, h2, idx, gates, ranks, counts = _outproj_router(
        y_hy.reshape(t, -1), y_att.reshape(t, -1), x.reshape(t, d), g1, sh2, sc2, norm2_g[ly],
        w_out[ly], router_w[ly], router_b[ly], length, min(512, length))
    out = _moe(h2, xn, g2, idx, gates, ranks, counts, exp_w_gu[ly], exp_b_gu[ly],
               exp_w_down[ly], exp_b_down[ly], length)
    return out.reshape(b, length, d)
```

```python
import functools
import math

import jax
import jax.numpy as jnp
import numpy as np
from jax import lax
from jax.experimental import pallas as pl
from jax.experimental.pallas import tpu as pltpu
from jax.experimental.pallas import tpu_sc as plsc

F32 = jnp.float32
BF16 = jnp.bfloat16
HIGHEST = lax.Precision.HIGHEST

GRID_W = 64
D_HYENA = 512
FILTER_ORDER = 64
POS_EMB_DIM = 33
MIN_DECAY = math.log(1e-2) / 0.3
MAX_DECAY = math.log(1e-2) / 1.5
NOPE_DIM = 128
ROPE_DIM = 64
QK_DIM = NOPE_DIM + ROPE_DIM
V_DIM = 128
MLA_HEADS = 4
Q_RANK = 256
KV_RANK = 128
ROPE_THETA = 10000.0
N_EXPERTS = 32
TOP_K = 4
SWIGLU_ALPHA = 1.702
SWIGLU_LIMIT = 7.0
NORM_EPS = 1e-6

LANES = 128
VMEM_LIMIT = 56 * 1024 * 1024

DFT_Q = LANES
MOE_ROWS = 512
SC_WINDOW = 128


def _cparams(sem, vmem=None):
    return pltpu.CompilerParams(dimension_semantics=sem, vmem_limit_bytes=vmem)


def _const_spec(shape):
    nd = len(shape)
    return pl.BlockSpec(shape, lambda *_: (0,) * nd)


def _single_spec(shape, index_map):
    return pl.BlockSpec(shape, index_map, pipeline_mode=pl.Buffered(1))


def _mod_body(c_ref, w_ref, b_ref, o_ref):
    cc = c_ref[...]
    s = cc * jax.nn.sigmoid(cc)
    o_ref[...] = jnp.dot(s, w_ref[...], precision=HIGHEST,
                         preferred_element_type=F32) + b_ref[...]


def _adaln_table(c_rows, mod_w, mod_b):
    rows, d = c_rows.shape
    n = mod_w.shape[1]
    tn = n // 8
    return pl.pallas_call(
        _mod_body,
        grid=(n // tn,),
        in_specs=[_const_spec((rows, d)),
                  pl.BlockSpec((d, tn), lambda j: (0, j)),
                  pl.BlockSpec((1, tn), lambda j: (0, j))],
        out_specs=pl.BlockSpec((rows, tn), lambda j: (0, j)),
        out_shape=jax.ShapeDtypeStruct((rows, n), F32),
        compiler_params=_cparams(("arbitrary",)),
        name="adaln_table",
    )(c_rows, mod_w, mod_b.reshape(1, n))


def _rms(x, eps=NORM_EPS):
    return x * lax.rsqrt(jnp.mean(x * x, axis=-1, keepdims=True) + eps)


def _inproj_body(x_ref, sh_ref, sc_ref, g_ref, why_ref, wmla_ref, qng_ref, wuq_ref,
                 kvng_ref, wukv_ref, ct_ref, st_ref, ctk_ref, gq1_ref, gq2_ref, gkn_ref,
                 gkr_ref, hy_ref, q_ref, k_ref, v_ref, mla_sc, *, q_scale):
    i = pl.program_id(0)

    @pl.when(i == 0)
    def _():
        mla_sc[...] = jnp.zeros_like(mla_sc)

    mla = mla_sc[...]
    cq = mla[:, :Q_RANK]
    ckv = mla[:, Q_RANK:Q_RANK + KV_RANK]
    pe2 = mla[:, Q_RANK + KV_RANK:]
    qf = jnp.dot((_rms(cq) * qng_ref[...]).astype(BF16), wuq_ref[...],
                 preferred_element_type=F32)
    kvf = jnp.dot((_rms(ckv) * kvng_ref[...]).astype(BF16), wukv_ref[...],
                  preferred_element_type=F32)
    ct = ct_ref[...]
    st = st_ref[...]
    lane256 = lax.broadcasted_iota(jnp.int32, (1, 2 * LANES), 1)
    qmask = (lane256 < QK_DIM).astype(F32)
    lane128 = lax.broadcasted_iota(jnp.int32, (1, LANES), 1)
    pemask = (lane128 < ROPE_DIM).astype(F32)
    kr0 = pe2 * gkr_ref[...] * ctk_ref[...]
    krs = kr0 + pltpu.roll(kr0, ROPE_DIM, axis=1)
    pem = pe2 * pemask
    ss_pe = jnp.sum(pem * pem, axis=-1, keepdims=True)
    gq1 = gq1_ref[...]
    gq2 = gq2_ref[...]
    gkn = gkn_ref[...]
    for hd in range(MLA_HEADS):
        slab = qf[:, hd * 2 * LANES:(hd + 1) * 2 * LANES]
        sm = slab * qmask
        rq = lax.rsqrt(jnp.sum(sm * sm, axis=-1, keepdims=True) / QK_DIM + NORM_EPS) * q_scale
        t = slab * gq1 * ct + pltpu.roll(slab * gq2 * st, QK_DIM, axis=1)
        q_ref[hd] = (t * rq)[:, :QK_DIM].astype(BF16)
        kn = kvf[:, hd * 2 * LANES:hd * 2 * LANES + NOPE_DIM]
        rk = lax.rsqrt((jnp.sum(kn * kn, axis=-1, keepdims=True) + ss_pe) / QK_DIM + NORM_EPS)
        kslab = jnp.concatenate([kn * gkn, krs], axis=-1) * rk
        k_ref[hd] = kslab[:, :QK_DIM].astype(BF16)
        v_ref[hd] = kvf[:, hd * 2 * LANES + NOPE_DIM:(hd + 1) * 2 * LANES].astype(BF16)

    h = _rms(x_ref[...]) * g_ref[...]
    h = h * (1.0 + sc_ref[...]) + sh_ref[...]
    hb = h.astype(BF16)
    hy_ref[...] = jnp.dot(hb, why_ref[...], preferred_element_type=F32).astype(BF16)
    mla_sc[...] = jnp.dot(hb, wmla_ref[...], preferred_element_type=F32)


@functools.lru_cache(maxsize=None)
def _rope_lane_tables(length, use_rope):
    if use_rope:
        n_freq = ROPE_DIM // 4
        t = np.arange(length)
        inv_freq = np.power(ROPE_THETA, -np.arange(n_freq, dtype=np.float64) / n_freq)
        ar = (t // GRID_W).astype(np.float64)[:, None] * inv_freq
        ac = (t % GRID_W).astype(np.float64)[:, None] * inv_freq
        c64 = np.concatenate([np.cos(ar), np.cos(ar), np.cos(ac), np.cos(ac)], axis=-1)
        s64 = np.concatenate([-np.sin(ar), np.sin(ar), -np.sin(ac), np.sin(ac)], axis=-1)
    else:
        c64 = np.ones((length, ROPE_DIM))
        s64 = np.zeros((length, ROPE_DIM))
    z64 = np.zeros((length, ROPE_DIM))
    ct = np.concatenate([np.ones((length, LANES)), c64, z64], axis=-1)
    st = np.concatenate([np.zeros((length, LANES)), z64, s64], axis=-1)
    ctk = np.concatenate([c64, s64], axis=-1)
    return ct.astype(np.float32), st.astype(np.float32), ctk.astype(np.float32)


_SWAP16 = np.concatenate([np.arange(16, 32), np.arange(0, 16), np.arange(48, 64), np.arange(32, 48)])


def _mla_weights(w_in, mla_w_uq, mla_w_ukv, qk_norm_q_g, qk_norm_k_g):
    hy_cols = 3 * D_HYENA
    w_hy = w_in[:, :hy_cols].astype(BF16)
    w_pe = w_in[:, hy_cols + Q_RANK + KV_RANK:]
    w_mla = jnp.concatenate([w_in[:, hy_cols:hy_cols + Q_RANK + KV_RANK], w_pe, w_pe[:, _SWAP16]],
                            axis=-1).astype(BF16)
    wq = mla_w_uq.reshape(Q_RANK, MLA_HEADS, QK_DIM)
    wq_rope = wq[:, :, NOPE_DIM:]
    w_uq2 = jnp.concatenate([wq[:, :, :NOPE_DIM], wq_rope, wq_rope[:, :, _SWAP16]], axis=-1)
    w_uq2 = w_uq2.reshape(Q_RANK, MLA_HEADS * 2 * LANES).astype(BF16)
    w_ukv2 = mla_w_ukv.astype(BF16)
    gq_r = qk_norm_q_g[NOPE_DIM:]
    z64 = jnp.zeros((ROPE_DIM,), F32)
    gq1 = jnp.concatenate([qk_norm_q_g[:NOPE_DIM], gq_r, z64]).reshape(1, -1)
    gq2 = jnp.concatenate([jnp.zeros((NOPE_DIM,), F32), z64, gq_r[_SWAP16]]).reshape(1, -1)
    gkn = qk_norm_k_g[:NOPE_DIM].reshape(1, -1)
    gk_r = qk_norm_k_g[NOPE_DIM:]
    gkr = jnp.concatenate([gk_r, gk_r[_SWAP16]]).reshape(1, -1)
    return w_hy, w_mla, w_uq2, w_ukv2, gq1, gq2, gkn, gkr


def _inproj(x, shift, scale, norm_g, weights, q_norm_g, kv_norm_g, use_rope, tl):
    b, length, d = x.shape
    w_hy, w_mla, w_uq2, w_ukv2, gq1, gq2, gkn, gkr = weights
    ct, st, ctk = _rope_lane_tables(length, use_rope)
    q_scale = QK_DIM ** -0.5 * math.log2(math.e)
    nt = length // tl
    n_tiles = b * nt
    hyc = w_hy.shape[1]
    cur = lambda i: jnp.minimum(i, n_tiles - 1)
    prev = lambda i: jnp.maximum(i - 1, 0)
    tok = lambda i: (cur(i), 0)
    per_b = lambda i: (cur(i) // nt, 0, 0)
    pos = lambda i: (prev(i) % nt, 0)
    head_blk = lambda i: (prev(i) // nt, 0, prev(i) % nt, 0)
    hy, q, k, v = pl.pallas_call(
        functools.partial(_inproj_body, q_scale=q_scale),
        grid=(n_tiles + 1,),
        in_specs=[pl.BlockSpec((tl, d), tok),
                  pl.BlockSpec((None, 1, d), per_b),
                  pl.BlockSpec((None, 1, d), per_b),
                  _const_spec((1, d)),
                  _const_spec(w_hy.shape), _const_spec(w_mla.shape),
                  _const_spec((1, Q_RANK)), _const_spec(w_uq2.shape),
                  _const_spec((1, KV_RANK)), _const_spec(w_ukv2.shape),
                  pl.BlockSpec((tl, 2 * LANES), pos), pl.BlockSpec((tl, 2 * LANES), pos),
                  pl.BlockSpec((tl, LANES), pos),
                  _const_spec((1, 2 * LANES)), _const_spec((1, 2 * LANES)),
                  _const_spec((1, LANES)), _const_spec((1, LANES))],
        out_specs=[pl.BlockSpec((tl, hyc), tok),
                   pl.BlockSpec((None, MLA_HEADS, tl, QK_DIM), head_blk),
                   pl.BlockSpec((None, MLA_HEADS, tl, QK_DIM), head_blk),
                   pl.BlockSpec((None, MLA_HEADS, tl, V_DIM), head_blk)],
        out_shape=[jax.ShapeDtypeStruct((b * length, hyc), BF16),
                   jax.ShapeDtypeStruct((b, MLA_HEADS, length, QK_DIM), BF16),
                   jax.ShapeDtypeStruct((b, MLA_HEADS, length, QK_DIM), BF16),
                   jax.ShapeDtypeStruct((b, MLA_HEADS, length, V_DIM), BF16)],
        scratch_shapes=[pltpu.VMEM((tl, w_mla.shape[1]), F32)],
        compiler_params=_cparams(("arbitrary",), VMEM_LIMIT),
        name="inproj_mla",
    )(x.reshape(b * length, d), shift, scale, norm_g.reshape(1, d), w_hy, w_mla,
      q_norm_g.reshape(1, -1), w_uq2, kv_norm_g.reshape(1, -1), w_ukv2, ct, st, ctk, gq1, gq2, gkn, gkr)
    return hy.reshape(b, length, hyc), q, k, v


def _filter_body(z_ref, w1_ref, b1_ref, w2_ref, b2_ref, w3_ref, b3_ref, w4_ref, fr_ref, dl_ref,
                 kern_ref, asum_ref, *, zero_row, tr):
    i = pl.program_id(0)
    fr = fr_ref[...]
    hr = tr // 2
    za = z_ref[:hr, :]
    zb = z_ref[hr:, :]
    dot = functools.partial(jnp.dot, precision=HIGHEST, preferred_element_type=F32)
    h = jnp.sin(fr * (dot(jnp.concatenate([za, zb], axis=1), w1_ref[...]) + b1_ref[...]))
    h = jnp.sin(fr * (dot(h, w2_ref[...]) + b2_ref[...]))
    h = jnp.sin(fr * (dot(h, w3_ref[...]) + b3_ref[...]))
    o2 = dot(h, w4_ref[...])
    nc = o2.shape[1] // 2
    oa = o2[:, :nc] * jnp.exp(-za[:, 0:1] * dl_ref[...])
    ob = o2[:, nc:] * jnp.exp(-zb[:, 0:1] * dl_ref[...])

    @pl.when(i == 0)
    def _():
        asum_ref[...] = jnp.zeros_like(asum_ref)

    asum_ref[...] += (jnp.sum(jnp.abs(oa), axis=0, keepdims=True)
                      + jnp.sum(jnp.abs(ob), axis=0, keepdims=True))
    row = i * tr + lax.broadcasted_iota(jnp.int32, (hr, 1), 0)
    kern_ref[:hr, :] = jnp.where(row == zero_row, 0.0, oa)
    kern_ref[hr:, :] = jnp.where(row + hr == zero_row, 0.0, ob)


@functools.lru_cache(maxsize=None)
def _filter_features(length):
    n = 2 * length
    bands = (POS_EMB_DIM - 1) // 2
    pos = np.concatenate([np.arange(length), (n - np.arange(length, n)) % length])
    t_tab = np.linspace(0.0, 1.0, length)[:, None]
    w_ang = 2.0 * np.pi * np.arange(length, dtype=np.float64)[:, None] / length
    f = np.linspace(1e-4, bands - 1, bands)[None, :]
    z_tab = np.concatenate([t_tab, np.cos(f * w_ang), -np.sin(f * w_ang)], axis=-1)
    z = np.pad(z_tab[pos], ((0, 0), (0, LANES - POS_EMB_DIM)))
    return z.astype(np.float32)


def _hyena_kernel_taps(length, w1, b1, w2, b2, w3, b3, w4, freq):
    n = 2 * length
    z = jnp.asarray(_filter_features(length))
    deltas = jnp.abs(jnp.linspace(MIN_DECAY, MAX_DECAY, D_HYENA, dtype=F32)).reshape(1, -1)
    tr = min(1024, length)
    half_steps = length // tr
    fo = FILTER_ORDER

    def pair(w):
        zeros = jnp.zeros_like(w)
        return jnp.concatenate([jnp.concatenate([w, zeros], axis=1),
                                jnp.concatenate([zeros, w], axis=1)], axis=0)

    def twice(v):
        return jnp.concatenate([v, v]).reshape(1, 2 * fo)

    w1p = pair(jnp.pad(w1, ((0, LANES - POS_EMB_DIM), (0, 0))))
    w4p = jnp.stack([pair(w4[:, :D_HYENA]), pair(w4[:, D_HYENA:])])
    kern, asum = pl.pallas_call(
        functools.partial(_filter_body, zero_row=length, tr=tr),
        grid=(n // tr,),
        in_specs=[pl.BlockSpec((tr, LANES), lambda i: (i, 0)),
                  _const_spec((2 * LANES, 2 * fo)), _const_spec((1, 2 * fo)),
                  _const_spec((2 * fo, 2 * fo)), _const_spec((1, 2 * fo)),
                  _const_spec((2 * fo, 2 * fo)), _const_spec((1, 2 * fo)),
                  pl.BlockSpec((None, 2 * fo, 2 * D_HYENA), lambda i: (i // half_steps, 0, 0)),
                  _const_spec((1, 2 * fo)), _const_spec((1, D_HYENA))],
        out_specs=[pl.BlockSpec((tr, D_HYENA), lambda i: (i, 0)),
                   _const_spec((1, D_HYENA))],
        out_shape=[jax.ShapeDtypeStruct((n, D_HYENA), F32),
                   jax.ShapeDtypeStruct((1, D_HYENA), F32)],
        compiler_params=_cparams(("arbitrary",), VMEM_LIMIT),
        name="hyena_filter",
    )(z, w1p, twice(b1), pair(w2), twice(b2), pair(w3), twice(b3), w4p, twice(freq), deltas)
    return kern, asum


def _half_rows(n):
    n_half = n // DFT_Q // 2 + 1
    return n_half, -(-n_half // 8) * 8


@functools.lru_cache(maxsize=None)
def _dft_tables(n, p_in):
    q_sz = DFT_Q
    p_sz = n // q_sz
    n_half, n_r = _half_rows(n)
    r = np.arange(n_r, dtype=np.float64)
    keep = (r < n_half).astype(np.float64)
    qq = np.arange(q_sz, dtype=np.float64)
    pp = np.arange(p_in, dtype=np.float64)
    tt = q_sz * pp[None, None, :] + qq[:, None, None]
    ang = -2.0 * np.pi * r[None, :, None] * tt / n
    t1 = np.concatenate([np.cos(ang), np.sin(ang)], axis=1) * np.tile(keep, 2)[None, :, None]
    mirror = np.where((r == 0) | (r == p_sz // 2), 1.0, 2.0)
    t4 = np.transpose(t1 * np.tile(mirror, 2)[None, :, None], (0, 2, 1)) / n
    a2 = -2.0 * np.pi * np.outer(qq, qq) / q_sz
    fre, fim = np.cos(a2), np.sin(a2)
    m2 = np.block([[fre, -fim], [fim, fre]])
    m2c = np.block([[fre, fim], [-fim, fre]])
    return (t1.astype(np.float32), t4.astype(np.float32), m2.astype(np.float32),
            m2c.astype(np.float32))


SUBLANES = 8


def _spec_block(rg, h):
    return (rg * 2 + h) * SUBLANES * DFT_Q


def _dft_stage1(x_ref, t1_ref, spec_ref, n_r, p_in):
    q_sz = DFT_Q

    def body(q, carry):
        xq = x_ref[pl.ds(q, p_in, stride=q_sz), :].astype(BF16)
        a = jnp.dot(t1_ref[q], xq, preferred_element_type=F32)
        row = pl.multiple_of(q * SUBLANES, SUBLANES)
        for h in range(2):
            for rg in range(n_r // SUBLANES):
                src = h * n_r + rg * SUBLANES
                spec_ref[pl.ds(_spec_block(rg, h) + row, SUBLANES), :] = a[src:src + SUBLANES]
        return carry

    lax.fori_loop(0, q_sz, body, 0, unroll=8)


def _spec_rows(rg, h, r8):
    return pl.ds(_spec_block(rg, h) + r8, DFT_Q, stride=SUBLANES)


def _spectrum_body(kern_ref, asum_ref, t1_ref, m2_ref, kf_ref, spec_ref, *, n_r, p_in):
    _dft_stage1(kern_ref, t1_ref, spec_ref, n_r, p_in)
    inv = 1.0 / asum_ref[...]
    m2 = m2_ref[...]

    ct = spec_ref.shape[1]

    def body(rg, carry):
        for r8 in range(0, SUBLANES, 2):
            blk = jnp.concatenate(
                [jnp.concatenate([spec_ref[_spec_rows(rg, 0, r8 + j), :],
                                  spec_ref[_spec_rows(rg, 1, r8 + j), :]], axis=0) for j in range(2)],
                axis=1)
            xf = jnp.dot(m2, blk.astype(BF16), preferred_element_type=F32)
            for j in range(2):
                kf_ref[rg * SUBLANES + r8 + j] = (xf[:, j * ct:(j + 1) * ct] * inv).astype(BF16)
        return carry

    lax.fori_loop(0, n_r // SUBLANES, body, 0)


def _hyena_filter_spectrum(kern, asum):
    n, c = kern.shape
    p_sz = n // DFT_Q
    _, n_r = _half_rows(n)
    t1, _, m2, _ = _dft_tables(n, p_sz)
    t1 = jnp.asarray(t1).astype(BF16)
    m2 = jnp.asarray(m2).astype(BF16)
    ct = LANES
    return pl.pallas_call(
        functools.partial(_spectrum_body, n_r=n_r, p_in=p_sz),
        grid=(c // ct,),
        in_specs=[_single_spec((n, ct), lambda j: (0, j)),
                  pl.BlockSpec((1, ct), lambda j: (0, j)),
                  _single_spec(t1.shape, lambda j: (0, 0, 0)),
                  _const_spec(m2.shape)],
        out_specs=pl.BlockSpec((n_r, 2 * DFT_Q, ct), lambda j: (0, 0, j)),
        out_shape=jax.ShapeDtypeStruct((n_r, 2 * DFT_Q, c), BF16),
        scratch_shapes=[pltpu.VMEM((n_r * 2 * DFT_Q, ct), F32)],
        compiler_params=_cparams(("parallel",), VMEM_LIMIT),
        name="hyena_filter_spectrum",
    )(kern, asum, t1, m2)


def _short_conv_chunk(u_ref, w_ref, b_ref, i, rows, length):
    pack = 16
    base = pl.multiple_of(i * rows, rows)
    u = u_ref[pl.ds(base, rows), :].astype(F32)
    lo = pl.multiple_of(jnp.maximum(base - pack, 0), pack)
    hi = pl.multiple_of(jnp.minimum(base + rows, length - pack), pack)
    prev = u_ref[pl.ds(lo, pack), :].astype(F32)[pack - 1:pack]
    nxt = u_ref[pl.ds(hi, pack), :].astype(F32)[0:1]
    prev = jnp.where(base == 0, 0.0, prev)
    nxt = jnp.where(base + rows == length, 0.0, nxt)
    ridx = lax.broadcasted_iota(jnp.int32, (rows, 1), 0)
    up = jnp.where(ridx == 0, prev, pltpu.roll(u, 1, axis=0))
    dn = jnp.where(ridx == rows - 1, nxt, pltpu.roll(u, rows - 1, axis=0))
    w = w_ref[...]
    return up * w[0:1] + u * w[1:2] + dn * w[2:3] + b_ref[...]


def _hyena_conv_body(x0_ref, x1_ref, v_ref, w0_ref, w1_ref, wv_ref, b0_ref, b1_ref, bv_ref,
                     bias_ref, kf_ref, t1_ref, t4_ref, m2_ref, m2c_ref, o_ref, vx_ref, spec_ref,
                     *, length, rows):
    q_sz = DFT_Q
    n_half, n_r = _half_rows(2 * length)
    p_in = length // q_sz
    n_chunks = length // rows

    def gate_in(i, carry):
        x1c = _short_conv_chunk(x1_ref, w1_ref, b1_ref, i, rows, length)
        vc = _short_conv_chunk(v_ref, wv_ref, bv_ref, i, rows, length)
        vx_ref[pl.ds(pl.multiple_of(i * rows, rows), rows), :] = vc * x1c
        return carry

    lax.fori_loop(0, n_chunks, gate_in, 0)
    _dft_stage1(vx_ref, t1_ref, spec_ref, n_r, p_in)
    m2 = m2_ref[...]
    m2c = m2c_ref[...]

    ct = spec_ref.shape[1]

    def forward_rows(rg, r8s):
        blk = jnp.concatenate(
            [jnp.concatenate([spec_ref[_spec_rows(rg, 0, r8), :], spec_ref[_spec_rows(rg, 1, r8), :]],
                             axis=0) for r8 in r8s], axis=1)
        return jnp.dot(m2, blk.astype(BF16), preferred_element_type=F32)

    def filter_inverse_rows(rg, r8s, xf):
        kf = jnp.concatenate([kf_ref[rg * SUBLANES + r8] for r8 in r8s], axis=1).astype(F32)
        xre, xim = xf[:q_sz], xf[q_sz:]
        kre, kim = kf[:q_sz], kf[q_sz:]
        z = jnp.concatenate([xre * kre - xim * kim, xre * kim + xim * kre], axis=0)
        bf = jnp.dot(m2c, z.astype(BF16), preferred_element_type=F32)
        for j, r8 in enumerate(r8s):
            spec_ref[_spec_rows(rg, 0, r8), :] = bf[:q_sz, j * ct:(j + 1) * ct]
            spec_ref[_spec_rows(rg, 1, r8), :] = bf[q_sz:, j * ct:(j + 1) * ct]

    def mid_rows(rg, n_rows):
        groups = [tuple(range(a, min(a + 2, n_rows))) for a in range(0, n_rows, 2)]
        pending = None
        for r8s in groups:
            xf = forward_rows(rg, r8s)
            if pending is not None:
                filter_inverse_rows(rg, *pending)
            pending = (r8s, xf)
        filter_inverse_rows(rg, *pending)

    def mid(rg, carry):
        mid_rows(rg, SUBLANES)
        return carry

    lax.fori_loop(0, n_half // SUBLANES, mid, 0)
    if n_half % SUBLANES:
        mid_rows(n_half // SUBLANES, n_half % SUBLANES)
    bias = bias_ref[...]

    def last(q, carry):
        row = pl.multiple_of(q * SUBLANES, SUBLANES)
        tiles = [spec_ref[pl.ds(_spec_block(rg, h) + row, SUBLANES), :]
                 for h in range(2) for rg in range(n_r // SUBLANES)]
        bq = jnp.concatenate(tiles, axis=0).astype(BF16)
        y = jnp.dot(t4_ref[q], bq, preferred_element_type=F32)
        sl = pl.ds(q, p_in, stride=q_sz)
        vx_ref[sl, :] = y + bias * vx_ref[sl, :]
        return carry

    lax.fori_loop(0, q_sz, last, 0, unroll=8)

    def gate_out(i, carry):
        x0c = _short_conv_chunk(x0_ref, w0_ref, b0_ref, i, rows, length)
        sl = pl.ds(pl.multiple_of(i * rows, rows), rows)
        o_ref[sl, :] = (vx_ref[sl, :] * x0c).astype(BF16)
        return carry

    lax.fori_loop(0, n_chunks, gate_out, 0)


def _hyena_conv(hy, conv_w, conv_b, bias, kf):
    b, length, _ = hy.shape
    ch = D_HYENA
    ct = LANES
    nct = ch // ct
    n = 2 * length
    _, n_r = _half_rows(n)
    p_in = length // DFT_Q
    t1, t4, m2, m2c = _dft_tables(n, p_in)
    t1, t4, m2, m2c = (jnp.asarray(a).astype(BF16) for a in (t1, t4, m2, m2c))
    rows = min(512, length)
    col = lambda off: (lambda j, bi: (bi, 0, off * nct + j))
    wcol = lambda off: (lambda j, bi: (0, off * nct + j))
    conv_b2 = conv_b.reshape(1, -1)
    return pl.pallas_call(
        functools.partial(_hyena_conv_body, length=length, rows=rows),
        grid=(nct, b),
        in_specs=[_single_spec((None, length, ct), col(0)),
                  _single_spec((None, length, ct), col(1)),
                  _single_spec((None, length, ct), col(2)),
                  pl.BlockSpec((3, ct), wcol(0)), pl.BlockSpec((3, ct), wcol(1)),
                  pl.BlockSpec((3, ct), wcol(2)),
                  pl.BlockSpec((1, ct), wcol(0)), pl.BlockSpec((1, ct), wcol(1)),
                  pl.BlockSpec((1, ct), wcol(2)),
                  pl.BlockSpec((1, ct), lambda j, bi: (0, j)),
                  _single_spec((n_r, 2 * DFT_Q, ct), lambda j, bi: (0, 0, j)),
                  _single_spec(t1.shape, lambda j, bi: (0, 0, 0)),
                  _single_spec(t4.shape, lambda j, bi: (0, 0, 0)),
                  _const_spec(m2.shape), _const_spec(m2c.shape)],
        out_specs=pl.BlockSpec((None, length, ct), lambda j, bi: (bi, 0, j)),
        out_shape=jax.ShapeDtypeStruct((b, length, ch), BF16),
        scratch_shapes=[pltpu.VMEM((length, ct), F32),
                        pltpu.VMEM((n_r * 2 * DFT_Q, ct), F32)],
        compiler_params=_cparams(("parallel", "parallel"), VMEM_LIMIT),
        name="hyena_conv",
    )(hy, hy, hy, conv_w, conv_w, conv_w, conv_b2, conv_b2, conv_b2, bias.reshape(1, ch), kf,
      t1, t4, m2, m2c)


def _flash_body(q_ref, k_ref, v_ref, kc_ref, vc_ref, o_ref, s_ref, p_ref, al_ref, m_ref, l_ref,
                acc_ref, *, tk, n_sub, rg):
    tq = q_ref.shape[0]
    sub = tq // n_sub
    nk = k_ref.shape[0] // tk
    nc = kc_ref.shape[0]
    n_chunks = nk + 1
    nt = (((1,), (1,)), ((), ()))

    def width(c):
        return nc if c == nk else tk

    def scores(c):
        keys = kc_ref[...] if c == nk else k_ref[c * tk:(c + 1) * tk, :]
        s_ref[c % 3, :, :width(c)] = lax.dot_general(q_ref[...], keys, nt, preferred_element_type=F32)

    def weighted_values(c):
        vals = vc_ref[...] if c == nk else v_ref[c * tk:(c + 1) * tk, :]
        return [jnp.dot(p_ref[c % 2, i * sub:(i + 1) * sub, :width(c)], vals,
                        preferred_element_type=F32) for i in range(n_sub)]

    def softmax(c):
        w = width(c)
        for g in range(tq // rg):
            rows = slice(g * rg, (g + 1) * rg)
            s = s_ref[c % 3, rows, :w]
            m_prev = m_ref[rows, :]
            m_new = jnp.maximum(m_prev, jnp.max(s, axis=-1, keepdims=True))
            alpha = jnp.exp2(m_prev - m_new)
            p = jnp.exp2(s - jnp.concatenate([m_new] * (w // LANES), axis=1))
            part = p[:, :LANES]
            for j in range(1, w // LANES):
                part = part + p[:, j * LANES:(j + 1) * LANES]
            l_ref[rows, :] = alpha * l_ref[rows, :] + part
            m_ref[rows, :] = m_new
            al_ref[rows, :] = alpha
            p_ref[c % 2, rows, :w] = p.astype(BF16)

    m_ref[...] = jnp.full_like(m_ref, -jnp.inf)
    l_ref[...] = jnp.zeros_like(l_ref)
    acc_ref[...] = jnp.zeros_like(acc_ref)
    scores(0)
    for c in range(n_chunks):
        prod = weighted_values(c - 1) if c >= 1 else None
        if c + 1 < n_chunks:
            scores(c + 1)
        softmax(c)
        if prod is not None:
            for i in range(n_sub):
                srows = slice(i * sub, (i + 1) * sub)
                acc_ref[srows, :] = al_ref[srows, :] * (acc_ref[srows, :] + prod[i])
    prod = weighted_values(n_chunks - 1)
    l_fin = jnp.sum(l_ref[...], axis=-1, keepdims=True)
    for i in range(n_sub):
        srows = slice(i * sub, (i + 1) * sub)
        o_ref[srows, :] = ((acc_ref[srows, :] + prod[i]) / l_fin[srows]).astype(o_ref.dtype)


def _flash_attention(q, k, v, kc, vc, tq, tk, n_sub, rg):
    b, hds, length, _ = q.shape
    nc = kc.shape[2]
    assert length % tk == 0 and nc <= tk and nc % LANES == 0 and V_DIM == LANES
    return pl.pallas_call(
        functools.partial(_flash_body, tk=tk, n_sub=n_sub, rg=rg),
        grid=(b, hds, length // tq),
        in_specs=[pl.BlockSpec((None, None, tq, QK_DIM), lambda bi, h, qi: (bi, h, qi, 0)),
                  pl.BlockSpec((None, None, length, QK_DIM), lambda bi, h, qi: (bi, h, 0, 0)),
                  pl.BlockSpec((None, None, length, V_DIM), lambda bi, h, qi: (bi, h, 0, 0)),
                  pl.BlockSpec((None, None, nc, QK_DIM), lambda bi, h, qi: (bi, h, 0, 0)),
                  pl.BlockSpec((None, None, nc, V_DIM), lambda bi, h, qi: (bi, h, 0, 0))],
        out_specs=pl.BlockSpec((None, tq, V_DIM), lambda bi, h, qi: (bi, qi, h)),
        out_shape=jax.ShapeDtypeStruct((b, length, hds * V_DIM), BF16),
        scratch_shapes=[pltpu.VMEM((3, tq, tk), F32), pltpu.VMEM((2, tq, tk), BF16),
                        pltpu.VMEM((tq, LANES), F32), pltpu.VMEM((tq, LANES), F32),
                        pltpu.VMEM((tq, LANES), F32), pltpu.VMEM((tq, V_DIM), F32)],
        compiler_params=_cparams(("parallel", "parallel", "parallel"), VMEM_LIMIT),
        name="mla_flash_attention",
    )(q, k, v, kc, vc)


def _store_packed_rows(dst_ref, x):
    half = x.shape[1] // 2
    for j in range(half // LANES):
        hi = x[:, j * LANES:(j + 1) * LANES].astype(BF16).astype(F32)
        lo = x[:, half + j * LANES:half + (j + 1) * LANES].astype(BF16).astype(F32)
        dst_ref[j] = (lax.bitcast_convert_type(hi, jnp.uint32)
                      | (lax.bitcast_convert_type(lo, jnp.uint32) >> 16))


def _unpack_words(w):
    hi = lax.bitcast_convert_type(w & jnp.uint32(0xFFFF0000), F32)
    lo = lax.bitcast_convert_type(w << 16, F32)
    return hi, lo


def _load_packed_rows(src_ref):
    parts = [_unpack_words(src_ref[j]) for j in range(src_ref.shape[0])]
    return jnp.concatenate([p[0] for p in parts] + [p[1] for p in parts], axis=-1)


def _outproj_body(yh_ref, ya_ref, x_ref, g1_ref, sh2_ref, sc2_ref, n2g_ref, wo1_ref, wo2_ref,
                  rwt_ref, rb_ref, tri_ref, xn_ref, h2_ref, idx_ref, gate_ref, rank_ref, cnt_ref,
                  carry_sc, lg_sc):
    i = pl.program_id(0)

    @pl.when(i == 0)
    def _():
        carry_sc[...] = jnp.zeros_like(carry_sc)
        lg_sc[...] = jnp.zeros_like(lg_sc)

    logits = lg_sc[...]
    n_e, tt = logits.shape
    eidx = lax.broadcasted_iota(jnp.int32, (n_e, tt), 0).astype(F32)
    work = logits
    vals, sels, idxs = [], [], []
    for _ in range(TOP_K):
        m = jnp.max(work, axis=0, keepdims=True)
        ix = jnp.min(jnp.where(work == m, eidx, float(n_e)), axis=0, keepdims=True)
        sel = eidx == ix
        work = jnp.where(sel, -jnp.inf, work)
        vals.append(m)
        idxs.append(ix)
        sels.append(sel)
    es = [jnp.exp(vk - vals[0]) for vk in vals]
    den = es[0] + es[1] + es[2] + es[3]
    gate_ref[...] = jnp.concatenate(es, axis=0) / den
    idx_ref[...] = jnp.concatenate(idxs, axis=0).astype(jnp.int32)
    onehot = jnp.zeros((n_e, tt), F32)
    for sel in sels:
        onehot = onehot + sel.astype(F32)

    mix = (jnp.dot(yh_ref[...], wo1_ref[...], preferred_element_type=F32)
           + jnp.dot(ya_ref[...], wo2_ref[...], preferred_element_type=F32))
    xn = x_ref[...] + g1_ref[...] * mix
    xn_ref[...] = xn
    h2 = _rms(xn) * n2g_ref[...]
    h2 = h2 * (1.0 + sc2_ref[...]) + sh2_ref[...]
    _store_packed_rows(h2_ref, h2)
    lg_sc[...] = lax.dot_general(rwt_ref[...], h2, (((1,), (1,)), ((), ())), precision=HIGHEST,
                                 preferred_element_type=F32) + rb_ref[...]

    prefix = jnp.dot(onehot.astype(BF16), tri_ref[...], preferred_element_type=F32) + carry_sc[...]
    ranks = [jnp.sum(jnp.where(sel, prefix, 0.0), axis=0, keepdims=True) for sel in sels]
    rank_ref[...] = jnp.concatenate(ranks, axis=0).astype(jnp.int32)
    carry_sc[...] += jnp.where(i > 0, jnp.sum(onehot, axis=1, keepdims=True), 0.0)
    cnt_ref[...] = jnp.broadcast_to(carry_sc[...], cnt_ref.shape)


def _outproj_router(y_hy, y_att, x2, g1, sh2, sc2, norm2_g, w_out, router_w, router_b, tokens_per_batch,
                    tt):
    t, d = x2.shape
    ch = y_hy.shape[1]
    n_e = router_w.shape[1]
    wo1 = w_out[:ch].astype(BF16)
    wo2 = w_out[ch:].astype(BF16)
    tri = jnp.asarray(np.triu(np.ones((tt, tt), np.float32), k=1), BF16)
    steps_per_batch = tokens_per_batch // tt
    n_tiles = t // tt
    cur = lambda i: jnp.minimum(i, n_tiles - 1)
    tok = lambda i: (cur(i), 0)
    per_b = lambda i: (cur(i) // steps_per_batch, 0, 0)
    lanes_tok = lambda i: (0, jnp.maximum(i - 1, 0))
    return pl.pallas_call(
        _outproj_body,
        grid=(n_tiles + 1,),
        in_specs=[pl.BlockSpec((tt, ch), tok), pl.BlockSpec((tt, ch), tok),
                  pl.BlockSpec((tt, d), tok),
                  pl.BlockSpec((None, 1, d), per_b), pl.BlockSpec((None, 1, d), per_b),
                  pl.BlockSpec((None, 1, d), per_b),
                  _const_spec((1, d)), _const_spec(wo1.shape), _const_spec(wo2.shape),
                  _const_spec((n_e, d)), _const_spec((n_e, 1)), _const_spec((tt, tt))],
        out_specs=[pl.BlockSpec((tt, d), tok),
                   pl.BlockSpec((d // (2 * LANES), tt, LANES), lambda i: (0, cur(i), 0)),
                   pl.BlockSpec((TOP_K, tt), lanes_tok), pl.BlockSpec((TOP_K, tt), lanes_tok),
                   pl.BlockSpec((TOP_K, tt), lanes_tok), _const_spec((n_e, LANES))],
        out_shape=[jax.ShapeDtypeStruct((t, d), F32),
                   jax.ShapeDtypeStruct((d // (2 * LANES), t, LANES), jnp.uint32),
                   jax.ShapeDtypeStruct((TOP_K, t), jnp.int32),
                   jax.ShapeDtypeStruct((TOP_K, t), F32),
                   jax.ShapeDtypeStruct((TOP_K, t), jnp.int32),
                   jax.ShapeDtypeStruct((n_e, LANES), F32)],
        scratch_shapes=[pltpu.VMEM((n_e, 1), F32), pltpu.VMEM((n_e, tt), F32)],
        compiler_params=_cparams(("arbitrary",), VMEM_LIMIT),
        name="outproj_router",
    )(y_hy, y_att, x2, g1, sh2, sc2, norm2_g.reshape(1, d), wo1, wo2, router_w.T,
      router_b.reshape(n_e, 1), tri)


def _cast_rows(src_ref, dst_ref, chunk):
    def body(c, carry):
        sl = pl.ds(pl.multiple_of(c * chunk, chunk), chunk)
        dst_ref[sl, :] = src_ref[sl, :].astype(dst_ref.dtype)
        return carry

    lax.fori_loop(0, src_ref.shape[0] // chunk, body, 0)


def _expert_body(be_ref, nvalid_ref, xs_ref, wgu_ref, bgu_ref, wd_ref, bd_ref, ys_ref, wgu_bf, wd_bf):
    i = pl.program_id(0)
    n_valid = nvalid_ref[i]
    active = n_valid > 0
    new_expert = jnp.logical_or(i == 0, be_ref[i] != be_ref[jnp.maximum(i - 1, 0)])

    @pl.when(jnp.logical_and(active, new_expert))
    def _():
        _cast_rows(wgu_ref, wgu_bf, 128)
        _cast_rows(wd_ref, wd_bf, 128)

    @pl.when(active)
    def _():
        row = lax.broadcasted_iota(jnp.int32, (xs_ref.shape[1], 1), 0)
        xs = jnp.where(row < n_valid, _load_packed_rows(xs_ref), 0.0).astype(BF16)
        gu = jnp.dot(xs, wgu_bf[...], preferred_element_type=F32) + bgu_ref[...]
        dff = gu.shape[1] // 2
        gate = jnp.minimum(gu[:, :dff], SWIGLU_LIMIT)
        up = jnp.clip(gu[:, dff:], -SWIGLU_LIMIT, SWIGLU_LIMIT)
        act = (up + 1.0) * (gate * jax.nn.sigmoid(SWIGLU_ALPHA * gate))
        ys = jnp.dot(act.astype(BF16), wd_bf[...], preferred_element_type=F32) + bd_ref[...]
        _store_packed_rows(ys_ref, ys)

    @pl.when(jnp.logical_not(active))
    def _():
        ys_ref[...] = jnp.zeros_like(ys_ref)


def _expert_blocks(xs, block_e, n_valid, w_gu, b_gu, w_down, b_down):
    n_seg, n_rows, _ = xs.shape
    n_e, d, dff2 = w_gu.shape
    bm = MOE_ROWS
    seg_block = pl.BlockSpec((n_seg, bm, LANES), lambda i, be, nu: (0, i, 0))
    grid_spec = pltpu.PrefetchScalarGridSpec(
        num_scalar_prefetch=2,
        grid=(n_rows // bm,),
        in_specs=[seg_block,
                  pl.BlockSpec((None, d, dff2), lambda i, be, nu: (be[i], 0, 0)),
                  pl.BlockSpec((None, 1, dff2), lambda i, be, nu: (be[i], 0, 0)),
                  pl.BlockSpec((None, dff2 // 2, d), lambda i, be, nu: (be[i], 0, 0)),
                  pl.BlockSpec((None, 1, d), lambda i, be, nu: (be[i], 0, 0))],
        out_specs=seg_block,
        scratch_shapes=[pltpu.VMEM((d, dff2), BF16), pltpu.VMEM((dff2 // 2, d), BF16)],
    )
    return pl.pallas_call(
        _expert_body,
        grid_spec=grid_spec,
        out_shape=jax.ShapeDtypeStruct(xs.shape, jnp.uint32),
        compiler_params=_cparams(("arbitrary",), VMEM_LIMIT),
        name="moe_experts",
    )(block_e, n_valid, xs, w_gu, b_gu.reshape(n_e, 1, dff2), w_down, b_down.reshape(n_e, 1, d))


def _sc_gather(table, idx):
    n = idx.shape[0]
    width = table.shape[1]
    mesh = plsc.VectorSubcoreMesh(core_axis_name="core", subcore_axis_name="subcore")
    n_workers = mesh.num_cores * mesh.num_subcores
    assert width == LANES and n % (SC_WINDOW * n_workers) == 0

    @functools.partial(pl.kernel, out_type=jax.ShapeDtypeStruct((n, width), table.dtype), mesh=mesh)
    def gather_kernel(table_hbm, idx_hbm, out_hbm):
        def body(idx_vmem, out_vmem):
            pltpu.sync_copy(table_hbm.at[idx_vmem.at[0]], out_vmem)

        pltpu.emit_pipeline(
            body,
            grid=(n // SC_WINDOW,),
            in_specs=[pl.BlockSpec((1, SC_WINDOW), lambda i: (0, i))],
            out_specs=[pl.BlockSpec((SC_WINDOW, width), lambda i: (i, 0))],
            core_axis_name=("core", "subcore"),
            dimension_semantics=(pltpu.PARALLEL,),
        )(idx_hbm, out_hbm)

    return gather_kernel(table, idx.reshape(1, n))


def _sc_scatter(rows, idx, n_copies, n_out):
    n, width = rows.shape
    mesh = plsc.VectorSubcoreMesh(core_axis_name="core", subcore_axis_name="subcore")
    n_workers = mesh.num_cores * mesh.num_subcores
    assert width == LANES and n % (SC_WINDOW * n_workers) == 0 and idx.shape == (8, n)

    @functools.partial(pl.kernel, out_type=jax.ShapeDtypeStruct((n_out, width), rows.dtype), mesh=mesh)
    def scatter_kernel(rows_hbm, idx_hbm, out_hbm):
        def body(rows_vmem, idx_vmem):
            for k in range(n_copies):
                pltpu.sync_copy(rows_vmem, out_hbm.at[idx_vmem.at[k]])

        pltpu.emit_pipeline(
            body,
            grid=(n // SC_WINDOW,),
            in_specs=[pl.BlockSpec((SC_WINDOW, width), lambda i: (i, 0)),
                      pl.BlockSpec((8, SC_WINDOW), lambda i: (0, i))],
            out_specs=[],
            core_axis_name=("core", "subcore"),
            dimension_semantics=(pltpu.PARALLEL,),
        )(rows_hbm, idx_hbm)

    return scatter_kernel(rows, idx)


def _combine_body(pk_ref, gt_ref, xn_ref, g2_ref, o_ref):
    n_seg = pk_ref.shape[0]
    half = o_ref.shape[1] // 2
    gt = gt_ref[...]
    g2 = g2_ref[...]
    for j in range(n_seg):
        acc_hi = None
        acc_lo = None
        for kk in range(TOP_K):
            hi, lo = _unpack_words(pk_ref[j, kk])
            g = gt[:, kk:kk + 1]
            acc_hi = g * hi if acc_hi is None else acc_hi + g * hi
            acc_lo = g * lo if acc_lo is None else acc_lo + g * lo
        c_hi = slice(j * LANES, (j + 1) * LANES)
        c_lo = slice(half + j * LANES, half + (j + 1) * LANES)
        o_ref[:, c_hi] = xn_ref[:, c_hi] + g2[:, c_hi] * acc_hi
        o_ref[:, c_lo] = xn_ref[:, c_lo] + g2[:, c_lo] * acc_lo


def _combine(picked, gates_t, xn, g2, tokens_per_batch, tt):
    n_seg, _, t, _ = picked.shape
    d = xn.shape[1]
    steps_per_batch = tokens_per_batch // tt
    return pl.pallas_call(
        _combine_body,
        grid=(t // tt,),
        in_specs=[pl.BlockSpec((n_seg, TOP_K, tt, LANES), lambda i: (0, 0, i, 0)),
                  pl.BlockSpec((tt, TOP_K), lambda i: (i, 0)),
                  pl.BlockSpec((tt, d), lambda i: (i, 0)),
                  pl.BlockSpec((None, 1, d), lambda i: (i // steps_per_batch, 0, 0))],
        out_specs=pl.BlockSpec((tt, d), lambda i: (i, 0)),
        out_shape=jax.ShapeDtypeStruct((t, d), F32),
        compiler_params=_cparams(("parallel",), VMEM_LIMIT),
        name="moe_combine",
    )(picked, gates_t, xn, g2)


def _moe(h2p, xn, g2, idx, gates, ranks, counts, w_gu, b_gu, w_down, b_down, tokens_per_batch):
    n_seg, t, _ = h2p.shape
    bm = MOE_ROWS
    n_e = w_gu.shape[0]
    cnt = counts[:, 0].astype(jnp.int32)
    padded = (cnt + bm - 1) // bm * bm
    padded_ends = jnp.cumsum(padded)
    padded_starts = padded_ends - padded
    experts = jnp.arange(n_e, dtype=jnp.int32)[:, None, None]
    dest = ranks + jnp.sum(jnp.where(idx[None] == experts, padded_starts[:, None, None], 0), axis=0)
    n_blocks = t * TOP_K // bm + n_e
    n_rows = n_blocks * bm
    block_start = jnp.arange(n_blocks, dtype=jnp.int32) * bm
    block_e = jnp.minimum(jnp.sum(padded_ends[None, :] <= block_start[:, None], axis=1),
                          n_e - 1).astype(jnp.int32)
    n_valid = jnp.clip(cnt[block_e] - (block_start - padded_starts[block_e]), 0, bm).astype(jnp.int32)
    seg = jnp.arange(n_seg, dtype=jnp.int32)
    scatter_idx = (seg[None, :, None] * n_rows + dest[:, None, :]).reshape(TOP_K, n_seg * t)
    scatter_idx = jnp.concatenate([scatter_idx, scatter_idx], axis=0)
    xs = _sc_scatter(h2p.reshape(n_seg * t, LANES), scatter_idx, TOP_K, n_seg * n_rows)
    ys = _expert_blocks(xs.reshape(n_seg, n_rows, LANES), block_e, n_valid, w_gu, b_gu, w_down, b_down)
    picked = _sc_gather(ys.reshape(n_seg * n_rows, LANES),
                        (seg[:, None, None] * n_rows + dest[None]).reshape(-1))
    return _combine(picked.reshape(n_seg, TOP_K, t, LANES), gates.T, xn, g2, tokens_per_batch,
                    min(512, tokens_per_batch))


def kernel(x, c, ctx, c_ctx, mod_w, mod_b, norm1_g, w_in, hy_conv_w, hy_conv_b, hy_f_w1, hy_f_b1,
           hy_f_w2, hy_f_b2, hy_f_w3, hy_f_b3, hy_f_w4, hy_f_freq, hy_bias, mla_q_norm_g, mla_w_uq,
           mla_kv_norm_g, mla_w_ukv, qk_norm_q_g, qk_norm_k_g, w_out, norm2_g, router_w, router_b,
           exp_w_gu, exp_b_gu, exp_w_down, exp_b_down):
    b, length, d = x.shape
    depth = mod_w.shape[0]
    assert depth == 1, "single-layer kernel"
    ly = 0
    c_rows = jnp.concatenate([c, c_ctx[None, :], jnp.zeros((8 - b - 1, d), F32)], axis=0)
    mod = _adaln_table(c_rows, mod_w[ly], mod_b[ly])
    mod6 = mod.reshape(8, 6, d)
    sh1, sc1, g1, sh2, sc2, g2 = (mod6[:b, j][:, None, :] for j in range(6))
    csh1 = mod6[b:b + 1, 0][:, None, :]
    csc1 = mod6[b:b + 1, 1][:, None, :]

    weights = _mla_weights(w_in[ly], mla_w_uq[ly], mla_w_ukv[ly], qk_norm_q_g[ly], qk_norm_k_g[ly])
    n_ctx = ctx.shape[1]
    _, _, k_c, v_c = _inproj(ctx, jnp.broadcast_to(csh1, (b, 1, d)), jnp.broadcast_to(csc1, (b, 1, d)),
                             norm1_g[ly], weights, mla_q_norm_g[ly], mla_kv_norm_g[ly], False, n_ctx)
    hy, q, k, v = _inproj(x, sh1, sc1, norm1_g[ly], weights, mla_q_norm_g[ly], mla_kv_norm_g[ly],
                          True, min(512, length))

    kern, asum = _hyena_kernel_taps(length, hy_f_w1[ly], hy_f_b1[ly], hy_f_w2[ly], hy_f_b2[ly],
                                    hy_f_w3[ly], hy_f_b3[ly], hy_f_w4[ly], hy_f_freq[ly])
    kf = _hyena_filter_spectrum(kern, asum)
    y_hy = _hyena_conv(hy, hy_conv_w[ly], hy_conv_b[ly], hy_bias[ly], kf)

    y_att = _flash_attention(q, k, v, k_c, v_c, min(512, length), min(512, length), 2, 32)

    t = b * length
    xn, h2, idx, gates, ranks, counts = _outproj_router(
        y_hy.reshape(t, -1), y_att.reshape(t, -1), x.reshape(t, d), g1, sh2, sc2, norm2_g[ly],
        w_out[ly], router_w[ly], router_b[ly], length, min(512, length))
    out = _moe(h2, xn, g2, idx, gates, ranks, counts, exp_w_gu[ly], exp_b_gu[ly],
               exp_w_down[ly], exp_b_down[ly], length)
    return out.reshape(b, length, d)
```

```python
import functools
import math

import jax
import jax.numpy as jnp
import numpy as np
from jax import lax
from jax.experimental import pallas as pl
from jax.experimental.pallas import tpu as pltpu
from jax.experimental.pallas import tpu_sc as plsc

F32 = jnp.float32
BF16 = jnp.bfloat16
HIGHEST = lax.Precision.HIGHEST

GRID_W = 64
D_HYENA = 512
FILTER_ORDER = 64
POS_EMB_DIM = 33
MIN_DECAY = math.log(1e-2) / 0.3
MAX_DECAY = math.log(1e-2) / 1.5
NOPE_DIM = 128
ROPE_DIM = 64
QK_DIM = NOPE_DIM + ROPE_DIM
V_DIM = 128
MLA_HEADS = 4
Q_RANK = 256
KV_RANK = 128
ROPE_THETA = 10000.0
N_EXPERTS = 32
TOP_K = 4
SWIGLU_ALPHA = 1.702
SWIGLU_LIMIT = 7.0
NORM_EPS = 1e-6

LANES = 128
VMEM_LIMIT = 56 * 1024 * 1024

DFT_Q = LANES
MOE_ROWS = 512
SC_WINDOW = 128


def _cparams(sem, vmem=None):
    return pltpu.CompilerParams(dimension_semantics=sem, vmem_limit_bytes=vmem)


def _const_spec(shape):
    nd = len(shape)
    return pl.BlockSpec(shape, lambda *_: (0,) * nd)


def _single_spec(shape, index_map):
    return pl.BlockSpec(shape, index_map, pipeline_mode=pl.Buffered(1))


def _mod_body(c_ref, w_ref, b_ref, o_ref):
    cc = c_ref[...]
    s = cc * jax.nn.sigmoid(cc)
    o_ref[...] = jnp.dot(s, w_ref[...], precision=HIGHEST,
                         preferred_element_type=F32) + b_ref[...]


def _adaln_table(c_rows, mod_w, mod_b):
    rows, d = c_rows.shape
    n = mod_w.shape[1]
    tn = n // 8
    return pl.pallas_call(
        _mod_body,
        grid=(n // tn,),
        in_specs=[_const_spec((rows, d)),
                  pl.BlockSpec((d, tn), lambda j: (0, j)),
                  pl.BlockSpec((1, tn), lambda j: (0, j))],
        out_specs=pl.BlockSpec((rows, tn), lambda j: (0, j)),
        out_shape=jax.ShapeDtypeStruct((rows, n), F32),
        compiler_params=_cparams(("arbitrary",)),
        name="adaln_table",
    )(c_rows, mod_w, mod_b.reshape(1, n))


def _rms(x, eps=NORM_EPS):
    return x * lax.rsqrt(jnp.mean(x * x, axis=-1, keepdims=True) + eps)


def _inproj_body(x_ref, sh_ref, sc_ref, g_ref, why_ref, wmla_ref, qng_ref, wuq_ref,
                 kvng_ref, wukv_ref, ct_ref, st_ref, ctk_ref, gq1_ref, gq2_ref, gkn_ref,
                 gkr_ref, hy_ref, q_ref, k_ref, v_ref, mla_sc, *, q_scale):
    i = pl.program_id(0)

    @pl.when(i == 0)
    def _():
        mla_sc[...] = jnp.zeros_like(mla_sc)

    mla = mla_sc[...]
    cq = mla[:, :Q_RANK]
    ckv = mla[:, Q_RANK:Q_RANK + KV_RANK]
    pe2 = mla[:, Q_RANK + KV_RANK:]
    qf = jnp.dot((_rms(cq) * qng_ref[...]).astype(BF16), wuq_ref[...],
                 preferred_element_type=F32)
    kvf = jnp.dot((_rms(ckv) * kvng_ref[...]).astype(BF16), wukv_ref[...],
                  preferred_element_type=F32)
    ct = ct_ref[...]
    st = st_ref[...]
    lane256 = lax.broadcasted_iota(jnp.int32, (1, 2 * LANES), 1)
    qmask = (lane256 < QK_DIM).astype(F32)
    lane128 = lax.broadcasted_iota(jnp.int32, (1, LANES), 1)
    pemask = (lane128 < ROPE_DIM).astype(F32)
    kr0 = pe2 * gkr_ref[...] * ctk_ref[...]
    krs = kr0 + pltpu.roll(kr0, ROPE_DIM, axis=1)
    pem = pe2 * pemask
    ss_pe = jnp.sum(pem * pem, axis=-1, keepdims=True)
    gq1 = gq1_ref[...]
    gq2 = gq2_ref[...]
    gkn = gkn_ref[...]
    for hd in range(MLA_HEADS):
        slab = qf[:, hd * 2 * LANES:(hd + 1) * 2 * LANES]
        sm = slab * qmask
        rq = lax.rsqrt(jnp.sum(sm * sm, axis=-1, keepdims=True) / QK_DIM + NORM_EPS) * q_scale
        t = slab * gq1 * ct + pltpu.roll(slab * gq2 * st, QK_DIM, axis=1)
        q_ref[hd] = (t * rq)[:, :QK_DIM].astype(BF16)
        kn = kvf[:, hd * 2 * LANES:hd * 2 * LANES + NOPE_DIM]
        rk = lax.rsqrt((jnp.sum(kn * kn, axis=-1, keepdims=True) + ss_pe) / QK_DIM + NORM_EPS)
        kslab = jnp.concatenate([kn * gkn, krs], axis=-1) * rk
        k_ref[hd] = kslab[:, :QK_DIM].astype(BF16)
        v_ref[hd] = kvf[:, hd * 2 * LANES + NOPE_DIM:(hd + 1) * 2 * LANES].astype(BF16)

    h = _rms(x_ref[...]) * g_ref[...]
    h = h * (1.0 + sc_ref[...]) + sh_ref[...]
    hb = h.astype(BF16)
    hy_ref[...] = jnp.dot(hb, why_ref[...], preferred_element_type=F32).astype(BF16)
    mla_sc[...] = jnp.dot(hb, wmla_ref[...], preferred_element_type=F32)


@functools.lru_cache(maxsize=None)
def _rope_lane_tables(length, use_rope):
    if use_rope:
        n_freq = ROPE_DIM // 4
        t = np.arange(length)
        inv_freq = np.power(ROPE_THETA, -np.arange(n_freq, dtype=np.float64) / n_freq)
        ar = (t // GRID_W).astype(np.float64)[:, None] * inv_freq
        ac = (t % GRID_W).astype(np.float64)[:, None] * inv_freq
        c64 = np.concatenate([np.cos(ar), np.cos(ar), np.cos(ac), np.cos(ac)], axis=-1)
        s64 = np.concatenate([-np.sin(ar), np.sin(ar), -np.sin(ac), np.sin(ac)], axis=-1)
    else:
        c64 = np.ones((length, ROPE_DIM))
        s64 = np.zeros((length, ROPE_DIM))
    z64 = np.zeros((length, ROPE_DIM))
    ct = np.concatenate([np.ones((length, LANES)), c64, z64], axis=-1)
    st = np.concatenate([np.zeros((length, LANES)), z64, s64], axis=-1)
    ctk = np.concatenate([c64, s64], axis=-1)
    return ct.astype(np.float32), st.astype(np.float32), ctk.astype(np.float32)


_SWAP16 = np.concatenate([np.arange(16, 32), np.arange(0, 16), np.arange(48, 64), np.arange(32, 48)])


def _mla_weights(w_in, mla_w_uq, mla_w_ukv, qk_norm_q_g, qk_norm_k_g):
    hy_cols = 3 * D_HYENA
    w_hy = w_in[:, :hy_cols].astype(BF16)
    w_pe = w_in[:, hy_cols + Q_RANK + KV_RANK:]
    w_mla = jnp.concatenate([w_in[:, hy_cols:hy_cols + Q_RANK + KV_RANK], w_pe, w_pe[:, _SWAP16]],
                            axis=-1).astype(BF16)
    wq = mla_w_uq.reshape(Q_RANK, MLA_HEADS, QK_DIM)
    wq_rope = wq[:, :, NOPE_DIM:]
    w_uq2 = jnp.concatenate([wq[:, :, :NOPE_DIM], wq_rope, wq_rope[:, :, _SWAP16]], axis=-1)
    w_uq2 = w_uq2.reshape(Q_RANK, MLA_HEADS * 2 * LANES).astype(BF16)
    w_ukv2 = mla_w_ukv.astype(BF16)
    gq_r = qk_norm_q_g[NOPE_DIM:]
    z64 = jnp.zeros((ROPE_DIM,), F32)
    gq1 = jnp.concatenate([qk_norm_q_g[:NOPE_DIM], gq_r, z64]).reshape(1, -1)
    gq2 = jnp.concatenate([jnp.zeros((NOPE_DIM,), F32), z64, gq_r[_SWAP16]]).reshape(1, -1)
    gkn = qk_norm_k_g[:NOPE_DIM].reshape(1, -1)
    gk_r = qk_norm_k_g[NOPE_DIM:]
    gkr = jnp.concatenate([gk_r, gk_r[_SWAP16]]).reshape(1, -1)
    return w_hy, w_mla, w_uq2, w_ukv2, gq1, gq2, gkn, gkr


def _inproj(x, shift, scale, norm_g, weights, q_norm_g, kv_norm_g, use_rope, tl):
    b, length, d = x.shape
    w_hy, w_mla, w_uq2, w_ukv2, gq1, gq2, gkn, gkr = weights
    ct, st, ctk = _rope_lane_tables(length, use_rope)
    q_scale = QK_DIM ** -0.5 * math.log2(math.e)
    nt = length // tl
    n_tiles = b * nt
    hyc = w_hy.shape[1]
    cur = lambda i: jnp.minimum(i, n_tiles - 1)
    prev = lambda i: jnp.maximum(i - 1, 0)
    tok = lambda i: (cur(i), 0)
    per_b = lambda i: (cur(i) // nt, 0, 0)
    pos = lambda i: (prev(i) % nt, 0)
    head_blk = lambda i: (prev(i) // nt, 0, prev(i) % nt, 0)
    hy, q, k, v = pl.pallas_call(
        functools.partial(_inproj_body, q_scale=q_scale),
        grid=(n_tiles + 1,),
        in_specs=[pl.BlockSpec((tl, d), tok),
                  pl.BlockSpec((None, 1, d), per_b),
                  pl.BlockSpec((None, 1, d), per_b),
                  _const_spec((1, d)),
                  _const_spec(w_hy.shape), _const_spec(w_mla.shape),
                  _const_spec((1, Q_RANK)), _const_spec(w_uq2.shape),
                  _const_spec((1, KV_RANK)), _const_spec(w_ukv2.shape),
                  pl.BlockSpec((tl, 2 * LANES), pos), pl.BlockSpec((tl, 2 * LANES), pos),
                  pl.BlockSpec((tl, LANES), pos),
                  _const_spec((1, 2 * LANES)), _const_spec((1, 2 * LANES)),
                  _const_spec((1, LANES)), _const_spec((1, LANES))],
        out_specs=[pl.BlockSpec((tl, hyc), tok),
                   pl.BlockSpec((None, MLA_HEADS, tl, QK_DIM), head_blk),
                   pl.BlockSpec((None, MLA_HEADS, tl, QK_DIM), head_blk),
                   pl.BlockSpec((None, MLA_HEADS, tl, V_DIM), head_blk)],
        out_shape=[jax.ShapeDtypeStruct((b * length, hyc), BF16),
                   jax.ShapeDtypeStruct((b, MLA_HEADS, length, QK_DIM), BF16),
                   jax.ShapeDtypeStruct((b, MLA_HEADS, length, QK_DIM), BF16),
                   jax.ShapeDtypeStruct((b, MLA_HEADS, length, V_DIM), BF16)],
        scratch_shapes=[pltpu.VMEM((tl, w_mla.shape[1]), F32)],
        compiler_params=_cparams(("arbitrary",), VMEM_LIMIT),
        name="inproj_mla",
    )(x.reshape(b * length, d), shift, scale, norm_g.reshape(1, d), w_hy, w_mla,
      q_norm_g.reshape(1, -1), w_uq2, kv_norm_g.reshape(1, -1), w_ukv2, ct, st, ctk, gq1, gq2, gkn, gkr)
    return hy.reshape(b, length, hyc), q, k, v


def _filter_body(z_ref, w1_ref, b1_ref, w2_ref, b2_ref, w3_ref, b3_ref, w4_ref, fr_ref, dl_ref,
                 kern_ref, asum_ref, *, zero_row, tr):
    i = pl.program_id(0)
    fr = fr_ref[...]
    hr = tr // 2
    za = z_ref[:hr, :]
    zb = z_ref[hr:, :]
    dot = functools.partial(jnp.dot, precision=HIGHEST, preferred_element_type=F32)
    h = jnp.sin(fr * (dot(jnp.concatenate([za, zb], axis=1), w1_ref[...]) + b1_ref[...]))
    h = jnp.sin(fr * (dot(h, w2_ref[...]) + b2_ref[...]))
    h = jnp.sin(fr * (dot(h, w3_ref[...]) + b3_ref[...]))
    o2 = dot(h, w4_ref[...])
    nc = o2.shape[1] // 2
    oa = o2[:, :nc] * jnp.exp(-za[:, 0:1] * dl_ref[...])
    ob = o2[:, nc:] * jnp.exp(-zb[:, 0:1] * dl_ref[...])

    @pl.when(i == 0)
    def _():
        asum_ref[...] = jnp.zeros_like(asum_ref)

    asum_ref[...] += (jnp.sum(jnp.abs(oa), axis=0, keepdims=True)
                      + jnp.sum(jnp.abs(ob), axis=0, keepdims=True))
    row = i * tr + lax.broadcasted_iota(jnp.int32, (hr, 1), 0)
    kern_ref[:hr, :] = jnp.where(row == zero_row, 0.0, oa)
    kern_ref[hr:, :] = jnp.where(row + hr == zero_row, 0.0, ob)


@functools.lru_cache(maxsize=None)
def _filter_features(length):
    n = 2 * length
    bands = (POS_EMB_DIM - 1) // 2
    pos = np.concatenate([np.arange(length), (n - np.arange(length, n)) % length])
    t_tab = np.linspace(0.0, 1.0, length)[:, None]
    w_ang = 2.0 * np.pi * np.arange(length, dtype=np.float64)[:, None] / length
    f = np.linspace(1e-4, bands - 1, bands)[None, :]
    z_tab = np.concatenate([t_tab, np.cos(f * w_ang), -np.sin(f * w_ang)], axis=-1)
    z = np.pad(z_tab[pos], ((0, 0), (0, LANES - POS_EMB_DIM)))
    return z.astype(np.float32)


def _hyena_kernel_taps(length, w1, b1, w2, b2, w3, b3, w4, freq):
    n = 2 * length
    z = jnp.asarray(_filter_features(length))
    deltas = jnp.abs(jnp.linspace(MIN_DECAY, MAX_DECAY, D_HYENA, dtype=F32)).reshape(1, -1)
    tr = min(1024, length)
    half_steps = length // tr
    fo = FILTER_ORDER

    def pair(w):
        zeros = jnp.zeros_like(w)
        return jnp.concatenate([jnp.concatenate([w, zeros], axis=1),
                                jnp.concatenate([zeros, w], axis=1)], axis=0)

    def twice(v):
        return jnp.concatenate([v, v]).reshape(1, 2 * fo)

    w1p = pair(jnp.pad(w1, ((0, LANES - POS_EMB_DIM), (0, 0))))
    w4p = jnp.stack([pair(w4[:, :D_HYENA]), pair(w4[:, D_HYENA:])])
    kern, asum = pl.pallas_call(
        functools.partial(_filter_body, zero_row=length, tr=tr),
        grid=(n // tr,),
        in_specs=[pl.BlockSpec((tr, LANES), lambda i: (i, 0)),
                  _const_spec((2 * LANES, 2 * fo)), _const_spec((1, 2 * fo)),
                  _const_spec((2 * fo, 2 * fo)), _const_spec((1, 2 * fo)),
                  _const_spec((2 * fo, 2 * fo)), _const_spec((1, 2 * fo)),
                  pl.BlockSpec((None, 2 * fo, 2 * D_HYENA), lambda i: (i // half_steps, 0, 0)),
                  _const_spec((1, 2 * fo)), _const_spec((1, D_HYENA))],
        out_specs=[pl.BlockSpec((tr, D_HYENA), lambda i: (i, 0)),
                   _const_spec((1, D_HYENA))],
        out_shape=[jax.ShapeDtypeStruct((n, D_HYENA), F32),
                   jax.ShapeDtypeStruct((1, D_HYENA), F32)],
        compiler_params=_cparams(("arbitrary",), VMEM_LIMIT),
        name="hyena_filter",
    )(z, w1p, twice(b1), pair(w2), twice(b2), pair(w3), twice(b3), w4p, twice(freq), deltas)
    return kern, asum


def _half_rows(n):
    n_half = n // DFT_Q // 2 + 1
    return n_half, -(-n_half // 8) * 8


@functools.lru_cache(maxsize=None)
def _dft_tables(n, p_in):
    q_sz = DFT_Q
    p_sz = n // q_sz
    n_half, n_r = _half_rows(n)
    r = np.arange(n_r, dtype=np.float64)
    keep = (r < n_half).astype(np.float64)
    qq = np.arange(q_sz, dtype=np.float64)
    pp = np.arange(p_in, dtype=np.float64)
    tt = q_sz * pp[None, None, :] + qq[:, None, None]
    ang = -2.0 * np.pi * r[None, :, None] * tt / n
    t1 = np.concatenate([np.cos(ang), np.sin(ang)], axis=1) * np.tile(keep, 2)[None, :, None]
    mirror = np.where((r == 0) | (r == p_sz // 2), 1.0, 2.0)
    t4 = np.transpose(t1 * np.tile(mirror, 2)[None, :, None], (0, 2, 1)) / n
    a2 = -2.0 * np.pi * np.outer(qq, qq) / q_sz
    fre, fim = np.cos(a2), np.sin(a2)
    m2 = np.block([[fre, -fim], [fim, fre]])
    m2c = np.block([[fre, fim], [-fim, fre]])
    t1 = np.concatenate([t1[0::2], t1[1::2]], axis=2)
    return (t1.astype(np.float32), t4.astype(np.float32), m2.astype(np.float32),
            m2c.astype(np.float32))


SUBLANES = 8


def _spec_block(rg, h):
    return (rg * 2 + h) * SUBLANES * DFT_Q


def _dft_stage1(x_ref, t1_ref, spec_ref, n_r, p_in):
    q_sz = DFT_Q
    ct = spec_ref.shape[1]

    def body(j, carry):
        q0 = 2 * j
        x0 = x_ref[pl.ds(q0, p_in, stride=q_sz), :].astype(BF16)
        x1 = x_ref[pl.ds(q0 + 1, p_in, stride=q_sz), :].astype(BF16)
        zeros = jnp.zeros_like(x0)
        rhs = jnp.concatenate([jnp.concatenate([x0, zeros], axis=1),
                               jnp.concatenate([zeros, x1], axis=1)], axis=0)
        a = jnp.dot(t1_ref[j], rhs, preferred_element_type=F32)
        for k in range(2):
            row = pl.multiple_of((q0 + k) * SUBLANES, SUBLANES)
            for h in range(2):
                for rg in range(n_r // SUBLANES):
                    src = h * n_r + rg * SUBLANES
                    spec_ref[pl.ds(_spec_block(rg, h) + row, SUBLANES), :] = (
                        a[src:src + SUBLANES, k * ct:(k + 1) * ct])
        return carry

    lax.fori_loop(0, q_sz // 2, body, 0, unroll=4)


def _spec_rows(rg, h, r8):
    return pl.ds(_spec_block(rg, h) + r8, DFT_Q, stride=SUBLANES)


def _spectrum_body(kern_ref, asum_ref, t1_ref, m2_ref, kf_ref, spec_ref, *, n_r, p_in):
    _dft_stage1(kern_ref, t1_ref, spec_ref, n_r, p_in)
    inv = 1.0 / asum_ref[...]
    m2 = m2_ref[...]

    ct = spec_ref.shape[1]

    def body(rg, carry):
        for r8 in range(0, SUBLANES, 2):
            blk = jnp.concatenate(
                [jnp.concatenate([spec_ref[_spec_rows(rg, 0, r8 + j), :],
                                  spec_ref[_spec_rows(rg, 1, r8 + j), :]], axis=0) for j in range(2)],
                axis=1)
            xf = jnp.dot(m2, blk.astype(BF16), preferred_element_type=F32)
            for j in range(2):
                kf_ref[rg * SUBLANES + r8 + j] = (xf[:, j * ct:(j + 1) * ct] * inv).astype(BF16)
        return carry

    lax.fori_loop(0, n_r // SUBLANES, body, 0)


def _hyena_filter_spectrum(kern, asum):
    n, c = kern.shape
    p_sz = n // DFT_Q
    _, n_r = _half_rows(n)
    t1, _, m2, _ = _dft_tables(n, p_sz)
    t1 = jnp.asarray(t1).astype(BF16)
    m2 = jnp.asarray(m2).astype(BF16)
    ct = LANES
    return pl.pallas_call(
        functools.partial(_spectrum_body, n_r=n_r, p_in=p_sz),
        grid=(c // ct,),
        in_specs=[_single_spec((n, ct), lambda j: (0, j)),
                  pl.BlockSpec((1, ct), lambda j: (0, j)),
                  _single_spec(t1.shape, lambda j: (0, 0, 0)),
                  _const_spec(m2.shape)],
        out_specs=pl.BlockSpec((n_r, 2 * DFT_Q, ct), lambda j: (0, 0, j)),
        out_shape=jax.ShapeDtypeStruct((n_r, 2 * DFT_Q, c), BF16),
        scratch_shapes=[pltpu.VMEM((n_r * 2 * DFT_Q, ct), F32)],
        compiler_params=_cparams(("parallel",), VMEM_LIMIT),
        name="hyena_filter_spectrum",
    )(kern, asum, t1, m2)


def _short_conv_chunk(u_ref, w_ref, b_ref, i, rows, length):
    pack = 16
    base = pl.multiple_of(i * rows, rows)
    u = u_ref[pl.ds(base, rows), :].astype(F32)
    lo = pl.multiple_of(jnp.maximum(base - pack, 0), pack)
    hi = pl.multiple_of(jnp.minimum(base + rows, length - pack), pack)
    prev = u_ref[pl.ds(lo, pack), :].astype(F32)[pack - 1:pack]
    nxt = u_ref[pl.ds(hi, pack), :].astype(F32)[0:1]
    prev = jnp.where(base == 0, 0.0, prev)
    nxt = jnp.where(base + rows == length, 0.0, nxt)
    ridx = lax.broadcasted_iota(jnp.int32, (rows, 1), 0)
    up = jnp.where(ridx == 0, prev, pltpu.roll(u, 1, axis=0))
    dn = jnp.where(ridx == rows - 1, nxt, pltpu.roll(u, rows - 1, axis=0))
    w = w_ref[...]
    return up * w[0:1] + u * w[1:2] + dn * w[2:3] + b_ref[...]


def _hyena_conv_body(x0_ref, x1_ref, v_ref, w0_ref, w1_ref, wv_ref, b0_ref, b1_ref, bv_ref,
                     bias_ref, kf_ref, t1_ref, t4_ref, m2_ref, m2c_ref, o_ref, vx_ref, spec_ref,
                     *, length, rows):
    q_sz = DFT_Q
    n_half, n_r = _half_rows(2 * length)
    p_in = length // q_sz
    n_chunks = length // rows

    def gate_in(i, carry):
        x1c = _short_conv_chunk(x1_ref, w1_ref, b1_ref, i, rows, length)
        vc = _short_conv_chunk(v_ref, wv_ref, bv_ref, i, rows, length)
        vx_ref[pl.ds(pl.multiple_of(i * rows, rows), rows), :] = vc * x1c
        return carry

    lax.fori_loop(0, n_chunks, gate_in, 0)
    _dft_stage1(vx_ref, t1_ref, spec_ref, n_r, p_in)
    m2 = m2_ref[...]
    m2c = m2c_ref[...]

    ct = spec_ref.shape[1]

    def forward_rows(rg, r8s):
        blk = jnp.concatenate(
            [jnp.concatenate([spec_ref[_spec_rows(rg, 0, r8), :], spec_ref[_spec_rows(rg, 1, r8), :]],
                             axis=0) for r8 in r8s], axis=1)
        return jnp.dot(m2, blk.astype(BF16), preferred_element_type=F32)

    def filter_inverse_rows(rg, r8s, xf):
        kf = jnp.concatenate([kf_ref[rg * SUBLANES + r8] for r8 in r8s], axis=1).astype(F32)
        xre, xim = xf[:q_sz], xf[q_sz:]
        kre, kim = kf[:q_sz], kf[q_sz:]
        z = jnp.concatenate([xre * kre - xim * kim, xre * kim + xim * kre], axis=0)
        bf = jnp.dot(m2c, z.astype(BF16), preferred_element_type=F32)
        for j, r8 in enumerate(r8s):
            spec_ref[_spec_rows(rg, 0, r8), :] = bf[:q_sz, j * ct:(j + 1) * ct]
            spec_ref[_spec_rows(rg, 1, r8), :] = bf[q_sz:, j * ct:(j + 1) * ct]

    def mid_rows(rg, n_rows):
        groups = [tuple(range(a, min(a + 2, n_rows))) for a in range(0, n_rows, 2)]
        pending = None
        for r8s in groups:
            xf = forward_rows(rg, r8s)
            if pending is not None:
                filter_inverse_rows(rg, *pending)
            pending = (r8s, xf)
        filter_inverse_rows(rg, *pending)

    def mid(rg, carry):
        mid_rows(rg, SUBLANES)
        return carry

    lax.fori_loop(0, n_half // SUBLANES, mid, 0)
    if n_half % SUBLANES:
        mid_rows(n_half // SUBLANES, n_half % SUBLANES)
    bias = bias_ref[...]

    def last(q, carry):
        row = pl.multiple_of(q * SUBLANES, SUBLANES)
        tiles = [spec_ref[pl.ds(_spec_block(rg, h) + row, SUBLANES), :]
                 for h in range(2) for rg in range(n_r // SUBLANES)]
        bq = jnp.concatenate(tiles, axis=0).astype(BF16)
        y = jnp.dot(t4_ref[q], bq, preferred_element_type=F32)
        sl = pl.ds(q, p_in, stride=q_sz)
        vx_ref[sl, :] = y + bias * vx_ref[sl, :]
        return carry

    lax.fori_loop(0, q_sz, last, 0, unroll=8)

    def gate_out(i, carry):
        x0c = _short_conv_chunk(x0_ref, w0_ref, b0_ref, i, rows, length)
        sl = pl.ds(pl.multiple_of(i * rows, rows), rows)
        o_ref[sl, :] = (vx_ref[sl, :] * x0c).astype(BF16)
        return carry

    lax.fori_loop(0, n_chunks, gate_out, 0)


def _hyena_conv(hy, conv_w, conv_b, bias, kf):
    b, length, _ = hy.shape
    ch = D_HYENA
    ct = LANES
    nct = ch // ct
    n = 2 * length
    _, n_r = _half_rows(n)
    p_in = length // DFT_Q
    t1, t4, m2, m2c = _dft_tables(n, p_in)
    t1, t4, m2, m2c = (jnp.asarray(a).astype(BF16) for a in (t1, t4, m2, m2c))
    rows = min(512, length)
    col = lambda off: (lambda j, bi: (bi, 0, off * nct + j))
    wcol = lambda off: (lambda j, bi: (0, off * nct + j))
    conv_b2 = conv_b.reshape(1, -1)
    return pl.pallas_call(
        functools.partial(_hyena_conv_body, length=length, rows=rows),
        grid=(nct, b),
        in_specs=[_single_spec((None, length, ct), col(0)),
                  _single_spec((None, length, ct), col(1)),
                  _single_spec((None, length, ct), col(2)),
                  pl.BlockSpec((3, ct), wcol(0)), pl.BlockSpec((3, ct), wcol(1)),
                  pl.BlockSpec((3, ct), wcol(2)),
                  pl.BlockSpec((1, ct), wcol(0)), pl.BlockSpec((1, ct), wcol(1)),
                  pl.BlockSpec((1, ct), wcol(2)),
                  pl.BlockSpec((1, ct), lambda j, bi: (0, j)),
                  _single_spec((n_r, 2 * DFT_Q, ct), lambda j, bi: (0, 0, j)),
                  _single_spec(t1.shape, lambda j, bi: (0, 0, 0)),
                  _single_spec(t4.shape, lambda j, bi: (0, 0, 0)),
                  _const_spec(m2.shape), _const_spec(m2c.shape)],
        out_specs=pl.BlockSpec((None, length, ct), lambda j, bi: (bi, 0, j)),
        out_shape=jax.ShapeDtypeStruct((b, length, ch), BF16),
        scratch_shapes=[pltpu.VMEM((length, ct), F32),
                        pltpu.VMEM((n_r * 2 * DFT_Q, ct), F32)],
        compiler_params=_cparams(("parallel", "parallel"), VMEM_LIMIT),
        name="hyena_conv",
    )(hy, hy, hy, conv_w, conv_w, conv_w, conv_b2, conv_b2, conv_b2, bias.reshape(1, ch), kf,
      t1, t4, m2, m2c)


def _flash_body(q_ref, k_ref, v_ref, kc_ref, vc_ref, o_ref, s_ref, p_ref, al_ref, m_ref, l_ref,
                acc_ref, *, tk, n_sub, rg):
    tq = q_ref.shape[0]
    sub = tq // n_sub
    nk = k_ref.shape[0] // tk
    nc = kc_ref.shape[0]
    n_chunks = nk + 1
    nt = (((1,), (1,)), ((), ()))

    def width(c):
        return nc if c == nk else tk

    def scores(c):
        keys = kc_ref[...] if c == nk else k_ref[c * tk:(c + 1) * tk, :]
        s_ref[c % 3, :, :width(c)] = lax.dot_general(q_ref[...], keys, nt, preferred_element_type=F32)

    def weighted_values(c):
        vals = vc_ref[...] if c == nk else v_ref[c * tk:(c + 1) * tk, :]
        return [jnp.dot(p_ref[c % 2, i * sub:(i + 1) * sub, :width(c)], vals,
                        preferred_element_type=F32) for i in range(n_sub)]

    def softmax(c):
        w = width(c)
        for g in range(tq // rg):
            rows = slice(g * rg, (g + 1) * rg)
            s = s_ref[c % 3, rows, :w]
            m_prev = m_ref[rows, :]
            m_new = jnp.maximum(m_prev, jnp.max(s, axis=-1, keepdims=True))
            alpha = jnp.exp2(m_prev - m_new)
            p = jnp.exp2(s - jnp.concatenate([m_new] * (w // LANES), axis=1))
            part = p[:, :LANES]
            for j in range(1, w // LANES):
                part = part + p[:, j * LANES:(j + 1) * LANES]
            l_ref[rows, :] = alpha * l_ref[rows, :] + part
            m_ref[rows, :] = m_new
            al_ref[rows, :] = alpha
            p_ref[c % 2, rows, :w] = p.astype(BF16)

    m_ref[...] = jnp.full_like(m_ref, -jnp.inf)
    l_ref[...] = jnp.zeros_like(l_ref)
    acc_ref[...] = jnp.zeros_like(acc_ref)
    scores(0)
    for c in range(n_chunks):
        prod = weighted_values(c - 1) if c >= 1 else None
        if c + 1 < n_chunks:
            scores(c + 1)
        softmax(c)
        if prod is not None:
            for i in range(n_sub):
                srows = slice(i * sub, (i + 1) * sub)
                acc_ref[srows, :] = al_ref[srows, :] * (acc_ref[srows, :] + prod[i])
    prod = weighted_values(n_chunks - 1)
    l_fin = jnp.sum(l_ref[...], axis=-1, keepdims=True)
    for i in range(n_sub):
        srows = slice(i * sub, (i + 1) * sub)
        o_ref[srows, :] = ((acc_ref[srows, :] + prod[i]) / l_fin[srows]).astype(o_ref.dtype)


def _flash_attention(q, k, v, kc, vc, tq, tk, n_sub, rg):
    b, hds, length, _ = q.shape
    nc = kc.shape[2]
    assert length % tk == 0 and nc <= tk and nc % LANES == 0 and V_DIM == LANES
    return pl.pallas_call(
        functools.partial(_flash_body, tk=tk, n_sub=n_sub, rg=rg),
        grid=(b, hds, length // tq),
        in_specs=[pl.BlockSpec((None, None, tq, QK_DIM), lambda bi, h, qi: (bi, h, qi, 0)),
                  pl.BlockSpec((None, None, length, QK_DIM), lambda bi, h, qi: (bi, h, 0, 0)),
                  pl.BlockSpec((None, None, length, V_DIM), lambda bi, h, qi: (bi, h, 0, 0)),
                  pl.BlockSpec((None, None, nc, QK_DIM), lambda bi, h, qi: (bi, h, 0, 0)),
                  pl.BlockSpec((None, None, nc, V_DIM), lambda bi, h, qi: (bi, h, 0, 0))],
        out_specs=pl.BlockSpec((None, tq, V_DIM), lambda bi, h, qi: (bi, qi, h)),
        out_shape=jax.ShapeDtypeStruct((b, length, hds * V_DIM), BF16),
        scratch_shapes=[pltpu.VMEM((3, tq, tk), F32), pltpu.VMEM((2, tq, tk), BF16),
                        pltpu.VMEM((tq, LANES), F32), pltpu.VMEM((tq, LANES), F32),
                        pltpu.VMEM((tq, LANES), F32), pltpu.VMEM((tq, V_DIM), F32)],
        compiler_params=_cparams(("parallel", "parallel", "parallel"), VMEM_LIMIT),
        name="mla_flash_attention",
    )(q, k, v, kc, vc)


def _store_packed_rows(dst_ref, x):
    half = x.shape[1] // 2
    for j in range(half // LANES):
        hi = x[:, j * LANES:(j + 1) * LANES].astype(BF16).astype(F32)
        lo = x[:, half + j * LANES:half + (j + 1) * LANES].astype(BF16).astype(F32)
        dst_ref[j] = (lax.bitcast_convert_type(hi, jnp.uint32)
                      | (lax.bitcast_convert_type(lo, jnp.uint32) >> 16))


def _unpack_words(w):
    hi = lax.bitcast_convert_type(w & jnp.uint32(0xFFFF0000), F32)
    lo = lax.bitcast_convert_type(w << 16, F32)
    return hi, lo


def _load_packed_rows(src_ref):
    parts = [_unpack_words(src_ref[j]) for j in range(src_ref.shape[0])]
    return jnp.concatenate([p[0] for p in parts] + [p[1] for p in parts], axis=-1)


def _outproj_body(yh_ref, ya_ref, x_ref, g1_ref, sh2_ref, sc2_ref, n2g_ref, wo1_ref, wo2_ref,
                  rwt_ref, rb_ref, tri_ref, xn_ref, h2_ref, idx_ref, gate_ref, rank_ref, cnt_ref,
                  carry_sc, lg_sc):
    i = pl.program_id(0)

    @pl.when(i == 0)
    def _():
        carry_sc[...] = jnp.zeros_like(carry_sc)
        lg_sc[...] = jnp.zeros_like(lg_sc)

    logits = lg_sc[...]
    n_e, tt = logits.shape
    eidx = lax.broadcasted_iota(jnp.int32, (n_e, tt), 0).astype(F32)
    work = logits
    vals, sels, idxs = [], [], []
    for _ in range(TOP_K):
        m = jnp.max(work, axis=0, keepdims=True)
        ix = jnp.min(jnp.where(work == m, eidx, float(n_e)), axis=0, keepdims=True)
        sel = eidx == ix
        work = jnp.where(sel, -jnp.inf, work)
        vals.append(m)
        idxs.append(ix)
        sels.append(sel)
    es = [jnp.exp(vk - vals[0]) for vk in vals]
    den = es[0] + es[1] + es[2] + es[3]
    gate_ref[...] = jnp.concatenate(es, axis=0) / den
    idx_ref[...] = jnp.concatenate(idxs, axis=0).astype(jnp.int32)
    onehot = jnp.zeros((n_e, tt), F32)
    for sel in sels:
        onehot = onehot + sel.astype(F32)

    mix = (jnp.dot(yh_ref[...], wo1_ref[...], preferred_element_type=F32)
           + jnp.dot(ya_ref[...], wo2_ref[...], preferred_element_type=F32))
    xn = x_ref[...] + g1_ref[...] * mix
    xn_ref[...] = xn
    h2 = _rms(xn) * n2g_ref[...]
    h2 = h2 * (1.0 + sc2_ref[...]) + sh2_ref[...]
    _store_packed_rows(h2_ref, h2)
    lg_sc[...] = lax.dot_general(rwt_ref[...], h2, (((1,), (1,)), ((), ())), precision=HIGHEST,
                                 preferred_element_type=F32) + rb_ref[...]

    prefix = jnp.dot(onehot.astype(BF16), tri_ref[...], preferred_element_type=F32) + carry_sc[...]
    ranks = [jnp.sum(jnp.where(sel, prefix, 0.0), axis=0, keepdims=True) for sel in sels]
    rank_ref[...] = jnp.concatenate(ranks, axis=0).astype(jnp.int32)
    carry_sc[...] += jnp.where(i > 0, jnp.sum(onehot, axis=1, keepdims=True), 0.0)
    cnt_ref[...] = jnp.broadcast_to(carry_sc[...], cnt_ref.shape)


def _outproj_router(y_hy, y_att, x2, g1, sh2, sc2, norm2_g, w_out, router_w, router_b, tokens_per_batch,
                    tt):
    t, d = x2.shape
    ch = y_hy.shape[1]
    n_e = router_w.shape[1]
    wo1 = w_out[:ch].astype(BF16)
    wo2 = w_out[ch:].astype(BF16)
    tri = jnp.asarray(np.triu(np.ones((tt, tt), np.float32), k=1), BF16)
    steps_per_batch = tokens_per_batch // tt
    n_tiles = t // tt
    cur = lambda i: jnp.minimum(i, n_tiles - 1)
    tok = lambda i: (cur(i), 0)
    per_b = lambda i: (cur(i) // steps_per_batch, 0, 0)
    lanes_tok = lambda i: (0, jnp.maximum(i - 1, 0))
    return pl.pallas_call(
        _outproj_body,
        grid=(n_tiles + 1,),
        in_specs=[pl.BlockSpec((tt, ch), tok), pl.BlockSpec((tt, ch), tok),
                  pl.BlockSpec((tt, d), tok),
                  pl.BlockSpec((None, 1, d), per_b), pl.BlockSpec((None, 1, d), per_b),
                  pl.BlockSpec((None, 1, d), per_b),
                  _const_spec((1, d)), _const_spec(wo1.shape), _const_spec(wo2.shape),
                  _const_spec((n_e, d)), _const_spec((n_e, 1)), _const_spec((tt, tt))],
        out_specs=[pl.BlockSpec((tt, d), tok),
                   pl.BlockSpec((d // (2 * LANES), tt, LANES), lambda i: (0, cur(i), 0)),
                   pl.BlockSpec((TOP_K, tt), lanes_tok), pl.BlockSpec((TOP_K, tt), lanes_tok),
                   pl.BlockSpec((TOP_K, tt), lanes_tok), _const_spec((n_e, LANES))],
        out_shape=[jax.ShapeDtypeStruct((t, d), F32),
                   jax.ShapeDtypeStruct((d // (2 * LANES), t, LANES), jnp.uint32),
                   jax.ShapeDtypeStruct((TOP_K, t), jnp.int32),
                   jax.ShapeDtypeStruct((TOP_K, t), F32),
                   jax.ShapeDtypeStruct((TOP_K, t), jnp.int32),
                   jax.ShapeDtypeStruct((n_e, LANES), F32)],
        scratch_shapes=[pltpu.VMEM((n_e, 1), F32), pltpu.VMEM((n_e, tt), F32)],
        compiler_params=_cparams(("arbitrary",), VMEM_LIMIT),
        name="outproj_router",
    )(y_hy, y_att, x2, g1, sh2, sc2, norm2_g.reshape(1, d), wo1, wo2, router_w.T,
      router_b.reshape(n_e, 1), tri)


def _cast_rows(src_ref, dst_ref, chunk):
    def body(c, carry):
        sl = pl.ds(pl.multiple_of(c * chunk, chunk), chunk)
        dst_ref[sl, :] = src_ref[sl, :].astype(dst_ref.dtype)
        return carry

    lax.fori_loop(0, src_ref.shape[0] // chunk, body, 0)


def _expert_body(be_ref, nvalid_ref, xs_ref, wgu_ref, bgu_ref, wd_ref, bd_ref, ys_ref, wgu_bf, wd_bf):
    i = pl.program_id(0)
    n_valid = nvalid_ref[i]
    active = n_valid > 0
    new_expert = jnp.logical_or(i == 0, be_ref[i] != be_ref[jnp.maximum(i - 1, 0)])

    @pl.when(jnp.logical_and(active, new_expert))
    def _():
        _cast_rows(wgu_ref, wgu_bf, 128)
        _cast_rows(wd_ref, wd_bf, 128)

    @pl.when(active)
    def _():
        row = lax.broadcasted_iota(jnp.int32, (xs_ref.shape[1], 1), 0)
        xs = jnp.where(row < n_valid, _load_packed_rows(xs_ref), 0.0).astype(BF16)
        gu = jnp.dot(xs, wgu_bf[...], preferred_element_type=F32) + bgu_ref[...]
        dff = gu.shape[1] // 2
        gate = jnp.minimum(gu[:, :dff], SWIGLU_LIMIT)
        up = jnp.clip(gu[:, dff:], -SWIGLU_LIMIT, SWIGLU_LIMIT)
        act = (up + 1.0) * (gate * jax.nn.sigmoid(SWIGLU_ALPHA * gate))
        ys = jnp.dot(act.astype(BF16), wd_bf[...], preferred_element_type=F32) + bd_ref[...]
        _store_packed_rows(ys_ref, ys)

    @pl.when(jnp.logical_not(active))
    def _():
        ys_ref[...] = jnp.zeros_like(ys_ref)


def _expert_blocks(xs, block_e, n_valid, w_gu, b_gu, w_down, b_down):
    n_seg, n_rows, _ = xs.shape
    n_e, d, dff2 = w_gu.shape
    bm = MOE_ROWS
    seg_block = pl.BlockSpec((n_seg, bm, LANES), lambda i, be, nu: (0, i, 0))
    grid_spec = pltpu.PrefetchScalarGridSpec(
        num_scalar_prefetch=2,
        grid=(n_rows // bm,),
        in_specs=[seg_block,
                  pl.BlockSpec((None, d, dff2), lambda i, be, nu: (be[i], 0, 0)),
                  pl.BlockSpec((None, 1, dff2), lambda i, be, nu: (be[i], 0, 0)),
                  pl.BlockSpec((None, dff2 // 2, d), lambda i, be, nu: (be[i], 0, 0)),
                  pl.BlockSpec((None, 1, d), lambda i, be, nu: (be[i], 0, 0))],
        out_specs=seg_block,
        scratch_shapes=[pltpu.VMEM((d, dff2), BF16), pltpu.VMEM((dff2 // 2, d), BF16)],
    )
    return pl.pallas_call(
        _expert_body,
        grid_spec=grid_spec,
        out_shape=jax.ShapeDtypeStruct(xs.shape, jnp.uint32),
        compiler_params=_cparams(("arbitrary",), VMEM_LIMIT),
        name="moe_experts",
    )(block_e, n_valid, xs, w_gu, b_gu.reshape(n_e, 1, dff2), w_down, b_down.reshape(n_e, 1, d))


def _sc_gather(table, idx):
    n = idx.shape[0]
    width = table.shape[1]
    mesh = plsc.VectorSubcoreMesh(core_axis_name="core", subcore_axis_name="subcore")
    n_workers = mesh.num_cores * mesh.num_subcores
    assert width == LANES and n % (SC_WINDOW * n_workers) == 0

    @functools.partial(pl.kernel, out_type=jax.ShapeDtypeStruct((n, width), table.dtype), mesh=mesh)
    def gather_kernel(table_hbm, idx_hbm, out_hbm):
        def body(idx_vmem, out_vmem):
            pltpu.sync_copy(table_hbm.at[idx_vmem.at[0]], out_vmem)

        pltpu.emit_pipeline(
            body,
            grid=(n // SC_WINDOW,),
            in_specs=[pl.BlockSpec((1, SC_WINDOW), lambda i: (0, i))],
            out_specs=[pl.BlockSpec((SC_WINDOW, width), lambda i: (i, 0))],
            core_axis_name=("core", "subcore"),
            dimension_semantics=(pltpu.PARALLEL,),
        )(idx_hbm, out_hbm)

    return gather_kernel(table, idx.reshape(1, n))


def _sc_scatter(rows, idx, n_copies, n_out):
    n, width = rows.shape
    mesh = plsc.VectorSubcoreMesh(core_axis_name="core", subcore_axis_name="subcore")
    n_workers = mesh.num_cores * mesh.num_subcores
    assert width == LANES and n % (SC_WINDOW * n_workers) == 0 and idx.shape == (8, n)

    @functools.partial(pl.kernel, out_type=jax.ShapeDtypeStruct((n_out, width), rows.dtype), mesh=mesh)
    def scatter_kernel(rows_hbm, idx_hbm, out_hbm):
        def body(rows_vmem, idx_vmem):
            for k in range(n_copies):
                pltpu.sync_copy(rows_vmem, out_hbm.at[idx_vmem.at[k]])

        pltpu.emit_pipeline(
            body,
            grid=(n // SC_WINDOW,),
            in_specs=[pl.BlockSpec((SC_WINDOW, width), lambda i: (i, 0)),
                      pl.BlockSpec((8, SC_WINDOW), lambda i: (0, i))],
            out_specs=[],
            core_axis_name=("core", "subcore"),
            dimension_semantics=(pltpu.PARALLEL,),
        )(rows_hbm, idx_hbm)

    return scatter_kernel(rows, idx)


def _combine_body(pk_ref, gt_ref, xn_ref, g2_ref, o_ref):
    n_seg = pk_ref.shape[0]
    half = o_ref.shape[1] // 2
    gt = gt_ref[...]
    g2 = g2_ref[...]
    for j in range(n_seg):
        acc_hi = None
        acc_lo = None
        for kk in range(TOP_K):
            hi, lo = _unpack_words(pk_ref[j, kk])
            g = gt[:, kk:kk + 1]
            acc_hi = g * hi if acc_hi is None else acc_hi + g * hi
            acc_lo = g * lo if acc_lo is None else acc_lo + g * lo
        c_hi = slice(j * LANES, (j + 1) * LANES)
        c_lo = slice(half + j * LANES, half + (j + 1) * LANES)
        o_ref[:, c_hi] = xn_ref[:, c_hi] + g2[:, c_hi] * acc_hi
        o_ref[:, c_lo] = xn_ref[:, c_lo] + g2[:, c_lo] * acc_lo


def _combine(picked, gates_t, xn, g2, tokens_per_batch, tt):
    n_seg, _, t, _ = picked.shape
    d = xn.shape[1]
    steps_per_batch = tokens_per_batch // tt
    return pl.pallas_call(
        _combine_body,
        grid=(t // tt,),
        in_specs=[pl.BlockSpec((n_seg, TOP_K, tt, LANES), lambda i: (0, 0, i, 0)),
                  pl.BlockSpec((tt, TOP_K), lambda i: (i, 0)),
                  pl.BlockSpec((tt, d), lambda i: (i, 0)),
                  pl.BlockSpec((None, 1, d), lambda i: (i // steps_per_batch, 0, 0))],
        out_specs=pl.BlockSpec((tt, d), lambda i: (i, 0)),
        out_shape=jax.ShapeDtypeStruct((t, d), F32),
        compiler_params=_cparams(("parallel",), VMEM_LIMIT),
        name="moe_combine",
    )(picked, gates_t, xn, g2)


def _moe(h2p, xn, g2, idx, gates, ranks, counts, w_gu, b_gu, w_down, b_down, tokens_per_batch):
    n_seg, t, _ = h2p.shape
    bm = MOE_ROWS
    n_e = w_gu.shape[0]
    cnt = counts[:, 0].astype(jnp.int32)
    padded = (cnt + bm - 1) // bm * bm
    padded_ends = jnp.cumsum(padded)
    padded_starts = padded_ends - padded
    experts = jnp.arange(n_e, dtype=jnp.int32)[:, None, None]
    dest = ranks + jnp.sum(jnp.where(idx[None] == experts, padded_starts[:, None, None], 0), axis=0)
    n_blocks = t * TOP_K // bm + n_e
    n_rows = n_blocks * bm
    block_start = jnp.arange(n_blocks, dtype=jnp.int32) * bm
    block_e = jnp.minimum(jnp.sum(padded_ends[None, :] <= block_start[:, None], axis=1),
                          n_e - 1).astype(jnp.int32)
    n_valid = jnp.clip(cnt[block_e] - (block_start - padded_starts[block_e]), 0, bm).astype(jnp.int32)
    seg = jnp.arange(n_seg, dtype=jnp.int32)
    scatter_idx = (seg[None, :, None] * n_rows + dest[:, None, :]).reshape(TOP_K, n_seg * t)
    scatter_idx = jnp.concatenate([scatter_idx, scatter_idx], axis=0)
    xs = _sc_scatter(h2p.reshape(n_seg * t, LANES), scatter_idx, TOP_K, n_seg * n_rows)
    ys = _expert_blocks(xs.reshape(n_seg, n_rows, LANES), block_e, n_valid, w_gu, b_gu, w_down, b_down)
    picked = _sc_gather(ys.reshape(n_seg * n_rows, LANES),
                        (seg[:, None, None] * n_rows + dest[None]).reshape(-1))
    return _combine(picked.reshape(n_seg, TOP_K, t, LANES), gates.T, xn, g2, tokens_per_batch,
                    min(512, tokens_per_batch))


def kernel(x, c, ctx, c_ctx, mod_w, mod_b, norm1_g, w_in, hy_conv_w, hy_conv_b, hy_f_w1, hy_f_b1,
           hy_f_w2, hy_f_b2, hy_f_w3, hy_f_b3, hy_f_w4, hy_f_freq, hy_bias, mla_q_norm_g, mla_w_uq,
           mla_kv_norm_g, mla_w_ukv, qk_norm_q_g, qk_norm_k_g, w_out, norm2_g, router_w, router_b,
           exp_w_gu, exp_b_gu, exp_w_down, exp_b_down):
    b, length, d = x.shape
    depth = mod_w.shape[0]
    assert depth == 1, "single-layer kernel"
    ly = 0
    c_rows = jnp.concatenate([c, c_ctx[None, :], jnp.zeros((8 - b - 1, d), F32)], axis=0)
    mod = _adaln_table(c_rows, mod_w[ly], mod_b[ly])
    mod6 = mod.reshape(8, 6, d)
    sh1, sc1, g1, sh2, sc2, g2 = (mod6[:b, j][:, None, :] for j in range(6))
    csh1 = mod6[b:b + 1, 0][:, None, :]
    csc1 = mod6[b:b + 1, 1][:, None, :]

    weights = _mla_weights(w_in[ly], mla_w_uq[ly], mla_w_ukv[ly], qk_norm_q_g[ly], qk_norm_k_g[ly])
    n_ctx = ctx.shape[1]
    _, _, k_c, v_c = _inproj(ctx, jnp.broadcast_to(csh1, (b, 1, d)), jnp.broadcast_to(csc1, (b, 1, d)),
                             norm1_g[ly], weights, mla_q_norm_g[ly], mla_kv_norm_g[ly], False, n_ctx)
    hy, q, k, v = _inproj(x, sh1, sc1, norm1_g[ly], weights, mla_q_norm_g[ly], mla_kv_norm_g[ly],
                          True, min(512, length))

    kern, asum = _hyena_kernel_taps(length, hy_f_w1[ly], hy_f_b1[ly], hy_f_w2[ly], hy_f_b2[ly],
                                    hy_f_w3[ly], hy_f_b3[ly], hy_f_w4[ly], hy_f_freq[ly])
    kf = _hyena_filter_spectrum(kern, asum)
    y_hy = _hyena_conv(hy, hy_conv_w[ly], hy_conv_b[ly], hy_bias[ly], kf)

    y_att = _flash_attention(q, k, v, k_c, v_c, min(512, length), min(512, length), 1, 32)

    t = b * length
    xn, h2, idx, gates, ranks, counts = _outproj_router(
        y_hy.reshape(t, -1), y_att.reshape(t, -1), x.reshape(t, d), g1, sh2, sc2, norm2_g[ly],
        w_out[ly], router_w[ly], router_b[ly], length, min(512, length))
    out = _moe(h2, xn, g2, idx, gates, ranks, counts, exp_w_gu[ly], exp_b_gu[ly],
               exp_w_down[ly], exp_b_down[ly], length)
    return out.reshape(b, length, d)
```

```python
import functools
import math

import jax
import jax.numpy as jnp
import numpy as np
from jax import lax
from jax.experimental import pallas as pl
from jax.experimental.pallas import tpu as pltpu
from jax.experimental.pallas import tpu_sc as plsc

F32 = jnp.float32
BF16 = jnp.bfloat16
HIGHEST = lax.Precision.HIGHEST

GRID_W = 64
D_HYENA = 512
FILTER_ORDER = 64
POS_EMB_DIM = 33
MIN_DECAY = math.log(1e-2) / 0.3
MAX_DECAY = math.log(1e-2) / 1.5
NOPE_DIM = 128
ROPE_DIM = 64
QK_DIM = NOPE_DIM + ROPE_DIM
V_DIM = 128
MLA_HEADS = 4
Q_RANK = 256
KV_RANK = 128
ROPE_THETA = 10000.0
N_EXPERTS = 32
TOP_K = 4
SWIGLU_ALPHA = 1.702
SWIGLU_LIMIT = 7.0
NORM_EPS = 1e-6

LANES = 128
VMEM_LIMIT = 56 * 1024 * 1024

BF16_PACK_ROWS = 16

DFT_Q = LANES
MOE_ROWS = 512
SC_WINDOW = 128

TOKEN_TILE = 512
ATTN_Q_TILE = 512
ATTN_K_TILE = 512
ATTN_ROW_GROUP = 32
FILTER_TILE = 1024
CONV_ROW_CHUNK = 512
CAST_ROW_CHUNK = 128


def _cparams(sem, vmem=None):
    return pltpu.CompilerParams(dimension_semantics=sem, vmem_limit_bytes=vmem)


def _const_spec(shape):
    nd = len(shape)
    return pl.BlockSpec(shape, lambda *_: (0,) * nd)


def _single_spec(shape, index_map):
    return pl.BlockSpec(shape, index_map, pipeline_mode=pl.Buffered(1))


def _mod_body(c_ref, w_ref, b_ref, o_ref):
    cc = c_ref[...]
    s = cc * jax.nn.sigmoid(cc)
    o_ref[...] = jnp.dot(s, w_ref[...], precision=HIGHEST,
                         preferred_element_type=F32) + b_ref[...]


def _adaln_table(c_rows, mod_w, mod_b):
    rows, d = c_rows.shape
    n = mod_w.shape[1]
    tn = n // 8
    return pl.pallas_call(
        _mod_body,
        grid=(n // tn,),
        in_specs=[_const_spec((rows, d)),
                  pl.BlockSpec((d, tn), lambda j: (0, j)),
                  pl.BlockSpec((1, tn), lambda j: (0, j))],
        out_specs=pl.BlockSpec((rows, tn), lambda j: (0, j)),
        out_shape=jax.ShapeDtypeStruct((rows, n), F32),
        compiler_params=_cparams(("arbitrary",)),
        name="adaln_table",
    )(c_rows, mod_w, mod_b.reshape(1, n))


def _rms(x, eps=NORM_EPS):
    return x * lax.rsqrt(jnp.mean(x * x, axis=-1, keepdims=True) + eps)


def _inproj_body(x_ref, sh_ref, sc_ref, g_ref, why_ref, wmla_ref, qng_ref, wuq_ref,
                 kvng_ref, wukv_ref, ct_ref, st_ref, ctk_ref, gq1_ref, gq2_ref, gkn_ref,
                 gkr_ref, hy_ref, q_ref, k_ref, v_ref, mla_sc, *, q_scale):
    i = pl.program_id(0)

    @pl.when(i == 0)
    def _():
        mla_sc[...] = jnp.zeros_like(mla_sc)

    mla = mla_sc[...]
    cq = mla[:, :Q_RANK]
    ckv = mla[:, Q_RANK:Q_RANK + KV_RANK]
    pe2 = mla[:, Q_RANK + KV_RANK:]
    qf = jnp.dot((_rms(cq) * qng_ref[...]).astype(BF16), wuq_ref[...],
                 preferred_element_type=F32)
    kvf = jnp.dot((_rms(ckv) * kvng_ref[...]).astype(BF16), wukv_ref[...],
                  preferred_element_type=F32)
    ct = ct_ref[...]
    st = st_ref[...]
    lane256 = lax.broadcasted_iota(jnp.int32, (1, 2 * LANES), 1)
    qmask = (lane256 < QK_DIM).astype(F32)
    lane128 = lax.broadcasted_iota(jnp.int32, (1, LANES), 1)
    pemask = (lane128 < ROPE_DIM).astype(F32)
    kr0 = pe2 * gkr_ref[...] * ctk_ref[...]
    krs = kr0 + pltpu.roll(kr0, ROPE_DIM, axis=1)
    pem = pe2 * pemask
    ss_pe = jnp.sum(pem * pem, axis=-1, keepdims=True)
    gq1 = gq1_ref[...]
    gq2 = gq2_ref[...]
    gkn = gkn_ref[...]
    for hd in range(MLA_HEADS):
        slab = qf[:, hd * 2 * LANES:(hd + 1) * 2 * LANES]
        sm = slab * qmask
        rq = lax.rsqrt(jnp.sum(sm * sm, axis=-1, keepdims=True) / QK_DIM + NORM_EPS) * q_scale
        t = slab * gq1 * ct + pltpu.roll(slab * gq2 * st, QK_DIM, axis=1)
        q_ref[hd] = (t * rq)[:, :QK_DIM].astype(BF16)
        kn = kvf[:, hd * 2 * LANES:hd * 2 * LANES + NOPE_DIM]
        rk = lax.rsqrt((jnp.sum(kn * kn, axis=-1, keepdims=True) + ss_pe) / QK_DIM + NORM_EPS)
        kslab = jnp.concatenate([kn * gkn, krs], axis=-1) * rk
        k_ref[hd] = kslab[:, :QK_DIM].astype(BF16)
        v_ref[hd] = kvf[:, hd * 2 * LANES + NOPE_DIM:(hd + 1) * 2 * LANES].astype(BF16)

    h = _rms(x_ref[...]) * g_ref[...]
    h = h * (1.0 + sc_ref[...]) + sh_ref[...]
    hb = h.astype(BF16)
    hy_ref[...] = jnp.dot(hb, why_ref[...], preferred_element_type=F32).astype(BF16)
    mla_sc[...] = jnp.dot(hb, wmla_ref[...], preferred_element_type=F32)


@functools.lru_cache(maxsize=None)
def _rope_lane_tables(length, use_rope):
    if use_rope:
        n_freq = ROPE_DIM // 4
        t = np.arange(length)
        inv_freq = np.power(ROPE_THETA, -np.arange(n_freq, dtype=np.float64) / n_freq)
        ar = (t // GRID_W).astype(np.float64)[:, None] * inv_freq
        ac = (t % GRID_W).astype(np.float64)[:, None] * inv_freq
        c64 = np.concatenate([np.cos(ar), np.cos(ar), np.cos(ac), np.cos(ac)], axis=-1)
        s64 = np.concatenate([-np.sin(ar), np.sin(ar), -np.sin(ac), np.sin(ac)], axis=-1)
    else:
        c64 = np.ones((length, ROPE_DIM))
        s64 = np.zeros((length, ROPE_DIM))
    z64 = np.zeros((length, ROPE_DIM))
    ct = np.concatenate([np.ones((length, LANES)), c64, z64], axis=-1)
    st = np.concatenate([np.zeros((length, LANES)), z64, s64], axis=-1)
    ctk = np.concatenate([c64, s64], axis=-1)
    return ct.astype(np.float32), st.astype(np.float32), ctk.astype(np.float32)


_SWAP16 = np.concatenate([np.arange(16, 32), np.arange(0, 16), np.arange(48, 64), np.arange(32, 48)])


def _mla_weights(w_in, mla_w_uq, mla_w_ukv, qk_norm_q_g, qk_norm_k_g):
    hy_cols = 3 * D_HYENA
    w_hy = w_in[:, :hy_cols].astype(BF16)
    w_pe = w_in[:, hy_cols + Q_RANK + KV_RANK:]
    w_mla = jnp.concatenate([w_in[:, hy_cols:hy_cols + Q_RANK + KV_RANK], w_pe, w_pe[:, _SWAP16]],
                            axis=-1).astype(BF16)
    wq = mla_w_uq.reshape(Q_RANK, MLA_HEADS, QK_DIM)
    wq_rope = wq[:, :, NOPE_DIM:]
    w_uq2 = jnp.concatenate([wq[:, :, :NOPE_DIM], wq_rope, wq_rope[:, :, _SWAP16]], axis=-1)
    w_uq2 = w_uq2.reshape(Q_RANK, MLA_HEADS * 2 * LANES).astype(BF16)
    w_ukv2 = mla_w_ukv.astype(BF16)
    gq_r = qk_norm_q_g[NOPE_DIM:]
    z64 = jnp.zeros((ROPE_DIM,), F32)
    gq1 = jnp.concatenate([qk_norm_q_g[:NOPE_DIM], gq_r, z64]).reshape(1, -1)
    gq2 = jnp.concatenate([jnp.zeros((NOPE_DIM,), F32), z64, gq_r[_SWAP16]]).reshape(1, -1)
    gkn = qk_norm_k_g[:NOPE_DIM].reshape(1, -1)
    gk_r = qk_norm_k_g[NOPE_DIM:]
    gkr = jnp.concatenate([gk_r, gk_r[_SWAP16]]).reshape(1, -1)
    return w_hy, w_mla, w_uq2, w_ukv2, gq1, gq2, gkn, gkr


def _inproj(x, shift, scale, norm_g, weights, q_norm_g, kv_norm_g, use_rope, tl):
    b, length, d = x.shape
    w_hy, w_mla, w_uq2, w_ukv2, gq1, gq2, gkn, gkr = weights
    ct, st, ctk = _rope_lane_tables(length, use_rope)
    q_scale = QK_DIM ** -0.5 * math.log2(math.e)
    nt = length // tl
    n_tiles = b * nt
    hyc = w_hy.shape[1]
    cur = lambda i: jnp.minimum(i, n_tiles - 1)
    prev = lambda i: jnp.maximum(i - 1, 0)
    tok = lambda i: (cur(i), 0)
    per_b = lambda i: (cur(i) // nt, 0, 0)
    pos = lambda i: (prev(i) % nt, 0)
    head_blk = lambda i: (prev(i) // nt, 0, prev(i) % nt, 0)
    hy, q, k, v = pl.pallas_call(
        functools.partial(_inproj_body, q_scale=q_scale),
        grid=(n_tiles + 1,),
        in_specs=[pl.BlockSpec((tl, d), tok),
                  pl.BlockSpec((None, 1, d), per_b),
                  pl.BlockSpec((None, 1, d), per_b),
                  _const_spec((1, d)),
                  _const_spec(w_hy.shape), _const_spec(w_mla.shape),
                  _const_spec((1, Q_RANK)), _const_spec(w_uq2.shape),
                  _const_spec((1, KV_RANK)), _const_spec(w_ukv2.shape),
                  pl.BlockSpec((tl, 2 * LANES), pos), pl.BlockSpec((tl, 2 * LANES), pos),
                  pl.BlockSpec((tl, LANES), pos),
                  _const_spec((1, 2 * LANES)), _const_spec((1, 2 * LANES)),
                  _const_spec((1, LANES)), _const_spec((1, LANES))],
        out_specs=[pl.BlockSpec((tl, hyc), tok),
                   pl.BlockSpec((None, MLA_HEADS, tl, QK_DIM), head_blk),
                   pl.BlockSpec((None, MLA_HEADS, tl, QK_DIM), head_blk),
                   pl.BlockSpec((None, MLA_HEADS, tl, V_DIM), head_blk)],
        out_shape=[jax.ShapeDtypeStruct((b * length, hyc), BF16),
                   jax.ShapeDtypeStruct((b, MLA_HEADS, length, QK_DIM), BF16),
                   jax.ShapeDtypeStruct((b, MLA_HEADS, length, QK_DIM), BF16),
                   jax.ShapeDtypeStruct((b, MLA_HEADS, length, V_DIM), BF16)],
        scratch_shapes=[pltpu.VMEM((tl, w_mla.shape[1]), F32)],
        compiler_params=_cparams(("arbitrary",), VMEM_LIMIT),
        name="inproj_mla",
    )(x.reshape(b * length, d), shift, scale, norm_g.reshape(1, d), w_hy, w_mla,
      q_norm_g.reshape(1, -1), w_uq2, kv_norm_g.reshape(1, -1), w_ukv2, ct, st, ctk, gq1, gq2, gkn, gkr)
    return hy.reshape(b, length, hyc), q, k, v


def _filter_body(z_ref, w1_ref, b1_ref, w2_ref, b2_ref, w3_ref, b3_ref, w4_ref, fr_ref, dl_ref,
                 kern_ref, asum_ref, *, zero_row, tr):
    i = pl.program_id(0)
    fr = fr_ref[...]
    hr = tr // 2
    za = z_ref[:hr, :]
    zb = z_ref[hr:, :]
    dot = functools.partial(jnp.dot, precision=HIGHEST, preferred_element_type=F32)
    h = jnp.sin(fr * (dot(jnp.concatenate([za, zb], axis=1), w1_ref[...]) + b1_ref[...]))
    h = jnp.sin(fr * (dot(h, w2_ref[...]) + b2_ref[...]))
    h = jnp.sin(fr * (dot(h, w3_ref[...]) + b3_ref[...]))
    o2 = dot(h, w4_ref[...])
    nc = o2.shape[1] // 2
    oa = o2[:, :nc] * jnp.exp(-za[:, 0:1] * dl_ref[...])
    ob = o2[:, nc:] * jnp.exp(-zb[:, 0:1] * dl_ref[...])

    @pl.when(i == 0)
    def _():
        asum_ref[...] = jnp.zeros_like(asum_ref)

    asum_ref[...] += (jnp.sum(jnp.abs(oa), axis=0, keepdims=True)
                      + jnp.sum(jnp.abs(ob), axis=0, keepdims=True))
    row = i * tr + lax.broadcasted_iota(jnp.int32, (hr, 1), 0)
    kern_ref[:hr, :] = jnp.where(row == zero_row, 0.0, oa)
    kern_ref[hr:, :] = jnp.where(row + hr == zero_row, 0.0, ob)


@functools.lru_cache(maxsize=None)
def _filter_features(length):
    n = 2 * length
    bands = (POS_EMB_DIM - 1) // 2
    pos = np.concatenate([np.arange(length), (n - np.arange(length, n)) % length])
    t_tab = np.linspace(0.0, 1.0, length)[:, None]
    w_ang = 2.0 * np.pi * np.arange(length, dtype=np.float64)[:, None] / length
    f = np.linspace(1e-4, bands - 1, bands)[None, :]
    z_tab = np.concatenate([t_tab, np.cos(f * w_ang), -np.sin(f * w_ang)], axis=-1)
    z = np.pad(z_tab[pos], ((0, 0), (0, LANES - POS_EMB_DIM)))
    return z.astype(np.float32)


def _hyena_kernel_taps(length, w1, b1, w2, b2, w3, b3, w4, freq):
    n = 2 * length
    z = jnp.asarray(_filter_features(length))
    deltas = jnp.abs(jnp.linspace(MIN_DECAY, MAX_DECAY, D_HYENA, dtype=F32)).reshape(1, -1)
    tr = min(FILTER_TILE, length)
    half_steps = length // tr
    fo = FILTER_ORDER

    def pair(w):
        zeros = jnp.zeros_like(w)
        return jnp.concatenate([jnp.concatenate([w, zeros], axis=1),
                                jnp.concatenate([zeros, w], axis=1)], axis=0)

    def twice(v):
        return jnp.concatenate([v, v]).reshape(1, 2 * fo)

    w1p = pair(jnp.pad(w1, ((0, LANES - POS_EMB_DIM), (0, 0))))
    w4p = jnp.stack([pair(w4[:, :D_HYENA]), pair(w4[:, D_HYENA:])])
    kern, asum = pl.pallas_call(
        functools.partial(_filter_body, zero_row=length, tr=tr),
        grid=(n // tr,),
        in_specs=[pl.BlockSpec((tr, LANES), lambda i: (i, 0)),
                  _const_spec((2 * LANES, 2 * fo)), _const_spec((1, 2 * fo)),
                  _const_spec((2 * fo, 2 * fo)), _const_spec((1, 2 * fo)),
                  _const_spec((2 * fo, 2 * fo)), _const_spec((1, 2 * fo)),
                  pl.BlockSpec((None, 2 * fo, 2 * D_HYENA), lambda i: (i // half_steps, 0, 0)),
                  _const_spec((1, 2 * fo)), _const_spec((1, D_HYENA))],
        out_specs=[pl.BlockSpec((tr, D_HYENA), lambda i: (i, 0)),
                   _const_spec((1, D_HYENA))],
        out_shape=[jax.ShapeDtypeStruct((n, D_HYENA), F32),
                   jax.ShapeDtypeStruct((1, D_HYENA), F32)],
        compiler_params=_cparams(("arbitrary",), VMEM_LIMIT),
        name="hyena_filter",
    )(z, w1p, twice(b1), pair(w2), twice(b2), pair(w3), twice(b3), w4p, twice(freq), deltas)
    return kern, asum


def _half_rows(n):
    n_half = n // DFT_Q // 2 + 1
    return n_half, -(-n_half // 8) * 8


@functools.lru_cache(maxsize=None)
def _dft_tables(n, p_in):
    q_sz = DFT_Q
    p_sz = n // q_sz
    n_half, n_r = _half_rows(n)
    r = np.arange(n_r, dtype=np.float64)
    keep = (r < n_half).astype(np.float64)
    qq = np.arange(q_sz, dtype=np.float64)
    pp = np.arange(p_in, dtype=np.float64)
    tt = q_sz * pp[None, None, :] + qq[:, None, None]
    ang = -2.0 * np.pi * r[None, :, None] * tt / n
    t1 = np.concatenate([np.cos(ang), np.sin(ang)], axis=1) * np.tile(keep, 2)[None, :, None]
    mirror = np.where((r == 0) | (r == p_sz // 2), 1.0, 2.0)
    t4 = np.transpose(t1 * np.tile(mirror, 2)[None, :, None], (0, 2, 1)) / n
    a2 = -2.0 * np.pi * np.outer(qq, qq) / q_sz
    fre, fim = np.cos(a2), np.sin(a2)
    m2 = np.block([[fre, -fim], [fim, fre]])
    m2c = np.block([[fre, fim], [-fim, fre]])
    t1 = np.concatenate([t1[0::2], t1[1::2]], axis=2)
    return (t1.astype(np.float32), t4.astype(np.float32), m2.astype(np.float32),
            m2c.astype(np.float32))


SUBLANES = 8


def _spec_block(rg, h):
    return (rg * 2 + h) * SUBLANES * DFT_Q


def _dft_stage1(x_ref, t1_ref, spec_ref, n_r, p_in):
    q_sz = DFT_Q
    ct = spec_ref.shape[1]

    def body(j, carry):
        q0 = 2 * j
        x0 = x_ref[pl.ds(q0, p_in, stride=q_sz), :].astype(BF16)
        x1 = x_ref[pl.ds(q0 + 1, p_in, stride=q_sz), :].astype(BF16)
        zeros = jnp.zeros_like(x0)
        rhs = jnp.concatenate([jnp.concatenate([x0, zeros], axis=1),
                               jnp.concatenate([zeros, x1], axis=1)], axis=0)
        a = jnp.dot(t1_ref[j], rhs, preferred_element_type=F32)
        for k in range(2):
            row = pl.multiple_of((q0 + k) * SUBLANES, SUBLANES)
            for h in range(2):
                for rg in range(n_r // SUBLANES):
                    src = h * n_r + rg * SUBLANES
                    spec_ref[pl.ds(_spec_block(rg, h) + row, SUBLANES), :] = (
                        a[src:src + SUBLANES, k * ct:(k + 1) * ct])
        return carry

    lax.fori_loop(0, q_sz // 2, body, 0, unroll=4)


def _spec_rows(rg, h, r8):
    return pl.ds(_spec_block(rg, h) + r8, DFT_Q, stride=SUBLANES)


def _spectrum_body(kern_ref, asum_ref, t1_ref, m2_ref, kf_ref, spec_ref, *, n_r, p_in):
    _dft_stage1(kern_ref, t1_ref, spec_ref, n_r, p_in)
    inv = 1.0 / asum_ref[...]
    m2 = m2_ref[...]

    ct = spec_ref.shape[1]

    def body(rg, carry):
        for r8 in range(0, SUBLANES, 2):
            blk = jnp.concatenate(
                [jnp.concatenate([spec_ref[_spec_rows(rg, 0, r8 + j), :],
                                  spec_ref[_spec_rows(rg, 1, r8 + j), :]], axis=0) for j in range(2)],
                axis=1)
            xf = jnp.dot(m2, blk.astype(BF16), preferred_element_type=F32)
            for j in range(2):
                kf_ref[rg * SUBLANES + r8 + j] = (xf[:, j * ct:(j + 1) * ct] * inv).astype(BF16)
        return carry

    lax.fori_loop(0, n_r // SUBLANES, body, 0)


def _hyena_filter_spectrum(kern, asum):
    n, c = kern.shape
    p_sz = n // DFT_Q
    _, n_r = _half_rows(n)
    t1, _, m2, _ = _dft_tables(n, p_sz)
    t1 = jnp.asarray(t1).astype(BF16)
    m2 = jnp.asarray(m2).astype(BF16)
    ct = LANES
    return pl.pallas_call(
        functools.partial(_spectrum_body, n_r=n_r, p_in=p_sz),
        grid=(c // ct,),
        in_specs=[_single_spec((n, ct), lambda j: (0, j)),
                  pl.BlockSpec((1, ct), lambda j: (0, j)),
                  _single_spec(t1.shape, lambda j: (0, 0, 0)),
                  _const_spec(m2.shape)],
        out_specs=pl.BlockSpec((n_r, 2 * DFT_Q, ct), lambda j: (0, 0, j)),
        out_shape=jax.ShapeDtypeStruct((n_r, 2 * DFT_Q, c), BF16),
        scratch_shapes=[pltpu.VMEM((n_r * 2 * DFT_Q, ct), F32)],
        compiler_params=_cparams(("parallel",), VMEM_LIMIT),
        name="hyena_filter_spectrum",
    )(kern, asum, t1, m2)


def _short_conv_chunk(u_ref, w_ref, b_ref, i, rows, length):
    pack = BF16_PACK_ROWS
    base = pl.multiple_of(i * rows, rows)
    u = u_ref[pl.ds(base, rows), :].astype(F32)
    lo = pl.multiple_of(jnp.maximum(base - pack, 0), pack)
    hi = pl.multiple_of(jnp.minimum(base + rows, length - pack), pack)
    prev = u_ref[pl.ds(lo, pack), :].astype(F32)[pack - 1:pack]
    nxt = u_ref[pl.ds(hi, pack), :].astype(F32)[0:1]
    prev = jnp.where(base == 0, 0.0, prev)
    nxt = jnp.where(base + rows == length, 0.0, nxt)
    ridx = lax.broadcasted_iota(jnp.int32, (rows, 1), 0)
    up = jnp.where(ridx == 0, prev, pltpu.roll(u, 1, axis=0))
    dn = jnp.where(ridx == rows - 1, nxt, pltpu.roll(u, rows - 1, axis=0))
    w = w_ref[...]
    return up * w[0:1] + u * w[1:2] + dn * w[2:3] + b_ref[...]


def _hyena_conv_body(x0_ref, x1_ref, v_ref, w0_ref, w1_ref, wv_ref, b0_ref, b1_ref, bv_ref,
                     bias_ref, kf_ref, t1_ref, t4_ref, m2_ref, m2c_ref, o_ref, vx_ref, spec_ref,
                     *, length, rows):
    q_sz = DFT_Q
    n_half, n_r = _half_rows(2 * length)
    p_in = length // q_sz
    n_chunks = length // rows

    def gate_in(i, carry):
        x1c = _short_conv_chunk(x1_ref, w1_ref, b1_ref, i, rows, length)
        vc = _short_conv_chunk(v_ref, wv_ref, bv_ref, i, rows, length)
        vx_ref[pl.ds(pl.multiple_of(i * rows, rows), rows), :] = vc * x1c
        return carry

    lax.fori_loop(0, n_chunks, gate_in, 0)
    _dft_stage1(vx_ref, t1_ref, spec_ref, n_r, p_in)
    m2 = m2_ref[...]
    m2c = m2c_ref[...]

    ct = spec_ref.shape[1]

    def forward_rows(rg, r8s):
        blk = jnp.concatenate(
            [jnp.concatenate([spec_ref[_spec_rows(rg, 0, r8), :], spec_ref[_spec_rows(rg, 1, r8), :]],
                             axis=0) for r8 in r8s], axis=1)
        return jnp.dot(m2, blk.astype(BF16), preferred_element_type=F32)

    def filter_inverse_rows(rg, r8s, xf):
        kf = jnp.concatenate([kf_ref[rg * SUBLANES + r8] for r8 in r8s], axis=1).astype(F32)
        xre, xim = xf[:q_sz], xf[q_sz:]
        kre, kim = kf[:q_sz], kf[q_sz:]
        z = jnp.concatenate([xre * kre - xim * kim, xre * kim + xim * kre], axis=0)
        bf = jnp.dot(m2c, z.astype(BF16), preferred_element_type=F32)
        for j, r8 in enumerate(r8s):
            spec_ref[_spec_rows(rg, 0, r8), :] = bf[:q_sz, j * ct:(j + 1) * ct]
            spec_ref[_spec_rows(rg, 1, r8), :] = bf[q_sz:, j * ct:(j + 1) * ct]

    def mid_rows(rg, n_rows):
        groups = [tuple(range(a, min(a + 2, n_rows))) for a in range(0, n_rows, 2)]
        pending = None
        for r8s in groups:
            xf = forward_rows(rg, r8s)
            if pending is not None:
                filter_inverse_rows(rg, *pending)
            pending = (r8s, xf)
        filter_inverse_rows(rg, *pending)

    def mid(rg, carry):
        mid_rows(rg, SUBLANES)
        return carry

    lax.fori_loop(0, n_half // SUBLANES, mid, 0)
    if n_half % SUBLANES:
        mid_rows(n_half // SUBLANES, n_half % SUBLANES)
    bias = bias_ref[...]

    def last(q, carry):
        row = pl.multiple_of(q * SUBLANES, SUBLANES)
        tiles = [spec_ref[pl.ds(_spec_block(rg, h) + row, SUBLANES), :]
                 for h in range(2) for rg in range(n_r // SUBLANES)]
        bq = jnp.concatenate(tiles, axis=0).astype(BF16)
        y = jnp.dot(t4_ref[q], bq, preferred_element_type=F32)
        sl = pl.ds(q, p_in, stride=q_sz)
        vx_ref[sl, :] = y + bias * vx_ref[sl, :]
        return carry

    lax.fori_loop(0, q_sz, last, 0, unroll=8)

    def gate_out(i, carry):
        x0c = _short_conv_chunk(x0_ref, w0_ref, b0_ref, i, rows, length)
        sl = pl.ds(pl.multiple_of(i * rows, rows), rows)
        o_ref[sl, :] = (vx_ref[sl, :] * x0c).astype(BF16)
        return carry

    lax.fori_loop(0, n_chunks, gate_out, 0)


def _hyena_conv(hy, conv_w, conv_b, bias, kf):
    b, length, _ = hy.shape
    ch = D_HYENA
    ct = LANES
    nct = ch // ct
    n = 2 * length
    _, n_r = _half_rows(n)
    p_in = length // DFT_Q
    t1, t4, m2, m2c = _dft_tables(n, p_in)
    t1, t4, m2, m2c = (jnp.asarray(a).astype(BF16) for a in (t1, t4, m2, m2c))
    rows = min(CONV_ROW_CHUNK, length)
    col = lambda off: (lambda j, bi: (bi, 0, off * nct + j))
    wcol = lambda off: (lambda j, bi: (0, off * nct + j))
    conv_b2 = conv_b.reshape(1, -1)
    return pl.pallas_call(
        functools.partial(_hyena_conv_body, length=length, rows=rows),
        grid=(nct, b),
        in_specs=[_single_spec((None, length, ct), col(0)),
                  _single_spec((None, length, ct), col(1)),
                  _single_spec((None, length, ct), col(2)),
                  pl.BlockSpec((3, ct), wcol(0)), pl.BlockSpec((3, ct), wcol(1)),
                  pl.BlockSpec((3, ct), wcol(2)),
                  pl.BlockSpec((1, ct), wcol(0)), pl.BlockSpec((1, ct), wcol(1)),
                  pl.BlockSpec((1, ct), wcol(2)),
                  pl.BlockSpec((1, ct), lambda j, bi: (0, j)),
                  _single_spec((n_r, 2 * DFT_Q, ct), lambda j, bi: (0, 0, j)),
                  _single_spec(t1.shape, lambda j, bi: (0, 0, 0)),
                  _single_spec(t4.shape, lambda j, bi: (0, 0, 0)),
                  _const_spec(m2.shape), _const_spec(m2c.shape)],
        out_specs=pl.BlockSpec((None, length, ct), lambda j, bi: (bi, 0, j)),
        out_shape=jax.ShapeDtypeStruct((b, length, ch), BF16),
        scratch_shapes=[pltpu.VMEM((length, ct), F32),
                        pltpu.VMEM((n_r * 2 * DFT_Q, ct), F32)],
        compiler_params=_cparams(("parallel", "parallel"), VMEM_LIMIT),
        name="hyena_conv",
    )(hy, hy, hy, conv_w, conv_w, conv_w, conv_b2, conv_b2, conv_b2, bias.reshape(1, ch), kf,
      t1, t4, m2, m2c)


def _flash_body(q_ref, k_ref, v_ref, kc_ref, vc_ref, o_ref, s_ref, p_ref, al_ref, m_ref, l_ref,
                acc_ref, *, tk, rg):
    tq = q_ref.shape[0]
    nk = k_ref.shape[0] // tk
    nc = kc_ref.shape[0]
    n_chunks = nk + 1
    nt = (((1,), (1,)), ((), ()))

    def width(c):
        return nc if c == nk else tk

    def scores(c):
        keys = kc_ref[...] if c == nk else k_ref[c * tk:(c + 1) * tk, :]
        s_ref[c % 3, :, :width(c)] = lax.dot_general(q_ref[...], keys, nt, preferred_element_type=F32)

    def weighted_values(c):
        vals = vc_ref[...] if c == nk else v_ref[c * tk:(c + 1) * tk, :]
        return jnp.dot(p_ref[c % 2, :, :width(c)], vals, preferred_element_type=F32)

    def softmax(c):
        w = width(c)
        for g in range(tq // rg):
            rows = slice(g * rg, (g + 1) * rg)
            s = s_ref[c % 3, rows, :w]
            m_prev = m_ref[rows, :]
            m_new = jnp.maximum(m_prev, jnp.max(s, axis=-1, keepdims=True))
            alpha = jnp.exp2(m_prev - m_new)
            p = jnp.exp2(s - jnp.concatenate([m_new] * (w // LANES), axis=1))
            part = p[:, :LANES]
            for j in range(1, w // LANES):
                part = part + p[:, j * LANES:(j + 1) * LANES]
            l_ref[rows, :] = alpha * l_ref[rows, :] + part
            m_ref[rows, :] = m_new
            al_ref[rows, :] = alpha
            p_ref[c % 2, rows, :w] = p.astype(BF16)

    m_ref[...] = jnp.full_like(m_ref, -jnp.inf)
    l_ref[...] = jnp.zeros_like(l_ref)
    acc_ref[...] = jnp.zeros_like(acc_ref)
    scores(0)
    for c in range(n_chunks):
        prod = weighted_values(c - 1) if c >= 1 else None
        if c + 1 < n_chunks:
            scores(c + 1)
        softmax(c)
        if prod is not None:
            acc_ref[...] = al_ref[...] * (acc_ref[...] + prod)
    prod = weighted_values(n_chunks - 1)
    l_fin = jnp.sum(l_ref[...], axis=-1, keepdims=True)
    o_ref[...] = ((acc_ref[...] + prod) / l_fin).astype(o_ref.dtype)


def _flash_attention(q, k, v, kc, vc, tq, tk, rg):
    b, hds, length, _ = q.shape
    nc = kc.shape[2]
    assert length % tk == 0 and nc <= tk and nc % LANES == 0 and V_DIM == LANES
    return pl.pallas_call(
        functools.partial(_flash_body, tk=tk, rg=rg),
        grid=(b, hds, length // tq),
        in_specs=[pl.BlockSpec((None, None, tq, QK_DIM), lambda bi, h, qi: (bi, h, qi, 0)),
                  pl.BlockSpec((None, None, length, QK_DIM), lambda bi, h, qi: (bi, h, 0, 0)),
                  pl.BlockSpec((None, None, length, V_DIM), lambda bi, h, qi: (bi, h, 0, 0)),
                  pl.BlockSpec((None, None, nc, QK_DIM), lambda bi, h, qi: (bi, h, 0, 0)),
                  pl.BlockSpec((None, None, nc, V_DIM), lambda bi, h, qi: (bi, h, 0, 0))],
        out_specs=pl.BlockSpec((None, tq, V_DIM), lambda bi, h, qi: (bi, qi, h)),
        out_shape=jax.ShapeDtypeStruct((b, length, hds * V_DIM), BF16),
        scratch_shapes=[pltpu.VMEM((3, tq, tk), F32), pltpu.VMEM((2, tq, tk), BF16),
                        pltpu.VMEM((tq, LANES), F32), pltpu.VMEM((tq, LANES), F32),
                        pltpu.VMEM((tq, LANES), F32), pltpu.VMEM((tq, V_DIM), F32)],
        compiler_params=_cparams(("parallel", "parallel", "parallel"), VMEM_LIMIT),
        name="mla_flash_attention",
    )(q, k, v, kc, vc)


def _store_packed_rows(dst_ref, x):
    half = x.shape[1] // 2
    for j in range(half // LANES):
        hi = x[:, j * LANES:(j + 1) * LANES].astype(BF16).astype(F32)
        lo = x[:, half + j * LANES:half + (j + 1) * LANES].astype(BF16).astype(F32)
        dst_ref[j] = (lax.bitcast_convert_type(hi, jnp.uint32)
                      | (lax.bitcast_convert_type(lo, jnp.uint32) >> 16))


def _unpack_words(w):
    hi = lax.bitcast_convert_type(w & jnp.uint32(0xFFFF0000), F32)
    lo = lax.bitcast_convert_type(w << 16, F32)
    return hi, lo


def _load_packed_rows(src_ref):
    parts = [_unpack_words(src_ref[j]) for j in range(src_ref.shape[0])]
    return jnp.concatenate([p[0] for p in parts] + [p[1] for p in parts], axis=-1)


def _outproj_body(yh_ref, ya_ref, x_ref, g1_ref, sh2_ref, sc2_ref, n2g_ref, wo1_ref, wo2_ref,
                  rwt_ref, rb_ref, tri_ref, xn_ref, h2_ref, idx_ref, gate_ref, rank_ref, cnt_ref,
                  carry_sc, lg_sc):
    i = pl.program_id(0)

    @pl.when(i == 0)
    def _():
        carry_sc[...] = jnp.zeros_like(carry_sc)
        lg_sc[...] = jnp.zeros_like(lg_sc)

    logits = lg_sc[...]
    n_e, tt = logits.shape
    eidx = lax.broadcasted_iota(jnp.int32, (n_e, tt), 0).astype(F32)
    work = logits
    vals, sels, idxs = [], [], []
    for _ in range(TOP_K):
        m = jnp.max(work, axis=0, keepdims=True)
        ix = jnp.min(jnp.where(work == m, eidx, float(n_e)), axis=0, keepdims=True)
        sel = eidx == ix
        work = jnp.where(sel, -jnp.inf, work)
        vals.append(m)
        idxs.append(ix)
        sels.append(sel)
    es = [jnp.exp(vk - vals[0]) for vk in vals]
    den = es[0] + es[1] + es[2] + es[3]
    gate_ref[...] = jnp.concatenate(es, axis=0) / den
    idx_ref[...] = jnp.concatenate(idxs, axis=0).astype(jnp.int32)
    onehot = jnp.zeros((n_e, tt), F32)
    for sel in sels:
        onehot = onehot + sel.astype(F32)

    mix = (jnp.dot(yh_ref[...], wo1_ref[...], preferred_element_type=F32)
           + jnp.dot(ya_ref[...], wo2_ref[...], preferred_element_type=F32))
    xn = x_ref[...] + g1_ref[...] * mix
    xn_ref[...] = xn
    h2 = _rms(xn) * n2g_ref[...]
    h2 = h2 * (1.0 + sc2_ref[...]) + sh2_ref[...]
    _store_packed_rows(h2_ref, h2)
    lg_sc[...] = lax.dot_general(rwt_ref[...], h2, (((1,), (1,)), ((), ())), precision=HIGHEST,
                                 preferred_element_type=F32) + rb_ref[...]

    prefix = jnp.dot(onehot.astype(BF16), tri_ref[...], preferred_element_type=F32) + carry_sc[...]
    ranks = [jnp.sum(jnp.where(sel, prefix, 0.0), axis=0, keepdims=True) for sel in sels]
    rank_ref[...] = jnp.concatenate(ranks, axis=0).astype(jnp.int32)
    carry_sc[...] += jnp.where(i > 0, jnp.sum(onehot, axis=1, keepdims=True), 0.0)
    cnt_ref[...] = jnp.broadcast_to(carry_sc[...], cnt_ref.shape)


def _outproj_router(y_hy, y_att, x2, g1, sh2, sc2, norm2_g, w_out, router_w, router_b, tokens_per_batch,
                    tt):
    t, d = x2.shape
    ch = y_hy.shape[1]
    n_e = router_w.shape[1]
    wo1 = w_out[:ch].astype(BF16)
    wo2 = w_out[ch:].astype(BF16)
    tri = jnp.asarray(np.triu(np.ones((tt, tt), np.float32), k=1), BF16)
    steps_per_batch = tokens_per_batch // tt
    n_tiles = t // tt
    cur = lambda i: jnp.minimum(i, n_tiles - 1)
    tok = lambda i: (cur(i), 0)
    per_b = lambda i: (cur(i) // steps_per_batch, 0, 0)
    lanes_tok = lambda i: (0, jnp.maximum(i - 1, 0))
    return pl.pallas_call(
        _outproj_body,
        grid=(n_tiles + 1,),
        in_specs=[pl.BlockSpec((tt, ch), tok), pl.BlockSpec((tt, ch), tok),
                  pl.BlockSpec((tt, d), tok),
                  pl.BlockSpec((None, 1, d), per_b), pl.BlockSpec((None, 1, d), per_b),
                  pl.BlockSpec((None, 1, d), per_b),
                  _const_spec((1, d)), _const_spec(wo1.shape), _const_spec(wo2.shape),
                  _const_spec((n_e, d)), _const_spec((n_e, 1)), _const_spec((tt, tt))],
        out_specs=[pl.BlockSpec((tt, d), tok),
                   pl.BlockSpec((d // (2 * LANES), tt, LANES), lambda i: (0, cur(i), 0)),
                   pl.BlockSpec((TOP_K, tt), lanes_tok), pl.BlockSpec((TOP_K, tt), lanes_tok),
                   pl.BlockSpec((TOP_K, tt), lanes_tok), _const_spec((n_e, LANES))],
        out_shape=[jax.ShapeDtypeStruct((t, d), F32),
                   jax.ShapeDtypeStruct((d // (2 * LANES), t, LANES), jnp.uint32),
                   jax.ShapeDtypeStruct((TOP_K, t), jnp.int32),
                   jax.ShapeDtypeStruct((TOP_K, t), F32),
                   jax.ShapeDtypeStruct((TOP_K, t), jnp.int32),
                   jax.ShapeDtypeStruct((n_e, LANES), F32)],
        scratch_shapes=[pltpu.VMEM((n_e, 1), F32), pltpu.VMEM((n_e, tt), F32)],
        compiler_params=_cparams(("arbitrary",), VMEM_LIMIT),
        name="outproj_router",
    )(y_hy, y_att, x2, g1, sh2, sc2, norm2_g.reshape(1, d), wo1, wo2, router_w.T,
      router_b.reshape(n_e, 1), tri)


def _cast_rows(src_ref, dst_ref, chunk):
    def body(c, carry):
        sl = pl.ds(pl.multiple_of(c * chunk, chunk), chunk)
        dst_ref[sl, :] = src_ref[sl, :].astype(dst_ref.dtype)
        return carry

    lax.fori_loop(0, src_ref.shape[0] // chunk, body, 0)


def _expert_body(be_ref, nvalid_ref, xs_ref, wgu_ref, bgu_ref, wd_ref, bd_ref, ys_ref, wgu_bf, wd_bf):
    i = pl.program_id(0)
    n_valid = nvalid_ref[i]
    active = n_valid > 0
    new_expert = jnp.logical_or(i == 0, be_ref[i] != be_ref[jnp.maximum(i - 1, 0)])

    @pl.when(jnp.logical_and(active, new_expert))
    def _():
        _cast_rows(wgu_ref, wgu_bf, CAST_ROW_CHUNK)
        _cast_rows(wd_ref, wd_bf, CAST_ROW_CHUNK)

    @pl.when(active)
    def _():
        row = lax.broadcasted_iota(jnp.int32, (xs_ref.shape[1], 1), 0)
        xs = jnp.where(row < n_valid, _load_packed_rows(xs_ref), 0.0).astype(BF16)
        gu = jnp.dot(xs, wgu_bf[...], preferred_element_type=F32) + bgu_ref[...]
        dff = gu.shape[1] // 2
        gate = jnp.minimum(gu[:, :dff], SWIGLU_LIMIT)
        up = jnp.clip(gu[:, dff:], -SWIGLU_LIMIT, SWIGLU_LIMIT)
        act = (up + 1.0) * (gate * jax.nn.sigmoid(SWIGLU_ALPHA * gate))
        ys = jnp.dot(act.astype(BF16), wd_bf[...], preferred_element_type=F32) + bd_ref[...]
        _store_packed_rows(ys_ref, ys)

    @pl.when(jnp.logical_not(active))
    def _():
        ys_ref[...] = jnp.zeros_like(ys_ref)


def _expert_blocks(xs, block_e, n_valid, w_gu, b_gu, w_down, b_down):
    n_seg, n_rows, _ = xs.shape
    n_e, d, dff2 = w_gu.shape
    bm = MOE_ROWS
    seg_block = pl.BlockSpec((n_seg, bm, LANES), lambda i, be, nu: (0, i, 0))
    grid_spec = pltpu.PrefetchScalarGridSpec(
        num_scalar_prefetch=2,
        grid=(n_rows // bm,),
        in_specs=[seg_block,
                  pl.BlockSpec((None, d, dff2), lambda i, be, nu: (be[i], 0, 0)),
                  pl.BlockSpec((None, 1, dff2), lambda i, be, nu: (be[i], 0, 0)),
                  pl.BlockSpec((None, dff2 // 2, d), lambda i, be, nu: (be[i], 0, 0)),
                  pl.BlockSpec((None, 1, d), lambda i, be, nu: (be[i], 0, 0))],
        out_specs=seg_block,
        scratch_shapes=[pltpu.VMEM((d, dff2), BF16), pltpu.VMEM((dff2 // 2, d), BF16)],
    )
    return pl.pallas_call(
        _expert_body,
        grid_spec=grid_spec,
        out_shape=jax.ShapeDtypeStruct(xs.shape, jnp.uint32),
        compiler_params=_cparams(("arbitrary",), VMEM_LIMIT),
        name="moe_experts",
    )(block_e, n_valid, xs, w_gu, b_gu.reshape(n_e, 1, dff2), w_down, b_down.reshape(n_e, 1, d))


def _sc_gather(table, idx):
    n = idx.shape[0]
    width = table.shape[1]
    mesh = plsc.VectorSubcoreMesh(core_axis_name="core", subcore_axis_name="subcore")
    n_workers = mesh.num_cores * mesh.num_subcores
    assert width == LANES and n % (SC_WINDOW * n_workers) == 0

    @functools.partial(pl.kernel, out_type=jax.ShapeDtypeStruct((n, width), table.dtype), mesh=mesh)
    def gather_kernel(table_hbm, idx_hbm, out_hbm):
        def body(idx_vmem, out_vmem):
            pltpu.sync_copy(table_hbm.at[idx_vmem.at[0]], out_vmem)

        pltpu.emit_pipeline(
            body,
            grid=(n // SC_WINDOW,),
            in_specs=[pl.BlockSpec((1, SC_WINDOW), lambda i: (0, i))],
            out_specs=[pl.BlockSpec((SC_WINDOW, width), lambda i: (i, 0))],
            core_axis_name=("core", "subcore"),
            dimension_semantics=(pltpu.PARALLEL,),
        )(idx_hbm, out_hbm)

    return gather_kernel(table, idx.reshape(1, n))


def _sc_scatter(rows, idx, n_copies, n_out):
    n, width = rows.shape
    mesh = plsc.VectorSubcoreMesh(core_axis_name="core", subcore_axis_name="subcore")
    n_workers = mesh.num_cores * mesh.num_subcores
    assert width == LANES and n % (SC_WINDOW * n_workers) == 0 and idx.shape == (8, n)

    @functools.partial(pl.kernel, out_type=jax.ShapeDtypeStruct((n_out, width), rows.dtype), mesh=mesh)
    def scatter_kernel(rows_hbm, idx_hbm, out_hbm):
        def body(rows_vmem, idx_vmem):
            for k in range(n_copies):
                pltpu.sync_copy(rows_vmem, out_hbm.at[idx_vmem.at[k]])

        pltpu.emit_pipeline(
            body,
            grid=(n // SC_WINDOW,),
            in_specs=[pl.BlockSpec((SC_WINDOW, width), lambda i: (i, 0)),
                      pl.BlockSpec((8, SC_WINDOW), lambda i: (0, i))],
            out_specs=[],
            core_axis_name=("core", "subcore"),
            dimension_semantics=(pltpu.PARALLEL,),
        )(rows_hbm, idx_hbm)

    return scatter_kernel(rows, idx)


def _combine_body(pk_ref, gt_ref, xn_ref, g2_ref, o_ref):
    n_seg = pk_ref.shape[0]
    half = o_ref.shape[1] // 2
    gt = gt_ref[...]
    g2 = g2_ref[...]
    for j in range(n_seg):
        acc_hi = None
        acc_lo = None
        for kk in range(TOP_K):
            hi, lo = _unpack_words(pk_ref[j, kk])
            g = gt[:, kk:kk + 1]
            acc_hi = g * hi if acc_hi is None else acc_hi + g * hi
            acc_lo = g * lo if acc_lo is None else acc_lo + g * lo
        c_hi = slice(j * LANES, (j + 1) * LANES)
        c_lo = slice(half + j * LANES, half + (j + 1) * LANES)
        o_ref[:, c_hi] = xn_ref[:, c_hi] + g2[:, c_hi] * acc_hi
        o_ref[:, c_lo] = xn_ref[:, c_lo] + g2[:, c_lo] * acc_lo


def _combine(picked, gates_t, xn, g2, tokens_per_batch, tt):
    n_seg, _, t, _ = picked.shape
    d = xn.shape[1]
    steps_per_batch = tokens_per_batch // tt
    return pl.pallas_call(
        _combine_body,
        grid=(t // tt,),
        in_specs=[pl.BlockSpec((n_seg, TOP_K, tt, LANES), lambda i: (0, 0, i, 0)),
                  pl.BlockSpec((tt, TOP_K), lambda i: (i, 0)),
                  pl.BlockSpec((tt, d), lambda i: (i, 0)),
                  pl.BlockSpec((None, 1, d), lambda i: (i // steps_per_batch, 0, 0))],
        out_specs=pl.BlockSpec((tt, d), lambda i: (i, 0)),
        out_shape=jax.ShapeDtypeStruct((t, d), F32),
        compiler_params=_cparams(("parallel",), VMEM_LIMIT),
        name="moe_combine",
    )(picked, gates_t, xn, g2)


def _moe(h2p, xn, g2, idx, gates, ranks, counts, w_gu, b_gu, w_down, b_down, tokens_per_batch):
    n_seg, t, _ = h2p.shape
    bm = MOE_ROWS
    n_e = w_gu.shape[0]
    cnt = counts[:, 0].astype(jnp.int32)
    padded = (cnt + bm - 1) // bm * bm
    padded_ends = jnp.cumsum(padded)
    padded_starts = padded_ends - padded
    experts = jnp.arange(n_e, dtype=jnp.int32)[:, None, None]
    dest = ranks + jnp.sum(jnp.where(idx[None] == experts, padded_starts[:, None, None], 0), axis=0)
    n_blocks = t * TOP_K // bm + n_e
    n_rows = n_blocks * bm
    block_start = jnp.arange(n_blocks, dtype=jnp.int32) * bm
    block_e = jnp.minimum(jnp.sum(padded_ends[None, :] <= block_start[:, None], axis=1),
                          n_e - 1).astype(jnp.int32)
    n_valid = jnp.clip(cnt[block_e] - (block_start - padded_starts[block_e]), 0, bm).astype(jnp.int32)
    seg = jnp.arange(n_seg, dtype=jnp.int32)
    scatter_idx = (seg[None, :, None] * n_rows + dest[:, None, :]).reshape(TOP_K, n_seg * t)
    scatter_idx = jnp.concatenate([scatter_idx, scatter_idx], axis=0)
    xs = _sc_scatter(h2p.reshape(n_seg * t, LANES), scatter_idx, TOP_K, n_seg * n_rows)
    ys = _expert_blocks(xs.reshape(n_seg, n_rows, LANES), block_e, n_valid, w_gu, b_gu, w_down, b_down)
    picked = _sc_gather(ys.reshape(n_seg * n_rows, LANES),
                        (seg[:, None, None] * n_rows + dest[None]).reshape(-1))
    return _combine(picked.reshape(n_seg, TOP_K, t, LANES), gates.T, xn, g2, tokens_per_batch,
                    min(TOKEN_TILE, tokens_per_batch))


def kernel(x, c, ctx, c_ctx, mod_w, mod_b, norm1_g, w_in, hy_conv_w, hy_conv_b, hy_f_w1, hy_f_b1,
           hy_f_w2, hy_f_b2, hy_f_w3, hy_f_b3, hy_f_w4, hy_f_freq, hy_bias, mla_q_norm_g, mla_w_uq,
           mla_kv_norm_g, mla_w_ukv, qk_norm_q_g, qk_norm_k_g, w_out, norm2_g, router_w, router_b,
           exp_w_gu, exp_b_gu, exp_w_down, exp_b_down):
    b, length, d = x.shape
    depth = mod_w.shape[0]
    assert depth == 1, "single-layer kernel"
    ly = 0
    c_rows = jnp.concatenate([c, c_ctx[None, :], jnp.zeros((8 - b - 1, d), F32)], axis=0)
    mod = _adaln_table(c_rows, mod_w[ly], mod_b[ly])
    mod6 = mod.reshape(8, 6, d)
    sh1, sc1, g1, sh2, sc2, g2 = (mod6[:b, j][:, None, :] for j in range(6))
    csh1 = mod6[b:b + 1, 0][:, None, :]
    csc1 = mod6[b:b + 1, 1][:, None, :]

    weights = _mla_weights(w_in[ly], mla_w_uq[ly], mla_w_ukv[ly], qk_norm_q_g[ly], qk_norm_k_g[ly])
    n_ctx = ctx.shape[1]
    _, _, k_c, v_c = _inproj(ctx, jnp.broadcast_to(csh1, (b, 1, d)), jnp.broadcast_to(csc1, (b, 1, d)),
                             norm1_g[ly], weights, mla_q_norm_g[ly], mla_kv_norm_g[ly], False, n_ctx)
    hy, q, k, v = _inproj(x, sh1, sc1, norm1_g[ly], weights, mla_q_norm_g[ly], mla_kv_norm_g[ly],
                          True, min(TOKEN_TILE, length))

    kern, asum = _hyena_kernel_taps(length, hy_f_w1[ly], hy_f_b1[ly], hy_f_w2[ly], hy_f_b2[ly],
                                    hy_f_w3[ly], hy_f_b3[ly], hy_f_w4[ly], hy_f_freq[ly])
    kf = _hyena_filter_spectrum(kern, asum)
    y_hy = _hyena_conv(hy, hy_conv_w[ly], hy_conv_b[ly], hy_bias[ly], kf)

    y_att = _flash_attention(q, k, v, k_c, v_c, min(ATTN_Q_TILE, length), min(ATTN_K_TILE, length),
                             ATTN_ROW_GROUP)

    t = b * length
    xn, h2, idx, gates, ranks, counts = _outproj_router(
        y_hy.reshape(t, -1), y_att.reshape(t, -1), x.reshape(t, d), g1, sh2, sc2, norm2_g[ly],
        w_out[ly], router_w[ly], router_b[ly], length, min(TOKEN_TILE, length))
    out = _moe(h2, xn, g2, idx, gates, ranks, counts, exp_w_gu[ly], exp_b_gu[ly],
               exp_w_down[ly], exp_b_down[ly], length)
    return out.reshape(b, length, d)
```

```python
import functools
import math

import jax
import jax.numpy as jnp
import numpy as np
from jax import lax
from jax.experimental import pallas as pl
from jax.experimental.pallas import tpu as pltpu
from jax.experimental.pallas import tpu_sc as plsc

F32 = jnp.float32
BF16 = jnp.bfloat16
HIGHEST = lax.Precision.HIGHEST

GRID_W = 64
D_HYENA = 512
FILTER_ORDER = 64
POS_EMB_DIM = 33
MIN_DECAY = math.log(1e-2) / 0.3
MAX_DECAY = math.log(1e-2) / 1.5
NOPE_DIM = 128
ROPE_DIM = 64
QK_DIM = NOPE_DIM + ROPE_DIM
V_DIM = 128
MLA_HEADS = 4
Q_RANK = 256
KV_RANK = 128
ROPE_THETA = 10000.0
N_EXPERTS = 32
TOP_K = 4
SWIGLU_ALPHA = 1.702
SWIGLU_LIMIT = 7.0
NORM_EPS = 1e-6

LANES = 128
VMEM_LIMIT = 56 * 1024 * 1024

BF16_PACK_ROWS = 16

DFT_Q = LANES
MOE_ROWS = 512
SC_WINDOW = 128

TOKEN_TILE = 512
ATTN_Q_TILE = 512
ATTN_K_TILE = 512
ATTN_ROW_GROUP = 32
FILTER_TILE = 1024
CONV_ROW_CHUNK = 512
CAST_ROW_CHUNK = 128


def _cparams(sem, vmem=None):
    return pltpu.CompilerParams(dimension_semantics=sem, vmem_limit_bytes=vmem)


def _const_spec(shape):
    nd = len(shape)
    return pl.BlockSpec(shape, lambda *_: (0,) * nd)


def _single_spec(shape, index_map):
    return pl.BlockSpec(shape, index_map, pipeline_mode=pl.Buffered(1))


def _mod_body(c_ref, w_ref, b_ref, o_ref):
    cc = c_ref[...]
    s = cc * jax.nn.sigmoid(cc)
    o_ref[...] = jnp.dot(s, w_ref[...], precision=HIGHEST,
                         preferred_element_type=F32) + b_ref[...]


def _adaln_table(c_rows, mod_w, mod_b):
    rows, d = c_rows.shape
    n = mod_w.shape[1]
    tn = n // 8
    return pl.pallas_call(
        _mod_body,
        grid=(n // tn,),
        in_specs=[_const_spec((rows, d)),
                  pl.BlockSpec((d, tn), lambda j: (0, j)),
                  pl.BlockSpec((1, tn), lambda j: (0, j))],
        out_specs=pl.BlockSpec((rows, tn), lambda j: (0, j)),
        out_shape=jax.ShapeDtypeStruct((rows, n), F32),
        compiler_params=_cparams(("arbitrary",)),
        name="adaln_table",
    )(c_rows, mod_w, mod_b.reshape(1, n))


def _rms(x, eps=NORM_EPS):
    return x * lax.rsqrt(jnp.mean(x * x, axis=-1, keepdims=True) + eps)


def _inproj_body(x_ref, sh_ref, sc_ref, g_ref, why_ref, wmla_ref, qng_ref, wuq_ref,
                 kvng_ref, wukv_ref, ct_ref, st_ref, ctk_ref, gq1_ref, gq2_ref, gkn_ref,
                 gkr_ref, hy_ref, q_ref, k_ref, v_ref, mla_sc, *, q_scale):
    i = pl.program_id(0)

    @pl.when(i == 0)
    def _():
        mla_sc[...] = jnp.zeros_like(mla_sc)

    mla = mla_sc[...]
    cq = mla[:, :Q_RANK]
    ckv = mla[:, Q_RANK:Q_RANK + KV_RANK]
    pe2 = mla[:, Q_RANK + KV_RANK:]
    qf = jnp.dot((_rms(cq) * qng_ref[...]).astype(BF16), wuq_ref[...],
                 preferred_element_type=F32)
    kvf = jnp.dot((_rms(ckv) * kvng_ref[...]).astype(BF16), wukv_ref[...],
                  preferred_element_type=F32)
    ct = ct_ref[...]
    st = st_ref[...]
    lane256 = lax.broadcasted_iota(jnp.int32, (1, 2 * LANES), 1)
    qmask = (lane256 < QK_DIM).astype(F32)
    lane128 = lax.broadcasted_iota(jnp.int32, (1, LANES), 1)
    pemask = (lane128 < ROPE_DIM).astype(F32)
    kr0 = pe2 * gkr_ref[...] * ctk_ref[...]
    krs = kr0 + pltpu.roll(kr0, ROPE_DIM, axis=1)
    pem = pe2 * pemask
    ss_pe = jnp.sum(pem * pem, axis=-1, keepdims=True)
    gq1 = gq1_ref[...]
    gq2 = gq2_ref[...]
    gkn = gkn_ref[...]
    for hd in range(MLA_HEADS):
        slab = qf[:, hd * 2 * LANES:(hd + 1) * 2 * LANES]
        sm = slab * qmask
        rq = lax.rsqrt(jnp.sum(sm * sm, axis=-1, keepdims=True) / QK_DIM + NORM_EPS) * q_scale
        t = slab * gq1 * ct + pltpu.roll(slab * gq2 * st, QK_DIM, axis=1)
        q_ref[hd] = (t * rq)[:, :QK_DIM].astype(BF16)
        kn = kvf[:, hd * 2 * LANES:hd * 2 * LANES + NOPE_DIM]
        rk = lax.rsqrt((jnp.sum(kn * kn, axis=-1, keepdims=True) + ss_pe) / QK_DIM + NORM_EPS)
        kslab = jnp.concatenate([kn * gkn, krs], axis=-1) * rk
        k_ref[hd] = kslab[:, :QK_DIM].astype(BF16)
        v_ref[hd] = kvf[:, hd * 2 * LANES + NOPE_DIM:(hd + 1) * 2 * LANES].astype(BF16)

    h = _rms(x_ref[...]) * g_ref[...]
    h = h * (1.0 + sc_ref[...]) + sh_ref[...]
    hb = h.astype(BF16)
    hy_ref[...] = jnp.dot(hb, why_ref[...], preferred_element_type=F32).astype(BF16)
    mla_sc[...] = jnp.dot(hb, wmla_ref[...], preferred_element_type=F32)


@functools.lru_cache(maxsize=None)
def _rope_lane_tables(length, use_rope):
    if use_rope:
        n_freq = ROPE_DIM // 4
        t = np.arange(length)
        inv_freq = np.power(ROPE_THETA, -np.arange(n_freq, dtype=np.float64) / n_freq)
        ar = (t // GRID_W).astype(np.float64)[:, None] * inv_freq
        ac = (t % GRID_W).astype(np.float64)[:, None] * inv_freq
        c64 = np.concatenate([np.cos(ar), np.cos(ar), np.cos(ac), np.cos(ac)], axis=-1)
        s64 = np.concatenate([-np.sin(ar), np.sin(ar), -np.sin(ac), np.sin(ac)], axis=-1)
    else:
        c64 = np.ones((length, ROPE_DIM))
        s64 = np.zeros((length, ROPE_DIM))
    z64 = np.zeros((length, ROPE_DIM))
    ct = np.concatenate([np.ones((length, LANES)), c64, z64], axis=-1)
    st = np.concatenate([np.zeros((length, LANES)), z64, s64], axis=-1)
    ctk = np.concatenate([c64, s64], axis=-1)
    return ct.astype(np.float32), st.astype(np.float32), ctk.astype(np.float32)


_SWAP16 = np.concatenate([np.arange(16, 32), np.arange(0, 16), np.arange(48, 64), np.arange(32, 48)])


def _mla_weights(w_in, mla_w_uq, mla_w_ukv, qk_norm_q_g, qk_norm_k_g):
    hy_cols = 3 * D_HYENA
    w_hy = w_in[:, :hy_cols].astype(BF16)
    w_pe = w_in[:, hy_cols + Q_RANK + KV_RANK:]
    w_mla = jnp.concatenate([w_in[:, hy_cols:hy_cols + Q_RANK + KV_RANK], w_pe, w_pe[:, _SWAP16]],
                            axis=-1).astype(BF16)
    wq = mla_w_uq.reshape(Q_RANK, MLA_HEADS, QK_DIM)
    wq_rope = wq[:, :, NOPE_DIM:]
    w_uq2 = jnp.concatenate([wq[:, :, :NOPE_DIM], wq_rope, wq_rope[:, :, _SWAP16]], axis=-1)
    w_uq2 = w_uq2.reshape(Q_RANK, MLA_HEADS * 2 * LANES).astype(BF16)
    w_ukv2 = mla_w_ukv.astype(BF16)
    gq_r = qk_norm_q_g[NOPE_DIM:]
    z64 = jnp.zeros((ROPE_DIM,), F32)
    gq1 = jnp.concatenate([qk_norm_q_g[:NOPE_DIM], gq_r, z64]).reshape(1, -1)
    gq2 = jnp.concatenate([jnp.zeros((NOPE_DIM,), F32), z64, gq_r[_SWAP16]]).reshape(1, -1)
    gkn = qk_norm_k_g[:NOPE_DIM].reshape(1, -1)
    gk_r = qk_norm_k_g[NOPE_DIM:]
    gkr = jnp.concatenate([gk_r, gk_r[_SWAP16]]).reshape(1, -1)
    return w_hy, w_mla, w_uq2, w_ukv2, gq1, gq2, gkn, gkr


def _inproj(x, shift, scale, norm_g, weights, q_norm_g, kv_norm_g, use_rope, tl):
    b, length, d = x.shape
    w_hy, w_mla, w_uq2, w_ukv2, gq1, gq2, gkn, gkr = weights
    ct, st, ctk = _rope_lane_tables(length, use_rope)
    q_scale = QK_DIM ** -0.5 * math.log2(math.e)
    nt = length // tl
    n_tiles = b * nt
    hyc = w_hy.shape[1]
    cur = lambda i: jnp.minimum(i, n_tiles - 1)
    prev = lambda i: jnp.maximum(i - 1, 0)
    tok = lambda i: (cur(i), 0)
    per_b = lambda i: (cur(i) // nt, 0, 0)
    pos = lambda i: (prev(i) % nt, 0)
    head_blk = lambda i: (prev(i) // nt, 0, prev(i) % nt, 0)
    hy, q, k, v = pl.pallas_call(
        functools.partial(_inproj_body, q_scale=q_scale),
        grid=(n_tiles + 1,),
        in_specs=[pl.BlockSpec((tl, d), tok),
                  pl.BlockSpec((None, 1, d), per_b),
                  pl.BlockSpec((None, 1, d), per_b),
                  _const_spec((1, d)),
                  _const_spec(w_hy.shape), _const_spec(w_mla.shape),
                  _const_spec((1, Q_RANK)), _const_spec(w_uq2.shape),
                  _const_spec((1, KV_RANK)), _const_spec(w_ukv2.shape),
                  pl.BlockSpec((tl, 2 * LANES), pos), pl.BlockSpec((tl, 2 * LANES), pos),
                  pl.BlockSpec((tl, LANES), pos),
                  _const_spec((1, 2 * LANES)), _const_spec((1, 2 * LANES)),
                  _const_spec((1, LANES)), _const_spec((1, LANES))],
        out_specs=[pl.BlockSpec((tl, hyc), tok),
                   pl.BlockSpec((None, MLA_HEADS, tl, QK_DIM), head_blk),
                   pl.BlockSpec((None, MLA_HEADS, tl, QK_DIM), head_blk),
                   pl.BlockSpec((None, MLA_HEADS, tl, V_DIM), head_blk)],
        out_shape=[jax.ShapeDtypeStruct((b * length, hyc), BF16),
                   jax.ShapeDtypeStruct((b, MLA_HEADS, length, QK_DIM), BF16),
                   jax.ShapeDtypeStruct((b, MLA_HEADS, length, QK_DIM), BF16),
                   jax.ShapeDtypeStruct((b, MLA_HEADS, length, V_DIM), BF16)],
        scratch_shapes=[pltpu.VMEM((tl, w_mla.shape[1]), F32)],
        compiler_params=_cparams(("arbitrary",), VMEM_LIMIT),
        name="inproj_mla",
    )(x.reshape(b * length, d), shift, scale, norm_g.reshape(1, d), w_hy, w_mla,
      q_norm_g.reshape(1, -1), w_uq2, kv_norm_g.reshape(1, -1), w_ukv2, ct, st, ctk, gq1, gq2, gkn, gkr)
    return hy.reshape(b, length, hyc), q, k, v


def _filter_body(z_ref, w1_ref, b1_ref, w2_ref, b2_ref, w3_ref, b3_ref, w4_ref, fr_ref, dl_ref,
                 kern_ref, asum_ref, *, zero_row, tr):
    i = pl.program_id(0)
    fr = fr_ref[...]
    hr = tr // 2
    za = z_ref[:hr, :]
    zb = z_ref[hr:, :]
    dot = functools.partial(jnp.dot, precision=HIGHEST, preferred_element_type=F32)
    h = jnp.sin(fr * (dot(jnp.concatenate([za, zb], axis=1), w1_ref[...]) + b1_ref[...]))
    h = jnp.sin(fr * (dot(h, w2_ref[...]) + b2_ref[...]))
    h = jnp.sin(fr * (dot(h, w3_ref[...]) + b3_ref[...]))
    o2 = dot(h, w4_ref[...])
    nc = o2.shape[1] // 2
    oa = o2[:, :nc] * jnp.exp(-za[:, 0:1] * dl_ref[...])
    ob = o2[:, nc:] * jnp.exp(-zb[:, 0:1] * dl_ref[...])

    @pl.when(i == 0)
    def _():
        asum_ref[...] = jnp.zeros_like(asum_ref)

    asum_ref[...] += (jnp.sum(jnp.abs(oa), axis=0, keepdims=True)
                      + jnp.sum(jnp.abs(ob), axis=0, keepdims=True))
    row = i * tr + lax.broadcasted_iota(jnp.int32, (hr, 1), 0)
    kern_ref[:hr, :] = jnp.where(row == zero_row, 0.0, oa)
    kern_ref[hr:, :] = jnp.where(row + hr == zero_row, 0.0, ob)


@functools.lru_cache(maxsize=None)
def _filter_features(length):
    n = 2 * length
    bands = (POS_EMB_DIM - 1) // 2
    pos = np.concatenate([np.arange(length), (n - np.arange(length, n)) % length])
    t_tab = np.linspace(0.0, 1.0, length)[:, None]
    w_ang = 2.0 * np.pi * np.arange(length, dtype=np.float64)[:, None] / length
    f = np.linspace(1e-4, bands - 1, bands)[None, :]
    z_tab = np.concatenate([t_tab, np.cos(f * w_ang), -np.sin(f * w_ang)], axis=-1)
    z = np.pad(z_tab[pos], ((0, 0), (0, LANES - POS_EMB_DIM)))
    return z.astype(np.float32)


def _hyena_kernel_taps(length, w1, b1, w2, b2, w3, b3, w4, freq):
    n = 2 * length
    z = jnp.asarray(_filter_features(length))
    deltas = jnp.abs(jnp.linspace(MIN_DECAY, MAX_DECAY, D_HYENA, dtype=F32)).reshape(1, -1)
    tr = min(FILTER_TILE, length)
    half_steps = length // tr
    fo = FILTER_ORDER

    def pair(w):
        zeros = jnp.zeros_like(w)
        return jnp.concatenate([jnp.concatenate([w, zeros], axis=1),
                                jnp.concatenate([zeros, w], axis=1)], axis=0)

    def twice(v):
        return jnp.concatenate([v, v]).reshape(1, 2 * fo)

    w1p = pair(jnp.pad(w1, ((0, LANES - POS_EMB_DIM), (0, 0))))
    w4p = jnp.stack([pair(w4[:, :D_HYENA]), pair(w4[:, D_HYENA:])])
    kern, asum = pl.pallas_call(
        functools.partial(_filter_body, zero_row=length, tr=tr),
        grid=(n // tr,),
        in_specs=[pl.BlockSpec((tr, LANES), lambda i: (i, 0)),
                  _const_spec((2 * LANES, 2 * fo)), _const_spec((1, 2 * fo)),
                  _const_spec((2 * fo, 2 * fo)), _const_spec((1, 2 * fo)),
                  _const_spec((2 * fo, 2 * fo)), _const_spec((1, 2 * fo)),
                  pl.BlockSpec((None, 2 * fo, 2 * D_HYENA), lambda i: (i // half_steps, 0, 0)),
                  _const_spec((1, 2 * fo)), _const_spec((1, D_HYENA))],
        out_specs=[pl.BlockSpec((tr, D_HYENA), lambda i: (i, 0)),
                   _const_spec((1, D_HYENA))],
        out_shape=[jax.ShapeDtypeStruct((n, D_HYENA), F32),
                   jax.ShapeDtypeStruct((1, D_HYENA), F32)],
        compiler_params=_cparams(("arbitrary",), VMEM_LIMIT),
        name="hyena_filter",
    )(z, w1p, twice(b1), pair(w2), twice(b2), pair(w3), twice(b3), w4p, twice(freq), deltas)
    return kern, asum


def _half_rows(n):
    n_half = n // DFT_Q // 2 + 1
    return n_half, -(-n_half // 8) * 8


@functools.lru_cache(maxsize=None)
def _dft_tables(n, p_in):
    q_sz = DFT_Q
    p_sz = n // q_sz
    n_half, n_r = _half_rows(n)
    r = np.arange(n_r, dtype=np.float64)
    keep = (r < n_half).astype(np.float64)
    qq = np.arange(q_sz, dtype=np.float64)
    pp = np.arange(p_in, dtype=np.float64)
    tt = q_sz * pp[None, None, :] + qq[:, None, None]
    ang = -2.0 * np.pi * r[None, :, None] * tt / n
    t1 = np.concatenate([np.cos(ang), np.sin(ang)], axis=1) * np.tile(keep, 2)[None, :, None]
    mirror = np.where((r == 0) | (r == p_sz // 2), 1.0, 2.0)
    t4 = np.transpose(t1 * np.tile(mirror, 2)[None, :, None], (0, 2, 1)) / n
    a2 = -2.0 * np.pi * np.outer(qq, qq) / q_sz
    fre, fim = np.cos(a2), np.sin(a2)
    m2 = np.block([[fre, -fim], [fim, fre]])
    m2c = np.block([[fre, fim], [-fim, fre]])
    t1 = np.concatenate([t1[0::2], t1[1::2]], axis=2)
    return (t1.astype(np.float32), t4.astype(np.float32), m2.astype(np.float32),
            m2c.astype(np.float32))


SUBLANES = 8


def _spec_block(rg, h):
    return (rg * 2 + h) * SUBLANES * DFT_Q


def _dft_stage1(x_ref, t1_ref, spec_ref, n_r, p_in):
    q_sz = DFT_Q
    ct = spec_ref.shape[1]

    def body(j, carry):
        q0 = 2 * j
        x0 = x_ref[pl.ds(q0, p_in, stride=q_sz), :].astype(BF16)
        x1 = x_ref[pl.ds(q0 + 1, p_in, stride=q_sz), :].astype(BF16)
        zeros = jnp.zeros_like(x0)
        rhs = jnp.concatenate([jnp.concatenate([x0, zeros], axis=1),
                               jnp.concatenate([zeros, x1], axis=1)], axis=0)
        a = jnp.dot(t1_ref[j], rhs, preferred_element_type=F32)
        for k in range(2):
            row = pl.multiple_of((q0 + k) * SUBLANES, SUBLANES)
            for h in range(2):
                for rg in range(n_r // SUBLANES):
                    src = h * n_r + rg * SUBLANES
                    spec_ref[pl.ds(_spec_block(rg, h) + row, SUBLANES), :] = (
                        a[src:src + SUBLANES, k * ct:(k + 1) * ct])
        return carry

    lax.fori_loop(0, q_sz // 2, body, 0, unroll=4)


def _spec_rows(rg, h, r8):
    return pl.ds(_spec_block(rg, h) + r8, DFT_Q, stride=SUBLANES)


def _spectrum_body(kern_ref, asum_ref, t1_ref, m2_ref, kf_ref, spec_ref, *, n_r, p_in):
    _dft_stage1(kern_ref, t1_ref, spec_ref, n_r, p_in)
    inv = 1.0 / asum_ref[...]
    m2 = m2_ref[...]

    ct = spec_ref.shape[1]

    def body(rg, carry):
        for r8 in range(0, SUBLANES, 2):
            blk = jnp.concatenate(
                [jnp.concatenate([spec_ref[_spec_rows(rg, 0, r8 + j), :],
                                  spec_ref[_spec_rows(rg, 1, r8 + j), :]], axis=0) for j in range(2)],
                axis=1)
            xf = jnp.dot(m2, blk.astype(BF16), preferred_element_type=F32)
            for j in range(2):
                kf_ref[rg * SUBLANES + r8 + j] = (xf[:, j * ct:(j + 1) * ct] * inv).astype(BF16)
        return carry

    lax.fori_loop(0, n_r // SUBLANES, body, 0)


def _hyena_filter_spectrum(kern, asum):
    n, c = kern.shape
    p_sz = n // DFT_Q
    _, n_r = _half_rows(n)
    t1, _, m2, _ = _dft_tables(n, p_sz)
    t1 = jnp.asarray(t1).astype(BF16)
    m2 = jnp.asarray(m2).astype(BF16)
    ct = LANES
    return pl.pallas_call(
        functools.partial(_spectrum_body, n_r=n_r, p_in=p_sz),
        grid=(c // ct,),
        in_specs=[_single_spec((n, ct), lambda j: (0, j)),
                  pl.BlockSpec((1, ct), lambda j: (0, j)),
                  _single_spec(t1.shape, lambda j: (0, 0, 0)),
                  _const_spec(m2.shape)],
        out_specs=pl.BlockSpec((n_r, 2 * DFT_Q, ct), lambda j: (0, 0, j)),
        out_shape=jax.ShapeDtypeStruct((n_r, 2 * DFT_Q, c), BF16),
        scratch_shapes=[pltpu.VMEM((n_r * 2 * DFT_Q, ct), F32)],
        compiler_params=_cparams(("parallel",), VMEM_LIMIT),
        name="hyena_filter_spectrum",
    )(kern, asum, t1, m2)


def _short_conv_chunk(u_ref, w_ref, b_ref, i, rows, length):
    pack = BF16_PACK_ROWS
    base = pl.multiple_of(i * rows, rows)
    u = u_ref[pl.ds(base, rows), :].astype(F32)
    lo = pl.multiple_of(jnp.maximum(base - pack, 0), pack)
    hi = pl.multiple_of(jnp.minimum(base + rows, length - pack), pack)
    prev = u_ref[pl.ds(lo, pack), :].astype(F32)[pack - 1:pack]
    nxt = u_ref[pl.ds(hi, pack), :].astype(F32)[0:1]
    prev = jnp.where(base == 0, 0.0, prev)
    nxt = jnp.where(base + rows == length, 0.0, nxt)
    ridx = lax.broadcasted_iota(jnp.int32, (rows, 1), 0)
    up = jnp.where(ridx == 0, prev, pltpu.roll(u, 1, axis=0))
    dn = jnp.where(ridx == rows - 1, nxt, pltpu.roll(u, rows - 1, axis=0))
    w = w_ref[...]
    return up * w[0:1] + u * w[1:2] + dn * w[2:3] + b_ref[...]


def _hyena_conv_body(x0_ref, x1_ref, v_ref, w0_ref, w1_ref, wv_ref, b0_ref, b1_ref, bv_ref,
                     bias_ref, kf_ref, t1_ref, t4_ref, m2_ref, m2c_ref, o_ref, vx_ref, spec_ref,
                     *, length, rows):
    q_sz = DFT_Q
    n_half, n_r = _half_rows(2 * length)
    p_in = length // q_sz
    n_chunks = length // rows

    def gate_in(i, carry):
        x1c = _short_conv_chunk(x1_ref, w1_ref, b1_ref, i, rows, length)
        vc = _short_conv_chunk(v_ref, wv_ref, bv_ref, i, rows, length)
        vx_ref[pl.ds(pl.multiple_of(i * rows, rows), rows), :] = vc * x1c
        return carry

    lax.fori_loop(0, n_chunks, gate_in, 0)
    _dft_stage1(vx_ref, t1_ref, spec_ref, n_r, p_in)
    m2 = m2_ref[...]
    m2c = m2c_ref[...]

    ct = spec_ref.shape[1]

    def forward_rows(rg, r8s):
        blk = jnp.concatenate(
            [jnp.concatenate([spec_ref[_spec_rows(rg, 0, r8), :], spec_ref[_spec_rows(rg, 1, r8), :]],
                             axis=0) for r8 in r8s], axis=1)
        return jnp.dot(m2, blk.astype(BF16), preferred_element_type=F32)

    def filter_inverse_rows(rg, r8s, xf):
        kf = jnp.concatenate([kf_ref[rg * SUBLANES + r8] for r8 in r8s], axis=1).astype(F32)
        xre, xim = xf[:q_sz], xf[q_sz:]
        kre, kim = kf[:q_sz], kf[q_sz:]
        z = jnp.concatenate([xre * kre - xim * kim, xre * kim + xim * kre], axis=0)
        bf = jnp.dot(m2c, z.astype(BF16), preferred_element_type=F32)
        for j, r8 in enumerate(r8s):
            spec_ref[_spec_rows(rg, 0, r8), :] = bf[:q_sz, j * ct:(j + 1) * ct]
            spec_ref[_spec_rows(rg, 1, r8), :] = bf[q_sz:, j * ct:(j + 1) * ct]

    def mid_rows(rg, n_rows):
        groups = [tuple(range(a, min(a + 2, n_rows))) for a in range(0, n_rows, 2)]
        pending = None
        for r8s in groups:
            xf = forward_rows(rg, r8s)
            if pending is not None:
                filter_inverse_rows(rg, *pending)
            pending = (r8s, xf)
        filter_inverse_rows(rg, *pending)

    def mid(rg, carry):
        mid_rows(rg, SUBLANES)
        return carry

    lax.fori_loop(0, n_half // SUBLANES, mid, 0)
    if n_half % SUBLANES:
        mid_rows(n_half // SUBLANES, n_half % SUBLANES)
    bias = bias_ref[...]

    def last(q, carry):
        row = pl.multiple_of(q * SUBLANES, SUBLANES)
        tiles = [spec_ref[pl.ds(_spec_block(rg, h) + row, SUBLANES), :]
                 for h in range(2) for rg in range(n_r // SUBLANES)]
        bq = jnp.concatenate(tiles, axis=0).astype(BF16)
        y = jnp.dot(t4_ref[q], bq, preferred_element_type=F32)
        sl = pl.ds(q, p_in, stride=q_sz)
        vx_ref[sl, :] = y + bias * vx_ref[sl, :]
        return carry

    lax.fori_loop(0, q_sz, last, 0, unroll=8)

    def gate_out(i, carry):
        x0c = _short_conv_chunk(x0_ref, w0_ref, b0_ref, i, rows, length)
        sl = pl.ds(pl.multiple_of(i * rows, rows), rows)
        o_ref[sl, :] = (vx_ref[sl, :] * x0c).astype(BF16)
        return carry

    lax.fori_loop(0, n_chunks, gate_out, 0)


def _hyena_conv(hy, conv_w, conv_b, bias, kf):
    b, length, _ = hy.shape
    ch = D_HYENA
    ct = LANES
    nct = ch // ct
    n = 2 * length
    _, n_r = _half_rows(n)
    p_in = length // DFT_Q
    t1, t4, m2, m2c = _dft_tables(n, p_in)
    t1, t4, m2, m2c = (jnp.asarray(a).astype(BF16) for a in (t1, t4, m2, m2c))
    rows = min(CONV_ROW_CHUNK, length)
    col = lambda off: (lambda j, bi: (bi, 0, off * nct + j))
    wcol = lambda off: (lambda j, bi: (0, off * nct + j))
    conv_b2 = conv_b.reshape(1, -1)
    return pl.pallas_call(
        functools.partial(_hyena_conv_body, length=length, rows=rows),
        grid=(nct, b),
        in_specs=[_single_spec((None, length, ct), col(0)),
                  _single_spec((None, length, ct), col(1)),
                  _single_spec((None, length, ct), col(2)),
                  pl.BlockSpec((3, ct), wcol(0)), pl.BlockSpec((3, ct), wcol(1)),
                  pl.BlockSpec((3, ct), wcol(2)),
                  pl.BlockSpec((1, ct), wcol(0)), pl.BlockSpec((1, ct), wcol(1)),
                  pl.BlockSpec((1, ct), wcol(2)),
                  pl.BlockSpec((1, ct), lambda j, bi: (0, j)),
                  _single_spec((n_r, 2 * DFT_Q, ct), lambda j, bi: (0, 0, j)),
                  _single_spec(t1.shape, lambda j, bi: (0, 0, 0)),
                  _single_spec(t4.shape, lambda j, bi: (0, 0, 0)),
                  _const_spec(m2.shape), _const_spec(m2c.shape)],
        out_specs=pl.BlockSpec((None, length, ct), lambda j, bi: (bi, 0, j)),
        out_shape=jax.ShapeDtypeStruct((b, length, ch), BF16),
        scratch_shapes=[pltpu.VMEM((length, ct), F32),
                        pltpu.VMEM((n_r * 2 * DFT_Q, ct), F32)],
        compiler_params=_cparams(("parallel", "parallel"), VMEM_LIMIT),
        name="hyena_conv",
    )(hy, hy, hy, conv_w, conv_w, conv_w, conv_b2, conv_b2, conv_b2, bias.reshape(1, ch), kf,
      t1, t4, m2, m2c)


def _flash_body(q_ref, k_ref, v_ref, kc_ref, vc_ref, o_ref, s_ref, p_ref, al_ref, m_ref, acc_ref,
                *, tk, rg):
    tq = q_ref.shape[0]
    nk = k_ref.shape[0] // tk
    nc = kc_ref.shape[0]
    n_chunks = nk + 1
    nt = (((1,), (1,)), ((), ()))

    def width(c):
        return nc if c == nk else tk

    def scores(c):
        keys = kc_ref[...] if c == nk else k_ref[c * tk:(c + 1) * tk, :]
        s_ref[c % 3, :, :width(c)] = lax.dot_general(q_ref[...], keys, nt, preferred_element_type=F32)

    def weighted_values(c):
        vals = vc_ref[...] if c == nk else v_ref[c * tk:(c + 1) * tk, :]
        lane = lax.broadcasted_iota(jnp.int32, vals.shape, 1)
        ones_col = jnp.where(lane == 0, 1.0, 0.0).astype(BF16)
        ext = jnp.concatenate([vals, ones_col], axis=1)
        return jnp.dot(p_ref[c % 2, :, :width(c)], ext, preferred_element_type=F32)

    def softmax(c):
        w = width(c)
        for g in range(tq // rg):
            rows = slice(g * rg, (g + 1) * rg)
            s = s_ref[c % 3, rows, :w]
            m_prev = m_ref[rows, :]
            m_new = jnp.maximum(m_prev, jnp.max(s, axis=-1, keepdims=True))
            al_ref[rows, :] = jnp.exp2(m_prev - m_new)
            m_ref[rows, :] = m_new
            d = (s - jnp.concatenate([m_new] * (w // LANES), axis=1)).astype(BF16)
            p_ref[c % 2, rows, :w] = jnp.exp2(d)

    def rescale(prod):
        alpha = al_ref[...]
        acc_ref[...] = jnp.concatenate([alpha, alpha], axis=1) * (acc_ref[...] + prod)

    m_ref[...] = jnp.full_like(m_ref, -jnp.inf)
    acc_ref[...] = jnp.zeros_like(acc_ref)
    scores(0)
    for c in range(n_chunks):
        prod = weighted_values(c - 1) if c >= 1 else None
        if c + 1 < n_chunks:
            scores(c + 1)
        softmax(c)
        if prod is not None:
            rescale(prod)
    total = acc_ref[...] + weighted_values(n_chunks - 1)
    o_ref[...] = (total[:, :V_DIM] / total[:, V_DIM:V_DIM + 1]).astype(o_ref.dtype)


def _flash_attention(q, k, v, kc, vc, tq, tk, rg):
    b, hds, length, _ = q.shape
    nc = kc.shape[2]
    assert length % tk == 0 and nc <= tk and nc % LANES == 0 and V_DIM == LANES
    return pl.pallas_call(
        functools.partial(_flash_body, tk=tk, rg=rg),
        grid=(b, hds, length // tq),
        in_specs=[pl.BlockSpec((None, None, tq, QK_DIM), lambda bi, h, qi: (bi, h, qi, 0)),
                  pl.BlockSpec((None, None, length, QK_DIM), lambda bi, h, qi: (bi, h, 0, 0)),
                  pl.BlockSpec((None, None, length, V_DIM), lambda bi, h, qi: (bi, h, 0, 0)),
                  pl.BlockSpec((None, None, nc, QK_DIM), lambda bi, h, qi: (bi, h, 0, 0)),
                  pl.BlockSpec((None, None, nc, V_DIM), lambda bi, h, qi: (bi, h, 0, 0))],
        out_specs=pl.BlockSpec((None, tq, V_DIM), lambda bi, h, qi: (bi, qi, h)),
        out_shape=jax.ShapeDtypeStruct((b, length, hds * V_DIM), BF16),
        scratch_shapes=[pltpu.VMEM((3, tq, tk), F32), pltpu.VMEM((2, tq, tk), BF16),
                        pltpu.VMEM((tq, LANES), F32), pltpu.VMEM((tq, LANES), F32),
                        pltpu.VMEM((tq, 2 * V_DIM), F32)],
        compiler_params=_cparams(("parallel", "parallel", "parallel"), VMEM_LIMIT),
        name="mla_flash_attention",
    )(q, k, v, kc, vc)


def _store_packed_rows(dst_ref, x):
    half = x.shape[1] // 2
    for j in range(half // LANES):
        hi = x[:, j * LANES:(j + 1) * LANES].astype(BF16).astype(F32)
        lo = x[:, half + j * LANES:half + (j + 1) * LANES].astype(BF16).astype(F32)
        dst_ref[j] = (lax.bitcast_convert_type(hi, jnp.uint32)
                      | (lax.bitcast_convert_type(lo, jnp.uint32) >> 16))


def _unpack_words(w):
    hi = lax.bitcast_convert_type(w & jnp.uint32(0xFFFF0000), F32)
    lo = lax.bitcast_convert_type(w << 16, F32)
    return hi, lo


def _load_packed_rows(src_ref):
    parts = [_unpack_words(src_ref[j]) for j in range(src_ref.shape[0])]
    return jnp.concatenate([p[0] for p in parts] + [p[1] for p in parts], axis=-1)


def _outproj_body(yh_ref, ya_ref, x_ref, g1_ref, sh2_ref, sc2_ref, n2g_ref, wo1_ref, wo2_ref,
                  rwt_ref, rb_ref, tri_ref, xn_ref, h2_ref, idx_ref, gate_ref, rank_ref, cnt_ref,
                  carry_sc, lg_sc):
    i = pl.program_id(0)

    @pl.when(i == 0)
    def _():
        carry_sc[...] = jnp.zeros_like(carry_sc)
        lg_sc[...] = jnp.zeros_like(lg_sc)

    logits = lg_sc[...]
    n_e, tt = logits.shape
    eidx = lax.broadcasted_iota(jnp.int32, (n_e, tt), 0).astype(F32)
    work = logits
    vals, sels, idxs = [], [], []
    for _ in range(TOP_K):
        m = jnp.max(work, axis=0, keepdims=True)
        ix = jnp.min(jnp.where(work == m, eidx, float(n_e)), axis=0, keepdims=True)
        sel = eidx == ix
        work = jnp.where(sel, -jnp.inf, work)
        vals.append(m)
        idxs.append(ix)
        sels.append(sel)
    es = [jnp.exp(vk - vals[0]) for vk in vals]
    den = es[0] + es[1] + es[2] + es[3]
    gate_ref[...] = jnp.concatenate(es, axis=0) / den
    idx_ref[...] = jnp.concatenate(idxs, axis=0).astype(jnp.int32)
    onehot = jnp.zeros((n_e, tt), F32)
    for sel in sels:
        onehot = onehot + sel.astype(F32)

    mix = (jnp.dot(yh_ref[...], wo1_ref[...], preferred_element_type=F32)
           + jnp.dot(ya_ref[...], wo2_ref[...], preferred_element_type=F32))
    xn = x_ref[...] + g1_ref[...] * mix
    xn_ref[...] = xn
    h2 = _rms(xn) * n2g_ref[...]
    h2 = h2 * (1.0 + sc2_ref[...]) + sh2_ref[...]
    _store_packed_rows(h2_ref, h2)
    lg_sc[...] = lax.dot_general(rwt_ref[...], h2, (((1,), (1,)), ((), ())), precision=HIGHEST,
                                 preferred_element_type=F32) + rb_ref[...]

    prefix = jnp.dot(onehot.astype(BF16), tri_ref[...], preferred_element_type=F32) + carry_sc[...]
    ranks = [jnp.sum(jnp.where(sel, prefix, 0.0), axis=0, keepdims=True) for sel in sels]
    rank_ref[...] = jnp.concatenate(ranks, axis=0).astype(jnp.int32)
    carry_sc[...] += jnp.where(i > 0, jnp.sum(onehot, axis=1, keepdims=True), 0.0)
    cnt_ref[...] = jnp.broadcast_to(carry_sc[...], cnt_ref.shape)


def _outproj_router(y_hy, y_att, x2, g1, sh2, sc2, norm2_g, w_out, router_w, router_b, tokens_per_batch,
                    tt):
    t, d = x2.shape
    ch = y_hy.shape[1]
    n_e = router_w.shape[1]
    wo1 = w_out[:ch].astype(BF16)
    wo2 = w_out[ch:].astype(BF16)
    tri = jnp.asarray(np.triu(np.ones((tt, tt), np.float32), k=1), BF16)
    steps_per_batch = tokens_per_batch // tt
    n_tiles = t // tt
    cur = lambda i: jnp.minimum(i, n_tiles - 1)
    tok = lambda i: (cur(i), 0)
    per_b = lambda i: (cur(i) // steps_per_batch, 0, 0)
    lanes_tok = lambda i: (0, jnp.maximum(i - 1, 0))
    return pl.pallas_call(
        _outproj_body,
        grid=(n_tiles + 1,),
        in_specs=[pl.BlockSpec((tt, ch), tok), pl.BlockSpec((tt, ch), tok),
                  pl.BlockSpec((tt, d), tok),
                  pl.BlockSpec((None, 1, d), per_b), pl.BlockSpec((None, 1, d), per_b),
                  pl.BlockSpec((None, 1, d), per_b),
                  _const_spec((1, d)), _const_spec(wo1.shape), _const_spec(wo2.shape),
                  _const_spec((n_e, d)), _const_spec((n_e, 1)), _const_spec((tt, tt))],
        out_specs=[pl.BlockSpec((tt, d), tok),
                   pl.BlockSpec((d // (2 * LANES), tt, LANES), lambda i: (0, cur(i), 0)),
                   pl.BlockSpec((TOP_K, tt), lanes_tok), pl.BlockSpec((TOP_K, tt), lanes_tok),
                   pl.BlockSpec((TOP_K, tt), lanes_tok), _const_spec((n_e, LANES))],
        out_shape=[jax.ShapeDtypeStruct((t, d), F32),
                   jax.ShapeDtypeStruct((d // (2 * LANES), t, LANES), jnp.uint32),
                   jax.ShapeDtypeStruct((TOP_K, t), jnp.int32),
                   jax.ShapeDtypeStruct((TOP_K, t), F32),
                   jax.ShapeDtypeStruct((TOP_K, t), jnp.int32),
                   jax.ShapeDtypeStruct((n_e, LANES), F32)],
        scratch_shapes=[pltpu.VMEM((n_e, 1), F32), pltpu.VMEM((n_e, tt), F32)],
        compiler_params=_cparams(("arbitrary",), VMEM_LIMIT),
        name="outproj_router",
    )(y_hy, y_att, x2, g1, sh2, sc2, norm2_g.reshape(1, d), wo1, wo2, router_w.T,
      router_b.reshape(n_e, 1), tri)


def _cast_rows(src_ref, dst_ref, chunk):
    def body(c, carry):
        sl = pl.ds(pl.multiple_of(c * chunk, chunk), chunk)
        dst_ref[sl, :] = src_ref[sl, :].astype(dst_ref.dtype)
        return carry

    lax.fori_loop(0, src_ref.shape[0] // chunk, body, 0)


def _expert_body(be_ref, nvalid_ref, xs_ref, wgu_ref, bgu_ref, wd_ref, bd_ref, ys_ref, wgu_bf, wd_bf):
    i = pl.program_id(0)
    n_valid = nvalid_ref[i]
    active = n_valid > 0
    new_expert = jnp.logical_or(i == 0, be_ref[i] != be_ref[jnp.maximum(i - 1, 0)])

    @pl.when(jnp.logical_and(active, new_expert))
    def _():
        _cast_rows(wgu_ref, wgu_bf, CAST_ROW_CHUNK)
        _cast_rows(wd_ref, wd_bf, CAST_ROW_CHUNK)

    @pl.when(active)
    def _():
        row = lax.broadcasted_iota(jnp.int32, (xs_ref.shape[1], 1), 0)
        xs = jnp.where(row < n_valid, _load_packed_rows(xs_ref), 0.0).astype(BF16)
        gu = jnp.dot(xs, wgu_bf[...], preferred_element_type=F32) + bgu_ref[...]
        dff = gu.shape[1] // 2
        gate = jnp.minimum(gu[:, :dff], SWIGLU_LIMIT)
        up = jnp.clip(gu[:, dff:], -SWIGLU_LIMIT, SWIGLU_LIMIT)
        act = (up + 1.0) * (gate * jax.nn.sigmoid(SWIGLU_ALPHA * gate))
        ys = jnp.dot(act.astype(BF16), wd_bf[...], preferred_element_type=F32) + bd_ref[...]
        _store_packed_rows(ys_ref, ys)

    @pl.when(jnp.logical_not(active))
    def _():
        ys_ref[...] = jnp.zeros_like(ys_ref)


def _expert_blocks(xs, block_e, n_valid, w_gu, b_gu, w_down, b_down):
    n_seg, n_rows, _ = xs.shape
    n_e, d, dff2 = w_gu.shape
    bm = MOE_ROWS
    seg_block = pl.BlockSpec((n_seg, bm, LANES), lambda i, be, nu: (0, i, 0))
    grid_spec = pltpu.PrefetchScalarGridSpec(
        num_scalar_prefetch=2,
        grid=(n_rows // bm,),
        in_specs=[seg_block,
                  pl.BlockSpec((None, d, dff2), lambda i, be, nu: (be[i], 0, 0)),
                  pl.BlockSpec((None, 1, dff2), lambda i, be, nu: (be[i], 0, 0)),
                  pl.BlockSpec((None, dff2 // 2, d), lambda i, be, nu: (be[i], 0, 0)),
                  pl.BlockSpec((None, 1, d), lambda i, be, nu: (be[i], 0, 0))],
        out_specs=seg_block,
        scratch_shapes=[pltpu.VMEM((d, dff2), BF16), pltpu.VMEM((dff2 // 2, d), BF16)],
    )
    return pl.pallas_call(
        _expert_body,
        grid_spec=grid_spec,
        out_shape=jax.ShapeDtypeStruct(xs.shape, jnp.uint32),
        compiler_params=_cparams(("arbitrary",), VMEM_LIMIT),
        name="moe_experts",
    )(block_e, n_valid, xs, w_gu, b_gu.reshape(n_e, 1, dff2), w_down, b_down.reshape(n_e, 1, d))


def _sc_gather(table, idx):
    n = idx.shape[0]
    width = table.shape[1]
    mesh = plsc.VectorSubcoreMesh(core_axis_name="core", subcore_axis_name="subcore")
    n_workers = mesh.num_cores * mesh.num_subcores
    assert width == LANES and n % (SC_WINDOW * n_workers) == 0

    @functools.partial(pl.kernel, out_type=jax.ShapeDtypeStruct((n, width), table.dtype), mesh=mesh)
    def gather_kernel(table_hbm, idx_hbm, out_hbm):
        def body(idx_vmem, out_vmem):
            pltpu.sync_copy(table_hbm.at[idx_vmem.at[0]], out_vmem)

        pltpu.emit_pipeline(
            body,
            grid=(n // SC_WINDOW,),
            in_specs=[pl.BlockSpec((1, SC_WINDOW), lambda i: (0, i))],
            out_specs=[pl.BlockSpec((SC_WINDOW, width), lambda i: (i, 0))],
            core_axis_name=("core", "subcore"),
            dimension_semantics=(pltpu.PARALLEL,),
        )(idx_hbm, out_hbm)

    return gather_kernel(table, idx.reshape(1, n))


def _sc_scatter(rows, idx, n_copies, n_out):
    n, width = rows.shape
    mesh = plsc.VectorSubcoreMesh(core_axis_name="core", subcore_axis_name="subcore")
    n_workers = mesh.num_cores * mesh.num_subcores
    assert width == LANES and n % (SC_WINDOW * n_workers) == 0 and idx.shape == (8, n)

    @functools.partial(pl.kernel, out_type=jax.ShapeDtypeStruct((n_out, width), rows.dtype), mesh=mesh)
    def scatter_kernel(rows_hbm, idx_hbm, out_hbm):
        def body(rows_vmem, idx_vmem):
            for k in range(n_copies):
                pltpu.sync_copy(rows_vmem, out_hbm.at[idx_vmem.at[k]])

        pltpu.emit_pipeline(
            body,
            grid=(n // SC_WINDOW,),
            in_specs=[pl.BlockSpec((SC_WINDOW, width), lambda i: (i, 0)),
                      pl.BlockSpec((8, SC_WINDOW), lambda i: (0, i))],
            out_specs=[],
            core_axis_name=("core", "subcore"),
            dimension_semantics=(pltpu.PARALLEL,),
        )(rows_hbm, idx_hbm)

    return scatter_kernel(rows, idx)


def _combine_body(pk_ref, gt_ref, xn_ref, g2_ref, o_ref):
    n_seg = pk_ref.shape[0]
    half = o_ref.shape[1] // 2
    gt = gt_ref[...]
    g2 = g2_ref[...]
    for j in range(n_seg):
        acc_hi = None
        acc_lo = None
        for kk in range(TOP_K):
            hi, lo = _unpack_words(pk_ref[j, kk])
            g = gt[:, kk:kk + 1]
            acc_hi = g * hi if acc_hi is None else acc_hi + g * hi
            acc_lo = g * lo if acc_lo is None else acc_lo + g * lo
        c_hi = slice(j * LANES, (j + 1) * LANES)
        c_lo = slice(half + j * LANES, half + (j + 1) * LANES)
        o_ref[:, c_hi] = xn_ref[:, c_hi] + g2[:, c_hi] * acc_hi
        o_ref[:, c_lo] = xn_ref[:, c_lo] + g2[:, c_lo] * acc_lo


def _combine(picked, gates_t, xn, g2, tokens_per_batch, tt):
    n_seg, _, t, _ = picked.shape
    d = xn.shape[1]
    steps_per_batch = tokens_per_batch // tt
    return pl.pallas_call(
        _combine_body,
        grid=(t // tt,),
        in_specs=[pl.BlockSpec((n_seg, TOP_K, tt, LANES), lambda i: (0, 0, i, 0)),
                  pl.BlockSpec((tt, TOP_K), lambda i: (i, 0)),
                  pl.BlockSpec((tt, d), lambda i: (i, 0)),
                  pl.BlockSpec((None, 1, d), lambda i: (i // steps_per_batch, 0, 0))],
        out_specs=pl.BlockSpec((tt, d), lambda i: (i, 0)),
        out_shape=jax.ShapeDtypeStruct((t, d), F32),
        compiler_params=_cparams(("parallel",), VMEM_LIMIT),
        name="moe_combine",
    )(picked, gates_t, xn, g2)


def _moe(h2p, xn, g2, idx, gates, ranks, counts, w_gu, b_gu, w_down, b_down, tokens_per_batch):
    n_seg, t, _ = h2p.shape
    bm = MOE_ROWS
    n_e = w_gu.shape[0]
    cnt = counts[:, 0].astype(jnp.int32)
    padded = (cnt + bm - 1) // bm * bm
    padded_ends = jnp.cumsum(padded)
    padded_starts = padded_ends - padded
    experts = jnp.arange(n_e, dtype=jnp.int32)[:, None, None]
    dest = ranks + jnp.sum(jnp.where(idx[None] == experts, padded_starts[:, None, None], 0), axis=0)
    n_blocks = t * TOP_K // bm + n_e
    n_rows = n_blocks * bm
    block_start = jnp.arange(n_blocks, dtype=jnp.int32) * bm
    block_e = jnp.minimum(jnp.sum(padded_ends[None, :] <= block_start[:, None], axis=1),
                          n_e - 1).astype(jnp.int32)
    n_valid = jnp.clip(cnt[block_e] - (block_start - padded_starts[block_e]), 0, bm).astype(jnp.int32)
    seg = jnp.arange(n_seg, dtype=jnp.int32)
    scatter_idx = (seg[None, :, None] * n_rows + dest[:, None, :]).reshape(TOP_K, n_seg * t)
    scatter_idx = jnp.concatenate([scatter_idx, scatter_idx], axis=0)
    xs = _sc_scatter(h2p.reshape(n_seg * t, LANES), scatter_idx, TOP_K, n_seg * n_rows)
    ys = _expert_blocks(xs.reshape(n_seg, n_rows, LANES), block_e, n_valid, w_gu, b_gu, w_down, b_down)
    picked = _sc_gather(ys.reshape(n_seg * n_rows, LANES),
                        (seg[:, None, None] * n_rows + dest[None]).reshape(-1))
    return _combine(picked.reshape(n_seg, TOP_K, t, LANES), gates.T, xn, g2, tokens_per_batch,
                    min(TOKEN_TILE, tokens_per_batch))


def kernel(x, c, ctx, c_ctx, mod_w, mod_b, norm1_g, w_in, hy_conv_w, hy_conv_b, hy_f_w1, hy_f_b1,
           hy_f_w2, hy_f_b2, hy_f_w3, hy_f_b3, hy_f_w4, hy_f_freq, hy_bias, mla_q_norm_g, mla_w_uq,
           mla_kv_norm_g, mla_w_ukv, qk_norm_q_g, qk_norm_k_g, w_out, norm2_g, router_w, router_b,
           exp_w_gu, exp_b_gu, exp_w_down, exp_b_down):
    b, length, d = x.shape
    depth = mod_w.shape[0]
    assert depth == 1, "single-layer kernel"
    ly = 0
    c_rows = jnp.concatenate([c, c_ctx[None, :], jnp.zeros((8 - b - 1, d), F32)], axis=0)
    mod = _adaln_table(c_rows, mod_w[ly], mod_b[ly])
    mod6 = mod.reshape(8, 6, d)
    sh1, sc1, g1, sh2, sc2, g2 = (mod6[:b, j][:, None, :] for j in range(6))
    csh1 = mod6[b:b + 1, 0][:, None, :]
    csc1 = mod6[b:b + 1, 1][:, None, :]

    weights = _mla_weights(w_in[ly], mla_w_uq[ly], mla_w_ukv[ly], qk_norm_q_g[ly], qk_norm_k_g[ly])
    n_ctx = ctx.shape[1]
    _, _, k_c, v_c = _inproj(ctx, jnp.broadcast_to(csh1, (b, 1, d)), jnp.broadcast_to(csc1, (b, 1, d)),
                             norm1_g[ly], weights, mla_q_norm_g[ly], mla_kv_norm_g[ly], False, n_ctx)
    hy, q, k, v = _inproj(x, sh1, sc1, norm1_g[ly], weights, mla_q_norm_g[ly], mla_kv_norm_g[ly],
                          True, min(TOKEN_TILE, length))

    kern, asum = _hyena_kernel_taps(length, hy_f_w1[ly], hy_f_b1[ly], hy_f_w2[ly], hy_f_b2[ly],
                                    hy_f_w3[ly], hy_f_b3[ly], hy_f_w4[ly], hy_f_freq[ly])
    kf = _hyena_filter_spectrum(kern, asum)
    y_hy = _hyena_conv(hy, hy_conv_w[ly], hy_conv_b[ly], hy_bias[ly], kf)

    y_att = _flash_attention(q, k, v, k_c, v_c, min(ATTN_Q_TILE, length), min(ATTN_K_TILE, length),
                             ATTN_ROW_GROUP)

    t = b * length
    xn, h2, idx, gates, ranks, counts = _outproj_router(
        y_hy.reshape(t, -1), y_att.reshape(t, -1), x.reshape(t, d), g1, sh2, sc2, norm2_g[ly],
        w_out[ly], router_w[ly], router_b[ly], length, min(TOKEN_TILE, length))
    out = _moe(h2, xn, g2, idx, gates, ranks, counts, exp_w_gu[ly], exp_b_gu[ly],
               exp_w_down[ly], exp_b_down[ly], length)
    return out.reshape(b, length, d)
```

```python
import functools
import math

import jax
import jax.numpy as jnp
import numpy as np
from jax import lax
from jax.experimental import pallas as pl
from jax.experimental.pallas import tpu as pltpu
from jax.experimental.pallas import tpu_sc as plsc

F32 = jnp.float32
BF16 = jnp.bfloat16
HIGHEST = lax.Precision.HIGHEST

GRID_W = 64
D_HYENA = 512
FILTER_ORDER = 64
POS_EMB_DIM = 33
MIN_DECAY = math.log(1e-2) / 0.3
MAX_DECAY = math.log(1e-2) / 1.5
NOPE_DIM = 128
ROPE_DIM = 64
QK_DIM = NOPE_DIM + ROPE_DIM
V_DIM = 128
MLA_HEADS = 4
Q_RANK = 256
KV_RANK = 128
ROPE_THETA = 10000.0
N_EXPERTS = 32
TOP_K = 4
SWIGLU_ALPHA = 1.702
SWIGLU_LIMIT = 7.0
NORM_EPS = 1e-6

LANES = 128
VMEM_LIMIT = 56 * 1024 * 1024

BF16_PACK_ROWS = 16

DFT_Q = LANES
MOE_ROWS = 512
SC_WINDOW = 128

TOKEN_TILE = 512
ATTN_Q_TILE = 512
ATTN_K_TILE = 512
ATTN_ROW_GROUP = 16
FILTER_TILE = 1024
CONV_ROW_CHUNK = 512
CAST_ROW_CHUNK = 128


def _cparams(sem, vmem=None):
    return pltpu.CompilerParams(dimension_semantics=sem, vmem_limit_bytes=vmem)


def _const_spec(shape):
    nd = len(shape)
    return pl.BlockSpec(shape, lambda *_: (0,) * nd)


def _single_spec(shape, index_map):
    return pl.BlockSpec(shape, index_map, pipeline_mode=pl.Buffered(1))


def _mod_body(c_ref, w_ref, b_ref, o_ref):
    cc = c_ref[...]
    s = cc * jax.nn.sigmoid(cc)
    o_ref[...] = jnp.dot(s, w_ref[...], precision=HIGHEST,
                         preferred_element_type=F32) + b_ref[...]


def _adaln_table(c_rows, mod_w, mod_b):
    rows, d = c_rows.shape
    n = mod_w.shape[1]
    tn = n // 8
    return pl.pallas_call(
        _mod_body,
        grid=(n // tn,),
        in_specs=[_const_spec((rows, d)),
                  pl.BlockSpec((d, tn), lambda j: (0, j)),
                  pl.BlockSpec((1, tn), lambda j: (0, j))],
        out_specs=pl.BlockSpec((rows, tn), lambda j: (0, j)),
        out_shape=jax.ShapeDtypeStruct((rows, n), F32),
        compiler_params=_cparams(("arbitrary",)),
        name="adaln_table",
    )(c_rows, mod_w, mod_b.reshape(1, n))


def _rms(x, eps=NORM_EPS):
    return x * lax.rsqrt(jnp.mean(x * x, axis=-1, keepdims=True) + eps)


def _inproj_body(x_ref, sh_ref, sc_ref, g_ref, why_ref, wmla_ref, qng_ref, wuq_ref,
                 kvng_ref, wukv_ref, ct_ref, st_ref, ctk_ref, gq1_ref, gq2_ref, gkn_ref,
                 gkr_ref, hy_ref, q_ref, k_ref, v_ref, mla_sc, *, q_scale):
    i = pl.program_id(0)

    @pl.when(i == 0)
    def _():
        mla_sc[...] = jnp.zeros_like(mla_sc)

    mla = mla_sc[...]
    cq = mla[:, :Q_RANK]
    ckv = mla[:, Q_RANK:Q_RANK + KV_RANK]
    pe2 = mla[:, Q_RANK + KV_RANK:]
    qf = jnp.dot((_rms(cq) * qng_ref[...]).astype(BF16), wuq_ref[...],
                 preferred_element_type=F32)
    kvf = jnp.dot((_rms(ckv) * kvng_ref[...]).astype(BF16), wukv_ref[...],
                  preferred_element_type=F32)
    ct = ct_ref[...]
    st = st_ref[...]
    lane256 = lax.broadcasted_iota(jnp.int32, (1, 2 * LANES), 1)
    qmask = (lane256 < QK_DIM).astype(F32)
    lane128 = lax.broadcasted_iota(jnp.int32, (1, LANES), 1)
    pemask = (lane128 < ROPE_DIM).astype(F32)
    kr0 = pe2 * gkr_ref[...] * ctk_ref[...]
    krs = kr0 + pltpu.roll(kr0, ROPE_DIM, axis=1)
    pem = pe2 * pemask
    ss_pe = jnp.sum(pem * pem, axis=-1, keepdims=True)
    gq1 = gq1_ref[...]
    gq2 = gq2_ref[...]
    gkn = gkn_ref[...]
    for hd in range(MLA_HEADS):
        slab = qf[:, hd * 2 * LANES:(hd + 1) * 2 * LANES]
        sm = slab * qmask
        rq = lax.rsqrt(jnp.sum(sm * sm, axis=-1, keepdims=True) / QK_DIM + NORM_EPS) * q_scale
        t = slab * gq1 * ct + pltpu.roll(slab * gq2 * st, QK_DIM, axis=1)
        q_ref[hd] = (t * rq)[:, :QK_DIM].astype(BF16)
        kn = kvf[:, hd * 2 * LANES:hd * 2 * LANES + NOPE_DIM]
        rk = lax.rsqrt((jnp.sum(kn * kn, axis=-1, keepdims=True) + ss_pe) / QK_DIM + NORM_EPS)
        kslab = jnp.concatenate([kn * gkn, krs], axis=-1) * rk
        k_ref[hd] = kslab[:, :QK_DIM].astype(BF16)
        v_ref[hd] = kvf[:, hd * 2 * LANES + NOPE_DIM:(hd + 1) * 2 * LANES].astype(BF16)

    h = _rms(x_ref[...]) * g_ref[...]
    h = h * (1.0 + sc_ref[...]) + sh_ref[...]
    hb = h.astype(BF16)
    hy_ref[...] = jnp.dot(hb, why_ref[...], preferred_element_type=F32).astype(BF16)
    mla_sc[...] = jnp.dot(hb, wmla_ref[...], preferred_element_type=F32)


@functools.lru_cache(maxsize=None)
def _rope_lane_tables(length, use_rope):
    if use_rope:
        n_freq = ROPE_DIM // 4
        t = np.arange(length)
        inv_freq = np.power(ROPE_THETA, -np.arange(n_freq, dtype=np.float64) / n_freq)
        ar = (t // GRID_W).astype(np.float64)[:, None] * inv_freq
        ac = (t % GRID_W).astype(np.float64)[:, None] * inv_freq
        c64 = np.concatenate([np.cos(ar), np.cos(ar), np.cos(ac), np.cos(ac)], axis=-1)
        s64 = np.concatenate([-np.sin(ar), np.sin(ar), -np.sin(ac), np.sin(ac)], axis=-1)
    else:
        c64 = np.ones((length, ROPE_DIM))
        s64 = np.zeros((length, ROPE_DIM))
    z64 = np.zeros((length, ROPE_DIM))
    ct = np.concatenate([np.ones((length, LANES)), c64, z64], axis=-1)
    st = np.concatenate([np.zeros((length, LANES)), z64, s64], axis=-1)
    ctk = np.concatenate([c64, s64], axis=-1)
    return ct.astype(np.float32), st.astype(np.float32), ctk.astype(np.float32)


_SWAP16 = np.concatenate([np.arange(16, 32), np.arange(0, 16), np.arange(48, 64), np.arange(32, 48)])


def _mla_weights(w_in, mla_w_uq, mla_w_ukv, qk_norm_q_g, qk_norm_k_g):
    hy_cols = 3 * D_HYENA
    w_hy = w_in[:, :hy_cols].astype(BF16)
    w_pe = w_in[:, hy_cols + Q_RANK + KV_RANK:]
    w_mla = jnp.concatenate([w_in[:, hy_cols:hy_cols + Q_RANK + KV_RANK], w_pe, w_pe[:, _SWAP16]],
                            axis=-1).astype(BF16)
    wq = mla_w_uq.reshape(Q_RANK, MLA_HEADS, QK_DIM)
    wq_rope = wq[:, :, NOPE_DIM:]
    w_uq2 = jnp.concatenate([wq[:, :, :NOPE_DIM], wq_rope, wq_rope[:, :, _SWAP16]], axis=-1)
    w_uq2 = w_uq2.reshape(Q_RANK, MLA_HEADS * 2 * LANES).astype(BF16)
    w_ukv2 = mla_w_ukv.astype(BF16)
    gq_r = qk_norm_q_g[NOPE_DIM:]
    z64 = jnp.zeros((ROPE_DIM,), F32)
    gq1 = jnp.concatenate([qk_norm_q_g[:NOPE_DIM], gq_r, z64]).reshape(1, -1)
    gq2 = jnp.concatenate([jnp.zeros((NOPE_DIM,), F32), z64, gq_r[_SWAP16]]).reshape(1, -1)
    gkn = qk_norm_k_g[:NOPE_DIM].reshape(1, -1)
    gk_r = qk_norm_k_g[NOPE_DIM:]
    gkr = jnp.concatenate([gk_r, gk_r[_SWAP16]]).reshape(1, -1)
    return w_hy, w_mla, w_uq2, w_ukv2, gq1, gq2, gkn, gkr


def _inproj(x, shift, scale, norm_g, weights, q_norm_g, kv_norm_g, use_rope, tl):
    b, length, d = x.shape
    w_hy, w_mla, w_uq2, w_ukv2, gq1, gq2, gkn, gkr = weights
    ct, st, ctk = _rope_lane_tables(length, use_rope)
    q_scale = QK_DIM ** -0.5 * math.log2(math.e)
    nt = length // tl
    n_tiles = b * nt
    hyc = w_hy.shape[1]
    cur = lambda i: jnp.minimum(i, n_tiles - 1)
    prev = lambda i: jnp.maximum(i - 1, 0)
    tok = lambda i: (cur(i), 0)
    per_b = lambda i: (cur(i) // nt, 0, 0)
    pos = lambda i: (prev(i) % nt, 0)
    head_blk = lambda i: (prev(i) // nt, 0, prev(i) % nt, 0)
    hy, q, k, v = pl.pallas_call(
        functools.partial(_inproj_body, q_scale=q_scale),
        grid=(n_tiles + 1,),
        in_specs=[pl.BlockSpec((tl, d), tok),
                  pl.BlockSpec((None, 1, d), per_b),
                  pl.BlockSpec((None, 1, d), per_b),
                  _const_spec((1, d)),
                  _const_spec(w_hy.shape), _const_spec(w_mla.shape),
                  _const_spec((1, Q_RANK)), _const_spec(w_uq2.shape),
                  _const_spec((1, KV_RANK)), _const_spec(w_ukv2.shape),
                  pl.BlockSpec((tl, 2 * LANES), pos), pl.BlockSpec((tl, 2 * LANES), pos),
                  pl.BlockSpec((tl, LANES), pos),
                  _const_spec((1, 2 * LANES)), _const_spec((1, 2 * LANES)),
                  _const_spec((1, LANES)), _const_spec((1, LANES))],
        out_specs=[pl.BlockSpec((tl, hyc), tok),
                   pl.BlockSpec((None, MLA_HEADS, tl, QK_DIM), head_blk),
                   pl.BlockSpec((None, MLA_HEADS, tl, QK_DIM), head_blk),
                   pl.BlockSpec((None, MLA_HEADS, tl, V_DIM), head_blk)],
        out_shape=[jax.ShapeDtypeStruct((b * length, hyc), BF16),
                   jax.ShapeDtypeStruct((b, MLA_HEADS, length, QK_DIM), BF16),
                   jax.ShapeDtypeStruct((b, MLA_HEADS, length, QK_DIM), BF16),
                   jax.ShapeDtypeStruct((b, MLA_HEADS, length, V_DIM), BF16)],
        scratch_shapes=[pltpu.VMEM((tl, w_mla.shape[1]), F32)],
        compiler_params=_cparams(("arbitrary",), VMEM_LIMIT),
        name="inproj_mla",
    )(x.reshape(b * length, d), shift, scale, norm_g.reshape(1, d), w_hy, w_mla,
      q_norm_g.reshape(1, -1), w_uq2, kv_norm_g.reshape(1, -1), w_ukv2, ct, st, ctk, gq1, gq2, gkn, gkr)
    return hy.reshape(b, length, hyc), q, k, v


def _filter_body(z_ref, w1_ref, b1_ref, w2_ref, b2_ref, w3_ref, b3_ref, w4_ref, fr_ref, dl_ref,
                 kern_ref, asum_ref, *, zero_row, tr):
    i = pl.program_id(0)
    fr = fr_ref[...]
    hr = tr // 2
    za = z_ref[:hr, :]
    zb = z_ref[hr:, :]
    dot = functools.partial(jnp.dot, precision=HIGHEST, preferred_element_type=F32)
    h = jnp.sin(fr * (dot(jnp.concatenate([za, zb], axis=1), w1_ref[...]) + b1_ref[...]))
    h = jnp.sin(fr * (dot(h, w2_ref[...]) + b2_ref[...]))
    h = jnp.sin(fr * (dot(h, w3_ref[...]) + b3_ref[...]))
    o2 = dot(h, w4_ref[...])
    nc = o2.shape[1] // 2
    oa = o2[:, :nc] * jnp.exp(-za[:, 0:1] * dl_ref[...])
    ob = o2[:, nc:] * jnp.exp(-zb[:, 0:1] * dl_ref[...])

    @pl.when(i == 0)
    def _():
        asum_ref[...] = jnp.zeros_like(asum_ref)

    asum_ref[...] += (jnp.sum(jnp.abs(oa), axis=0, keepdims=True)
                      + jnp.sum(jnp.abs(ob), axis=0, keepdims=True))
    row = i * tr + lax.broadcasted_iota(jnp.int32, (hr, 1), 0)
    kern_ref[:hr, :] = jnp.where(row == zero_row, 0.0, oa)
    kern_ref[hr:, :] = jnp.where(row + hr == zero_row, 0.0, ob)


@functools.lru_cache(maxsize=None)
def _filter_features(length):
    n = 2 * length
    bands = (POS_EMB_DIM - 1) // 2
    pos = np.concatenate([np.arange(length), (n - np.arange(length, n)) % length])
    t_tab = np.linspace(0.0, 1.0, length)[:, None]
    w_ang = 2.0 * np.pi * np.arange(length, dtype=np.float64)[:, None] / length
    f = np.linspace(1e-4, bands - 1, bands)[None, :]
    z_tab = np.concatenate([t_tab, np.cos(f * w_ang), -np.sin(f * w_ang)], axis=-1)
    z = np.pad(z_tab[pos], ((0, 0), (0, LANES - POS_EMB_DIM)))
    return z.astype(np.float32)


def _hyena_kernel_taps(length, w1, b1, w2, b2, w3, b3, w4, freq):
    n = 2 * length
    z = jnp.asarray(_filter_features(length))
    deltas = jnp.abs(jnp.linspace(MIN_DECAY, MAX_DECAY, D_HYENA, dtype=F32)).reshape(1, -1)
    tr = min(FILTER_TILE, length)
    half_steps = length // tr
    fo = FILTER_ORDER

    def pair(w):
        zeros = jnp.zeros_like(w)
        return jnp.concatenate([jnp.concatenate([w, zeros], axis=1),
                                jnp.concatenate([zeros, w], axis=1)], axis=0)

    def twice(v):
        return jnp.concatenate([v, v]).reshape(1, 2 * fo)

    w1p = pair(jnp.pad(w1, ((0, LANES - POS_EMB_DIM), (0, 0))))
    w4p = jnp.stack([pair(w4[:, :D_HYENA]), pair(w4[:, D_HYENA:])])
    kern, asum = pl.pallas_call(
        functools.partial(_filter_body, zero_row=length, tr=tr),
        grid=(n // tr,),
        in_specs=[pl.BlockSpec((tr, LANES), lambda i: (i, 0)),
                  _const_spec((2 * LANES, 2 * fo)), _const_spec((1, 2 * fo)),
                  _const_spec((2 * fo, 2 * fo)), _const_spec((1, 2 * fo)),
                  _const_spec((2 * fo, 2 * fo)), _const_spec((1, 2 * fo)),
                  pl.BlockSpec((None, 2 * fo, 2 * D_HYENA), lambda i: (i // half_steps, 0, 0)),
                  _const_spec((1, 2 * fo)), _const_spec((1, D_HYENA))],
        out_specs=[pl.BlockSpec((tr, D_HYENA), lambda i: (i, 0)),
                   _const_spec((1, D_HYENA))],
        out_shape=[jax.ShapeDtypeStruct((n, D_HYENA), F32),
                   jax.ShapeDtypeStruct((1, D_HYENA), F32)],
        compiler_params=_cparams(("arbitrary",), VMEM_LIMIT),
        name="hyena_filter",
    )(z, w1p, twice(b1), pair(w2), twice(b2), pair(w3), twice(b3), w4p, twice(freq), deltas)
    return kern, asum


def _half_rows(n):
    n_half = n // DFT_Q // 2 + 1
    return n_half, -(-n_half // 8) * 8


@functools.lru_cache(maxsize=None)
def _dft_tables(n, p_in):
    q_sz = DFT_Q
    p_sz = n // q_sz
    n_half, n_r = _half_rows(n)
    r = np.arange(n_r, dtype=np.float64)
    keep = (r < n_half).astype(np.float64)
    qq = np.arange(q_sz, dtype=np.float64)
    pp = np.arange(p_in, dtype=np.float64)
    tt = q_sz * pp[None, None, :] + qq[:, None, None]
    ang = -2.0 * np.pi * r[None, :, None] * tt / n
    t1 = np.concatenate([np.cos(ang), np.sin(ang)], axis=1) * np.tile(keep, 2)[None, :, None]
    mirror = np.where((r == 0) | (r == p_sz // 2), 1.0, 2.0)
    t4 = np.transpose(t1 * np.tile(mirror, 2)[None, :, None], (0, 2, 1)) / n
    a2 = -2.0 * np.pi * np.outer(qq, qq) / q_sz
    fre, fim = np.cos(a2), np.sin(a2)
    m2 = np.block([[fre, -fim], [fim, fre]])
    m2c = np.block([[fre, fim], [-fim, fre]])
    t1 = np.concatenate([t1[0::2], t1[1::2]], axis=2)
    return (t1.astype(np.float32), t4.astype(np.float32), m2.astype(np.float32),
            m2c.astype(np.float32))


SUBLANES = 8


def _spec_block(rg, h):
    return (rg * 2 + h) * SUBLANES * DFT_Q


def _dft_stage1(x_ref, t1_ref, spec_ref, n_r, p_in):
    q_sz = DFT_Q
    ct = spec_ref.shape[1]

    def body(j, carry):
        q0 = 2 * j
        x0 = x_ref[pl.ds(q0, p_in, stride=q_sz), :].astype(BF16)
        x1 = x_ref[pl.ds(q0 + 1, p_in, stride=q_sz), :].astype(BF16)
        zeros = jnp.zeros_like(x0)
        rhs = jnp.concatenate([jnp.concatenate([x0, zeros], axis=1),
                               jnp.concatenate([zeros, x1], axis=1)], axis=0)
        a = jnp.dot(t1_ref[j], rhs, preferred_element_type=F32)
        for k in range(2):
            row = pl.multiple_of((q0 + k) * SUBLANES, SUBLANES)
            for h in range(2):
                for rg in range(n_r // SUBLANES):
                    src = h * n_r + rg * SUBLANES
                    spec_ref[pl.ds(_spec_block(rg, h) + row, SUBLANES), :] = (
                        a[src:src + SUBLANES, k * ct:(k + 1) * ct])
        return carry

    lax.fori_loop(0, q_sz // 2, body, 0, unroll=4)


def _spec_rows(rg, h, r8):
    return pl.ds(_spec_block(rg, h) + r8, DFT_Q, stride=SUBLANES)


def _spectrum_body(kern_ref, asum_ref, t1_ref, m2_ref, kf_ref, spec_ref, *, n_r, p_in):
    _dft_stage1(kern_ref, t1_ref, spec_ref, n_r, p_in)
    inv = 1.0 / asum_ref[...]
    m2 = m2_ref[...]

    ct = spec_ref.shape[1]

    def body(rg, carry):
        for r8 in range(0, SUBLANES, 2):
            blk = jnp.concatenate(
                [jnp.concatenate([spec_ref[_spec_rows(rg, 0, r8 + j), :],
                                  spec_ref[_spec_rows(rg, 1, r8 + j), :]], axis=0) for j in range(2)],
                axis=1)
            xf = jnp.dot(m2, blk.astype(BF16), preferred_element_type=F32)
            for j in range(2):
                kf_ref[rg * SUBLANES + r8 + j] = (xf[:, j * ct:(j + 1) * ct] * inv).astype(BF16)
        return carry

    lax.fori_loop(0, n_r // SUBLANES, body, 0)


def _hyena_filter_spectrum(kern, asum):
    n, c = kern.shape
    p_sz = n // DFT_Q
    _, n_r = _half_rows(n)
    t1, _, m2, _ = _dft_tables(n, p_sz)
    t1 = jnp.asarray(t1).astype(BF16)
    m2 = jnp.asarray(m2).astype(BF16)
    ct = LANES
    return pl.pallas_call(
        functools.partial(_spectrum_body, n_r=n_r, p_in=p_sz),
        grid=(c // ct,),
        in_specs=[_single_spec((n, ct), lambda j: (0, j)),
                  pl.BlockSpec((1, ct), lambda j: (0, j)),
                  _single_spec(t1.shape, lambda j: (0, 0, 0)),
                  _const_spec(m2.shape)],
        out_specs=pl.BlockSpec((n_r, 2 * DFT_Q, ct), lambda j: (0, 0, j)),
        out_shape=jax.ShapeDtypeStruct((n_r, 2 * DFT_Q, c), BF16),
        scratch_shapes=[pltpu.VMEM((n_r * 2 * DFT_Q, ct), F32)],
        compiler_params=_cparams(("parallel",), VMEM_LIMIT),
        name="hyena_filter_spectrum",
    )(kern, asum, t1, m2)


def _short_conv_chunk(u_ref, w_ref, b_ref, i, rows, length):
    pack = BF16_PACK_ROWS
    base = pl.multiple_of(i * rows, rows)
    u = u_ref[pl.ds(base, rows), :].astype(F32)
    lo = pl.multiple_of(jnp.maximum(base - pack, 0), pack)
    hi = pl.multiple_of(jnp.minimum(base + rows, length - pack), pack)
    prev = u_ref[pl.ds(lo, pack), :].astype(F32)[pack - 1:pack]
    nxt = u_ref[pl.ds(hi, pack), :].astype(F32)[0:1]
    prev = jnp.where(base == 0, 0.0, prev)
    nxt = jnp.where(base + rows == length, 0.0, nxt)
    ridx = lax.broadcasted_iota(jnp.int32, (rows, 1), 0)
    up = jnp.where(ridx == 0, prev, pltpu.roll(u, 1, axis=0))
    dn = jnp.where(ridx == rows - 1, nxt, pltpu.roll(u, rows - 1, axis=0))
    w = w_ref[...]
    return up * w[0:1] + u * w[1:2] + dn * w[2:3] + b_ref[...]


def _hyena_conv_body(x0_ref, x1_ref, v_ref, w0_ref, w1_ref, wv_ref, b0_ref, b1_ref, bv_ref,
                     bias_ref, kf_ref, t1_ref, t4_ref, m2_ref, m2c_ref, o_ref, vx_ref, spec_ref,
                     *, length, rows):
    q_sz = DFT_Q
    n_half, n_r = _half_rows(2 * length)
    p_in = length // q_sz
    n_chunks = length // rows

    def gate_in(i, carry):
        x1c = _short_conv_chunk(x1_ref, w1_ref, b1_ref, i, rows, length)
        vc = _short_conv_chunk(v_ref, wv_ref, bv_ref, i, rows, length)
        vx_ref[pl.ds(pl.multiple_of(i * rows, rows), rows), :] = vc * x1c
        return carry

    lax.fori_loop(0, n_chunks, gate_in, 0)
    _dft_stage1(vx_ref, t1_ref, spec_ref, n_r, p_in)
    m2 = m2_ref[...]
    m2c = m2c_ref[...]

    ct = spec_ref.shape[1]

    def forward_rows(rg, r8s):
        blk = jnp.concatenate(
            [jnp.concatenate([spec_ref[_spec_rows(rg, 0, r8), :], spec_ref[_spec_rows(rg, 1, r8), :]],
                             axis=0) for r8 in r8s], axis=1)
        return jnp.dot(m2, blk.astype(BF16), preferred_element_type=F32)

    def filter_inverse_rows(rg, r8s, xf):
        kf = jnp.concatenate([kf_ref[rg * SUBLANES + r8] for r8 in r8s], axis=1).astype(F32)
        xre, xim = xf[:q_sz], xf[q_sz:]
        kre, kim = kf[:q_sz], kf[q_sz:]
        z = jnp.concatenate([xre * kre - xim * kim, xre * kim + xim * kre], axis=0)
        bf = jnp.dot(m2c, z.astype(BF16), preferred_element_type=F32)
        for j, r8 in enumerate(r8s):
            spec_ref[_spec_rows(rg, 0, r8), :] = bf[:q_sz, j * ct:(j + 1) * ct]
            spec_ref[_spec_rows(rg, 1, r8), :] = bf[q_sz:, j * ct:(j + 1) * ct]

    def mid_rows(rg, n_rows):
        groups = [tuple(range(a, min(a + 2, n_rows))) for a in range(0, n_rows, 2)]
        pending = None
        for r8s in groups:
            xf = forward_rows(rg, r8s)
            if pending is not None:
                filter_inverse_rows(rg, *pending)
            pending = (r8s, xf)
        filter_inverse_rows(rg, *pending)

    def mid(rg, carry):
        mid_rows(rg, SUBLANES)
        return carry

    lax.fori_loop(0, n_half // SUBLANES, mid, 0)
    if n_half % SUBLANES:
        mid_rows(n_half // SUBLANES, n_half % SUBLANES)
    bias = bias_ref[...]

    def last(q, carry):
        row = pl.multiple_of(q * SUBLANES, SUBLANES)
        tiles = [spec_ref[pl.ds(_spec_block(rg, h) + row, SUBLANES), :]
                 for h in range(2) for rg in range(n_r // SUBLANES)]
        bq = jnp.concatenate(tiles, axis=0).astype(BF16)
        y = jnp.dot(t4_ref[q], bq, preferred_element_type=F32)
        sl = pl.ds(q, p_in, stride=q_sz)
        vx_ref[sl, :] = y + bias * vx_ref[sl, :]
        return carry

    lax.fori_loop(0, q_sz, last, 0, unroll=8)

    def gate_out(i, carry):
        x0c = _short_conv_chunk(x0_ref, w0_ref, b0_ref, i, rows, length)
        sl = pl.ds(pl.multiple_of(i * rows, rows), rows)
        o_ref[sl, :] = (vx_ref[sl, :] * x0c).astype(BF16)
        return carry

    lax.fori_loop(0, n_chunks, gate_out, 0)


def _hyena_conv(hy, conv_w, conv_b, bias, kf):
    b, length, _ = hy.shape
    ch = D_HYENA
    ct = LANES
    nct = ch // ct
    n = 2 * length
    _, n_r = _half_rows(n)
    p_in = length // DFT_Q
    t1, t4, m2, m2c = _dft_tables(n, p_in)
    t1, t4, m2, m2c = (jnp.asarray(a).astype(BF16) for a in (t1, t4, m2, m2c))
    rows = min(CONV_ROW_CHUNK, length)
    col = lambda off: (lambda j, bi: (bi, 0, off * nct + j))
    wcol = lambda off: (lambda j, bi: (0, off * nct + j))
    conv_b2 = conv_b.reshape(1, -1)
    return pl.pallas_call(
        functools.partial(_hyena_conv_body, length=length, rows=rows),
        grid=(nct, b),
        in_specs=[_single_spec((None, length, ct), col(0)),
                  _single_spec((None, length, ct), col(1)),
                  _single_spec((None, length, ct), col(2)),
                  pl.BlockSpec((3, ct), wcol(0)), pl.BlockSpec((3, ct), wcol(1)),
                  pl.BlockSpec((3, ct), wcol(2)),
                  pl.BlockSpec((1, ct), wcol(0)), pl.BlockSpec((1, ct), wcol(1)),
                  pl.BlockSpec((1, ct), wcol(2)),
                  pl.BlockSpec((1, ct), lambda j, bi: (0, j)),
                  _single_spec((n_r, 2 * DFT_Q, ct), lambda j, bi: (0, 0, j)),
                  _single_spec(t1.shape, lambda j, bi: (0, 0, 0)),
                  _single_spec(t4.shape, lambda j, bi: (0, 0, 0)),
                  _const_spec(m2.shape), _const_spec(m2c.shape)],
        out_specs=pl.BlockSpec((None, length, ct), lambda j, bi: (bi, 0, j)),
        out_shape=jax.ShapeDtypeStruct((b, length, ch), BF16),
        scratch_shapes=[pltpu.VMEM((length, ct), F32),
                        pltpu.VMEM((n_r * 2 * DFT_Q, ct), F32)],
        compiler_params=_cparams(("parallel", "parallel"), VMEM_LIMIT),
        name="hyena_conv",
    )(hy, hy, hy, conv_w, conv_w, conv_w, conv_b2, conv_b2, conv_b2, bias.reshape(1, ch), kf,
      t1, t4, m2, m2c)


def _flash_body(q_ref, k_ref, v_ref, kc_ref, vc_ref, o_ref, s_ref, p_ref, al_ref, m_ref, acc_ref,
                *, tk, rg):
    tq = q_ref.shape[0]
    nk = k_ref.shape[0] // tk
    nc = kc_ref.shape[0]
    n_chunks = nk + 1
    nt = (((1,), (1,)), ((), ()))

    def width(c):
        return nc if c == nk else tk

    def scores(c):
        keys = kc_ref[...] if c == nk else k_ref[c * tk:(c + 1) * tk, :]
        s_ref[c % 3, :, :width(c)] = lax.dot_general(q_ref[...], keys, nt, preferred_element_type=F32)

    def weighted_values(c):
        vals = vc_ref[...] if c == nk else v_ref[c * tk:(c + 1) * tk, :]
        lane = lax.broadcasted_iota(jnp.int32, vals.shape, 1)
        ones_col = jnp.where(lane == 0, 1.0, 0.0).astype(BF16)
        ext = jnp.concatenate([vals, ones_col], axis=1)
        return jnp.dot(p_ref[c % 2, :, :width(c)], ext, preferred_element_type=F32)

    def softmax(c):
        w = width(c)
        for g in range(tq // rg):
            rows = slice(g * rg, (g + 1) * rg)
            s = s_ref[c % 3, rows, :w]
            m_prev = m_ref[rows, :]
            m_new = jnp.maximum(m_prev, jnp.max(s, axis=-1, keepdims=True))
            al_ref[rows, :] = jnp.exp2(m_prev - m_new)
            m_ref[rows, :] = m_new
            d = (s - jnp.concatenate([m_new] * (w // LANES), axis=1)).astype(BF16)
            p_ref[c % 2, rows, :w] = jnp.exp2(d)

    def rescale(prod):
        alpha = al_ref[...]
        acc_ref[...] = jnp.concatenate([alpha, alpha], axis=1) * (acc_ref[...] + prod)

    m_ref[...] = jnp.full_like(m_ref, -jnp.inf)
    acc_ref[...] = jnp.zeros_like(acc_ref)
    scores(0)
    for c in range(n_chunks):
        prod = weighted_values(c - 1) if c >= 1 else None
        if c + 1 < n_chunks:
            scores(c + 1)
        softmax(c)
        if prod is not None:
            rescale(prod)
    total = acc_ref[...] + weighted_values(n_chunks - 1)
    o_ref[...] = (total[:, :V_DIM] / total[:, V_DIM:V_DIM + 1]).astype(o_ref.dtype)


def _flash_attention(q, k, v, kc, vc, tq, tk, rg):
    b, hds, length, _ = q.shape
    nc = kc.shape[2]
    assert length % tk == 0 and nc <= tk and nc % LANES == 0 and V_DIM == LANES
    return pl.pallas_call(
        functools.partial(_flash_body, tk=tk, rg=rg),
        grid=(b, hds, length // tq),
        in_specs=[pl.BlockSpec((None, None, tq, QK_DIM), lambda bi, h, qi: (bi, h, qi, 0)),
                  pl.BlockSpec((None, None, length, QK_DIM), lambda bi, h, qi: (bi, h, 0, 0)),
                  pl.BlockSpec((None, None, length, V_DIM), lambda bi, h, qi: (bi, h, 0, 0)),
                  pl.BlockSpec((None, None, nc, QK_DIM), lambda bi, h, qi: (bi, h, 0, 0)),
                  pl.BlockSpec((None, None, nc, V_DIM), lambda bi, h, qi: (bi, h, 0, 0))],
        out_specs=pl.BlockSpec((None, tq, V_DIM), lambda bi, h, qi: (bi, qi, h)),
        out_shape=jax.ShapeDtypeStruct((b, length, hds * V_DIM), BF16),
        scratch_shapes=[pltpu.VMEM((3, tq, tk), F32), pltpu.VMEM((2, tq, tk), BF16),
                        pltpu.VMEM((tq, LANES), F32), pltpu.VMEM((tq, LANES), F32),
                        pltpu.VMEM((tq, 2 * V_DIM), F32)],
        compiler_params=_cparams(("parallel", "parallel", "parallel"), VMEM_LIMIT),
        name="mla_flash_attention",
    )(q, k, v, kc, vc)


def _store_packed_rows(dst_ref, x):
    half = x.shape[1] // 2
    for j in range(half // LANES):
        hi = x[:, j * LANES:(j + 1) * LANES].astype(BF16).astype(F32)
        lo = x[:, half + j * LANES:half + (j + 1) * LANES].astype(BF16).astype(F32)
        dst_ref[j] = (lax.bitcast_convert_type(hi, jnp.uint32)
                      | (lax.bitcast_convert_type(lo, jnp.uint32) >> 16))


def _unpack_words(w):
    hi = lax.bitcast_convert_type(w & jnp.uint32(0xFFFF0000), F32)
    lo = lax.bitcast_convert_type(w << 16, F32)
    return hi, lo


def _load_packed_rows(src_ref):
    parts = [_unpack_words(src_ref[j]) for j in range(src_ref.shape[0])]
    return jnp.concatenate([p[0] for p in parts] + [p[1] for p in parts], axis=-1)


def _outproj_body(yh_ref, ya_ref, x_ref, g1_ref, sh2_ref, sc2_ref, n2g_ref, wo1_ref, wo2_ref,
                  rwt_ref, rb_ref, tri_ref, xn_ref, h2_ref, idx_ref, gate_ref, rank_ref, cnt_ref,
                  carry_sc, lg_sc):
    i = pl.program_id(0)

    @pl.when(i == 0)
    def _():
        carry_sc[...] = jnp.zeros_like(carry_sc)
        lg_sc[...] = jnp.zeros_like(lg_sc)

    logits = lg_sc[...]
    n_e, tt = logits.shape
    eidx = lax.broadcasted_iota(jnp.int32, (n_e, tt), 0).astype(F32)
    work = logits
    vals, sels, idxs = [], [], []
    for _ in range(TOP_K):
        m = jnp.max(work, axis=0, keepdims=True)
        ix = jnp.min(jnp.where(work == m, eidx, float(n_e)), axis=0, keepdims=True)
        sel = eidx == ix
        work = jnp.where(sel, -jnp.inf, work)
        vals.append(m)
        idxs.append(ix)
        sels.append(sel)
    es = [jnp.exp(vk - vals[0]) for vk in vals]
    den = es[0] + es[1] + es[2] + es[3]
    gate_ref[...] = jnp.concatenate(es, axis=0) / den
    idx_ref[...] = jnp.concatenate(idxs, axis=0).astype(jnp.int32)
    onehot = jnp.zeros((n_e, tt), F32)
    for sel in sels:
        onehot = onehot + sel.astype(F32)

    mix = (jnp.dot(yh_ref[...], wo1_ref[...], preferred_element_type=F32)
           + jnp.dot(ya_ref[...], wo2_ref[...], preferred_element_type=F32))
    xn = x_ref[...] + g1_ref[...] * mix
    xn_ref[...] = xn
    h2 = _rms(xn) * n2g_ref[...]
    h2 = h2 * (1.0 + sc2_ref[...]) + sh2_ref[...]
    _store_packed_rows(h2_ref, h2)
    lg_sc[...] = lax.dot_general(rwt_ref[...], h2, (((1,), (1,)), ((), ())), precision=HIGHEST,
                                 preferred_element_type=F32) + rb_ref[...]

    prefix = jnp.dot(onehot.astype(BF16), tri_ref[...], preferred_element_type=F32) + carry_sc[...]
    ranks = [jnp.sum(jnp.where(sel, prefix, 0.0), axis=0, keepdims=True) for sel in sels]
    rank_ref[...] = jnp.concatenate(ranks, axis=0).astype(jnp.int32)
    carry_sc[...] += jnp.where(i > 0, jnp.sum(onehot, axis=1, keepdims=True), 0.0)
    cnt_ref[...] = jnp.broadcast_to(carry_sc[...], cnt_ref.shape)


def _outproj_router(y_hy, y_att, x2, g1, sh2, sc2, norm2_g, w_out, router_w, router_b, tokens_per_batch,
                    tt):
    t, d = x2.shape
    ch = y_hy.shape[1]
    n_e = router_w.shape[1]
    wo1 = w_out[:ch].astype(BF16)
    wo2 = w_out[ch:].astype(BF16)
    tri = jnp.asarray(np.triu(np.ones((tt, tt), np.float32), k=1), BF16)
    steps_per_batch = tokens_per_batch // tt
    n_tiles = t // tt
    cur = lambda i: jnp.minimum(i, n_tiles - 1)
    tok = lambda i: (cur(i), 0)
    per_b = lambda i: (cur(i) // steps_per_batch, 0, 0)
    lanes_tok = lambda i: (0, jnp.maximum(i - 1, 0))
    return pl.pallas_call(
        _outproj_body,
        grid=(n_tiles + 1,),
        in_specs=[pl.BlockSpec((tt, ch), tok), pl.BlockSpec((tt, ch), tok),
                  pl.BlockSpec((tt, d), tok),
                  pl.BlockSpec((None, 1, d), per_b), pl.BlockSpec((None, 1, d), per_b),
                  pl.BlockSpec((None, 1, d), per_b),
                  _const_spec((1, d)), _const_spec(wo1.shape), _const_spec(wo2.shape),
                  _const_spec((n_e, d)), _const_spec((n_e, 1)), _const_spec((tt, tt))],
        out_specs=[pl.BlockSpec((tt, d), tok),
                   pl.BlockSpec((d // (2 * LANES), tt, LANES), lambda i: (0, cur(i), 0)),
                   pl.BlockSpec((TOP_K, tt), lanes_tok), pl.BlockSpec((TOP_K, tt), lanes_tok),
                   pl.BlockSpec((TOP_K, tt), lanes_tok), _const_spec((n_e, LANES))],
        out_shape=[jax.ShapeDtypeStruct((t, d), F32),
                   jax.ShapeDtypeStruct((d // (2 * LANES), t, LANES), jnp.uint32),
                   jax.ShapeDtypeStruct((TOP_K, t), jnp.int32),
                   jax.ShapeDtypeStruct((TOP_K, t), F32),
                   jax.ShapeDtypeStruct((TOP_K, t), jnp.int32),
                   jax.ShapeDtypeStruct((n_e, LANES), F32)],
        scratch_shapes=[pltpu.VMEM((n_e, 1), F32), pltpu.VMEM((n_e, tt), F32)],
        compiler_params=_cparams(("arbitrary",), VMEM_LIMIT),
        name="outproj_router",
    )(y_hy, y_att, x2, g1, sh2, sc2, norm2_g.reshape(1, d), wo1, wo2, router_w.T,
      router_b.reshape(n_e, 1), tri)


def _cast_rows(src_ref, dst_ref, chunk):
    def body(c, carry):
        sl = pl.ds(pl.multiple_of(c * chunk, chunk), chunk)
        dst_ref[sl, :] = src_ref[sl, :].astype(dst_ref.dtype)
        return carry

    lax.fori_loop(0, src_ref.shape[0] // chunk, body, 0)


def _expert_body(be_ref, nvalid_ref, xs_ref, wgu_ref, bgu_ref, wd_ref, bd_ref, ys_ref, wgu_bf, wd_bf):
    i = pl.program_id(0)
    n_valid = nvalid_ref[i]
    active = n_valid > 0
    new_expert = jnp.logical_or(i == 0, be_ref[i] != be_ref[jnp.maximum(i - 1, 0)])

    @pl.when(jnp.logical_and(active, new_expert))
    def _():
        _cast_rows(wgu_ref, wgu_bf, CAST_ROW_CHUNK)
        _cast_rows(wd_ref, wd_bf, CAST_ROW_CHUNK)

    @pl.when(active)
    def _():
        row = lax.broadcasted_iota(jnp.int32, (xs_ref.shape[1], 1), 0)
        xs = jnp.where(row < n_valid, _load_packed_rows(xs_ref), 0.0).astype(BF16)
        gu = jnp.dot(xs, wgu_bf[...], preferred_element_type=F32) + bgu_ref[...]
        dff = gu.shape[1] // 2
        gate = jnp.minimum(gu[:, :dff], SWIGLU_LIMIT)
        up = jnp.clip(gu[:, dff:], -SWIGLU_LIMIT, SWIGLU_LIMIT)
        act = (up + 1.0) * (gate * jax.nn.sigmoid(SWIGLU_ALPHA * gate))
        ys = jnp.dot(act.astype(BF16), wd_bf[...], preferred_element_type=F32) + bd_ref[...]
        _store_packed_rows(ys_ref, ys)

    @pl.when(jnp.logical_not(active))
    def _():
        ys_ref[...] = jnp.zeros_like(ys_ref)


def _expert_blocks(xs, block_e, n_valid, w_gu, b_gu, w_down, b_down):
    n_seg, n_rows, _ = xs.shape
    n_e, d, dff2 = w_gu.shape
    bm = MOE_ROWS
    seg_block = pl.BlockSpec((n_seg, bm, LANES), lambda i, be, nu: (0, i, 0))
    grid_spec = pltpu.PrefetchScalarGridSpec(
        num_scalar_prefetch=2,
        grid=(n_rows // bm,),
        in_specs=[seg_block,
                  pl.BlockSpec((None, d, dff2), lambda i, be, nu: (be[i], 0, 0)),
                  pl.BlockSpec((None, 1, dff2), lambda i, be, nu: (be[i], 0, 0)),
                  pl.BlockSpec((None, dff2 // 2, d), lambda i, be, nu: (be[i], 0, 0)),
                  pl.BlockSpec((None, 1, d), lambda i, be, nu: (be[i], 0, 0))],
        out_specs=seg_block,
        scratch_shapes=[pltpu.VMEM((d, dff2), BF16), pltpu.VMEM((dff2 // 2, d), BF16)],
    )
    return pl.pallas_call(
        _expert_body,
        grid_spec=grid_spec,
        out_shape=jax.ShapeDtypeStruct(xs.shape, jnp.uint32),
        compiler_params=_cparams(("arbitrary",), VMEM_LIMIT),
        name="moe_experts",
    )(block_e, n_valid, xs, w_gu, b_gu.reshape(n_e, 1, dff2), w_down, b_down.reshape(n_e, 1, d))


def _sc_gather(table, idx):
    n = idx.shape[0]
    width = table.shape[1]
    mesh = plsc.VectorSubcoreMesh(core_axis_name="core", subcore_axis_name="subcore")
    n_workers = mesh.num_cores * mesh.num_subcores
    assert width == LANES and n % (SC_WINDOW * n_workers) == 0

    @functools.partial(pl.kernel, out_type=jax.ShapeDtypeStruct((n, width), table.dtype), mesh=mesh)
    def gather_kernel(table_hbm, idx_hbm, out_hbm):
        def body(idx_vmem, out_vmem):
            pltpu.sync_copy(table_hbm.at[idx_vmem.at[0]], out_vmem)

        pltpu.emit_pipeline(
            body,
            grid=(n // SC_WINDOW,),
            in_specs=[pl.BlockSpec((1, SC_WINDOW), lambda i: (0, i))],
            out_specs=[pl.BlockSpec((SC_WINDOW, width), lambda i: (i, 0))],
            core_axis_name=("core", "subcore"),
            dimension_semantics=(pltpu.PARALLEL,),
        )(idx_hbm, out_hbm)

    return gather_kernel(table, idx.reshape(1, n))


def _sc_scatter(rows, idx, n_copies, n_out):
    n, width = rows.shape
    mesh = plsc.VectorSubcoreMesh(core_axis_name="core", subcore_axis_name="subcore")
    n_workers = mesh.num_cores * mesh.num_subcores
    assert width == LANES and n % (SC_WINDOW * n_workers) == 0 and idx.shape == (8, n)

    @functools.partial(pl.kernel, out_type=jax.ShapeDtypeStruct((n_out, width), rows.dtype), mesh=mesh)
    def scatter_kernel(rows_hbm, idx_hbm, out_hbm):
        def body(rows_vmem, idx_vmem):
            for k in range(n_copies):
                pltpu.sync_copy(rows_vmem, out_hbm.at[idx_vmem.at[k]])

        pltpu.emit_pipeline(
            body,
            grid=(n // SC_WINDOW,),
            in_specs=[pl.BlockSpec((SC_WINDOW, width), lambda i: (i, 0)),
                      pl.BlockSpec((8, SC_WINDOW), lambda i: (0, i))],
            out_specs=[],
            core_axis_name=("core", "subcore"),
            dimension_semantics=(pltpu.PARALLEL,),
        )(rows_hbm, idx_hbm)

    return scatter_kernel(rows, idx)


def _combine_body(pk_ref, gt_ref, xn_ref, g2_ref, o_ref):
    n_seg = pk_ref.shape[0]
    half = o_ref.shape[1] // 2
    gt = gt_ref[...]
    g2 = g2_ref[...]
    for j in range(n_seg):
        acc_hi = None
        acc_lo = None
        for kk in range(TOP_K):
            hi, lo = _unpack_words(pk_ref[j, kk])
            g = gt[:, kk:kk + 1]
            acc_hi = g * hi if acc_hi is None else acc_hi + g * hi
            acc_lo = g * lo if acc_lo is None else acc_lo + g * lo
        c_hi = slice(j * LANES, (j + 1) * LANES)
        c_lo = slice(half + j * LANES, half + (j + 1) * LANES)
        o_ref[:, c_hi] = xn_ref[:, c_hi] + g2[:, c_hi] * acc_hi
        o_ref[:, c_lo] = xn_ref[:, c_lo] + g2[:, c_lo] * acc_lo


def _combine(picked, gates_t, xn, g2, tokens_per_batch, tt):
    n_seg, _, t, _ = picked.shape
    d = xn.shape[1]
    steps_per_batch = tokens_per_batch // tt
    return pl.pallas_call(
        _combine_body,
        grid=(t // tt,),
        in_specs=[pl.BlockSpec((n_seg, TOP_K, tt, LANES), lambda i: (0, 0, i, 0)),
                  pl.BlockSpec((tt, TOP_K), lambda i: (i, 0)),
                  pl.BlockSpec((tt, d), lambda i: (i, 0)),
                  pl.BlockSpec((None, 1, d), lambda i: (i // steps_per_batch, 0, 0))],
        out_specs=pl.BlockSpec((tt, d), lambda i: (i, 0)),
        out_shape=jax.ShapeDtypeStruct((t, d), F32),
        compiler_params=_cparams(("parallel",), VMEM_LIMIT),
        name="moe_combine",
    )(picked, gates_t, xn, g2)


def _moe(h2p, xn, g2, idx, gates, ranks, counts, w_gu, b_gu, w_down, b_down, tokens_per_batch):
    n_seg, t, _ = h2p.shape
    bm = MOE_ROWS
    n_e = w_gu.shape[0]
    cnt = counts[:, 0].astype(jnp.int32)
    padded = (cnt + bm - 1) // bm * bm
    padded_ends = jnp.cumsum(padded)
    padded_starts = padded_ends - padded
    experts = jnp.arange(n_e, dtype=jnp.int32)[:, None, None]
    dest = ranks + jnp.sum(jnp.where(idx[None] == experts, padded_starts[:, None, None], 0), axis=0)
    n_blocks = t * TOP_K // bm + n_e
    n_rows = n_blocks * bm
    block_start = jnp.arange(n_blocks, dtype=jnp.int32) * bm
    block_e = jnp.minimum(jnp.sum(padded_ends[None, :] <= block_start[:, None], axis=1),
                          n_e - 1).astype(jnp.int32)
    n_valid = jnp.clip(cnt[block_e] - (block_start - padded_starts[block_e]), 0, bm).astype(jnp.int32)
    seg = jnp.arange(n_seg, dtype=jnp.int32)
    scatter_idx = (seg[None, :, None] * n_rows + dest[:, None, :]).reshape(TOP_K, n_seg * t)
    scatter_idx = jnp.concatenate([scatter_idx, scatter_idx], axis=0)
    xs = _sc_scatter(h2p.reshape(n_seg * t, LANES), scatter_idx, TOP_K, n_seg * n_rows)
    ys = _expert_blocks(xs.reshape(n_seg, n_rows, LANES), block_e, n_valid, w_gu, b_gu, w_down, b_down)
    picked = _sc_gather(ys.reshape(n_seg * n_rows, LANES),
                        (seg[:, None, None] * n_rows + dest[None]).reshape(-1))
    return _combine(picked.reshape(n_seg, TOP_K, t, LANES), gates.T, xn, g2, tokens_per_batch,
                    min(TOKEN_TILE, tokens_per_batch))


def kernel(x, c, ctx, c_ctx, mod_w, mod_b, norm1_g, w_in, hy_conv_w, hy_conv_b, hy_f_w1, hy_f_b1,
           hy_f_w2, hy_f_b2, hy_f_w3, hy_f_b3, hy_f_w4, hy_f_freq, hy_bias, mla_q_norm_g, mla_w_uq,
           mla_kv_norm_g, mla_w_ukv, qk_norm_q_g, qk_norm_k_g, w_out, norm2_g, router_w, router_b,
           exp_w_gu, exp_b_gu, exp_w_down, exp_b_down):
    b, length, d = x.shape
    depth = mod_w.shape[0]
    assert depth == 1, "single-layer kernel"
    ly = 0
    c_rows = jnp.concatenate([c, c_ctx[None, :], jnp.zeros((8 - b - 1, d), F32)], axis=0)
    mod = _adaln_table(c_rows, mod_w[ly], mod_b[ly])
    mod6 = mod.reshape(8, 6, d)
    sh1, sc1, g1, sh2, sc2, g2 = (mod6[:b, j][:, None, :] for j in range(6))
    csh1 = mod6[b:b + 1, 0][:, None, :]
    csc1 = mod6[b:b + 1, 1][:, None, :]

    weights = _mla_weights(w_in[ly], mla_w_uq[ly], mla_w_ukv[ly], qk_norm_q_g[ly], qk_norm_k_g[ly])
    n_ctx = ctx.shape[1]
    _, _, k_c, v_c = _inproj(ctx, jnp.broadcast_to(csh1, (b, 1, d)), jnp.broadcast_to(csc1, (b, 1, d)),
                             norm1_g[ly], weights, mla_q_norm_g[ly], mla_kv_norm_g[ly], False, n_ctx)
    hy, q, k, v = _inproj(x, sh1, sc1, norm1_g[ly], weights, mla_q_norm_g[ly], mla_kv_norm_g[ly],
                          True, min(TOKEN_TILE, length))

    kern, asum = _hyena_kernel_taps(length, hy_f_w1[ly], hy_f_b1[ly], hy_f_w2[ly], hy_f_b2[ly],
                                    hy_f_w3[ly], hy_f_b3[ly], hy_f_w4[ly], hy_f_freq[ly])
    kf = _hyena_filter_spectrum(kern, asum)
    y_hy = _hyena_conv(hy, hy_conv_w[ly], hy_conv_b[ly], hy_bias[ly], kf)

    y_att = _flash_attention(q, k, v, k_c, v_c, min(ATTN_Q_TILE, length), min(ATTN_K_TILE, length),
                             ATTN_ROW_GROUP)

    t = b * length
    xn, h2, idx, gates, ranks, counts = _outproj_router(
        y_hy.reshape(t, -1), y_att.reshape(t, -1), x.reshape(t, d), g1, sh2, sc2, norm2_g[ly],
        w_out[ly], router_w[ly], router_b[ly], length, min(TOKEN_TILE, length))
    out = _moe(h2, xn, g2, idx, gates, ranks, counts, exp_w_gu[ly], exp_b_gu[ly],
               exp_w_down[ly], exp_b_down[ly], length)
    return out.reshape(b, length, d)
```

```python
import functools
import math

import jax
import jax.numpy as jnp
import numpy as np
from jax import lax
from jax.experimental import pallas as pl
from jax.experimental.pallas import tpu as pltpu
from jax.experimental.pallas import tpu_sc as plsc

F32 = jnp.float32
BF16 = jnp.bfloat16
HIGHEST = lax.Precision.HIGHEST

GRID_W = 64
D_HYENA = 512
FILTER_ORDER = 64
POS_EMB_DIM = 33
MIN_DECAY = math.log(1e-2) / 0.3
MAX_DECAY = math.log(1e-2) / 1.5
NOPE_DIM = 128
ROPE_DIM = 64
QK_DIM = NOPE_DIM + ROPE_DIM
V_DIM = 128
MLA_HEADS = 4
Q_RANK = 256
KV_RANK = 128
ROPE_THETA = 10000.0
N_EXPERTS = 32
TOP_K = 4
SWIGLU_ALPHA = 1.702
SWIGLU_LIMIT = 7.0
NORM_EPS = 1e-6

LANES = 128
VMEM_LIMIT = 56 * 1024 * 1024

BF16_PACK_ROWS = 16

DFT_Q = LANES
CONV_PITCH = DFT_Q + 8
MOE_ROWS = 512
SC_WINDOW = 128

TOKEN_TILE = 512
ATTN_Q_TILE = 512
ATTN_K_TILE = 512
ATTN_ROW_GROUP = 32
FILTER_TILE = 1024
CONV_ROW_CHUNK = 512
CAST_ROW_CHUNK = 128


def _cparams(sem, vmem=None):
    return pltpu.CompilerParams(dimension_semantics=sem, vmem_limit_bytes=vmem)


def _const_spec(shape):
    nd = len(shape)
    return pl.BlockSpec(shape, lambda *_: (0,) * nd)


def _single_spec(shape, index_map):
    return pl.BlockSpec(shape, index_map, pipeline_mode=pl.Buffered(1))


def _mod_body(c_ref, w_ref, b_ref, o_ref):
    cc = c_ref[...]
    s = cc * jax.nn.sigmoid(cc)
    o_ref[...] = jnp.dot(s, w_ref[...], precision=HIGHEST,
                         preferred_element_type=F32) + b_ref[...]


def _adaln_table(c_rows, mod_w, mod_b):
    rows, d = c_rows.shape
    n = mod_w.shape[1]
    tn = n // 8
    return pl.pallas_call(
        _mod_body,
        grid=(n // tn,),
        in_specs=[_const_spec((rows, d)),
                  pl.BlockSpec((d, tn), lambda j: (0, j)),
                  pl.BlockSpec((1, tn), lambda j: (0, j))],
        out_specs=pl.BlockSpec((rows, tn), lambda j: (0, j)),
        out_shape=jax.ShapeDtypeStruct((rows, n), F32),
        compiler_params=_cparams(("arbitrary",)),
        name="adaln_table",
    )(c_rows, mod_w, mod_b.reshape(1, n))


def _rms(x, eps=NORM_EPS):
    return x * lax.rsqrt(jnp.mean(x * x, axis=-1, keepdims=True) + eps)


def _inproj_body(x_ref, sh_ref, sc_ref, g_ref, why_ref, wmla_ref, qng_ref, wuq_ref,
                 kvng_ref, wukv_ref, ct_ref, st_ref, ctk_ref, gq1_ref, gq2_ref, gkn_ref,
                 gkr_ref, hy_ref, q_ref, k_ref, v_ref, mla_sc, *, q_scale):
    i = pl.program_id(0)

    @pl.when(i == 0)
    def _():
        mla_sc[...] = jnp.zeros_like(mla_sc)

    mla = mla_sc[...]
    cq = mla[:, :Q_RANK]
    ckv = mla[:, Q_RANK:Q_RANK + KV_RANK]
    pe2 = mla[:, Q_RANK + KV_RANK:]
    qf = jnp.dot((_rms(cq) * qng_ref[...]).astype(BF16), wuq_ref[...],
                 preferred_element_type=F32)
    kvf = jnp.dot((_rms(ckv) * kvng_ref[...]).astype(BF16), wukv_ref[...],
                  preferred_element_type=F32)
    ct = ct_ref[...]
    st = st_ref[...]
    lane256 = lax.broadcasted_iota(jnp.int32, (1, 2 * LANES), 1)
    qmask = (lane256 < QK_DIM).astype(F32)
    lane128 = lax.broadcasted_iota(jnp.int32, (1, LANES), 1)
    pemask = (lane128 < ROPE_DIM).astype(F32)
    kr0 = pe2 * gkr_ref[...] * ctk_ref[...]
    krs = kr0 + pltpu.roll(kr0, ROPE_DIM, axis=1)
    pem = pe2 * pemask
    ss_pe = jnp.sum(pem * pem, axis=-1, keepdims=True)
    gq1 = gq1_ref[...]
    gq2 = gq2_ref[...]
    gkn = gkn_ref[...]
    for hd in range(MLA_HEADS):
        slab = qf[:, hd * 2 * LANES:(hd + 1) * 2 * LANES]
        sm = slab * qmask
        rq = lax.rsqrt(jnp.sum(sm * sm, axis=-1, keepdims=True) / QK_DIM + NORM_EPS) * q_scale
        t = slab * gq1 * ct + pltpu.roll(slab * gq2 * st, QK_DIM, axis=1)
        q_ref[hd] = (t * rq)[:, :QK_DIM].astype(BF16)
        kn = kvf[:, hd * 2 * LANES:hd * 2 * LANES + NOPE_DIM]
        rk = lax.rsqrt((jnp.sum(kn * kn, axis=-1, keepdims=True) + ss_pe) / QK_DIM + NORM_EPS)
        kslab = jnp.concatenate([kn * gkn, krs], axis=-1) * rk
        k_ref[hd] = kslab[:, :QK_DIM].astype(BF16)
        v_ref[hd] = kvf[:, hd * 2 * LANES + NOPE_DIM:(hd + 1) * 2 * LANES].astype(BF16)

    h = _rms(x_ref[...]) * g_ref[...]
    h = h * (1.0 + sc_ref[...]) + sh_ref[...]
    hb = h.astype(BF16)
    hy_ref[...] = jnp.dot(hb, why_ref[...], preferred_element_type=F32).astype(BF16)
    mla_sc[...] = jnp.dot(hb, wmla_ref[...], preferred_element_type=F32)


@functools.lru_cache(maxsize=None)
def _rope_lane_tables(length, use_rope):
    if use_rope:
        n_freq = ROPE_DIM // 4
        t = np.arange(length)
        inv_freq = np.power(ROPE_THETA, -np.arange(n_freq, dtype=np.float64) / n_freq)
        ar = (t // GRID_W).astype(np.float64)[:, None] * inv_freq
        ac = (t % GRID_W).astype(np.float64)[:, None] * inv_freq
        c64 = np.concatenate([np.cos(ar), np.cos(ar), np.cos(ac), np.cos(ac)], axis=-1)
        s64 = np.concatenate([-np.sin(ar), np.sin(ar), -np.sin(ac), np.sin(ac)], axis=-1)
    else:
        c64 = np.ones((length, ROPE_DIM))
        s64 = np.zeros((length, ROPE_DIM))
    z64 = np.zeros((length, ROPE_DIM))
    ct = np.concatenate([np.ones((length, LANES)), c64, z64], axis=-1)
    st = np.concatenate([np.zeros((length, LANES)), z64, s64], axis=-1)
    ctk = np.concatenate([c64, s64], axis=-1)
    return ct.astype(np.float32), st.astype(np.float32), ctk.astype(np.float32)


_SWAP16 = np.concatenate([np.arange(16, 32), np.arange(0, 16), np.arange(48, 64), np.arange(32, 48)])


def _mla_weights(w_in, mla_w_uq, mla_w_ukv, qk_norm_q_g, qk_norm_k_g):
    hy_cols = 3 * D_HYENA
    w_hy = w_in[:, :hy_cols].astype(BF16)
    w_pe = w_in[:, hy_cols + Q_RANK + KV_RANK:]
    w_mla = jnp.concatenate([w_in[:, hy_cols:hy_cols + Q_RANK + KV_RANK], w_pe, w_pe[:, _SWAP16]],
                            axis=-1).astype(BF16)
    wq = mla_w_uq.reshape(Q_RANK, MLA_HEADS, QK_DIM)
    wq_rope = wq[:, :, NOPE_DIM:]
    w_uq2 = jnp.concatenate([wq[:, :, :NOPE_DIM], wq_rope, wq_rope[:, :, _SWAP16]], axis=-1)
    w_uq2 = w_uq2.reshape(Q_RANK, MLA_HEADS * 2 * LANES).astype(BF16)
    w_ukv2 = mla_w_ukv.astype(BF16)
    gq_r = qk_norm_q_g[NOPE_DIM:]
    z64 = jnp.zeros((ROPE_DIM,), F32)
    gq1 = jnp.concatenate([qk_norm_q_g[:NOPE_DIM], gq_r, z64]).reshape(1, -1)
    gq2 = jnp.concatenate([jnp.zeros((NOPE_DIM,), F32), z64, gq_r[_SWAP16]]).reshape(1, -1)
    gkn = qk_norm_k_g[:NOPE_DIM].reshape(1, -1)
    gk_r = qk_norm_k_g[NOPE_DIM:]
    gkr = jnp.concatenate([gk_r, gk_r[_SWAP16]]).reshape(1, -1)
    return w_hy, w_mla, w_uq2, w_ukv2, gq1, gq2, gkn, gkr


def _inproj(x, shift, scale, norm_g, weights, q_norm_g, kv_norm_g, use_rope, tl):
    b, length, d = x.shape
    w_hy, w_mla, w_uq2, w_ukv2, gq1, gq2, gkn, gkr = weights
    ct, st, ctk = _rope_lane_tables(length, use_rope)
    q_scale = QK_DIM ** -0.5 * math.log2(math.e)
    nt = length // tl
    n_tiles = b * nt
    hyc = w_hy.shape[1]
    cur = lambda i: jnp.minimum(i, n_tiles - 1)
    prev = lambda i: jnp.maximum(i - 1, 0)
    tok = lambda i: (cur(i), 0)
    per_b = lambda i: (cur(i) // nt, 0, 0)
    pos = lambda i: (prev(i) % nt, 0)
    head_blk = lambda i: (prev(i) // nt, 0, prev(i) % nt, 0)
    hy, q, k, v = pl.pallas_call(
        functools.partial(_inproj_body, q_scale=q_scale),
        grid=(n_tiles + 1,),
        in_specs=[pl.BlockSpec((tl, d), tok),
                  pl.BlockSpec((None, 1, d), per_b),
                  pl.BlockSpec((None, 1, d), per_b),
                  _const_spec((1, d)),
                  _const_spec(w_hy.shape), _const_spec(w_mla.shape),
                  _const_spec((1, Q_RANK)), _const_spec(w_uq2.shape),
                  _const_spec((1, KV_RANK)), _const_spec(w_ukv2.shape),
                  pl.BlockSpec((tl, 2 * LANES), pos), pl.BlockSpec((tl, 2 * LANES), pos),
                  pl.BlockSpec((tl, LANES), pos),
                  _const_spec((1, 2 * LANES)), _const_spec((1, 2 * LANES)),
                  _const_spec((1, LANES)), _const_spec((1, LANES))],
        out_specs=[pl.BlockSpec((tl, hyc), tok),
                   pl.BlockSpec((None, MLA_HEADS, tl, QK_DIM), head_blk),
                   pl.BlockSpec((None, MLA_HEADS, tl, QK_DIM), head_blk),
                   pl.BlockSpec((None, MLA_HEADS, tl, V_DIM), head_blk)],
        out_shape=[jax.ShapeDtypeStruct((b * length, hyc), BF16),
                   jax.ShapeDtypeStruct((b, MLA_HEADS, length, QK_DIM), BF16),
                   jax.ShapeDtypeStruct((b, MLA_HEADS, length, QK_DIM), BF16),
                   jax.ShapeDtypeStruct((b, MLA_HEADS, length, V_DIM), BF16)],
        scratch_shapes=[pltpu.VMEM((tl, w_mla.shape[1]), F32)],
        compiler_params=_cparams(("arbitrary",), VMEM_LIMIT),
        name="inproj_mla",
    )(x.reshape(b * length, d), shift, scale, norm_g.reshape(1, d), w_hy, w_mla,
      q_norm_g.reshape(1, -1), w_uq2, kv_norm_g.reshape(1, -1), w_ukv2, ct, st, ctk, gq1, gq2, gkn, gkr)
    return hy.reshape(b, length, hyc), q, k, v


def _filter_body(z_ref, w1_ref, b1_ref, w2_ref, b2_ref, w3_ref, b3_ref, w4_ref, fr_ref, dl_ref,
                 kern_ref, asum_ref, *, zero_row, tr):
    i = pl.program_id(0)
    fr = fr_ref[...]
    hr = tr // 2
    za = z_ref[:hr, :]
    zb = z_ref[hr:, :]
    dot = functools.partial(jnp.dot, precision=HIGHEST, preferred_element_type=F32)
    h = jnp.sin(fr * (dot(jnp.concatenate([za, zb], axis=1), w1_ref[...]) + b1_ref[...]))
    h = jnp.sin(fr * (dot(h, w2_ref[...]) + b2_ref[...]))
    h = jnp.sin(fr * (dot(h, w3_ref[...]) + b3_ref[...]))
    o2 = dot(h, w4_ref[...])
    nc = o2.shape[1] // 2
    oa = o2[:, :nc] * jnp.exp(-za[:, 0:1] * dl_ref[...])
    ob = o2[:, nc:] * jnp.exp(-zb[:, 0:1] * dl_ref[...])

    @pl.when(i == 0)
    def _():
        asum_ref[...] = jnp.zeros_like(asum_ref)

    asum_ref[...] += (jnp.sum(jnp.abs(oa), axis=0, keepdims=True)
                      + jnp.sum(jnp.abs(ob), axis=0, keepdims=True))
    row = i * tr + lax.broadcasted_iota(jnp.int32, (hr, 1), 0)
    kern_ref[:hr, :] = jnp.where(row == zero_row, 0.0, oa)
    kern_ref[hr:, :] = jnp.where(row + hr == zero_row, 0.0, ob)


@functools.lru_cache(maxsize=None)
def _filter_features(length):
    n = 2 * length
    bands = (POS_EMB_DIM - 1) // 2
    pos = np.concatenate([np.arange(length), (n - np.arange(length, n)) % length])
    t_tab = np.linspace(0.0, 1.0, length)[:, None]
    w_ang = 2.0 * np.pi * np.arange(length, dtype=np.float64)[:, None] / length
    f = np.linspace(1e-4, bands - 1, bands)[None, :]
    z_tab = np.concatenate([t_tab, np.cos(f * w_ang), -np.sin(f * w_ang)], axis=-1)
    z = np.pad(z_tab[pos], ((0, 0), (0, LANES - POS_EMB_DIM)))
    return z.astype(np.float32)


def _hyena_kernel_taps(length, w1, b1, w2, b2, w3, b3, w4, freq):
    n = 2 * length
    z = jnp.asarray(_filter_features(length))
    deltas = jnp.abs(jnp.linspace(MIN_DECAY, MAX_DECAY, D_HYENA, dtype=F32)).reshape(1, -1)
    tr = min(FILTER_TILE, length)
    half_steps = length // tr
    fo = FILTER_ORDER

    def pair(w):
        zeros = jnp.zeros_like(w)
        return jnp.concatenate([jnp.concatenate([w, zeros], axis=1),
                                jnp.concatenate([zeros, w], axis=1)], axis=0)

    def twice(v):
        return jnp.concatenate([v, v]).reshape(1, 2 * fo)

    w1p = pair(jnp.pad(w1, ((0, LANES - POS_EMB_DIM), (0, 0))))
    w4p = jnp.stack([pair(w4[:, :D_HYENA]), pair(w4[:, D_HYENA:])])
    kern, asum = pl.pallas_call(
        functools.partial(_filter_body, zero_row=length, tr=tr),
        grid=(n // tr,),
        in_specs=[pl.BlockSpec((tr, LANES), lambda i: (i, 0)),
                  _const_spec((2 * LANES, 2 * fo)), _const_spec((1, 2 * fo)),
                  _const_spec((2 * fo, 2 * fo)), _const_spec((1, 2 * fo)),
                  _const_spec((2 * fo, 2 * fo)), _const_spec((1, 2 * fo)),
                  pl.BlockSpec((None, 2 * fo, 2 * D_HYENA), lambda i: (i // half_steps, 0, 0)),
                  _const_spec((1, 2 * fo)), _const_spec((1, D_HYENA))],
        out_specs=[pl.BlockSpec((tr, D_HYENA), lambda i: (i, 0)),
                   _const_spec((1, D_HYENA))],
        out_shape=[jax.ShapeDtypeStruct((n, D_HYENA), F32),
                   jax.ShapeDtypeStruct((1, D_HYENA), F32)],
        compiler_params=_cparams(("arbitrary",), VMEM_LIMIT),
        name="hyena_filter",
    )(z, w1p, twice(b1), pair(w2), twice(b2), pair(w3), twice(b3), w4p, twice(freq), deltas)
    return kern, asum


def _half_rows(n):
    n_half = n // DFT_Q // 2 + 1
    return n_half, -(-n_half // 8) * 8


@functools.lru_cache(maxsize=None)
def _dft_tables(n, p_in):
    q_sz = DFT_Q
    p_sz = n // q_sz
    n_half, n_r = _half_rows(n)
    r = np.arange(n_r, dtype=np.float64)
    keep = (r < n_half).astype(np.float64)
    qq = np.arange(q_sz, dtype=np.float64)
    pp = np.arange(p_in, dtype=np.float64)
    tt = q_sz * pp[None, None, :] + qq[:, None, None]
    ang = -2.0 * np.pi * r[None, :, None] * tt / n
    t1 = np.concatenate([np.cos(ang), np.sin(ang)], axis=1) * np.tile(keep, 2)[None, :, None]
    mirror = np.where((r == 0) | (r == p_sz // 2), 1.0, 2.0)
    t4 = np.transpose(t1 * np.tile(mirror, 2)[None, :, None], (0, 2, 1)) / n
    a2 = -2.0 * np.pi * np.outer(qq, qq) / q_sz
    fre, fim = np.cos(a2), np.sin(a2)
    m2 = np.block([[fre, -fim], [fim, fre]])
    m2c = np.block([[fre, fim], [-fim, fre]])
    t1 = np.concatenate([t1[0::2], t1[1::2]], axis=2)
    return (t1.astype(np.float32), t4.astype(np.float32), m2.astype(np.float32),
            m2c.astype(np.float32))


SUBLANES = 8


def _spec_block(rg, h):
    return (rg * 2 + h) * SUBLANES * DFT_Q


def _dft_stage1(x_ref, t1_ref, spec_ref, n_r, p_in, pitch=DFT_Q):
    q_sz = DFT_Q
    ct = spec_ref.shape[1]

    def body(j, carry):
        q0 = 2 * j
        x0 = x_ref[pl.ds(q0, p_in, stride=pitch), :].astype(BF16)
        x1 = x_ref[pl.ds(q0 + 1, p_in, stride=pitch), :].astype(BF16)
        zeros = jnp.zeros_like(x0)
        rhs = jnp.concatenate([jnp.concatenate([x0, zeros], axis=1),
                               jnp.concatenate([zeros, x1], axis=1)], axis=0)
        a = jnp.dot(t1_ref[j], rhs, preferred_element_type=F32)
        for k in range(2):
            row = pl.multiple_of((q0 + k) * SUBLANES, SUBLANES)
            for h in range(2):
                for rg in range(n_r // SUBLANES):
                    src = h * n_r + rg * SUBLANES
                    spec_ref[pl.ds(_spec_block(rg, h) + row, SUBLANES), :] = (
                        a[src:src + SUBLANES, k * ct:(k + 1) * ct])
        return carry

    lax.fori_loop(0, q_sz // 2, body, 0, unroll=4)


def _spec_rows(rg, h, r8):
    return pl.ds(_spec_block(rg, h) + r8, DFT_Q, stride=SUBLANES)


def _spectrum_body(kern_ref, asum_ref, t1_ref, m2_ref, kf_ref, spec_ref, *, n_r, p_in):
    _dft_stage1(kern_ref, t1_ref, spec_ref, n_r, p_in)
    inv = 1.0 / asum_ref[...]
    m2 = m2_ref[...]

    ct = spec_ref.shape[1]

    def body(rg, carry):
        for r8 in range(0, SUBLANES, 2):
            blk = jnp.concatenate(
                [jnp.concatenate([spec_ref[_spec_rows(rg, 0, r8 + j), :],
                                  spec_ref[_spec_rows(rg, 1, r8 + j), :]], axis=0) for j in range(2)],
                axis=1)
            xf = jnp.dot(m2, blk.astype(BF16), preferred_element_type=F32)
            for j in range(2):
                kf_ref[rg * SUBLANES + r8 + j] = (xf[:, j * ct:(j + 1) * ct] * inv).astype(BF16)
        return carry

    lax.fori_loop(0, n_r // SUBLANES, body, 0)


def _hyena_filter_spectrum(kern, asum):
    n, c = kern.shape
    p_sz = n // DFT_Q
    _, n_r = _half_rows(n)
    t1, _, m2, _ = _dft_tables(n, p_sz)
    t1 = jnp.asarray(t1).astype(BF16)
    m2 = jnp.asarray(m2).astype(BF16)
    ct = LANES
    return pl.pallas_call(
        functools.partial(_spectrum_body, n_r=n_r, p_in=p_sz),
        grid=(c // ct,),
        in_specs=[_single_spec((n, ct), lambda j: (0, j)),
                  pl.BlockSpec((1, ct), lambda j: (0, j)),
                  _single_spec(t1.shape, lambda j: (0, 0, 0)),
                  _const_spec(m2.shape)],
        out_specs=pl.BlockSpec((n_r, 2 * DFT_Q, ct), lambda j: (0, 0, j)),
        out_shape=jax.ShapeDtypeStruct((n_r, 2 * DFT_Q, c), BF16),
        scratch_shapes=[pltpu.VMEM((n_r * 2 * DFT_Q, ct), F32)],
        compiler_params=_cparams(("parallel",), VMEM_LIMIT),
        name="hyena_filter_spectrum",
    )(kern, asum, t1, m2)


def _short_conv_chunk(u_ref, w_ref, b_ref, i, rows, length):
    pack = BF16_PACK_ROWS
    base = pl.multiple_of(i * rows, rows)
    u = u_ref[pl.ds(base, rows), :].astype(F32)
    lo = pl.multiple_of(jnp.maximum(base - pack, 0), pack)
    hi = pl.multiple_of(jnp.minimum(base + rows, length - pack), pack)
    prev = u_ref[pl.ds(lo, pack), :].astype(F32)[pack - 1:pack]
    nxt = u_ref[pl.ds(hi, pack), :].astype(F32)[0:1]
    prev = jnp.where(base == 0, 0.0, prev)
    nxt = jnp.where(base + rows == length, 0.0, nxt)
    ridx = lax.broadcasted_iota(jnp.int32, (rows, 1), 0)
    up = jnp.where(ridx == 0, prev, pltpu.roll(u, 1, axis=0))
    dn = jnp.where(ridx == rows - 1, nxt, pltpu.roll(u, rows - 1, axis=0))
    w = w_ref[...]
    return up * w[0:1] + u * w[1:2] + dn * w[2:3] + b_ref[...]


def _hyena_conv_body(x0_ref, x1_ref, v_ref, w0_ref, w1_ref, wv_ref, b0_ref, b1_ref, bv_ref,
                     bias_ref, kf_ref, t1_ref, t4_ref, m2_ref, m2c_ref, o_ref, vx_ref, spec_ref,
                     *, length, rows):
    q_sz = DFT_Q
    n_half, n_r = _half_rows(2 * length)
    p_in = length // q_sz
    n_chunks = length // rows

    p_per_chunk = rows // q_sz

    def vx_rows(i, j):
        return pl.ds(pl.multiple_of((i * p_per_chunk + j) * CONV_PITCH, SUBLANES), q_sz)

    def gate_in(i, carry):
        x1c = _short_conv_chunk(x1_ref, w1_ref, b1_ref, i, rows, length)
        vc = _short_conv_chunk(v_ref, wv_ref, bv_ref, i, rows, length)
        vx = vc * x1c
        for j in range(p_per_chunk):
            vx_ref[vx_rows(i, j), :] = vx[j * q_sz:(j + 1) * q_sz]
        return carry

    lax.fori_loop(0, n_chunks, gate_in, 0)
    _dft_stage1(vx_ref, t1_ref, spec_ref, n_r, p_in, CONV_PITCH)
    m2 = m2_ref[...]
    m2c = m2c_ref[...]

    ct = spec_ref.shape[1]

    def forward_rows(rg, r8s):
        blk = jnp.concatenate(
            [jnp.concatenate([spec_ref[_spec_rows(rg, 0, r8), :], spec_ref[_spec_rows(rg, 1, r8), :]],
                             axis=0) for r8 in r8s], axis=1)
        return jnp.dot(m2, blk.astype(BF16), preferred_element_type=F32)

    def filter_inverse_rows(rg, r8s, xf):
        kf = jnp.concatenate([kf_ref[rg * SUBLANES + r8] for r8 in r8s], axis=1).astype(F32)
        xre, xim = xf[:q_sz], xf[q_sz:]
        kre, kim = kf[:q_sz], kf[q_sz:]
        z = jnp.concatenate([xre * kre - xim * kim, xre * kim + xim * kre], axis=0)
        bf = jnp.dot(m2c, z.astype(BF16), preferred_element_type=F32)
        for j, r8 in enumerate(r8s):
            spec_ref[_spec_rows(rg, 0, r8), :] = bf[:q_sz, j * ct:(j + 1) * ct]
            spec_ref[_spec_rows(rg, 1, r8), :] = bf[q_sz:, j * ct:(j + 1) * ct]

    def mid_rows(rg, n_rows):
        groups = [tuple(range(a, min(a + 2, n_rows))) for a in range(0, n_rows, 2)]
        pending = None
        for r8s in groups:
            xf = forward_rows(rg, r8s)
            if pending is not None:
                filter_inverse_rows(rg, *pending)
            pending = (r8s, xf)
        filter_inverse_rows(rg, *pending)

    def mid(rg, carry):
        mid_rows(rg, SUBLANES)
        return carry

    lax.fori_loop(0, n_half // SUBLANES, mid, 0)
    if n_half % SUBLANES:
        mid_rows(n_half // SUBLANES, n_half % SUBLANES)
    bias = bias_ref[...]

    def last(q, carry):
        row = pl.multiple_of(q * SUBLANES, SUBLANES)
        tiles = [spec_ref[pl.ds(_spec_block(rg, h) + row, SUBLANES), :]
                 for h in range(2) for rg in range(n_r // SUBLANES)]
        bq = jnp.concatenate(tiles, axis=0).astype(BF16)
        y = jnp.dot(t4_ref[q], bq, preferred_element_type=F32)
        sl = pl.ds(q, p_in, stride=CONV_PITCH)
        vx_ref[sl, :] = y + bias * vx_ref[sl, :]
        return carry

    lax.fori_loop(0, q_sz, last, 0, unroll=8)

    def gate_out(i, carry):
        x0c = _short_conv_chunk(x0_ref, w0_ref, b0_ref, i, rows, length)
        sl = pl.ds(pl.multiple_of(i * rows, rows), rows)
        yc = jnp.concatenate([vx_ref[vx_rows(i, j), :] for j in range(p_per_chunk)], axis=0)
        o_ref[sl, :] = (yc * x0c).astype(BF16)
        return carry

    lax.fori_loop(0, n_chunks, gate_out, 0)


def _hyena_conv(hy, conv_w, conv_b, bias, kf):
    b, length, _ = hy.shape
    ch = D_HYENA
    ct = LANES
    nct = ch // ct
    n = 2 * length
    _, n_r = _half_rows(n)
    p_in = length // DFT_Q
    t1, t4, m2, m2c = _dft_tables(n, p_in)
    t1, t4, m2, m2c = (jnp.asarray(a).astype(BF16) for a in (t1, t4, m2, m2c))
    rows = min(CONV_ROW_CHUNK, length)
    col = lambda off: (lambda j, bi: (bi, 0, off * nct + j))
    wcol = lambda off: (lambda j, bi: (0, off * nct + j))
    conv_b2 = conv_b.reshape(1, -1)
    return pl.pallas_call(
        functools.partial(_hyena_conv_body, length=length, rows=rows),
        grid=(nct, b),
        in_specs=[_single_spec((None, length, ct), col(0)),
                  _single_spec((None, length, ct), col(1)),
                  _single_spec((None, length, ct), col(2)),
                  pl.BlockSpec((3, ct), wcol(0)), pl.BlockSpec((3, ct), wcol(1)),
                  pl.BlockSpec((3, ct), wcol(2)),
                  pl.BlockSpec((1, ct), wcol(0)), pl.BlockSpec((1, ct), wcol(1)),
                  pl.BlockSpec((1, ct), wcol(2)),
                  pl.BlockSpec((1, ct), lambda j, bi: (0, j)),
                  _single_spec((n_r, 2 * DFT_Q, ct), lambda j, bi: (0, 0, j)),
                  _single_spec(t1.shape, lambda j, bi: (0, 0, 0)),
                  _single_spec(t4.shape, lambda j, bi: (0, 0, 0)),
                  _const_spec(m2.shape), _const_spec(m2c.shape)],
        out_specs=pl.BlockSpec((None, length, ct), lambda j, bi: (bi, 0, j)),
        out_shape=jax.ShapeDtypeStruct((b, length, ch), BF16),
        scratch_shapes=[pltpu.VMEM((p_in * CONV_PITCH, ct), F32),
                        pltpu.VMEM((n_r * 2 * DFT_Q, ct), F32)],
        compiler_params=_cparams(("parallel", "parallel"), VMEM_LIMIT),
        name="hyena_conv",
    )(hy, hy, hy, conv_w, conv_w, conv_w, conv_b2, conv_b2, conv_b2, bias.reshape(1, ch), kf,
      t1, t4, m2, m2c)


def _flash_body(q_ref, k_ref, v_ref, kc_ref, vc_ref, o_ref, s_ref, p_ref, al_ref, m_ref, acc_ref,
                *, tk, rg):
    tq = q_ref.shape[0]
    nk = k_ref.shape[0] // tk
    nc = kc_ref.shape[0]
    n_chunks = nk + 1
    nt = (((1,), (1,)), ((), ()))

    def width(c):
        return nc if c == nk else tk

    def scores(c):
        keys = kc_ref[...] if c == nk else k_ref[c * tk:(c + 1) * tk, :]
        s_ref[c % 3, :, :width(c)] = lax.dot_general(q_ref[...], keys, nt, preferred_element_type=F32)

    def weighted_values(c):
        vals = vc_ref[...] if c == nk else v_ref[c * tk:(c + 1) * tk, :]
        lane = lax.broadcasted_iota(jnp.int32, vals.shape, 1)
        ones_col = jnp.where(lane == 0, 1.0, 0.0).astype(BF16)
        ext = jnp.concatenate([vals, ones_col], axis=1)
        return jnp.dot(p_ref[c % 2, :, :width(c)], ext, preferred_element_type=F32)

    def softmax(c):
        w = width(c)
        for g in range(tq // rg):
            rows = slice(g * rg, (g + 1) * rg)
            s = s_ref[c % 3, rows, :w]
            m_prev = m_ref[rows, :]
            m_new = jnp.maximum(m_prev, jnp.max(s, axis=-1, keepdims=True))
            al_ref[rows, :] = jnp.exp2(m_prev - m_new)
            m_ref[rows, :] = m_new
            d = (s - jnp.concatenate([m_new] * (w // LANES), axis=1)).astype(BF16)
            p_ref[c % 2, rows, :w] = jnp.exp2(d)

    def rescale(prod):
        alpha = al_ref[...]
        acc_ref[...] = jnp.concatenate([alpha, alpha], axis=1) * (acc_ref[...] + prod)

    m_ref[...] = jnp.full_like(m_ref, -jnp.inf)
    acc_ref[...] = jnp.zeros_like(acc_ref)
    scores(0)
    for c in range(n_chunks):
        prod = weighted_values(c - 1) if c >= 1 else None
        if c + 1 < n_chunks:
            scores(c + 1)
        softmax(c)
        if prod is not None:
            rescale(prod)
    total = acc_ref[...] + weighted_values(n_chunks - 1)
    o_ref[...] = (total[:, :V_DIM] / total[:, V_DIM:V_DIM + 1]).astype(o_ref.dtype)


def _flash_attention(q, k, v, kc, vc, tq, tk, rg):
    b, hds, length, _ = q.shape
    nc = kc.shape[2]
    assert length % tk == 0 and nc <= tk and nc % LANES == 0 and V_DIM == LANES
    return pl.pallas_call(
        functools.partial(_flash_body, tk=tk, rg=rg),
        grid=(b, hds, length // tq),
        in_specs=[pl.BlockSpec((None, None, tq, QK_DIM), lambda bi, h, qi: (bi, h, qi, 0)),
                  pl.BlockSpec((None, None, length, QK_DIM), lambda bi, h, qi: (bi, h, 0, 0)),
                  pl.BlockSpec((None, None, length, V_DIM), lambda bi, h, qi: (bi, h, 0, 0)),
                  pl.BlockSpec((None, None, nc, QK_DIM), lambda bi, h, qi: (bi, h, 0, 0)),
                  pl.BlockSpec((None, None, nc, V_DIM), lambda bi, h, qi: (bi, h, 0, 0))],
        out_specs=pl.BlockSpec((None, tq, V_DIM), lambda bi, h, qi: (bi, qi, h)),
        out_shape=jax.ShapeDtypeStruct((b, length, hds * V_DIM), BF16),
        scratch_shapes=[pltpu.VMEM((3, tq, tk), F32), pltpu.VMEM((2, tq, tk), BF16),
                        pltpu.VMEM((tq, LANES), F32), pltpu.VMEM((tq, LANES), F32),
                        pltpu.VMEM((tq, 2 * V_DIM), F32)],
        compiler_params=_cparams(("parallel", "parallel", "parallel"), VMEM_LIMIT),
        name="mla_flash_attention",
    )(q, k, v, kc, vc)


def _store_packed_rows(dst_ref, x):
    half = x.shape[1] // 2
    for j in range(half // LANES):
        hi = x[:, j * LANES:(j + 1) * LANES].astype(BF16).astype(F32)
        lo = x[:, half + j * LANES:half + (j + 1) * LANES].astype(BF16).astype(F32)
        dst_ref[j] = (lax.bitcast_convert_type(hi, jnp.uint32)
                      | (lax.bitcast_convert_type(lo, jnp.uint32) >> 16))


def _unpack_words(w):
    hi = lax.bitcast_convert_type(w & jnp.uint32(0xFFFF0000), F32)
    lo = lax.bitcast_convert_type(w << 16, F32)
    return hi, lo


def _load_packed_rows(src_ref):
    parts = [_unpack_words(src_ref[j]) for j in range(src_ref.shape[0])]
    return jnp.concatenate([p[0] for p in parts] + [p[1] for p in parts], axis=-1)


def _outproj_body(yh_ref, ya_ref, x_ref, g1_ref, sh2_ref, sc2_ref, n2g_ref, wo1_ref, wo2_ref,
                  rwt_ref, rb_ref, tri_ref, xn_ref, h2_ref, idx_ref, gate_ref, rank_ref, cnt_ref,
                  carry_sc, lg_sc):
    i = pl.program_id(0)

    @pl.when(i == 0)
    def _():
        carry_sc[...] = jnp.zeros_like(carry_sc)
        lg_sc[...] = jnp.zeros_like(lg_sc)

    logits = lg_sc[...]
    n_e, tt = logits.shape
    eidx = lax.broadcasted_iota(jnp.int32, (n_e, tt), 0).astype(F32)
    work = logits
    vals, sels, idxs = [], [], []
    for _ in range(TOP_K):
        m = jnp.max(work, axis=0, keepdims=True)
        ix = jnp.min(jnp.where(work == m, eidx, float(n_e)), axis=0, keepdims=True)
        sel = eidx == ix
        work = jnp.where(sel, -jnp.inf, work)
        vals.append(m)
        idxs.append(ix)
        sels.append(sel)
    es = [jnp.exp(vk - vals[0]) for vk in vals]
    den = es[0] + es[1] + es[2] + es[3]
    gate_ref[...] = jnp.concatenate(es, axis=0) / den
    idx_ref[...] = jnp.concatenate(idxs, axis=0).astype(jnp.int32)
    onehot = jnp.zeros((n_e, tt), F32)
    for sel in sels:
        onehot = onehot + sel.astype(F32)

    mix = (jnp.dot(yh_ref[...], wo1_ref[...], preferred_element_type=F32)
           + jnp.dot(ya_ref[...], wo2_ref[...], preferred_element_type=F32))
    xn = x_ref[...] + g1_ref[...] * mix
    xn_ref[...] = xn
    h2 = _rms(xn) * n2g_ref[...]
    h2 = h2 * (1.0 + sc2_ref[...]) + sh2_ref[...]
    _store_packed_rows(h2_ref, h2)
    lg_sc[...] = lax.dot_general(rwt_ref[...], h2, (((1,), (1,)), ((), ())), precision=HIGHEST,
                                 preferred_element_type=F32) + rb_ref[...]

    prefix = jnp.dot(onehot.astype(BF16), tri_ref[...], preferred_element_type=F32) + carry_sc[...]
    ranks = [jnp.sum(jnp.where(sel, prefix, 0.0), axis=0, keepdims=True) for sel in sels]
    rank_ref[...] = jnp.concatenate(ranks, axis=0).astype(jnp.int32)
    carry_sc[...] += jnp.where(i > 0, jnp.sum(onehot, axis=1, keepdims=True), 0.0)
    cnt_ref[...] = jnp.broadcast_to(carry_sc[...], cnt_ref.shape)


def _outproj_router(y_hy, y_att, x2, g1, sh2, sc2, norm2_g, w_out, router_w, router_b, tokens_per_batch,
                    tt):
    t, d = x2.shape
    ch = y_hy.shape[1]
    n_e = router_w.shape[1]
    wo1 = w_out[:ch].astype(BF16)
    wo2 = w_out[ch:].astype(BF16)
    tri = jnp.asarray(np.triu(np.ones((tt, tt), np.float32), k=1), BF16)
    steps_per_batch = tokens_per_batch // tt
    n_tiles = t // tt
    cur = lambda i: jnp.minimum(i, n_tiles - 1)
    tok = lambda i: (cur(i), 0)
    per_b = lambda i: (cur(i) // steps_per_batch, 0, 0)
    lanes_tok = lambda i: (0, jnp.maximum(i - 1, 0))
    return pl.pallas_call(
        _outproj_body,
        grid=(n_tiles + 1,),
        in_specs=[pl.BlockSpec((tt, ch), tok), pl.BlockSpec((tt, ch), tok),
                  pl.BlockSpec((tt, d), tok),
                  pl.BlockSpec((None, 1, d), per_b), pl.BlockSpec((None, 1, d), per_b),
                  pl.BlockSpec((None, 1, d), per_b),
                  _const_spec((1, d)), _const_spec(wo1.shape), _const_spec(wo2.shape),
                  _const_spec((n_e, d)), _const_spec((n_e, 1)), _const_spec((tt, tt))],
        out_specs=[pl.BlockSpec((tt, d), tok),
                   pl.BlockSpec((d // (2 * LANES), tt, LANES), lambda i: (0, cur(i), 0)),
                   pl.BlockSpec((TOP_K, tt), lanes_tok), pl.BlockSpec((TOP_K, tt), lanes_tok),
                   pl.BlockSpec((TOP_K, tt), lanes_tok), _const_spec((n_e, LANES))],
        out_shape=[jax.ShapeDtypeStruct((t, d), F32),
                   jax.ShapeDtypeStruct((d // (2 * LANES), t, LANES), jnp.uint32),
                   jax.ShapeDtypeStruct((TOP_K, t), jnp.int32),
                   jax.ShapeDtypeStruct((TOP_K, t), F32),
                   jax.ShapeDtypeStruct((TOP_K, t), jnp.int32),
                   jax.ShapeDtypeStruct((n_e, LANES), F32)],
        scratch_shapes=[pltpu.VMEM((n_e, 1), F32), pltpu.VMEM((n_e, tt), F32)],
        compiler_params=_cparams(("arbitrary",), VMEM_LIMIT),
        name="outproj_router",
    )(y_hy, y_att, x2, g1, sh2, sc2, norm2_g.reshape(1, d), wo1, wo2, router_w.T,
      router_b.reshape(n_e, 1), tri)


def _cast_rows(src_ref, dst_ref, chunk):
    def body(c, carry):
        sl = pl.ds(pl.multiple_of(c * chunk, chunk), chunk)
        dst_ref[sl, :] = src_ref[sl, :].astype(dst_ref.dtype)
        return carry

    lax.fori_loop(0, src_ref.shape[0] // chunk, body, 0)


def _expert_body(be_ref, nvalid_ref, xs_ref, wgu_ref, bgu_ref, wd_ref, bd_ref, ys_ref, wgu_bf, wd_bf):
    i = pl.program_id(0)
    n_valid = nvalid_ref[i]
    active = n_valid > 0
    new_expert = jnp.logical_or(i == 0, be_ref[i] != be_ref[jnp.maximum(i - 1, 0)])

    @pl.when(jnp.logical_and(active, new_expert))
    def _():
        _cast_rows(wgu_ref, wgu_bf, CAST_ROW_CHUNK)
        _cast_rows(wd_ref, wd_bf, CAST_ROW_CHUNK)

    @pl.when(active)
    def _():
        row = lax.broadcasted_iota(jnp.int32, (xs_ref.shape[1], 1), 0)
        xs = jnp.where(row < n_valid, _load_packed_rows(xs_ref), 0.0).astype(BF16)
        gu = jnp.dot(xs, wgu_bf[...], preferred_element_type=F32) + bgu_ref[...]
        dff = gu.shape[1] // 2
        gate = jnp.minimum(gu[:, :dff], SWIGLU_LIMIT)
        up = jnp.clip(gu[:, dff:], -SWIGLU_LIMIT, SWIGLU_LIMIT)
        act = (up + 1.0) * (gate * jax.nn.sigmoid(SWIGLU_ALPHA * gate))
        ys = jnp.dot(act.astype(BF16), wd_bf[...], preferred_element_type=F32) + bd_ref[...]
        _store_packed_rows(ys_ref, ys)

    @pl.when(jnp.logical_not(active))
    def _():
        ys_ref[...] = jnp.zeros_like(ys_ref)


def _expert_blocks(xs, block_e, n_valid, w_gu, b_gu, w_down, b_down):
    n_seg, n_rows, _ = xs.shape
    n_e, d, dff2 = w_gu.shape
    bm = MOE_ROWS
    seg_block = pl.BlockSpec((n_seg, bm, LANES), lambda i, be, nu: (0, i, 0))
    grid_spec = pltpu.PrefetchScalarGridSpec(
        num_scalar_prefetch=2,
        grid=(n_rows // bm,),
        in_specs=[seg_block,
                  pl.BlockSpec((None, d, dff2), lambda i, be, nu: (be[i], 0, 0)),
                  pl.BlockSpec((None, 1, dff2), lambda i, be, nu: (be[i], 0, 0)),
                  pl.BlockSpec((None, dff2 // 2, d), lambda i, be, nu: (be[i], 0, 0)),
                  pl.BlockSpec((None, 1, d), lambda i, be, nu: (be[i], 0, 0))],
        out_specs=seg_block,
        scratch_shapes=[pltpu.VMEM((d, dff2), BF16), pltpu.VMEM((dff2 // 2, d), BF16)],
    )
    return pl.pallas_call(
        _expert_body,
        grid_spec=grid_spec,
        out_shape=jax.ShapeDtypeStruct(xs.shape, jnp.uint32),
        compiler_params=_cparams(("arbitrary",), VMEM_LIMIT),
        name="moe_experts",
    )(block_e, n_valid, xs, w_gu, b_gu.reshape(n_e, 1, dff2), w_down, b_down.reshape(n_e, 1, d))


def _sc_gather(table, idx):
    n = idx.shape[0]
    width = table.shape[1]
    mesh = plsc.VectorSubcoreMesh(core_axis_name="core", subcore_axis_name="subcore")
    n_workers = mesh.num_cores * mesh.num_subcores
    assert width == LANES and n % (SC_WINDOW * n_workers) == 0

    @functools.partial(pl.kernel, out_type=jax.ShapeDtypeStruct((n, width), table.dtype), mesh=mesh)
    def gather_kernel(table_hbm, idx_hbm, out_hbm):
        def body(idx_vmem, out_vmem):
            pltpu.sync_copy(table_hbm.at[idx_vmem.at[0]], out_vmem)

        pltpu.emit_pipeline(
            body,
            grid=(n // SC_WINDOW,),
            in_specs=[pl.BlockSpec((1, SC_WINDOW), lambda i: (0, i))],
            out_specs=[pl.BlockSpec((SC_WINDOW, width), lambda i: (i, 0))],
            core_axis_name=("core", "subcore"),
            dimension_semantics=(pltpu.PARALLEL,),
        )(idx_hbm, out_hbm)

    return gather_kernel(table, idx.reshape(1, n))


def _sc_scatter(rows, idx, n_copies, n_out):
    n, width = rows.shape
    mesh = plsc.VectorSubcoreMesh(core_axis_name="core", subcore_axis_name="subcore")
    n_workers = mesh.num_cores * mesh.num_subcores
    assert width == LANES and n % (SC_WINDOW * n_workers) == 0 and idx.shape == (8, n)

    @functools.partial(pl.kernel, out_type=jax.ShapeDtypeStruct((n_out, width), rows.dtype), mesh=mesh)
    def scatter_kernel(rows_hbm, idx_hbm, out_hbm):
        def body(rows_vmem, idx_vmem):
            for k in range(n_copies):
                pltpu.sync_copy(rows_vmem, out_hbm.at[idx_vmem.at[k]])

        pltpu.emit_pipeline(
            body,
            grid=(n // SC_WINDOW,),
            in_specs=[pl.BlockSpec((SC_WINDOW, width), lambda i: (i, 0)),
                      pl.BlockSpec((8, SC_WINDOW), lambda i: (0, i))],
            out_specs=[],
            core_axis_name=("core", "subcore"),
            dimension_semantics=(pltpu.PARALLEL,),
        )(rows_hbm, idx_hbm)

    return scatter_kernel(rows, idx)


def _combine_body(pk_ref, gt_ref, xn_ref, g2_ref, o_ref):
    n_seg = pk_ref.shape[0]
    half = o_ref.shape[1] // 2
    gt = gt_ref[...]
    g2 = g2_ref[...]
    for j in range(n_seg):
        acc_hi = None
        acc_lo = None
        for kk in range(TOP_K):
            hi, lo = _unpack_words(pk_ref[j, kk])
            g = gt[:, kk:kk + 1]
            acc_hi = g * hi if acc_hi is None else acc_hi + g * hi
            acc_lo = g * lo if acc_lo is None else acc_lo + g * lo
        c_hi = slice(j * LANES, (j + 1) * LANES)
        c_lo = slice(half + j * LANES, half + (j + 1) * LANES)
        o_ref[:, c_hi] = xn_ref[:, c_hi] + g2[:, c_hi] * acc_hi
        o_ref[:, c_lo] = xn_ref[:, c_lo] + g2[:, c_lo] * acc_lo


def _combine(picked, gates_t, xn, g2, tokens_per_batch, tt):
    n_seg, _, t, _ = picked.shape
    d = xn.shape[1]
    steps_per_batch = tokens_per_batch // tt
    return pl.pallas_call(
        _combine_body,
        grid=(t // tt,),
        in_specs=[pl.BlockSpec((n_seg, TOP_K, tt, LANES), lambda i: (0, 0, i, 0)),
                  pl.BlockSpec((tt, TOP_K), lambda i: (i, 0)),
                  pl.BlockSpec((tt, d), lambda i: (i, 0)),
                  pl.BlockSpec((None, 1, d), lambda i: (i // steps_per_batch, 0, 0))],
        out_specs=pl.BlockSpec((tt, d), lambda i: (i, 0)),
        out_shape=jax.ShapeDtypeStruct((t, d), F32),
        compiler_params=_cparams(("parallel",), VMEM_LIMIT),
        name="moe_combine",
    )(picked, gates_t, xn, g2)


def _moe(h2p, xn, g2, idx, gates, ranks, counts, w_gu, b_gu, w_down, b_down, tokens_per_batch):
    n_seg, t, _ = h2p.shape
    bm = MOE_ROWS
    n_e = w_gu.shape[0]
    cnt = counts[:, 0].astype(jnp.int32)
    padded = (cnt + bm - 1) // bm * bm
    padded_ends = jnp.cumsum(padded)
    padded_starts = padded_ends - padded
    experts = jnp.arange(n_e, dtype=jnp.int32)[:, None, None]
    dest = ranks + jnp.sum(jnp.where(idx[None] == experts, padded_starts[:, None, None], 0), axis=0)
    n_blocks = t * TOP_K // bm + n_e
    n_rows = n_blocks * bm
    block_start = jnp.arange(n_blocks, dtype=jnp.int32) * bm
    block_e = jnp.minimum(jnp.sum(padded_ends[None, :] <= block_start[:, None], axis=1),
                          n_e - 1).astype(jnp.int32)
    n_valid = jnp.clip(cnt[block_e] - (block_start - padded_starts[block_e]), 0, bm).astype(jnp.int32)
    seg = jnp.arange(n_seg, dtype=jnp.int32)
    scatter_idx = (seg[None, :, None] * n_rows + dest[:, None, :]).reshape(TOP_K, n_seg * t)
    scatter_idx = jnp.concatenate([scatter_idx, scatter_idx], axis=0)
    xs = _sc_scatter(h2p.reshape(n_seg * t, LANES), scatter_idx, TOP_K, n_seg * n_rows)
    ys = _expert_blocks(xs.reshape(n_seg, n_rows, LANES), block_e, n_valid, w_gu, b_gu, w_down, b_down)
    picked = _sc_gather(ys.reshape(n_seg * n_rows, LANES),
                        (seg[:, None, None] * n_rows + dest[None]).reshape(-1))
    return _combine(picked.reshape(n_seg, TOP_K, t, LANES), gates.T, xn, g2, tokens_per_batch,
                    min(TOKEN_TILE, tokens_per_batch))


def kernel(x, c, ctx, c_ctx, mod_w, mod_b, norm1_g, w_in, hy_conv_w, hy_conv_b, hy_f_w1, hy_f_b1,
           hy_f_w2, hy_f_b2, hy_f_w3, hy_f_b3, hy_f_w4, hy_f_freq, hy_bias, mla_q_norm_g, mla_w_uq,
           mla_kv_norm_g, mla_w_ukv, qk_norm_q_g, qk_norm_k_g, w_out, norm2_g, router_w, router_b,
           exp_w_gu, exp_b_gu, exp_w_down, exp_b_down):
    b, length, d = x.shape
    depth = mod_w.shape[0]
    assert depth == 1, "single-layer kernel"
    ly = 0
    c_rows = jnp.concatenate([c, c_ctx[None, :], jnp.zeros((8 - b - 1, d), F32)], axis=0)
    mod = _adaln_table(c_rows, mod_w[ly], mod_b[ly])
    mod6 = mod.reshape(8, 6, d)
    sh1, sc1, g1, sh2, sc2, g2 = (mod6[:b, j][:, None, :] for j in range(6))
    csh1 = mod6[b:b + 1, 0][:, None, :]
    csc1 = mod6[b:b + 1, 1][:, None, :]

    weights = _mla_weights(w_in[ly], mla_w_uq[ly], mla_w_ukv[ly], qk_norm_q_g[ly], qk_norm_k_g[ly])
    n_ctx = ctx.shape[1]
    _, _, k_c, v_c = _inproj(ctx, jnp.broadcast_to(csh1, (b, 1, d)), jnp.broadcast_to(csc1, (b, 1, d)),
                             norm1_g[ly], weights, mla_q_norm_g[ly], mla_kv_norm_g[ly], False, n_ctx)
    hy, q, k, v = _inproj(x, sh1, sc1, norm1_g[ly], weights, mla_q_norm_g[ly], mla_kv_norm_g[ly],
                          True, min(TOKEN_TILE, length))

    kern, asum = _hyena_kernel_taps(length, hy_f_w1[ly], hy_f_b1[ly], hy_f_w2[ly], hy_f_b2[ly],
                                    hy_f_w3[ly], hy_f_b3[ly], hy_f_w4[ly], hy_f_freq[ly])
    kf = _hyena_filter_spectrum(kern, asum)
    y_hy = _hyena_conv(hy, hy_conv_w[ly], hy_conv_b[ly], hy_bias[ly], kf)

    y_att = _flash_attention(q, k, v, k_c, v_c, min(ATTN_Q_TILE, length), min(ATTN_K_TILE, length),
                             ATTN_ROW_GROUP)

    t = b * length
    xn, h2, idx, gates, ranks, counts = _outproj_router(
        y_hy.reshape(t, -1), y_att.reshape(t, -1), x.reshape(t, d), g1, sh2, sc2, norm2_g[ly],
        w_out[ly], router_w[ly], router_b[ly], length, min(TOKEN_TILE, length))
    out = _moe(h2, xn, g2, idx, gates, ranks, counts, exp_w_gu[ly], exp_b_gu[ly],
               exp_w_down[ly], exp_b_down[ly], length)
    return out.reshape(b, length, d)
```

```python
import functools
import math

import jax
import jax.numpy as jnp
import numpy as np
from jax import lax
from jax.experimental import pallas as pl
from jax.experimental.pallas import tpu as pltpu
from jax.experimental.pallas import tpu_sc as plsc

F32 = jnp.float32
BF16 = jnp.bfloat16
HIGHEST = lax.Precision.HIGHEST

GRID_W = 64
D_HYENA = 512
FILTER_ORDER = 64
POS_EMB_DIM = 33
MIN_DECAY = math.log(1e-2) / 0.3
MAX_DECAY = math.log(1e-2) / 1.5
NOPE_DIM = 128
ROPE_DIM = 64
QK_DIM = NOPE_DIM + ROPE_DIM
V_DIM = 128
MLA_HEADS = 4
Q_RANK = 256
KV_RANK = 128
ROPE_THETA = 10000.0
N_EXPERTS = 32
TOP_K = 4
SWIGLU_ALPHA = 1.702
SWIGLU_LIMIT = 7.0
NORM_EPS = 1e-6

LANES = 128
VMEM_LIMIT = 56 * 1024 * 1024

BF16_PACK_ROWS = 16

DFT_Q = LANES
CONV_PITCH = DFT_Q + 8
MOE_ROWS = 512
SC_WINDOW = 128

TOKEN_TILE = 512
ATTN_Q_TILE = 512
ATTN_K_TILE = 512
ATTN_ROW_GROUP = 32
FILTER_TILE = 1024
CONV_ROW_CHUNK = 512
CAST_ROW_CHUNK = 128


def _cparams(sem, vmem=None):
    return pltpu.CompilerParams(dimension_semantics=sem, vmem_limit_bytes=vmem)


def _const_spec(shape):
    nd = len(shape)
    return pl.BlockSpec(shape, lambda *_: (0,) * nd)


def _single_spec(shape, index_map):
    return pl.BlockSpec(shape, index_map, pipeline_mode=pl.Buffered(1))


def _mod_body(c_ref, w_ref, b_ref, o_ref):
    cc = c_ref[...]
    s = cc * jax.nn.sigmoid(cc)
    o_ref[...] = jnp.dot(s, w_ref[...], precision=HIGHEST,
                         preferred_element_type=F32) + b_ref[...]


def _adaln_table(c_rows, mod_w, mod_b):
    rows, d = c_rows.shape
    n = mod_w.shape[1]
    tn = n // 8
    return pl.pallas_call(
        _mod_body,
        grid=(n // tn,),
        in_specs=[_const_spec((rows, d)),
                  pl.BlockSpec((d, tn), lambda j: (0, j)),
                  pl.BlockSpec((1, tn), lambda j: (0, j))],
        out_specs=pl.BlockSpec((rows, tn), lambda j: (0, j)),
        out_shape=jax.ShapeDtypeStruct((rows, n), F32),
        compiler_params=_cparams(("arbitrary",)),
        name="adaln_table",
    )(c_rows, mod_w, mod_b.reshape(1, n))


def _rms(x, eps=NORM_EPS):
    return x * lax.rsqrt(jnp.mean(x * x, axis=-1, keepdims=True) + eps)


def _inproj_body(x_ref, sh_ref, sc_ref, g_ref, why_ref, wmla_ref, qng_ref, wuq_ref,
                 kvng_ref, wukv_ref, ct_ref, st_ref, ctk_ref, gq1_ref, gq2_ref, gkn_ref,
                 gkr_ref, hy_ref, q_ref, k_ref, v_ref, mla_sc, *, q_scale):
    i = pl.program_id(0)

    @pl.when(i == 0)
    def _():
        mla_sc[...] = jnp.zeros_like(mla_sc)

    mla = mla_sc[...]
    cq = mla[:, :Q_RANK]
    ckv = mla[:, Q_RANK:Q_RANK + KV_RANK]
    pe2 = mla[:, Q_RANK + KV_RANK:]
    qf = jnp.dot((_rms(cq) * qng_ref[...]).astype(BF16), wuq_ref[...],
                 preferred_element_type=F32)
    kvf = jnp.dot((_rms(ckv) * kvng_ref[...]).astype(BF16), wukv_ref[...],
                  preferred_element_type=F32)
    ct = ct_ref[...]
    st = st_ref[...]
    lane256 = lax.broadcasted_iota(jnp.int32, (1, 2 * LANES), 1)
    qmask = (lane256 < QK_DIM).astype(F32)
    lane128 = lax.broadcasted_iota(jnp.int32, (1, LANES), 1)
    pemask = (lane128 < ROPE_DIM).astype(F32)
    kr0 = pe2 * gkr_ref[...] * ctk_ref[...]
    krs = kr0 + pltpu.roll(kr0, ROPE_DIM, axis=1)
    pem = pe2 * pemask
    ss_pe = jnp.sum(pem * pem, axis=-1, keepdims=True)
    gq1 = gq1_ref[...]
    gq2 = gq2_ref[...]
    gkn = gkn_ref[...]
    for hd in range(MLA_HEADS):
        slab = qf[:, hd * 2 * LANES:(hd + 1) * 2 * LANES]
        sm = slab * qmask
        rq = lax.rsqrt(jnp.sum(sm * sm, axis=-1, keepdims=True) / QK_DIM + NORM_EPS) * q_scale
        t = slab * gq1 * ct + pltpu.roll(slab * gq2 * st, QK_DIM, axis=1)
        q_ref[hd] = (t * rq)[:, :QK_DIM].astype(BF16)
        kn = kvf[:, hd * 2 * LANES:hd * 2 * LANES + NOPE_DIM]
        rk = lax.rsqrt((jnp.sum(kn * kn, axis=-1, keepdims=True) + ss_pe) / QK_DIM + NORM_EPS)
        kslab = jnp.concatenate([kn * gkn, krs], axis=-1) * rk
        k_ref[hd] = kslab[:, :QK_DIM].astype(BF16)
        v_ref[hd] = kvf[:, hd * 2 * LANES + NOPE_DIM:(hd + 1) * 2 * LANES].astype(BF16)

    h = _rms(x_ref[...]) * g_ref[...]
    h = h * (1.0 + sc_ref[...]) + sh_ref[...]
    hb = h.astype(BF16)
    hy_ref[...] = jnp.dot(hb, why_ref[...], preferred_element_type=F32).astype(BF16)
    mla_sc[...] = jnp.dot(hb, wmla_ref[...], preferred_element_type=F32)


@functools.lru_cache(maxsize=None)
def _rope_lane_tables(length, use_rope):
    if use_rope:
        n_freq = ROPE_DIM // 4
        t = np.arange(length)
        inv_freq = np.power(ROPE_THETA, -np.arange(n_freq, dtype=np.float64) / n_freq)
        ar = (t // GRID_W).astype(np.float64)[:, None] * inv_freq
        ac = (t % GRID_W).astype(np.float64)[:, None] * inv_freq
        c64 = np.concatenate([np.cos(ar), np.cos(ar), np.cos(ac), np.cos(ac)], axis=-1)
        s64 = np.concatenate([-np.sin(ar), np.sin(ar), -np.sin(ac), np.sin(ac)], axis=-1)
    else:
        c64 = np.ones((length, ROPE_DIM))
        s64 = np.zeros((length, ROPE_DIM))
    z64 = np.zeros((length, ROPE_DIM))
    ct = np.concatenate([np.ones((length, LANES)), c64, z64], axis=-1)
    st = np.concatenate([np.zeros((length, LANES)), z64, s64], axis=-1)
    ctk = np.concatenate([c64, s64], axis=-1)
    return ct.astype(np.float32), st.astype(np.float32), ctk.astype(np.float32)


_SWAP16 = np.concatenate([np.arange(16, 32), np.arange(0, 16), np.arange(48, 64), np.arange(32, 48)])


def _mla_weights(w_in, mla_w_uq, mla_w_ukv, qk_norm_q_g, qk_norm_k_g):
    hy_cols = 3 * D_HYENA
    w_hy = w_in[:, :hy_cols].astype(BF16)
    w_pe = w_in[:, hy_cols + Q_RANK + KV_RANK:]
    w_mla = jnp.concatenate([w_in[:, hy_cols:hy_cols + Q_RANK + KV_RANK], w_pe, w_pe[:, _SWAP16]],
                            axis=-1).astype(BF16)
    wq = mla_w_uq.reshape(Q_RANK, MLA_HEADS, QK_DIM)
    wq_rope = wq[:, :, NOPE_DIM:]
    w_uq2 = jnp.concatenate([wq[:, :, :NOPE_DIM], wq_rope, wq_rope[:, :, _SWAP16]], axis=-1)
    w_uq2 = w_uq2.reshape(Q_RANK, MLA_HEADS * 2 * LANES).astype(BF16)
    w_ukv2 = mla_w_ukv.astype(BF16)
    gq_r = qk_norm_q_g[NOPE_DIM:]
    z64 = jnp.zeros((ROPE_DIM,), F32)
    gq1 = jnp.concatenate([qk_norm_q_g[:NOPE_DIM], gq_r, z64]).reshape(1, -1)
    gq2 = jnp.concatenate([jnp.zeros((NOPE_DIM,), F32), z64, gq_r[_SWAP16]]).reshape(1, -1)
    gkn = qk_norm_k_g[:NOPE_DIM].reshape(1, -1)
    gk_r = qk_norm_k_g[NOPE_DIM:]
    gkr = jnp.concatenate([gk_r, gk_r[_SWAP16]]).reshape(1, -1)
    return w_hy, w_mla, w_uq2, w_ukv2, gq1, gq2, gkn, gkr


def _inproj(x, shift, scale, norm_g, weights, q_norm_g, kv_norm_g, use_rope, tl):
    b, length, d = x.shape
    w_hy, w_mla, w_uq2, w_ukv2, gq1, gq2, gkn, gkr = weights
    ct, st, ctk = _rope_lane_tables(length, use_rope)
    q_scale = QK_DIM ** -0.5 * math.log2(math.e)
    nt = length // tl
    n_tiles = b * nt
    hyc = w_hy.shape[1]
    cur = lambda i: jnp.minimum(i, n_tiles - 1)
    prev = lambda i: jnp.maximum(i - 1, 0)
    tok = lambda i: (cur(i), 0)
    per_b = lambda i: (cur(i) // nt, 0, 0)
    pos = lambda i: (prev(i) % nt, 0)
    head_blk = lambda i: (prev(i) // nt, 0, prev(i) % nt, 0)
    hy, q, k, v = pl.pallas_call(
        functools.partial(_inproj_body, q_scale=q_scale),
        grid=(n_tiles + 1,),
        in_specs=[pl.BlockSpec((tl, d), tok),
                  pl.BlockSpec((None, 1, d), per_b),
                  pl.BlockSpec((None, 1, d), per_b),
                  _const_spec((1, d)),
                  _const_spec(w_hy.shape), _const_spec(w_mla.shape),
                  _const_spec((1, Q_RANK)), _const_spec(w_uq2.shape),
                  _const_spec((1, KV_RANK)), _const_spec(w_ukv2.shape),
                  pl.BlockSpec((tl, 2 * LANES), pos), pl.BlockSpec((tl, 2 * LANES), pos),
                  pl.BlockSpec((tl, LANES), pos),
                  _const_spec((1, 2 * LANES)), _const_spec((1, 2 * LANES)),
                  _const_spec((1, LANES)), _const_spec((1, LANES))],
        out_specs=[pl.BlockSpec((tl, hyc), tok),
                   pl.BlockSpec((None, MLA_HEADS, tl, QK_DIM), head_blk),
                   pl.BlockSpec((None, MLA_HEADS, tl, QK_DIM), head_blk),
                   pl.BlockSpec((None, MLA_HEADS, tl, V_DIM), head_blk)],
        out_shape=[jax.ShapeDtypeStruct((b * length, hyc), BF16),
                   jax.ShapeDtypeStruct((b, MLA_HEADS, length, QK_DIM), BF16),
                   jax.ShapeDtypeStruct((b, MLA_HEADS, length, QK_DIM), BF16),
                   jax.ShapeDtypeStruct((b, MLA_HEADS, length, V_DIM), BF16)],
        scratch_shapes=[pltpu.VMEM((tl, w_mla.shape[1]), F32)],
        compiler_params=_cparams(("arbitrary",), VMEM_LIMIT),
        name="inproj_mla",
    )(x.reshape(b * length, d), shift, scale, norm_g.reshape(1, d), w_hy, w_mla,
      q_norm_g.reshape(1, -1), w_uq2, kv_norm_g.reshape(1, -1), w_ukv2, ct, st, ctk, gq1, gq2, gkn, gkr)
    return hy.reshape(b, length, hyc), q, k, v


def _filter_body(z_ref, w1_ref, b1_ref, w2_ref, b2_ref, w3_ref, b3_ref, w4_ref, fr_ref, dl_ref,
                 kern_ref, asum_ref, *, zero_row, tr):
    i = pl.program_id(0)
    fr = fr_ref[...]
    hr = tr // 2
    za = z_ref[:hr, :]
    zb = z_ref[hr:, :]
    dot = functools.partial(jnp.dot, precision=HIGHEST, preferred_element_type=F32)
    h = jnp.sin(fr * (dot(jnp.concatenate([za, zb], axis=1), w1_ref[...]) + b1_ref[...]))
    h = jnp.sin(fr * (dot(h, w2_ref[...]) + b2_ref[...]))
    h = jnp.sin(fr * (dot(h, w3_ref[...]) + b3_ref[...]))
    o2 = dot(h, w4_ref[...])
    nc = o2.shape[1] // 2
    oa = o2[:, :nc] * jnp.exp(-za[:, 0:1] * dl_ref[...])
    ob = o2[:, nc:] * jnp.exp(-zb[:, 0:1] * dl_ref[...])

    @pl.when(i == 0)
    def _():
        asum_ref[...] = jnp.zeros_like(asum_ref)

    asum_ref[...] += (jnp.sum(jnp.abs(oa), axis=0, keepdims=True)
                      + jnp.sum(jnp.abs(ob), axis=0, keepdims=True))
    row = i * tr + lax.broadcasted_iota(jnp.int32, (hr, 1), 0)
    ka = jnp.where(row == zero_row, 0.0, oa)
    kb = jnp.where(row + hr == zero_row, 0.0, ob)
    slabs_per_half = hr // DFT_Q
    pad = jnp.zeros((CONV_PITCH - DFT_Q, ka.shape[1]), F32)
    for j in range(2 * slabs_per_half):
        src = ka if j < slabs_per_half else kb
        lo = (j % slabs_per_half) * DFT_Q
        kern_ref[j * CONV_PITCH:j * CONV_PITCH + DFT_Q, :] = src[lo:lo + DFT_Q]
        kern_ref[j * CONV_PITCH + DFT_Q:(j + 1) * CONV_PITCH, :] = pad


@functools.lru_cache(maxsize=None)
def _filter_features(length):
    n = 2 * length
    bands = (POS_EMB_DIM - 1) // 2
    pos = np.concatenate([np.arange(length), (n - np.arange(length, n)) % length])
    t_tab = np.linspace(0.0, 1.0, length)[:, None]
    w_ang = 2.0 * np.pi * np.arange(length, dtype=np.float64)[:, None] / length
    f = np.linspace(1e-4, bands - 1, bands)[None, :]
    z_tab = np.concatenate([t_tab, np.cos(f * w_ang), -np.sin(f * w_ang)], axis=-1)
    z = np.pad(z_tab[pos], ((0, 0), (0, LANES - POS_EMB_DIM)))
    return z.astype(np.float32)


def _hyena_kernel_taps(length, w1, b1, w2, b2, w3, b3, w4, freq):
    n = 2 * length
    z = jnp.asarray(_filter_features(length))
    deltas = jnp.abs(jnp.linspace(MIN_DECAY, MAX_DECAY, D_HYENA, dtype=F32)).reshape(1, -1)
    tr = min(FILTER_TILE, length)
    half_steps = length // tr
    fo = FILTER_ORDER

    def pair(w):
        zeros = jnp.zeros_like(w)
        return jnp.concatenate([jnp.concatenate([w, zeros], axis=1),
                                jnp.concatenate([zeros, w], axis=1)], axis=0)

    def twice(v):
        return jnp.concatenate([v, v]).reshape(1, 2 * fo)

    w1p = pair(jnp.pad(w1, ((0, LANES - POS_EMB_DIM), (0, 0))))
    w4p = jnp.stack([pair(w4[:, :D_HYENA]), pair(w4[:, D_HYENA:])])
    kern, asum = pl.pallas_call(
        functools.partial(_filter_body, zero_row=length, tr=tr),
        grid=(n // tr,),
        in_specs=[pl.BlockSpec((tr, LANES), lambda i: (i, 0)),
                  _const_spec((2 * LANES, 2 * fo)), _const_spec((1, 2 * fo)),
                  _const_spec((2 * fo, 2 * fo)), _const_spec((1, 2 * fo)),
                  _const_spec((2 * fo, 2 * fo)), _const_spec((1, 2 * fo)),
                  pl.BlockSpec((None, 2 * fo, 2 * D_HYENA), lambda i: (i // half_steps, 0, 0)),
                  _const_spec((1, 2 * fo)), _const_spec((1, D_HYENA))],
        out_specs=[pl.BlockSpec((tr // DFT_Q * CONV_PITCH, D_HYENA), lambda i: (i, 0)),
                   _const_spec((1, D_HYENA))],
        out_shape=[jax.ShapeDtypeStruct((n // DFT_Q * CONV_PITCH, D_HYENA), F32),
                   jax.ShapeDtypeStruct((1, D_HYENA), F32)],
        compiler_params=_cparams(("arbitrary",), VMEM_LIMIT),
        name="hyena_filter",
    )(z, w1p, twice(b1), pair(w2), twice(b2), pair(w3), twice(b3), w4p, twice(freq), deltas)
    return kern, asum


def _half_rows(n):
    n_half = n // DFT_Q // 2 + 1
    return n_half, -(-n_half // 8) * 8


@functools.lru_cache(maxsize=None)
def _dft_tables(n, p_in):
    q_sz = DFT_Q
    p_sz = n // q_sz
    n_half, n_r = _half_rows(n)
    r = np.arange(n_r, dtype=np.float64)
    keep = (r < n_half).astype(np.float64)
    qq = np.arange(q_sz, dtype=np.float64)
    pp = np.arange(p_in, dtype=np.float64)
    tt = q_sz * pp[None, None, :] + qq[:, None, None]
    ang = -2.0 * np.pi * r[None, :, None] * tt / n
    t1 = np.concatenate([np.cos(ang), np.sin(ang)], axis=1) * np.tile(keep, 2)[None, :, None]
    mirror = np.where((r == 0) | (r == p_sz // 2), 1.0, 2.0)
    t4 = np.transpose(t1 * np.tile(mirror, 2)[None, :, None], (0, 2, 1)) / n
    a2 = -2.0 * np.pi * np.outer(qq, qq) / q_sz
    fre, fim = np.cos(a2), np.sin(a2)
    m2 = np.block([[fre, -fim], [fim, fre]])
    m2c = np.block([[fre, fim], [-fim, fre]])
    t1 = np.concatenate([t1[0::2], t1[1::2]], axis=2)
    return (t1.astype(np.float32), t4.astype(np.float32), m2.astype(np.float32),
            m2c.astype(np.float32))


SUBLANES = 8


def _spec_block(rg, h):
    return (rg * 2 + h) * SUBLANES * DFT_Q


def _dft_stage1(x_ref, t1_ref, spec_ref, n_r, p_in, pitch=DFT_Q):
    q_sz = DFT_Q
    ct = spec_ref.shape[1]

    def body(j, carry):
        q0 = 2 * j
        x0 = x_ref[pl.ds(q0, p_in, stride=pitch), :].astype(BF16)
        x1 = x_ref[pl.ds(q0 + 1, p_in, stride=pitch), :].astype(BF16)
        zeros = jnp.zeros_like(x0)
        rhs = jnp.concatenate([jnp.concatenate([x0, zeros], axis=1),
                               jnp.concatenate([zeros, x1], axis=1)], axis=0)
        a = jnp.dot(t1_ref[j], rhs, preferred_element_type=F32)
        for k in range(2):
            row = pl.multiple_of((q0 + k) * SUBLANES, SUBLANES)
            for h in range(2):
                for rg in range(n_r // SUBLANES):
                    src = h * n_r + rg * SUBLANES
                    spec_ref[pl.ds(_spec_block(rg, h) + row, SUBLANES), :] = (
                        a[src:src + SUBLANES, k * ct:(k + 1) * ct])
        return carry

    lax.fori_loop(0, q_sz // 2, body, 0, unroll=4)


def _spec_rows(rg, h, r8):
    return pl.ds(_spec_block(rg, h) + r8, DFT_Q, stride=SUBLANES)


def _spectrum_body(kern_ref, asum_ref, t1_ref, m2_ref, kf_ref, spec_ref, *, n_r, p_in):
    _dft_stage1(kern_ref, t1_ref, spec_ref, n_r, p_in, CONV_PITCH)
    inv = 1.0 / asum_ref[...]
    m2 = m2_ref[...]

    ct = spec_ref.shape[1]

    def body(rg, carry):
        for r8 in range(0, SUBLANES, 2):
            blk = jnp.concatenate(
                [jnp.concatenate([spec_ref[_spec_rows(rg, 0, r8 + j), :],
                                  spec_ref[_spec_rows(rg, 1, r8 + j), :]], axis=0) for j in range(2)],
                axis=1)
            xf = jnp.dot(m2, blk.astype(BF16), preferred_element_type=F32)
            for j in range(2):
                kf_ref[rg * SUBLANES + r8 + j] = (xf[:, j * ct:(j + 1) * ct] * inv).astype(BF16)
        return carry

    lax.fori_loop(0, n_r // SUBLANES, body, 0)


def _hyena_filter_spectrum(kern, asum):
    n_pitched, c = kern.shape
    p_sz = n_pitched // CONV_PITCH
    n = p_sz * DFT_Q
    _, n_r = _half_rows(n)
    t1, _, m2, _ = _dft_tables(n, p_sz)
    t1 = jnp.asarray(t1).astype(BF16)
    m2 = jnp.asarray(m2).astype(BF16)
    ct = LANES
    return pl.pallas_call(
        functools.partial(_spectrum_body, n_r=n_r, p_in=p_sz),
        grid=(c // ct,),
        in_specs=[_single_spec((n_pitched, ct), lambda j: (0, j)),
                  pl.BlockSpec((1, ct), lambda j: (0, j)),
                  _single_spec(t1.shape, lambda j: (0, 0, 0)),
                  _const_spec(m2.shape)],
        out_specs=pl.BlockSpec((n_r, 2 * DFT_Q, ct), lambda j: (0, 0, j)),
        out_shape=jax.ShapeDtypeStruct((n_r, 2 * DFT_Q, c), BF16),
        scratch_shapes=[pltpu.VMEM((n_r * 2 * DFT_Q, ct), F32)],
        compiler_params=_cparams(("parallel",), VMEM_LIMIT),
        name="hyena_filter_spectrum",
    )(kern, asum, t1, m2)


def _short_conv_chunk(u_ref, w_ref, b_ref, i, rows, length):
    pack = BF16_PACK_ROWS
    base = pl.multiple_of(i * rows, rows)
    u = u_ref[pl.ds(base, rows), :].astype(F32)
    lo = pl.multiple_of(jnp.maximum(base - pack, 0), pack)
    hi = pl.multiple_of(jnp.minimum(base + rows, length - pack), pack)
    prev = u_ref[pl.ds(lo, pack), :].astype(F32)[pack - 1:pack]
    nxt = u_ref[pl.ds(hi, pack), :].astype(F32)[0:1]
    prev = jnp.where(base == 0, 0.0, prev)
    nxt = jnp.where(base + rows == length, 0.0, nxt)
    ridx = lax.broadcasted_iota(jnp.int32, (rows, 1), 0)
    up = jnp.where(ridx == 0, prev, pltpu.roll(u, 1, axis=0))
    dn = jnp.where(ridx == rows - 1, nxt, pltpu.roll(u, rows - 1, axis=0))
    w = w_ref[...]
    return up * w[0:1] + u * w[1:2] + dn * w[2:3] + b_ref[...]


def _hyena_conv_body(x0_ref, x1_ref, v_ref, w0_ref, w1_ref, wv_ref, b0_ref, b1_ref, bv_ref,
                     bias_ref, kf_ref, t1_ref, t4_ref, m2_ref, m2c_ref, o_ref, vx_ref, spec_ref,
                     *, length, rows):
    q_sz = DFT_Q
    n_half, n_r = _half_rows(2 * length)
    p_in = length // q_sz
    n_chunks = length // rows

    p_per_chunk = rows // q_sz

    def vx_rows(i, j):
        return pl.ds(pl.multiple_of((i * p_per_chunk + j) * CONV_PITCH, SUBLANES), q_sz)

    def gate_in(i, carry):
        x1c = _short_conv_chunk(x1_ref, w1_ref, b1_ref, i, rows, length)
        vc = _short_conv_chunk(v_ref, wv_ref, bv_ref, i, rows, length)
        vx = vc * x1c
        for j in range(p_per_chunk):
            vx_ref[vx_rows(i, j), :] = vx[j * q_sz:(j + 1) * q_sz]
        return carry

    lax.fori_loop(0, n_chunks, gate_in, 0)
    _dft_stage1(vx_ref, t1_ref, spec_ref, n_r, p_in, CONV_PITCH)
    m2 = m2_ref[...]
    m2c = m2c_ref[...]

    ct = spec_ref.shape[1]

    def forward_rows(rg, r8s):
        blk = jnp.concatenate(
            [jnp.concatenate([spec_ref[_spec_rows(rg, 0, r8), :], spec_ref[_spec_rows(rg, 1, r8), :]],
                             axis=0) for r8 in r8s], axis=1)
        return jnp.dot(m2, blk.astype(BF16), preferred_element_type=F32)

    def filter_inverse_rows(rg, r8s, xf):
        kf = jnp.concatenate([kf_ref[rg * SUBLANES + r8] for r8 in r8s], axis=1).astype(F32)
        xre, xim = xf[:q_sz], xf[q_sz:]
        kre, kim = kf[:q_sz], kf[q_sz:]
        z = jnp.concatenate([xre * kre - xim * kim, xre * kim + xim * kre], axis=0)
        bf = jnp.dot(m2c, z.astype(BF16), preferred_element_type=F32)
        for j, r8 in enumerate(r8s):
            spec_ref[_spec_rows(rg, 0, r8), :] = bf[:q_sz, j * ct:(j + 1) * ct]
            spec_ref[_spec_rows(rg, 1, r8), :] = bf[q_sz:, j * ct:(j + 1) * ct]

    def mid_rows(rg, n_rows):
        groups = [tuple(range(a, min(a + 2, n_rows))) for a in range(0, n_rows, 2)]
        pending = None
        for r8s in groups:
            xf = forward_rows(rg, r8s)
            if pending is not None:
                filter_inverse_rows(rg, *pending)
            pending = (r8s, xf)
        filter_inverse_rows(rg, *pending)

    def mid(rg, carry):
        mid_rows(rg, SUBLANES)
        return carry

    lax.fori_loop(0, n_half // SUBLANES, mid, 0)
    if n_half % SUBLANES:
        mid_rows(n_half // SUBLANES, n_half % SUBLANES)
    bias = bias_ref[...]

    def last(q, carry):
        row = pl.multiple_of(q * SUBLANES, SUBLANES)
        tiles = [spec_ref[pl.ds(_spec_block(rg, h) + row, SUBLANES), :]
                 for h in range(2) for rg in range(n_r // SUBLANES)]
        bq = jnp.concatenate(tiles, axis=0).astype(BF16)
        y = jnp.dot(t4_ref[q], bq, preferred_element_type=F32)
        sl = pl.ds(q, p_in, stride=CONV_PITCH)
        vx_ref[sl, :] = y + bias * vx_ref[sl, :]
        return carry

    lax.fori_loop(0, q_sz, last, 0, unroll=8)

    def gate_out(i, carry):
        x0c = _short_conv_chunk(x0_ref, w0_ref, b0_ref, i, rows, length)
        sl = pl.ds(pl.multiple_of(i * rows, rows), rows)
        yc = jnp.concatenate([vx_ref[vx_rows(i, j), :] for j in range(p_per_chunk)], axis=0)
        o_ref[sl, :] = (yc * x0c).astype(BF16)
        return carry

    lax.fori_loop(0, n_chunks, gate_out, 0)


def _hyena_conv(hy, conv_w, conv_b, bias, kf):
    b, length, _ = hy.shape
    ch = D_HYENA
    ct = LANES
    nct = ch // ct
    n = 2 * length
    _, n_r = _half_rows(n)
    p_in = length // DFT_Q
    t1, t4, m2, m2c = _dft_tables(n, p_in)
    t1, t4, m2, m2c = (jnp.asarray(a).astype(BF16) for a in (t1, t4, m2, m2c))
    rows = min(CONV_ROW_CHUNK, length)
    col = lambda off: (lambda j, bi: (bi, 0, off * nct + j))
    wcol = lambda off: (lambda j, bi: (0, off * nct + j))
    conv_b2 = conv_b.reshape(1, -1)
    return pl.pallas_call(
        functools.partial(_hyena_conv_body, length=length, rows=rows),
        grid=(nct, b),
        in_specs=[_single_spec((None, length, ct), col(0)),
                  _single_spec((None, length, ct), col(1)),
                  _single_spec((None, length, ct), col(2)),
                  pl.BlockSpec((3, ct), wcol(0)), pl.BlockSpec((3, ct), wcol(1)),
                  pl.BlockSpec((3, ct), wcol(2)),
                  pl.BlockSpec((1, ct), wcol(0)), pl.BlockSpec((1, ct), wcol(1)),
                  pl.BlockSpec((1, ct), wcol(2)),
                  pl.BlockSpec((1, ct), lambda j, bi: (0, j)),
                  _single_spec((n_r, 2 * DFT_Q, ct), lambda j, bi: (0, 0, j)),
                  _single_spec(t1.shape, lambda j, bi: (0, 0, 0)),
                  _single_spec(t4.shape, lambda j, bi: (0, 0, 0)),
                  _const_spec(m2.shape), _const_spec(m2c.shape)],
        out_specs=pl.BlockSpec((None, length, ct), lambda j, bi: (bi, 0, j)),
        out_shape=jax.ShapeDtypeStruct((b, length, ch), BF16),
        scratch_shapes=[pltpu.VMEM((p_in * CONV_PITCH, ct), F32),
                        pltpu.VMEM((n_r * 2 * DFT_Q, ct), F32)],
        compiler_params=_cparams(("parallel", "parallel"), VMEM_LIMIT),
        name="hyena_conv",
    )(hy, hy, hy, conv_w, conv_w, conv_w, conv_b2, conv_b2, conv_b2, bias.reshape(1, ch), kf,
      t1, t4, m2, m2c)


def _flash_body(q_ref, k_ref, v_ref, kc_ref, vc_ref, o_ref, s_ref, p_ref, al_ref, m_ref, acc_ref,
                *, tk, rg):
    tq = q_ref.shape[0]
    nk = k_ref.shape[0] // tk
    nc = kc_ref.shape[0]
    n_chunks = nk + 1
    nt = (((1,), (1,)), ((), ()))

    def width(c):
        return nc if c == nk else tk

    def scores(c):
        keys = kc_ref[...] if c == nk else k_ref[c * tk:(c + 1) * tk, :]
        s_ref[c % 3, :, :width(c)] = lax.dot_general(q_ref[...], keys, nt, preferred_element_type=F32)

    def weighted_values(c):
        vals = vc_ref[...] if c == nk else v_ref[c * tk:(c + 1) * tk, :]
        lane = lax.broadcasted_iota(jnp.int32, vals.shape, 1)
        ones_col = jnp.where(lane == 0, 1.0, 0.0).astype(BF16)
        ext = jnp.concatenate([vals, ones_col], axis=1)
        return jnp.dot(p_ref[c % 2, :, :width(c)], ext, preferred_element_type=F32)

    def softmax(c):
        w = width(c)
        for g in range(tq // rg):
            rows = slice(g * rg, (g + 1) * rg)
            s = s_ref[c % 3, rows, :w]
            m_prev = m_ref[rows, :]
            m_new = jnp.maximum(m_prev, jnp.max(s, axis=-1, keepdims=True))
            al_ref[rows, :] = jnp.exp2(m_prev - m_new)
            m_ref[rows, :] = m_new
            d = (s - jnp.concatenate([m_new] * (w // LANES), axis=1)).astype(BF16)
            p_ref[c % 2, rows, :w] = jnp.exp2(d)

    def rescale(prod):
        alpha = al_ref[...]
        acc_ref[...] = jnp.concatenate([alpha, alpha], axis=1) * (acc_ref[...] + prod)

    m_ref[...] = jnp.full_like(m_ref, -jnp.inf)
    acc_ref[...] = jnp.zeros_like(acc_ref)
    scores(0)
    for c in range(n_chunks):
        prod = weighted_values(c - 1) if c >= 1 else None
        if c + 1 < n_chunks:
            scores(c + 1)
        softmax(c)
        if prod is not None:
            rescale(prod)
    total = acc_ref[...] + weighted_values(n_chunks - 1)
    o_ref[...] = (total[:, :V_DIM] / total[:, V_DIM:V_DIM + 1]).astype(o_ref.dtype)


def _flash_attention(q, k, v, kc, vc, tq, tk, rg):
    b, hds, length, _ = q.shape
    nc = kc.shape[2]
    assert length % tk == 0 and nc <= tk and nc % LANES == 0 and V_DIM == LANES
    return pl.pallas_call(
        functools.partial(_flash_body, tk=tk, rg=rg),
        grid=(b, hds, length // tq),
        in_specs=[pl.BlockSpec((None, None, tq, QK_DIM), lambda bi, h, qi: (bi, h, qi, 0)),
                  pl.BlockSpec((None, None, length, QK_DIM), lambda bi, h, qi: (bi, h, 0, 0)),
                  pl.BlockSpec((None, None, length, V_DIM), lambda bi, h, qi: (bi, h, 0, 0)),
                  pl.BlockSpec((None, None, nc, QK_DIM), lambda bi, h, qi: (bi, h, 0, 0)),
                  pl.BlockSpec((None, None, nc, V_DIM), lambda bi, h, qi: (bi, h, 0, 0))],
        out_specs=pl.BlockSpec((None, tq, V_DIM), lambda bi, h, qi: (bi, qi, h)),
        out_shape=jax.ShapeDtypeStruct((b, length, hds * V_DIM), BF16),
        scratch_shapes=[pltpu.VMEM((3, tq, tk), F32), pltpu.VMEM((2, tq, tk), BF16),
                        pltpu.VMEM((tq, LANES), F32), pltpu.VMEM((tq, LANES), F32),
                        pltpu.VMEM((tq, 2 * V_DIM), F32)],
        compiler_params=_cparams(("parallel", "parallel", "parallel"), VMEM_LIMIT),
        name="mla_flash_attention",
    )(q, k, v, kc, vc)


def _store_packed_rows(dst_ref, x):
    half = x.shape[1] // 2
    for j in range(half // LANES):
        hi = x[:, j * LANES:(j + 1) * LANES].astype(BF16).astype(F32)
        lo = x[:, half + j * LANES:half + (j + 1) * LANES].astype(BF16).astype(F32)
        dst_ref[j] = (lax.bitcast_convert_type(hi, jnp.uint32)
                      | (lax.bitcast_convert_type(lo, jnp.uint32) >> 16))


def _unpack_words(w):
    hi = lax.bitcast_convert_type(w & jnp.uint32(0xFFFF0000), F32)
    lo = lax.bitcast_convert_type(w << 16, F32)
    return hi, lo


def _load_packed_rows(src_ref):
    parts = [_unpack_words(src_ref[j]) for j in range(src_ref.shape[0])]
    return jnp.concatenate([p[0] for p in parts] + [p[1] for p in parts], axis=-1)


def _outproj_body(yh_ref, ya_ref, x_ref, g1_ref, sh2_ref, sc2_ref, n2g_ref, wo1_ref, wo2_ref,
                  rwt_ref, rb_ref, tri_ref, xn_ref, h2_ref, idx_ref, gate_ref, rank_ref, cnt_ref,
                  carry_sc, lg_sc):
    i = pl.program_id(0)

    @pl.when(i == 0)
    def _():
        carry_sc[...] = jnp.zeros_like(carry_sc)
        lg_sc[...] = jnp.zeros_like(lg_sc)

    logits = lg_sc[...]
    n_e, tt = logits.shape
    eidx = lax.broadcasted_iota(jnp.int32, (n_e, tt), 0).astype(F32)
    work = logits
    vals, sels, idxs = [], [], []
    for _ in range(TOP_K):
        m = jnp.max(work, axis=0, keepdims=True)
        ix = jnp.min(jnp.where(work == m, eidx, float(n_e)), axis=0, keepdims=True)
        sel = eidx == ix
        work = jnp.where(sel, -jnp.inf, work)
        vals.append(m)
        idxs.append(ix)
        sels.append(sel)
    es = [jnp.exp(vk - vals[0]) for vk in vals]
    den = es[0] + es[1] + es[2] + es[3]
    gate_ref[...] = jnp.concatenate(es, axis=0) / den
    idx_ref[...] = jnp.concatenate(idxs, axis=0).astype(jnp.int32)
    onehot = jnp.zeros((n_e, tt), F32)
    for sel in sels:
        onehot = onehot + sel.astype(F32)

    mix = (jnp.dot(yh_ref[...], wo1_ref[...], preferred_element_type=F32)
           + jnp.dot(ya_ref[...], wo2_ref[...], preferred_element_type=F32))
    xn = x_ref[...] + g1_ref[...] * mix
    xn_ref[...] = xn
    h2 = _rms(xn) * n2g_ref[...]
    h2 = h2 * (1.0 + sc2_ref[...]) + sh2_ref[...]
    _store_packed_rows(h2_ref, h2)
    lg_sc[...] = lax.dot_general(rwt_ref[...], h2, (((1,), (1,)), ((), ())), precision=HIGHEST,
                                 preferred_element_type=F32) + rb_ref[...]

    prefix = jnp.dot(onehot.astype(BF16), tri_ref[...], preferred_element_type=F32) + carry_sc[...]
    ranks = [jnp.sum(jnp.where(sel, prefix, 0.0), axis=0, keepdims=True) for sel in sels]
    rank_ref[...] = jnp.concatenate(ranks, axis=0).astype(jnp.int32)
    carry_sc[...] += jnp.where(i > 0, jnp.sum(onehot, axis=1, keepdims=True), 0.0)
    cnt_ref[...] = jnp.broadcast_to(carry_sc[...], cnt_ref.shape)


def _outproj_router(y_hy, y_att, x2, g1, sh2, sc2, norm2_g, w_out, router_w, router_b, tokens_per_batch,
                    tt):
    t, d = x2.shape
    ch = y_hy.shape[1]
    n_e = router_w.shape[1]
    wo1 = w_out[:ch].astype(BF16)
    wo2 = w_out[ch:].astype(BF16)
    tri = jnp.asarray(np.triu(np.ones((tt, tt), np.float32), k=1), BF16)
    steps_per_batch = tokens_per_batch // tt
    n_tiles = t // tt
    cur = lambda i: jnp.minimum(i, n_tiles - 1)
    tok = lambda i: (cur(i), 0)
    per_b = lambda i: (cur(i) // steps_per_batch, 0, 0)
    lanes_tok = lambda i: (0, jnp.maximum(i - 1, 0))
    return pl.pallas_call(
        _outproj_body,
        grid=(n_tiles + 1,),
        in_specs=[pl.BlockSpec((tt, ch), tok), pl.BlockSpec((tt, ch), tok),
                  pl.BlockSpec((tt, d), tok),
                  pl.BlockSpec((None, 1, d), per_b), pl.BlockSpec((None, 1, d), per_b),
                  pl.BlockSpec((None, 1, d), per_b),
                  _const_spec((1, d)), _const_spec(wo1.shape), _const_spec(wo2.shape),
                  _const_spec((n_e, d)), _const_spec((n_e, 1)), _const_spec((tt, tt))],
        out_specs=[pl.BlockSpec((tt, d), tok),
                   pl.BlockSpec((d // (2 * LANES), tt, LANES), lambda i: (0, cur(i), 0)),
                   pl.BlockSpec((TOP_K, tt), lanes_tok), pl.BlockSpec((TOP_K, tt), lanes_tok),
                   pl.BlockSpec((TOP_K, tt), lanes_tok), _const_spec((n_e, LANES))],
        out_shape=[jax.ShapeDtypeStruct((t, d), F32),
                   jax.ShapeDtypeStruct((d // (2 * LANES), t, LANES), jnp.uint32),
                   jax.ShapeDtypeStruct((TOP_K, t), jnp.int32),
                   jax.ShapeDtypeStruct((TOP_K, t), F32),
                   jax.ShapeDtypeStruct((TOP_K, t), jnp.int32),
                   jax.ShapeDtypeStruct((n_e, LANES), F32)],
        scratch_shapes=[pltpu.VMEM((n_e, 1), F32), pltpu.VMEM((n_e, tt), F32)],
        compiler_params=_cparams(("arbitrary",), VMEM_LIMIT),
        name="outproj_router",
    )(y_hy, y_att, x2, g1, sh2, sc2, norm2_g.reshape(1, d), wo1, wo2, router_w.T,
      router_b.reshape(n_e, 1), tri)


def _cast_rows(src_ref, dst_ref, chunk):
    def body(c, carry):
        sl = pl.ds(pl.multiple_of(c * chunk, chunk), chunk)
        dst_ref[sl, :] = src_ref[sl, :].astype(dst_ref.dtype)
        return carry

    lax.fori_loop(0, src_ref.shape[0] // chunk, body, 0)


def _expert_body(be_ref, nvalid_ref, xs_ref, wgu_ref, bgu_ref, wd_ref, bd_ref, ys_ref, wgu_bf, wd_bf):
    i = pl.program_id(0)
    n_valid = nvalid_ref[i]
    active = n_valid > 0
    new_expert = jnp.logical_or(i == 0, be_ref[i] != be_ref[jnp.maximum(i - 1, 0)])

    @pl.when(jnp.logical_and(active, new_expert))
    def _():
        _cast_rows(wgu_ref, wgu_bf, CAST_ROW_CHUNK)
        _cast_rows(wd_ref, wd_bf, CAST_ROW_CHUNK)

    @pl.when(active)
    def _():
        row = lax.broadcasted_iota(jnp.int32, (xs_ref.shape[1], 1), 0)
        xs = jnp.where(row < n_valid, _load_packed_rows(xs_ref), 0.0).astype(BF16)
        gu = jnp.dot(xs, wgu_bf[...], preferred_element_type=F32) + bgu_ref[...]
        dff = gu.shape[1] // 2
        gate = jnp.minimum(gu[:, :dff], SWIGLU_LIMIT)
        up = jnp.clip(gu[:, dff:], -SWIGLU_LIMIT, SWIGLU_LIMIT)
        act = (up + 1.0) * (gate * jax.nn.sigmoid(SWIGLU_ALPHA * gate))
        ys = jnp.dot(act.astype(BF16), wd_bf[...], preferred_element_type=F32) + bd_ref[...]
        _store_packed_rows(ys_ref, ys)

    @pl.when(jnp.logical_not(active))
    def _():
        ys_ref[...] = jnp.zeros_like(ys_ref)


def _expert_blocks(xs, block_e, n_valid, w_gu, b_gu, w_down, b_down):
    n_seg, n_rows, _ = xs.shape
    n_e, d, dff2 = w_gu.shape
    bm = MOE_ROWS
    seg_block = pl.BlockSpec((n_seg, bm, LANES), lambda i, be, nu: (0, i, 0))
    grid_spec = pltpu.PrefetchScalarGridSpec(
        num_scalar_prefetch=2,
        grid=(n_rows // bm,),
        in_specs=[seg_block,
                  pl.BlockSpec((None, d, dff2), lambda i, be, nu: (be[i], 0, 0)),
                  pl.BlockSpec((None, 1, dff2), lambda i, be, nu: (be[i], 0, 0)),
                  pl.BlockSpec((None, dff2 // 2, d), lambda i, be, nu: (be[i], 0, 0)),
                  pl.BlockSpec((None, 1, d), lambda i, be, nu: (be[i], 0, 0))],
        out_specs=seg_block,
        scratch_shapes=[pltpu.VMEM((d, dff2), BF16), pltpu.VMEM((dff2 // 2, d), BF16)],
    )
    return pl.pallas_call(
        _expert_body,
        grid_spec=grid_spec,
        out_shape=jax.ShapeDtypeStruct(xs.shape, jnp.uint32),
        compiler_params=_cparams(("arbitrary",), VMEM_LIMIT),
        name="moe_experts",
    )(block_e, n_valid, xs, w_gu, b_gu.reshape(n_e, 1, dff2), w_down, b_down.reshape(n_e, 1, d))


def _sc_gather(table, idx):
    n = idx.shape[0]
    width = table.shape[1]
    mesh = plsc.VectorSubcoreMesh(core_axis_name="core", subcore_axis_name="subcore")
    n_workers = mesh.num_cores * mesh.num_subcores
    assert width == LANES and n % (SC_WINDOW * n_workers) == 0

    @functools.partial(pl.kernel, out_type=jax.ShapeDtypeStruct((n, width), table.dtype), mesh=mesh)
    def gather_kernel(table_hbm, idx_hbm, out_hbm):
        def body(idx_vmem, out_vmem):
            pltpu.sync_copy(table_hbm.at[idx_vmem.at[0]], out_vmem)

        pltpu.emit_pipeline(
            body,
            grid=(n // SC_WINDOW,),
            in_specs=[pl.BlockSpec((1, SC_WINDOW), lambda i: (0, i))],
            out_specs=[pl.BlockSpec((SC_WINDOW, width), lambda i: (i, 0))],
            core_axis_name=("core", "subcore"),
            dimension_semantics=(pltpu.PARALLEL,),
        )(idx_hbm, out_hbm)

    return gather_kernel(table, idx.reshape(1, n))


def _sc_scatter(rows, idx, n_copies, n_out):
    n, width = rows.shape
    mesh = plsc.VectorSubcoreMesh(core_axis_name="core", subcore_axis_name="subcore")
    n_workers = mesh.num_cores * mesh.num_subcores
    assert width == LANES and n % (SC_WINDOW * n_workers) == 0 and idx.shape == (8, n)

    @functools.partial(pl.kernel, out_type=jax.ShapeDtypeStruct((n_out, width), rows.dtype), mesh=mesh)
    def scatter_kernel(rows_hbm, idx_hbm, out_hbm):
        def body(rows_vmem, idx_vmem):
            for k in range(n_copies):
                pltpu.sync_copy(rows_vmem, out_hbm.at[idx_vmem.at[k]])

        pltpu.emit_pipeline(
            body,
            grid=(n // SC_WINDOW,),
            in_specs=[pl.BlockSpec((SC_WINDOW, width), lambda i: (i, 0)),
                      pl.BlockSpec((8, SC_WINDOW), lambda i: (0, i))],
            out_specs=[],
            core_axis_name=("core", "subcore"),
            dimension_semantics=(pltpu.PARALLEL,),
        )(rows_hbm, idx_hbm)

    return scatter_kernel(rows, idx)


def _combine_body(pk_ref, gt_ref, xn_ref, g2_ref, o_ref):
    n_seg = pk_ref.shape[0]
    half = o_ref.shape[1] // 2
    gt = gt_ref[...]
    g2 = g2_ref[...]
    for j in range(n_seg):
        acc_hi = None
        acc_lo = None
        for kk in range(TOP_K):
            hi, lo = _unpack_words(pk_ref[j, kk])
            g = gt[:, kk:kk + 1]
            acc_hi = g * hi if acc_hi is None else acc_hi + g * hi
            acc_lo = g * lo if acc_lo is None else acc_lo + g * lo
        c_hi = slice(j * LANES, (j + 1) * LANES)
        c_lo = slice(half + j * LANES, half + (j + 1) * LANES)
        o_ref[:, c_hi] = xn_ref[:, c_hi] + g2[:, c_hi] * acc_hi
        o_ref[:, c_lo] = xn_ref[:, c_lo] + g2[:, c_lo] * acc_lo


def _combine(picked, gates_t, xn, g2, tokens_per_batch, tt):
    n_seg, _, t, _ = picked.shape
    d = xn.shape[1]
    steps_per_batch = tokens_per_batch // tt
    return pl.pallas_call(
        _combine_body,
        grid=(t // tt,),
        in_specs=[pl.BlockSpec((n_seg, TOP_K, tt, LANES), lambda i: (0, 0, i, 0)),
                  pl.BlockSpec((tt, TOP_K), lambda i: (i, 0)),
                  pl.BlockSpec((tt, d), lambda i: (i, 0)),
                  pl.BlockSpec((None, 1, d), lambda i: (i // steps_per_batch, 0, 0))],
        out_specs=pl.BlockSpec((tt, d), lambda i: (i, 0)),
        out_shape=jax.ShapeDtypeStruct((t, d), F32),
        compiler_params=_cparams(("parallel",), VMEM_LIMIT),
        name="moe_combine",
    )(picked, gates_t, xn, g2)


def _moe(h2p, xn, g2, idx, gates, ranks, counts, w_gu, b_gu, w_down, b_down, tokens_per_batch):
    n_seg, t, _ = h2p.shape
    bm = MOE_ROWS
    n_e = w_gu.shape[0]
    cnt = counts[:, 0].astype(jnp.int32)
    padded = (cnt + bm - 1) // bm * bm
    padded_ends = jnp.cumsum(padded)
    padded_starts = padded_ends - padded
    experts = jnp.arange(n_e, dtype=jnp.int32)[:, None, None]
    dest = ranks + jnp.sum(jnp.where(idx[None] == experts, padded_starts[:, None, None], 0), axis=0)
    n_blocks = t * TOP_K // bm + n_e
    n_rows = n_blocks * bm
    block_start = jnp.arange(n_blocks, dtype=jnp.int32) * bm
    block_e = jnp.minimum(jnp.sum(padded_ends[None, :] <= block_start[:, None], axis=1),
                          n_e - 1).astype(jnp.int32)
    n_valid = jnp.clip(cnt[block_e] - (block_start - padded_starts[block_e]), 0, bm).astype(jnp.int32)
    seg = jnp.arange(n_seg, dtype=jnp.int32)
    scatter_idx = (seg[None, :, None] * n_rows + dest[:, None, :]).reshape(TOP_K, n_seg * t)
    scatter_idx = jnp.concatenate([scatter_idx, scatter_idx], axis=0)
    xs = _sc_scatter(h2p.reshape(n_seg * t, LANES), scatter_idx, TOP_K, n_seg * n_rows)
    ys = _expert_blocks(xs.reshape(n_seg, n_rows, LANES), block_e, n_valid, w_gu, b_gu, w_down, b_down)
    picked = _sc_gather(ys.reshape(n_seg * n_rows, LANES),
                        (seg[:, None, None] * n_rows + dest[None]).reshape(-1))
    return _combine(picked.reshape(n_seg, TOP_K, t, LANES), gates.T, xn, g2, tokens_per_batch,
                    min(TOKEN_TILE, tokens_per_batch))


def kernel(x, c, ctx, c_ctx, mod_w, mod_b, norm1_g, w_in, hy_conv_w, hy_conv_b, hy_f_w1, hy_f_b1,
           hy_f_w2, hy_f_b2, hy_f_w3, hy_f_b3, hy_f_w4, hy_f_freq, hy_bias, mla_q_norm_g, mla_w_uq,
           mla_kv_norm_g, mla_w_ukv, qk_norm_q_g, qk_norm_k_g, w_out, norm2_g, router_w, router_b,
           exp_w_gu, exp_b_gu, exp_w_down, exp_b_down):
    b, length, d = x.shape
    depth = mod_w.shape[0]
    assert depth == 1, "single-layer kernel"
    ly = 0
    c_rows = jnp.concatenate([c, c_ctx[None, :], jnp.zeros((8 - b - 1, d), F32)], axis=0)
    mod = _adaln_table(c_rows, mod_w[ly], mod_b[ly])
    mod6 = mod.reshape(8, 6, d)
    sh1, sc1, g1, sh2, sc2, g2 = (mod6[:b, j][:, None, :] for j in range(6))
    csh1 = mod6[b:b + 1, 0][:, None, :]
    csc1 = mod6[b:b + 1, 1][:, None, :]

    weights = _mla_weights(w_in[ly], mla_w_uq[ly], mla_w_ukv[ly], qk_norm_q_g[ly], qk_norm_k_g[ly])
    n_ctx = ctx.shape[1]
    _, _, k_c, v_c = _inproj(ctx, jnp.broadcast_to(csh1, (b, 1, d)), jnp.broadcast_to(csc1, (b, 1, d)),
                             norm1_g[ly], weights, mla_q_norm_g[ly], mla_kv_norm_g[ly], False, n_ctx)
    hy, q, k, v = _inproj(x, sh1, sc1, norm1_g[ly], weights, mla_q_norm_g[ly], mla_kv_norm_g[ly],
                          True, min(TOKEN_TILE, length))

    kern, asum = _hyena_kernel_taps(length, hy_f_w1[ly], hy_f_b1[ly], hy_f_w2[ly], hy_f_b2[ly],
                                    hy_f_w3[ly], hy_f_b3[ly], hy_f_w4[ly], hy_f_freq[ly])
    kf = _hyena_filter_spectrum(kern, asum)
    y_hy = _hyena_conv(hy, hy_conv_w[ly], hy_conv_b[ly], hy_bias[ly], kf)

    y_att = _flash_attention(q, k, v, k_c, v_c, min(ATTN_Q_TILE, length), min(ATTN_K_TILE, length),
                             ATTN_ROW_GROUP)

    t = b * length
    xn, h2, idx, gates, ranks, counts = _outproj_router(
        y_hy.reshape(t, -1), y_att.reshape(t, -1), x.reshape(t, d), g1, sh2, sc2, norm2_g[ly],
        w_out[ly], router_w[ly], router_b[ly], length, min(TOKEN_TILE, length))
    out = _moe(h2, xn, g2, idx, gates, ranks, counts, exp_w_gu[ly], exp_b_gu[ly],
               exp_w_down[ly], exp_b_down[ly], length)
    return out.reshape(b, length, d)
```
